```python
import math
import jax, jax.numpy as jnp
from jax import lax
import numpy as np

D_MODEL = 1024
BATCH = 8
SEQ = 4096
DEPTH = 4

HEAD_DIM = 64
N_HEADS = D_MODEL // HEAD_DIM
A_KV_HEADS = 4
A_GROUP = N_HEADS // A_KV_HEADS
A_WINDOW = 128
A_BLOCK = 128
B_BRANCHES = ((128, 1), (512, 4), (2048, 16))
GRID_W = 64
NA_ROWS = 8
NA_COLS = 16
T5_BUCKETS = 32
T5_MAX_DIST = 1024
N_GROUPS = 4
EXPERTS_PER_GROUP = 8
N_EXPERTS = N_GROUPS * EXPERTS_PER_GROUP
TOP_K = 2
D_EXPERT = 512
MOE_BLOCK = 128
N_MIXERS = 3
ALPHA = (2 * DEPTH) ** 0.25
BETA = (8 * DEPTH) ** -0.25
LN_EPS = 1e-5
NEG_INF = -1e30

kernel_name = "hybrid_dilated_window_neighbourhood_hmoe_encoder"


def layer_norm(x, g, b):
    xf = x.astype(jnp.float32)
    mu = xf.mean(-1, keepdims=True)
    var = jnp.square(xf - mu).mean(-1, keepdims=True)
    return ((xf - mu) * lax.rsqrt(var + LN_EPS) * g.astype(jnp.float32) + b.astype(jnp.float32)).astype(x.dtype)


def t5_bucket(rel):
    half = T5_BUCKETS // 2
    exact = half // 2
    n = np.abs(rel)
    big = exact + (np.log(np.maximum(n, 1) / exact) / math.log(T5_MAX_DIST / exact) * (half - exact)).astype(np.int32)
    return ((rel > 0) * half + np.where(n < exact, n, np.minimum(big, half - 1))).astype(np.int32)


def relpos_bias(table, rel):
    b = jnp.take(table, jnp.asarray(t5_bucket(rel)), axis=0)
    return jnp.moveaxis(b, -1, 0).astype(jnp.float32)


def band_rel(blk):
    return (np.arange(3 * blk)[None, :] - blk) - np.arange(blk)[:, None]


def banded_attention(q, k, v, window, blk, bias, sink):
    n, length, kh, g, hd = q.shape
    nb = -(-length // blk)
    pad = nb * blk - length
    qb = jnp.pad(q, ((0, 0), (0, pad), (0, 0), (0, 0), (0, 0))).reshape(n, nb, blk, kh, g, hd)

    def band(t):
        t = jnp.pad(t, ((0, 0), (blk, blk + pad), (0, 0), (0, 0))).reshape(n, nb + 2, blk, kh, hd)
        return jnp.concatenate([t[:, :-2], t[:, 1:-1], t[:, 2:]], axis=2)

    kb, vb = band(k), band(v)
    s = jnp.einsum("nbqhgd,nbkhd->nbhgqk", qb, kb, preferred_element_type=jnp.float32) * hd ** -0.5
    rel = band_rel(blk)
    kpos = np.arange(nb)[:, None, None] * blk + (np.arange(3 * blk) - blk)[None, None, :]
    valid = (np.abs(rel) <= window)[None] & (kpos >= 0) & (kpos < length)
    s = jnp.where(valid[None, :, None, None], s + bias[None, None], NEG_INF)
    m = s.max(-1)
    if sink is not None:
        sk = sink.astype(jnp.float32)[None, None, :, :, None]
        m = jnp.maximum(m, sk)
    e = jnp.exp(s - m[..., None])
    den = e.sum(-1)
    if sink is not None:
        den = den + jnp.exp(sk - m)
    o = jnp.einsum("nbhgqk,nbkhd->nbqhgd", (e / den[..., None]).astype(v.dtype), vb)
    o = o.reshape(n, nb * blk, kh, g, hd)[:, :length]
    lse = (m + jnp.log(den)).transpose(0, 1, 4, 2, 3).reshape(n, nb * blk, kh, g)[:, :length]
    return o, lse


def mixer_a(x, w_qkv, sink, w_o, t5_table):
    bsz, s_len, _ = x.shape
    qkv = x @ w_qkv
    qw, kw = N_HEADS * HEAD_DIM, A_KV_HEADS * HEAD_DIM
    q = qkv[..., :qw].reshape(bsz, s_len, A_KV_HEADS, A_GROUP, HEAD_DIM)
    k = qkv[..., qw:qw + kw].reshape(bsz, s_len, A_KV_HEADS, HEAD_DIM)
    v = qkv[..., qw + kw:].reshape(bsz, s_len, A_KV_HEADS, HEAD_DIM)
    bias = relpos_bias(t5_table, band_rel(A_BLOCK)).reshape(A_KV_HEADS, A_GROUP, A_BLOCK, 3 * A_BLOCK)
    out, _ = banded_attention(q, k, v, A_WINDOW, A_BLOCK, bias, sink.reshape(A_KV_HEADS, A_GROUP))
    return out.reshape(bsz, s_len, qw) @ w_o


def to_residues(t, dil):
    bsz, s_len, h, hd = t.shape
    return t.reshape(bsz, s_len // dil, dil, h, hd).transpose(0, 2, 1, 3, 4).reshape(bsz * dil, s_len // dil, h, hd)


def from_residues(t, bsz, dil):
    sub = t.shape[1]
    t = t.reshape(bsz, dil, sub, *t.shape[2:])
    t = jnp.swapaxes(t, 1, 2)
    return t.reshape(bsz, sub * dil, *t.shape[3:])


def mixer_b(x, w_qkv, w_o, t5_table):
    bsz, s_len, _ = x.shape
    q, k, v = jnp.split(x @ w_qkv, 3, axis=-1)
    q = q.reshape(bsz, s_len, N_HEADS, HEAD_DIM)
    k = k.reshape(bsz, s_len, N_HEADS, HEAD_DIM)
    v = v.reshape(bsz, s_len, N_HEADS, HEAD_DIM)
    outs, lses = [], []
    for window, dil in B_BRANCHES:
        half = window // 2 // dil
        bias = relpos_bias(t5_table, band_rel(half) * dil)[:, None]
        o, lse = banded_attention(to_residues(q, dil)[:, :, :, None], to_residues(k, dil), to_residues(v, dil),
                                  half, half, bias, None)
        outs.append(from_residues(o[:, :, :, 0], bsz, dil))
        lses.append(from_residues(lse[:, :, :, 0], bsz, dil))
    wts = jax.nn.softmax(jnp.stack(lses), axis=0)
    mixed = jnp.einsum("nbsh,nbshd->bshd", wts, jnp.stack(outs).astype(jnp.float32))
    return mixed.astype(x.dtype).reshape(bsz, s_len, N_HEADS * HEAD_DIM) @ w_o


def mixer_c(x, w_qkv, rpb, w_o):
    bsz, s_len, _ = x.shape
    rows = s_len // GRID_W
    kr = min(NA_ROWS, rows)
    q, k, v = jnp.split(x @ w_qkv, 3, axis=-1)
    q = q.reshape(bsz, rows, GRID_W, N_HEADS, HEAD_DIM)
    k = k.reshape(bsz, rows, GRID_W, N_HEADS, HEAD_DIM)
    v = v.reshape(bsz, rows, GRID_W, N_HEADS, HEAD_DIM)
    ncb = GRID_W // NA_COLS
    qcol = np.arange(GRID_W).reshape(ncb, NA_COLS)
    win_start = np.clip(qcol - NA_COLS // 2, 0, GRID_W - NA_COLS)
    band_start = np.clip(np.arange(ncb) * NA_COLS - NA_COLS // 2, 0, GRID_W - 2 * NA_COLS)
    band_cols = band_start[:, None] + np.arange(2 * NA_COLS)
    col_ok = (band_cols[:, None, :] >= win_start[..., None]) & (band_cols[:, None, :] < win_start[..., None] + NA_COLS)
    col_idx = np.clip(band_cols[:, None, :] - qcol[..., None], 1 - NA_COLS, NA_COLS - 1) + NA_COLS - 1
    rpb_cols = rpb.astype(jnp.float32)[:, :, col_idx]
    scale = HEAD_DIM ** -0.5

    def row_attend(args):
        r, q_row = args
        start = jnp.clip(r - kr // 2, 0, rows - kr)
        kb = lax.dynamic_slice_in_dim(k, start, kr, axis=1)[:, :, band_cols]
        vb = lax.dynamic_slice_in_dim(v, start, kr, axis=1)[:, :, band_cols]
        qb = q_row.reshape(bsz, ncb, NA_COLS, N_HEADS, HEAD_DIM)
        s = jnp.einsum("bjqhd,brjkhd->bhjqrk", qb, kb, preferred_element_type=jnp.float32) * scale
        row_idx = start + jnp.arange(kr) - r + NA_ROWS - 1
        bias = jnp.take(rpb_cols, row_idx, axis=1).transpose(0, 2, 3, 1, 4)
        s = jnp.where(col_ok[None, None, :, :, None, :], s + bias[None], NEG_INF)
        p = jax.nn.softmax(s.reshape(*s.shape[:4], -1), axis=-1).reshape(s.shape)
        o = jnp.einsum("bhjqrk,brjkhd->bjqhd", p.astype(vb.dtype), vb)
        return o.reshape(bsz, GRID_W, N_HEADS, HEAD_DIM)

    out = lax.map(row_attend, (jnp.arange(rows), q.transpose(1, 0, 2, 3, 4)))
    return out.transpose(1, 0, 2, 3, 4).reshape(bsz, s_len, N_HEADS * HEAD_DIM) @ w_o


def hier_moe(h, rg_w, rg_b, re_w, re_b, w_gate, w_up, w_down):
    bsz, s_len, d = h.shape
    x2 = h.reshape(-1, d)
    n = x2.shape[0]
    lg = (x2 @ rg_w).astype(jnp.float32) + rg_b.astype(jnp.float32)
    grp = jnp.argmax(lg, axis=-1)
    p_grp = jnp.take_along_axis(jax.nn.softmax(lg, axis=-1), grp[:, None], axis=-1)
    le = (x2 @ re_w).astype(jnp.float32).reshape(n, N_GROUPS, EXPERTS_PER_GROUP) + re_b.astype(jnp.float32)
    le = jnp.take_along_axis(le, grp[:, None, None], axis=1)[:, 0]
    top_v, top_i = lax.top_k(le, TOP_K)
    gate = p_grp * jax.nn.softmax(top_v, axis=-1)
    expert = (grp[:, None] * EXPERTS_PER_GROUP + top_i).astype(jnp.int32)
    a = n * TOP_K
    flat_e = expert.reshape(a)
    flat_t = jnp.repeat(jnp.arange(n, dtype=jnp.int32), TOP_K)
    order = jnp.argsort(flat_e)
    se = flat_e[order]
    counts = jnp.bincount(flat_e, length=N_EXPERTS)
    padded = (counts + MOE_BLOCK - 1) // MOE_BLOCK * MOE_BLOCK
    pad_start = jnp.cumsum(padded) - padded
    seg_start = jnp.cumsum(counts) - counts
    dest = pad_start[se] + jnp.arange(a, dtype=jnp.int32) - seg_start[se]
    cap = a + N_EXPERTS * MOE_BLOCK
    slot_tok = jnp.full((cap,), n, jnp.int32).at[dest].set(flat_t[order])
    slot_w = jnp.zeros((cap,), jnp.float32).at[dest].set(gate.reshape(a)[order])
    nblk = cap // MOE_BLOCK
    blk_exp = jnp.minimum(jnp.searchsorted(pad_start + padded, jnp.arange(nblk, dtype=jnp.int32) * MOE_BLOCK,
                                           side="right"), N_EXPERTS - 1)
    xs = jnp.concatenate([x2, jnp.zeros((1, d), x2.dtype)])[slot_tok].reshape(nblk, MOE_BLOCK, d)

    def expert_block(args):
        xb, e = args
        return (jax.nn.silu(xb @ w_gate[e]) * (xb @ w_up[e])) @ w_down[e]

    ys = lax.map(expert_block, (xs, blk_exp)).reshape(cap, d)
    y = jax.ops.segment_sum(ys * slot_w[:, None].astype(ys.dtype), slot_tok, num_segments=n + 1)[:n]
    return y.reshape(bsz, s_len, d)


def setup_inputs(seed: int = 0) -> dict:
    key = jax.random.key(seed)
    keys = iter(jax.random.split(key, 16 * DEPTH + 4))

    def nrm(shape, scale):
        return jax.random.normal(next(keys), shape, jnp.float32) * scale

    d = D_MODEL
    inner = N_HEADS * HEAD_DIM
    inputs = {"x": nrm((BATCH, SEQ, d), 1.0), "t5_table": nrm((T5_BUCKETS, N_HEADS), 0.5)}
    for i in range(DEPTH):
        kind = i % N_MIXERS
        p = f"l{i}_"
        kv_width = A_KV_HEADS * HEAD_DIM if kind == 0 else inner
        inputs[p + "w_qkv"] = jnp.concatenate([nrm((d, inner), d ** -0.5), nrm((d, kv_width), d ** -0.5),
                                               nrm((d, kv_width), BETA * d ** -0.5)], axis=1)
        if kind == 0:
            inputs[p + "sink"] = nrm((N_HEADS,), 1.0)
        if kind == 2:
            inputs[p + "rpb"] = nrm((N_HEADS, 2 * NA_ROWS - 1, 2 * NA_COLS - 1), 0.5)
        inputs[p + "w_o"] = nrm((inner, d), BETA * inner ** -0.5)
        inputs[p + "ln1_g"] = 1.0 + nrm((d,), 0.02)
        inputs[p + "ln1_b"] = nrm((d,), 0.02)
        inputs[p + "rg_w"] = nrm((d, N_GROUPS), d ** -0.5)
        inputs[p + "rg_b"] = nrm((N_GROUPS,), 0.01)
        inputs[p + "re_w"] = nrm((d, N_EXPERTS), d ** -0.5)
        inputs[p + "re_b"] = nrm((N_GROUPS, EXPERTS_PER_GROUP), 0.01)
        inputs[p + "w_gate"] = nrm((N_EXPERTS, d, D_EXPERT), d ** -0.5)
        inputs[p + "w_up"] = nrm((N_EXPERTS, d, D_EXPERT), BETA * d ** -0.5)
        inputs[p + "w_down"] = nrm((N_EXPERTS, D_EXPERT, d), BETA * D_EXPERT ** -0.5)
        inputs[p + "ln2_g"] = 1.0 + nrm((d,), 0.02)
        inputs[p + "ln2_b"] = nrm((d,), 0.02)
    return inputs


def reference(x, t5_table,
              l0_w_qkv, l0_sink, l0_w_o, l0_ln1_g, l0_ln1_b, l0_rg_w, l0_rg_b, l0_re_w, l0_re_b,
              l0_w_gate, l0_w_up, l0_w_down, l0_ln2_g, l0_ln2_b,
              l1_w_qkv, l1_w_o, l1_ln1_g, l1_ln1_b, l1_rg_w, l1_rg_b, l1_re_w, l1_re_b,
              l1_w_gate, l1_w_up, l1_w_down, l1_ln2_g, l1_ln2_b,
              l2_w_qkv, l2_rpb, l2_w_o, l2_ln1_g, l2_ln1_b, l2_rg_w, l2_rg_b, l2_re_w, l2_re_b,
              l2_w_gate, l2_w_up, l2_w_down, l2_ln2_g, l2_ln2_b,
              l3_w_qkv, l3_sink, l3_w_o, l3_ln1_g, l3_ln1_b, l3_rg_w, l3_rg_b, l3_re_w, l3_re_b,
              l3_w_gate, l3_w_up, l3_w_down, l3_ln2_g, l3_ln2_b):
    mixer_args = [(l0_w_qkv, l0_sink, l0_w_o), (l1_w_qkv, l1_w_o),
                  (l2_w_qkv, l2_rpb, l2_w_o), (l3_w_qkv, l3_sink, l3_w_o)]
    norm1 = [(l0_ln1_g, l0_ln1_b), (l1_ln1_g, l1_ln1_b), (l2_ln1_g, l2_ln1_b), (l3_ln1_g, l3_ln1_b)]
    moe_args = [(l0_rg_w, l0_rg_b, l0_re_w, l0_re_b, l0_w_gate, l0_w_up, l0_w_down),
                (l1_rg_w, l1_rg_b, l1_re_w, l1_re_b, l1_w_gate, l1_w_up, l1_w_down),
                (l2_rg_w, l2_rg_b, l2_re_w, l2_re_b, l2_w_gate, l2_w_up, l2_w_down),
                (l3_rg_w, l3_rg_b, l3_re_w, l3_re_b, l3_w_gate, l3_w_up, l3_w_down)]
    norm2 = [(l0_ln2_g, l0_ln2_b), (l1_ln2_g, l1_ln2_b), (l2_ln2_g, l2_ln2_b), (l3_ln2_g, l3_ln2_b)]
    h = x
    for i in range(DEPTH):
        kind = i % N_MIXERS
        if kind == 0:
            y = mixer_a(h, *mixer_args[i], t5_table)
        elif kind == 1:
            y = mixer_b(h, *mixer_args[i], t5_table)
        else:
            y = mixer_c(h, *mixer_args[i])
        h = layer_norm(ALPHA * h + y, *norm1[i])
        h = layer_norm(ALPHA * h + hier_moe(h, *moe_args[i]), *norm2[i])
    return h
```

```python
import functools
import math

import jax
import jax.numpy as jnp
import numpy as np
from jax import lax
from jax.experimental import pallas as pl
from jax.experimental.pallas import tpu as pltpu

HEAD_DIM = 64
N_HEADS = 16
A_KV_HEADS = 4
A_WINDOW = 128
A_BLOCK = 128
B_BRANCHES = ((128, 1), (512, 4), (2048, 16))
GRID_W = 64
NA_ROWS = 8
NA_COLS = 16
T5_BUCKETS = 32
T5_MAX_DIST = 1024
N_GROUPS = 4
EXPERTS_PER_GROUP = 8
N_EXPERTS = N_GROUPS * EXPERTS_PER_GROUP
MOE_BLOCK = 128
DEPTH = 4
N_MIXERS = 3
ALPHA = (2 * DEPTH) ** 0.25
LN_EPS = 1e-5
NEG_INF = -1e30

LANES = 128
PAIR = 2 * HEAD_DIM
VMEM_LIMIT = 56 * 1024 * 1024

PROJ_TM = 512
PROJ_CHUNK = 512
POST_TM = 256
TOK_TM = 256
A_TQ = 512
C_ROWS_PER_STEP = 8
ROUTE_LANE0 = N_GROUPS

_F32 = jnp.float32
_BF16 = jnp.bfloat16


def _params(sem, vmem=VMEM_LIMIT):
    return pltpu.CompilerParams(dimension_semantics=sem, vmem_limit_bytes=vmem)


def _proj_kernel(x_ref, w_ref, *o_refs, widths, scales):
    xb = x_ref[...].astype(_BF16)
    col = 0
    for o_ref, width, scale in zip(o_refs, widths, scales):
        for c in range(0, width, PROJ_CHUNK):
            cw = min(PROJ_CHUNK, width - c)
            acc = jnp.dot(xb, w_ref[:, col + c:col + c + cw], preferred_element_type=_F32)
            if scale != 1.0:
                acc = acc * scale
            o_ref[:, c:c + cw] = acc.astype(o_ref.dtype)
        col += width


def _project(x2, w_bf16, widths, scales):
    n, d = x2.shape
    ctot = sum(widths)
    return pl.pallas_call(
        functools.partial(_proj_kernel, widths=tuple(widths), scales=tuple(scales)),
        out_shape=[jax.ShapeDtypeStruct((n, w), _BF16) for w in widths],
        grid=(n // PROJ_TM,),
        in_specs=[pl.BlockSpec((PROJ_TM, d), lambda i: (i, 0)),
                  pl.BlockSpec((d, ctot), lambda i: (0, 0))],
        out_specs=[pl.BlockSpec((PROJ_TM, w), lambda i: (i, 0)) for w in widths],
        compiler_params=_params(("arbitrary",)),
        name="qkv_proj",
    )(x2, w_bf16)


def _lane_lo(rows=1):
    return lax.broadcasted_iota(jnp.int32, (rows, PAIR), 1) < HEAD_DIM


def _swap_halves(x):
    return pltpu.roll(x.astype(_F32), HEAD_DIM, axis=1).astype(_BF16)


def _softmax_head(q_masked, k, v, bias, sink):
    s = lax.dot_general(q_masked, k, (((1,), (1,)), ((), ())), preferred_element_type=_F32) + bias
    m = jnp.max(s, axis=-1, keepdims=True)
    if sink is not None:
        m = jnp.maximum(m, sink)
    e = jnp.exp(s - m)
    den = jnp.sum(e, axis=-1, keepdims=True)
    if sink is not None:
        den = den + jnp.exp(sink - m)
    o = jnp.dot(e.astype(_BF16), v, preferred_element_type=_F32)
    return o * (1.0 / den), m + jnp.log(den)


def _attend_pair(q2, ka, kb, va, vb, bias_a, bias_b, sink_a=None, sink_b=None):
    lo = _lane_lo()
    zero = jnp.zeros_like(q2)
    oa, lse_a = _softmax_head(jnp.where(lo, q2, zero), ka, va, bias_a, sink_a)
    ob, lse_b = _softmax_head(jnp.where(lo, zero, q2), kb, vb, bias_b, sink_b)
    return jnp.where(lo, oa, ob), lse_a, lse_b


def _t5_bucket(rel):
    half = T5_BUCKETS // 2
    exact = half // 2
    n = np.abs(rel)
    big = exact + (np.log(np.maximum(n, 1) / exact) / math.log(T5_MAX_DIST / exact) * (half - exact)).astype(np.int32)
    return ((rel > 0) * half + np.where(n < exact, n, np.minimum(big, half - 1))).astype(np.int32)


def _band_bias(t5_table, blk, window, dil):
    q = np.arange(blk)[:, None]
    k = np.arange(3 * blk)[None, :]
    out = []
    for shift in (blk, 0, 2 * blk):
        rel = k - shift - q
        b = jnp.take(t5_table, jnp.asarray(_t5_bucket(rel * dil)), axis=0).astype(_F32)
        b = jnp.where(jnp.asarray(np.abs(rel) <= window)[..., None], b, NEG_INF)
        out.append(jnp.moveaxis(b, -1, 0))
    return jnp.stack(out, axis=1)


def _band_window(gb, nblk, blk, length):
    start = pl.multiple_of(jnp.clip((gb - 1) * blk, 0, length - 3 * blk), blk)
    var = jnp.where(gb == 0, 1, jnp.where(gb == nblk - 1, 2, 0))
    return start, var


def _attn_a_kernel(sink_ref, q_ref, k_ref, v_ref, bias_ref, o_ref, kr_ref, vr_ref, *, seq):
    kvb = pl.program_id(0)
    qt = pl.program_id(2)

    @pl.when(qt == 0)
    def _():
        kr_ref[...] = _swap_halves(k_ref[0])
        vr_ref[...] = _swap_halves(v_ref[0])

    nsub = A_TQ // A_BLOCK
    nblk = seq // A_BLOCK
    npair = q_ref.shape[2] // PAIR

    def body(si, carry):
        gb = qt * nsub + si
        start, var = _band_window(gb, nblk, A_BLOCK, seq)
        q0 = pl.multiple_of(si * A_BLOCK, A_BLOCK)
        band = pl.ds(start, 3 * A_BLOCK)
        k_n, v_n = k_ref[0, band, :], v_ref[0, band, :]
        k_s, v_s = kr_ref[band, :], vr_ref[band, :]
        for jj in range(npair):
            hk = jj // (npair // 2)
            q2 = q_ref[0, pl.ds(q0, A_BLOCK), jj * PAIR:(jj + 1) * PAIR]
            ka, va = (k_n, v_n) if hk == 0 else (k_s, v_s)
            kb, vb = (k_n, v_n) if hk == 1 else (k_s, v_s)
            ha, hb = 2 * jj, 2 * jj + 1
            hbase = kvb * (2 * npair)
            o, _, _ = _attend_pair(q2, ka, kb, va, vb, bias_ref[ha, var], bias_ref[hb, var],
                                   sink_ref[hbase + ha], sink_ref[hbase + hb])
            o_ref[0, pl.ds(q0, A_BLOCK), jj * PAIR:(jj + 1) * PAIR] = o.astype(o_ref.dtype)
        return carry

    lax.fori_loop(0, nsub, body, 0)


def _mixer_a(q, k, v, sink, bias, bsz, seq):
    nkvb = k.shape[2] // PAIR
    qw = q.shape[2] // nkvb
    heads_per = N_HEADS // nkvb
    return pl.pallas_call(
        functools.partial(_attn_a_kernel, seq=seq),
        out_shape=jax.ShapeDtypeStruct(q.shape, _BF16),
        grid=(nkvb, bsz, seq // A_TQ),
        in_specs=[pl.BlockSpec(memory_space=pltpu.SMEM),
                  pl.BlockSpec((1, A_TQ, qw), lambda c, b, t: (b, t, c)),
                  pl.BlockSpec((1, seq, PAIR), lambda c, b, t: (b, 0, c)),
                  pl.BlockSpec((1, seq, PAIR), lambda c, b, t: (b, 0, c)),
                  pl.BlockSpec((heads_per, 3, A_BLOCK, 3 * A_BLOCK), lambda c, b, t: (c, 0, 0, 0))],
        out_specs=pl.BlockSpec((1, A_TQ, qw), lambda c, b, t: (b, t, c)),
        scratch_shapes=[pltpu.VMEM((seq, PAIR), _BF16), pltpu.VMEM((seq, PAIR), _BF16)],
        compiler_params=_params(("arbitrary", "arbitrary", "arbitrary")),
        name="mixer_a",
    )(sink, q, k, v, bias)


def _attn_band_kernel(q_ref, k_ref, v_ref, bias_ref, o_ref, lse_ref, *, length, blk, npairs):
    col = pl.program_id(2)

    @pl.when(col == 0)
    def _():
        lse_ref[...] = jnp.zeros_like(lse_ref)

    nblk = length // blk
    lane = lax.broadcasted_iota(jnp.int32, (blk, LANES), 1)

    def body(gb, carry):
        start, var = _band_window(gb, nblk, blk, length)
        q0 = pl.multiple_of(gb * blk, blk)
        rows = pl.ds(q0, blk)
        band = pl.ds(start, 3 * blk)
        lse_tile = lse_ref[0, rows, :]
        for p in range(npairs):
            cols = slice(p * PAIR, (p + 1) * PAIR)
            kk, vv = k_ref[0, band, cols], v_ref[0, band, cols]
            o, lse_a, lse_b = _attend_pair(q_ref[0, rows, cols], kk, kk, vv, vv,
                                           bias_ref[2 * p, var], bias_ref[2 * p + 1, var])
            o_ref[0, rows, cols] = o.astype(o_ref.dtype)
            ha = col * (2 * npairs) + 2 * p
            lse_tile = jnp.where(lane == ha, lse_a, jnp.where(lane == ha + 1, lse_b, lse_tile))
        lse_ref[0, rows, :] = lse_tile
        return carry

    lax.fori_loop(0, nblk, body, 0)


def _mixer_b_branch(q, k, v, bias, bsz, seq, dil, blk):
    length = seq // dil
    width = q.shape[2]
    npairs = (width // PAIR) if length * width * 2 <= 2 * 1024 * 1024 else 1
    ncol = width // (npairs * PAIR)
    bw = npairs * PAIR

    def view(t):
        return t.reshape(bsz, length, dil * width)

    def cmap(b, r, c):
        return (b, 0, r * ncol + c)

    o, lse = pl.pallas_call(
        functools.partial(_attn_band_kernel, length=length, blk=blk, npairs=npairs),
        out_shape=[jax.ShapeDtypeStruct((bsz, length, dil * width), _BF16),
                   jax.ShapeDtypeStruct((bsz, length, dil * LANES), _F32)],
        grid=(bsz, dil, ncol),
        in_specs=[pl.BlockSpec((1, length, bw), cmap),
                  pl.BlockSpec((1, length, bw), cmap),
                  pl.BlockSpec((1, length, bw), cmap),
                  pl.BlockSpec((2 * npairs, 3, blk, 3 * blk), lambda b, r, c: (c, 0, 0, 0))],
        out_specs=[pl.BlockSpec((1, length, bw), cmap),
                   pl.BlockSpec((1, length, LANES), lambda b, r, c: (b, 0, r))],
        compiler_params=_params(("arbitrary", "arbitrary", "arbitrary")),
        name=f"mixer_b_dil{dil}",
    )(view(q), view(k), view(v), bias)
    return o.reshape(bsz, seq, width), lse.reshape(bsz, seq, LANES)


def _na_bias(rpb):
    qc = np.arange(GRID_W)[:, None]
    kc = np.arange(GRID_W)[None, :]
    ws = np.clip(qc - NA_COLS // 2, 0, GRID_W - NA_COLS)
    col_ok = (kc >= ws) & (kc < ws + NA_COLS)
    col_idx = np.clip(kc - qc, 1 - NA_COLS, NA_COLS - 1) + NA_COLS - 1
    u = np.arange(NA_ROWS)[:, None]
    kr = np.arange(NA_ROWS)[None, :]
    row_idx = kr - u + NA_ROWS - 1
    b = rpb.astype(_F32)[:, jnp.asarray(row_idx)][:, :, :, jnp.asarray(col_idx)]
    b = jnp.where(jnp.asarray(col_ok)[None, None, None], b, NEG_INF)
    b = b.transpose(0, 1, 3, 2, 4)
    return b.reshape(rpb.shape[0], NA_ROWS, GRID_W, NA_ROWS * GRID_W)


def _attn_c_kernel(q_ref, k_ref, v_ref, bias_ref, o_ref, *, rows):
    rt = pl.program_id(2)

    def body(ri, carry):
        i = rt * C_ROWS_PER_STEP + ri
        start = jnp.clip(i - NA_ROWS // 2, 0, rows - NA_ROWS)
        var = i - start
        keys = pl.ds(pl.multiple_of(start * GRID_W, GRID_W), NA_ROWS * GRID_W)
        qrows = pl.ds(pl.multiple_of(ri * GRID_W, GRID_W), GRID_W)
        kk, vv = k_ref[0, keys, :], v_ref[0, keys, :]
        o, _, _ = _attend_pair(q_ref[0, qrows, :], kk, kk, vv, vv, bias_ref[0, var], bias_ref[1, var])
        o_ref[0, qrows, :] = o.astype(o_ref.dtype)
        return carry

    lax.fori_loop(0, C_ROWS_PER_STEP, body, 0)


def _mixer_c(q, k, v, bias, bsz, seq):
    rows = seq // GRID_W
    tq = C_ROWS_PER_STEP * GRID_W
    npair = q.shape[2] // PAIR
    return pl.pallas_call(
        functools.partial(_attn_c_kernel, rows=rows),
        out_shape=jax.ShapeDtypeStruct(q.shape, _BF16),
        grid=(npair, bsz, seq // tq),
        in_specs=[pl.BlockSpec((1, tq, PAIR), lambda c, b, t: (b, t, c)),
                  pl.BlockSpec((1, seq, PAIR), lambda c, b, t: (b, 0, c)),
                  pl.BlockSpec((1, seq, PAIR), lambda c, b, t: (b, 0, c)),
                  pl.BlockSpec((2, NA_ROWS, GRID_W, NA_ROWS * GRID_W), lambda c, b, t: (c, 0, 0, 0))],
        out_specs=pl.BlockSpec((1, tq, PAIR), lambda c, b, t: (b, t, c)),
        compiler_params=_params(("arbitrary", "arbitrary", "arbitrary")),
        name="mixer_c",
    )(q, k, v, bias)


def _layer_norm(z, g, b):
    mu = jnp.mean(z, axis=-1, keepdims=True)
    zc = z - mu
    var = jnp.mean(zc * zc, axis=-1, keepdims=True)
    return zc * lax.rsqrt(var + LN_EPS) * g + b


def _split_bf16(x):
    hi = x.astype(_BF16)
    return hi, (x - hi.astype(_F32)).astype(_BF16)


def _route(hn, rw_ref, rb_ref, tri_ref, carry_ref):
    tm = hn.shape[0]
    h_hi, h_lo = _split_bf16(hn)
    w_hi, w_lo = _split_bf16(rw_ref[...])
    logits = (jnp.dot(h_hi, w_hi, preferred_element_type=_F32)
              + (jnp.dot(h_hi, w_lo, preferred_element_type=_F32)
                 + jnp.dot(h_lo, w_hi, preferred_element_type=_F32))) + rb_ref[...]
    lane = lax.broadcasted_iota(jnp.int32, (tm, LANES), 1).astype(_F32)
    ninf = -jnp.inf
    is_grp = lane < N_GROUPS
    lg = jnp.where(is_grp, logits, ninf)
    gmax = jnp.max(lg, axis=-1, keepdims=True)
    grp = jnp.min(jnp.where(lg == gmax, lane, float(LANES)), axis=-1, keepdims=True)
    p_grp = 1.0 / jnp.sum(jnp.where(is_grp, jnp.exp(lg - gmax), 0.0), axis=-1, keepdims=True)
    lo_lane = ROUTE_LANE0 + EXPERTS_PER_GROUP * grp
    in_grp = (lane >= lo_lane) & (lane < lo_lane + EXPERTS_PER_GROUP)
    le = jnp.where(in_grp, logits, ninf)
    v1 = jnp.max(le, axis=-1, keepdims=True)
    i1 = jnp.min(jnp.where(le == v1, lane, float(LANES)), axis=-1, keepdims=True)
    le2 = jnp.where(lane == i1, ninf, le)
    v2 = jnp.max(le2, axis=-1, keepdims=True)
    i2 = jnp.min(jnp.where(le2 == v2, lane, float(LANES)), axis=-1, keepdims=True)
    t = jnp.exp(v2 - v1)
    s1 = 1.0 / (1.0 + t)
    g1 = p_grp * s1
    g2 = p_grp * (t * s1)
    sel1 = lane == i1
    sel2 = lane == i2
    onehot = jnp.where(sel1 | sel2, 1.0, 0.0)
    before = jnp.dot(tri_ref[...], onehot.astype(_BF16), preferred_element_type=_F32) + carry_ref[...]
    rank1 = jnp.sum(jnp.where(sel1, before, 0.0), axis=-1, keepdims=True)
    rank2 = jnp.sum(jnp.where(sel2, before, 0.0), axis=-1, keepdims=True)
    carry_ref[...] = carry_ref[...] + jnp.sum(onehot, axis=0, keepdims=True)
    out = jnp.zeros((tm, LANES), _F32)
    for idx, val in enumerate((i1 - ROUTE_LANE0, i2 - ROUTE_LANE0, g1, g2, rank1, rank2)):
        out = jnp.where(lane == idx, val, out)
    return out


def _post_attn_kernel(*refs, n_branch):
    attn_refs = refs[:n_branch]
    pos = n_branch
    if n_branch > 1:
        lse_refs = refs[pos:pos + n_branch]
        expand_ref = refs[pos + n_branch]
        pos += n_branch + 1
    wo_ref, h_ref, g_ref, b_ref, rw_ref, rb_ref, tri_ref = refs[pos:pos + 7]
    h1_ref, route_ref, counts_ref, carry_ref = refs[pos + 7:pos + 11]

    @pl.when(pl.program_id(0) == 0)
    def _():
        carry_ref[...] = jnp.zeros_like(carry_ref)

    if n_branch == 1:
        a = attn_refs[0][...]
    else:
        lses = [r[...] for r in lse_refs]
        mx = functools.reduce(jnp.maximum, lses)
        es = [jnp.exp(l - mx) for l in lses]
        inv = 1.0 / functools.reduce(jnp.add, es)
        mixed = None
        for e, a_ref in zip(es, attn_refs):
            w_hi, w_lo = _split_bf16(e * inv)
            w_full = (jnp.dot(w_hi, expand_ref[...], preferred_element_type=_F32)
                      + jnp.dot(w_lo, expand_ref[...], preferred_element_type=_F32))
            term = w_full * a_ref[...].astype(_F32)
            mixed = term if mixed is None else mixed + term
        a = mixed.astype(_BF16)
    y = jnp.dot(a, wo_ref[...], preferred_element_type=_F32)
    hn = _layer_norm(ALPHA * h_ref[...] + y, g_ref[...], b_ref[...])
    h1_ref[...] = hn
    route_ref[...] = _route(hn, rw_ref, rb_ref, tri_ref, carry_ref)
    counts_ref[...] = carry_ref[...]


def _post_attn(attn, lses, h2, w_o, g, b, rw, rb):
    n, d = h2.shape
    tm = POST_TM
    nb = len(attn)
    tri = jnp.asarray(np.tril(np.ones((tm, tm), np.float32), -1), _BF16)
    row = lambda i: (i, 0)
    const = lambda i: (0, 0)
    args = list(attn)
    specs = [pl.BlockSpec((tm, d), row) for _ in attn]
    if nb > 1:
        expand = np.zeros((LANES, d), np.float32)
        for hd in range(N_HEADS):
            expand[hd, hd * HEAD_DIM:(hd + 1) * HEAD_DIM] = 1.0
        args += list(lses) + [jnp.asarray(expand, _BF16)]
        specs += [pl.BlockSpec((tm, LANES), row) for _ in lses] + [pl.BlockSpec((LANES, d), const)]
    args += [w_o, h2, g, b, rw, rb, tri]
    specs += [pl.BlockSpec((d, d), const), pl.BlockSpec((tm, d), row), pl.BlockSpec((1, d), const),
              pl.BlockSpec((1, d), const), pl.BlockSpec((d, LANES), const), pl.BlockSpec((1, LANES), const),
              pl.BlockSpec((tm, tm), const)]
    return pl.pallas_call(
        functools.partial(_post_attn_kernel, n_branch=nb),
        out_shape=[jax.ShapeDtypeStruct((n, d), _F32), jax.ShapeDtypeStruct((n, LANES), _F32),
                   jax.ShapeDtypeStruct((1, LANES), _F32)],
        grid=(n // tm,),
        in_specs=specs,
        out_specs=[pl.BlockSpec((tm, d), row), pl.BlockSpec((tm, LANES), row), pl.BlockSpec((1, LANES), const)],
        scratch_shapes=[pltpu.VMEM((1, LANES), _F32)],
        compiler_params=_params(("arbitrary",)),
        name="post_attn_router",
    )(*args)


def _dest_kernel(route_ref, ps_ref, o_ref):
    r = route_ref[...]
    lane = lax.broadcasted_iota(jnp.int32, r.shape, 1)
    lanef = lane.astype(_F32)
    ps = ps_ref[...]
    d = []
    for k in range(2):
        e = r[:, k:k + 1] + float(ROUTE_LANE0)
        d.append(jnp.sum(jnp.where(lanef == e, ps, 0.0), axis=-1, keepdims=True) + r[:, 4 + k:5 + k])
    o_ref[...] = jnp.where(lane == 0, d[0], jnp.where(lane == 1, d[1], 0.0)).astype(jnp.int32)


def _dest(route, pad_start_row):
    n = route.shape[0]
    tm = 1024
    return pl.pallas_call(
        _dest_kernel,
        out_shape=jax.ShapeDtypeStruct((n, LANES), jnp.int32),
        grid=(n // tm,),
        in_specs=[pl.BlockSpec((tm, LANES), lambda i: (i, 0)), pl.BlockSpec((1, LANES), lambda i: (0, 0))],
        out_specs=pl.BlockSpec((tm, LANES), lambda i: (i, 0)),
        compiler_params=_params(("arbitrary",)),
        name="moe_dest",
    )(route, pad_start_row)


def _row_copy(src_hbm, dst_hbm, s, d, sem):
    return pltpu.make_async_copy(src_hbm.at[pl.ds(s, 1)], dst_hbm.at[pl.ds(d, 1)], sem)


def _dispatch_kernel(seg_ref, dest_ref, x_hbm, xs_hbm, zero_ref, zsem, sem):
    i = pl.program_id(0)
    last = pl.num_programs(0) - 1
    ntok = TOK_TM

    def zero_fill(e):
        first = pl.multiple_of(seg_ref[e] + seg_ref[N_EXPERTS + e] - MOE_BLOCK, MOE_BLOCK)
        return pltpu.make_async_copy(zero_ref, xs_hbm.at[pl.ds(first, MOE_BLOCK)], zsem)

    def has_pad(e):
        return seg_ref[N_EXPERTS + e] > seg_ref[2 * N_EXPERTS + e]

    @pl.when(i == 0)
    def _():
        zero_ref[...] = jnp.zeros_like(zero_ref)

        def start(e, c):
            @pl.when(has_pad(e))
            def _():
                zero_fill(e).start()
            return c

        def wait(e, c):
            @pl.when(has_pad(e))
            def _():
                zero_fill(e).wait()
            return c

        lax.fori_loop(0, N_EXPERTS, start, 0)
        lax.fori_loop(0, N_EXPERTS, wait, 0)

    def issue(j, c):
        tok = i * ntok + j // 2
        _row_copy(x_hbm, xs_hbm, tok, dest_ref[0, 0, j], sem).start()
        return c

    def drain(j, c):
        _row_copy(x_hbm, xs_hbm, 0, 0, sem).wait()
        return c

    lax.fori_loop(0, 2 * ntok, issue, 0, unroll=8)

    @pl.when(i > 0)
    def _():
        lax.fori_loop(0, 2 * ntok, drain, 0, unroll=8)

    @pl.when(i == last)
    def _():
        lax.fori_loop(0, 2 * ntok, drain, 0, unroll=8)


def _dispatch(x2, dest_flat, seg, cap):
    n, d = x2.shape
    tm = TOK_TM
    nt = n // tm
    return pl.pallas_call(
        _dispatch_kernel,
        out_shape=jax.ShapeDtypeStruct((cap, d), x2.dtype),
        grid_spec=pltpu.PrefetchScalarGridSpec(
            num_scalar_prefetch=1,
            grid=(nt,),
            in_specs=[pl.BlockSpec((1, 1, 2 * tm), lambda i, seg: (i, 0, 0), memory_space=pltpu.SMEM),
                      pl.BlockSpec(memory_space=pl.ANY)],
            out_specs=pl.BlockSpec(memory_space=pl.ANY),
            scratch_shapes=[pltpu.VMEM((MOE_BLOCK, d), x2.dtype), pltpu.SemaphoreType.DMA,
                            pltpu.SemaphoreType.DMA]),
        compiler_params=_params(("arbitrary",)),
        name="moe_dispatch",
    )(seg, dest_flat.reshape(nt, 1, 2 * tm), x2)


def _expert_kernel(blk_ref, nvalid_ref, xs_ref, wg_ref, wu_ref, wd_ref, ys_ref, wgb, wub, wdb):
    i = pl.program_id(0)
    prev = blk_ref[jnp.maximum(i - 1, 0)]
    valid = i < nvalid_ref[0]

    @pl.when(valid & ((i == 0) | (blk_ref[i] != prev)))
    def _():
        wgb[...] = wg_ref[0].astype(_BF16)
        wub[...] = wu_ref[0].astype(_BF16)
        wdb[...] = wd_ref[0].astype(_BF16)

    @pl.when(valid)
    def _():
        xb = xs_ref[...].astype(_BF16)
        gate = jnp.dot(xb, wgb[...], preferred_element_type=_F32)
        up = jnp.dot(xb, wub[...], preferred_element_type=_F32)
        mid = (gate * jax.nn.sigmoid(gate)) * up
        ys_ref[...] = jnp.dot(mid.astype(_BF16), wdb[...], preferred_element_type=_F32)

    @pl.when(jnp.logical_not(valid))
    def _():
        ys_ref[...] = jnp.zeros_like(ys_ref)


def _experts(xs, blk_exp, nvalid, w_gate, w_up, w_down):
    cap, d = xs.shape
    de = w_gate.shape[2]
    nblk = cap // MOE_BLOCK

    def xmap(i, blk, nv):
        return (jnp.minimum(i, nv[0] - 1), 0)

    def wmap(i, blk, nv):
        return (blk[i], 0, 0)

    return pl.pallas_call(
        _expert_kernel,
        out_shape=jax.ShapeDtypeStruct((cap, d), _F32),
        grid_spec=pltpu.PrefetchScalarGridSpec(
            num_scalar_prefetch=2,
            grid=(nblk,),
            in_specs=[pl.BlockSpec((MOE_BLOCK, d), xmap),
                      pl.BlockSpec((1, d, de), wmap), pl.BlockSpec((1, d, de), wmap),
                      pl.BlockSpec((1, de, d), wmap)],
            out_specs=pl.BlockSpec((MOE_BLOCK, d), lambda i, blk, nv: (i, 0)),
            scratch_shapes=[pltpu.VMEM((d, de), _BF16), pltpu.VMEM((d, de), _BF16), pltpu.VMEM((de, d), _BF16)]),
        compiler_params=_params(("arbitrary",)),
        name="moe_experts",
    )(blk_exp, nvalid, xs, w_gate, w_up, w_down)


def _combine_kernel(dcur_ref, dnxt_ref, ys_hbm, h_ref, route_ref, g_ref, b_ref, o_ref, buf, sem):
    i = pl.program_id(0)
    n = pl.num_programs(0)
    tm = TOK_TM

    def gather(dref, slot):
        def body(j, c):
            for k in range(2):
                pltpu.make_async_copy(ys_hbm.at[pl.ds(dref[0, 0, 2 * j + k], 1)],
                                      buf.at[slot, k, pl.ds(j, 1)], sem.at[slot]).start()
            return c
        lax.fori_loop(0, tm, body, 0, unroll=4)

    @pl.when(i == 0)
    def _():
        gather(dcur_ref, 0)

    @pl.when(i + 1 < n)
    def _():
        gather(dnxt_ref, (i + 1) % 2)

    slot = i % 2

    def drain(j, c):
        pltpu.make_async_copy(ys_hbm.at[pl.ds(0, 1)], buf.at[slot, 0, pl.ds(0, 1)], sem.at[slot]).wait()
        return c

    lax.fori_loop(0, 2 * tm, drain, 0, unroll=8)
    r = route_ref[...]
    y = r[:, 2:3] * buf[slot, 0] + r[:, 3:4] * buf[slot, 1]
    o_ref[...] = _layer_norm(ALPHA * h_ref[...] + y, g_ref[...], b_ref[...])


def _combine(ys, dest_flat, h1, route, g, b):
    n, d = h1.shape
    tm = TOK_TM
    nt = n // tm
    dest3 = dest_flat.reshape(nt, 1, 2 * tm)
    row = lambda i: (i, 0)
    const = lambda i: (0, 0)
    return pl.pallas_call(
        _combine_kernel,
        out_shape=jax.ShapeDtypeStruct((n, d), _F32),
        grid=(nt,),
        in_specs=[pl.BlockSpec((1, 1, 2 * tm), lambda i: (i, 0, 0), memory_space=pltpu.SMEM),
                  pl.BlockSpec((1, 1, 2 * tm), lambda i: (jnp.minimum(i + 1, nt - 1), 0, 0),
                               memory_space=pltpu.SMEM),
                  pl.BlockSpec(memory_space=pl.ANY),
                  pl.BlockSpec((tm, d), row), pl.BlockSpec((tm, LANES), row),
                  pl.BlockSpec((1, d), const), pl.BlockSpec((1, d), const)],
        out_specs=pl.BlockSpec((tm, d), row),
        scratch_shapes=[pltpu.VMEM((2, 2, tm, d), _F32), pltpu.SemaphoreType.DMA((2,))],
        compiler_params=_params(("arbitrary",)),
        name="moe_combine",
    )(dest3, dest3, ys, h1, route, g, b)


def _moe(h1, route, counts_row, w_gate, w_up, w_down, g, b):
    n, d = h1.shape
    cap = 2 * n + N_EXPERTS * MOE_BLOCK
    counts = counts_row[0, ROUTE_LANE0:ROUTE_LANE0 + N_EXPERTS].astype(jnp.int32)
    padded = (counts + MOE_BLOCK - 1) // MOE_BLOCK * MOE_BLOCK
    ends = jnp.cumsum(padded)
    pad_start = ends - padded
    nblk = cap // MOE_BLOCK
    blk_exp = jnp.minimum(jnp.searchsorted(ends, jnp.arange(nblk, dtype=jnp.int32) * MOE_BLOCK, side="right"),
                          N_EXPERTS - 1).astype(jnp.int32)
    nvalid = (ends[-1:] // MOE_BLOCK).astype(jnp.int32)
    ps_row = jnp.zeros((1, LANES), _F32).at[0, ROUTE_LANE0:ROUTE_LANE0 + N_EXPERTS].set(pad_start.astype(_F32))
    dest_flat = _dest(route, ps_row)[:, :2].reshape(2 * n)
    seg = jnp.concatenate([pad_start, padded, counts]).astype(jnp.int32)
    xs = _dispatch(h1, dest_flat, seg, cap)
    ys = _experts(xs, blk_exp, nvalid, w_gate, w_up, w_down)
    return _combine(ys, dest_flat, h1, route, g, b)


def _router_params(rg_w, rg_b, re_w, re_b):
    d = rg_w.shape[0]
    pad = LANES - N_GROUPS - N_EXPERTS
    rw = jnp.concatenate([rg_w, re_w, jnp.zeros((d, pad), _F32)], axis=1)
    rb = jnp.concatenate([rg_b, re_b.reshape(-1), jnp.zeros((pad,), _F32)]).reshape(1, LANES)
    return rw, rb


def kernel(x, t5_table, l0_w_qkv, l0_sink, l0_w_o, l0_ln1_g, l0_ln1_b, l0_rg_w, l0_rg_b, l0_re_w, l0_re_b, l0_w_gate, l0_w_up, l0_w_down, l0_ln2_g, l0_ln2_b, l1_w_qkv, l1_w_o, l1_ln1_g, l1_ln1_b, l1_rg_w, l1_rg_b, l1_re_w, l1_re_b, l1_w_gate, l1_w_up, l1_w_down, l1_ln2_g, l1_ln2_b, l2_w_qkv, l2_rpb, l2_w_o, l2_ln1_g, l2_ln1_b, l2_rg_w, l2_rg_b, l2_re_w, l2_re_b, l2_w_gate, l2_w_up, l2_w_down, l2_ln2_g, l2_ln2_b, l3_w_qkv, l3_sink, l3_w_o, l3_ln1_g, l3_ln1_b, l3_rg_w, l3_rg_b, l3_re_w, l3_re_b, l3_w_gate, l3_w_up, l3_w_down, l3_ln2_g, l3_ln2_b):
    mixers = [(l0_w_qkv, l0_sink, l0_w_o), (l1_w_qkv, None, l1_w_o), (l2_w_qkv, l2_rpb, l2_w_o),
              (l3_w_qkv, l3_sink, l3_w_o)]
    norm1 = [(l0_ln1_g, l0_ln1_b), (l1_ln1_g, l1_ln1_b), (l2_ln1_g, l2_ln1_b), (l3_ln1_g, l3_ln1_b)]
    routers = [(l0_rg_w, l0_rg_b, l0_re_w, l0_re_b), (l1_rg_w, l1_rg_b, l1_re_w, l1_re_b),
               (l2_rg_w, l2_rg_b, l2_re_w, l2_re_b), (l3_rg_w, l3_rg_b, l3_re_w, l3_re_b)]
    experts = [(l0_w_gate, l0_w_up, l0_w_down), (l1_w_gate, l1_w_up, l1_w_down),
               (l2_w_gate, l2_w_up, l2_w_down), (l3_w_gate, l3_w_up, l3_w_down)]
    norm2 = [(l0_ln2_g, l0_ln2_b), (l1_ln2_g, l1_ln2_b), (l2_ln2_g, l2_ln2_b), (l3_ln2_g, l3_ln2_b)]

    bsz, seq, d = x.shape
    n = bsz * seq
    inner = N_HEADS * HEAD_DIM
    scale = HEAD_DIM ** -0.5
    h = x.reshape(n, d)
    for i in range(DEPTH):
        kind = i % N_MIXERS
        w_qkv, extra, w_o = mixers[i]
        kvw = A_KV_HEADS * HEAD_DIM if kind == 0 else inner
        q, k, v = _project(h, w_qkv.astype(_BF16), (inner, kvw, kvw), (scale, 1.0, 1.0))
        q, k, v = (t.reshape(bsz, seq, t.shape[1]) for t in (q, k, v))
        lses = ()
        if kind == 0:
            bias = _band_bias(t5_table, A_BLOCK, A_WINDOW, 1)
            attn = [_mixer_a(q, k, v, extra.astype(_F32), bias, bsz, seq)]
        elif kind == 1:
            attn, lses = [], []
            for window, dil in B_BRANCHES:
                half = window // 2 // dil
                o, lse = _mixer_b_branch(q, k, v, _band_bias(t5_table, half, half, dil), bsz, seq, dil, half)
                attn.append(o)
                lses.append(lse.reshape(n, LANES))
        else:
            attn = [_mixer_c(q, k, v, _na_bias(extra), bsz, seq)]
        attn = [a.reshape(n, inner) for a in attn]
        g1, b1 = norm1[i]
        rw, rb = _router_params(*routers[i])
        h1, route, counts = _post_attn(attn, lses, h, w_o.astype(_BF16), g1.reshape(1, d), b1.reshape(1, d), rw, rb)
        g2, b2 = norm2[i]
        h = _moe(h1, route, counts, *experts[i], g2.reshape(1, d), b2.reshape(1, d))
    return h.reshape(bsz, seq, d)
```

```python
import functools
import math

import jax
import jax.numpy as jnp
import numpy as np
from jax import lax
from jax.experimental import pallas as pl
from jax.experimental.pallas import tpu as pltpu

HEAD_DIM = 64
N_HEADS = 16
A_KV_HEADS = 4
A_WINDOW = 128
A_BLOCK = 128
B_BRANCHES = ((128, 1), (512, 4), (2048, 16))
GRID_W = 64
NA_ROWS = 8
NA_COLS = 16
T5_BUCKETS = 32
T5_MAX_DIST = 1024
N_GROUPS = 4
EXPERTS_PER_GROUP = 8
N_EXPERTS = N_GROUPS * EXPERTS_PER_GROUP
MOE_BLOCK = 128
DEPTH = 4
N_MIXERS = 3
ALPHA = (2 * DEPTH) ** 0.25
LN_EPS = 1e-5
NEG_INF = -1e30

LANES = 128
PAIR = 2 * HEAD_DIM
VMEM_LIMIT = 56 * 1024 * 1024

PROJ_TM = 512
PROJ_CHUNK = 512
POST_TM = 256
TOK_TM = 256
A_TQ = 512
C_ROWS_PER_STEP = 8
PAIRS_PER_PASS = 8
ROUTE_LANE0 = N_GROUPS

_F32 = jnp.float32
_BF16 = jnp.bfloat16


def _params(sem, vmem=VMEM_LIMIT, flags=None):
    return pltpu.CompilerParams(dimension_semantics=sem, vmem_limit_bytes=vmem, flags=flags)


def _proj_kernel(x_ref, w_ref, *o_refs, widths, scales):
    xb = x_ref[...].astype(_BF16)
    col = 0
    for o_ref, width, scale in zip(o_refs, widths, scales):
        for c in range(0, width, PROJ_CHUNK):
            cw = min(PROJ_CHUNK, width - c)
            acc = jnp.dot(xb, w_ref[:, col + c:col + c + cw], preferred_element_type=_F32)
            if scale != 1.0:
                acc = acc * scale
            o_ref[:, c:c + cw] = acc.astype(o_ref.dtype)
        col += width


def _project(x2, w_bf16, widths, scales):
    n, d = x2.shape
    ctot = sum(widths)
    return pl.pallas_call(
        functools.partial(_proj_kernel, widths=tuple(widths), scales=tuple(scales)),
        out_shape=[jax.ShapeDtypeStruct((n, w), _BF16) for w in widths],
        grid=(n // PROJ_TM,),
        in_specs=[pl.BlockSpec((PROJ_TM, d), lambda i: (i, 0)),
                  pl.BlockSpec((d, ctot), lambda i: (0, 0))],
        out_specs=[pl.BlockSpec((PROJ_TM, w), lambda i: (i, 0)) for w in widths],
        compiler_params=_params(("arbitrary",)),
        name="qkv_proj",
    )(x2, w_bf16)


def _one():
    return jnp.ones((1,), jnp.int32)


def _lane_lo(rows=1):
    return lax.broadcasted_iota(jnp.int32, (rows, PAIR), 1) < HEAD_DIM


def _swap_halves(x):
    return pltpu.roll(x.astype(_F32), HEAD_DIM, axis=1).astype(_BF16)


class _HeadPair:
    def __init__(self, q2, ka, kb, va, vb, bias_a, bias_b, emit, sink_a=None, sink_b=None):
        self.q2, self.ka, self.kb, self.va, self.vb = q2, ka, kb, va, vb
        self.bias_a, self.bias_b, self.emit, self.sink_a, self.sink_b = bias_a, bias_b, emit, sink_a, sink_b


def _attend_pairs(pairs, s_scr, m_scr, one):
    lo = _lane_lo()
    nt = (((1,), (1,)), ((), ()))

    def scores(_, carry):
        for p, unit in enumerate(pairs):
            q2 = unit.q2()
            zero = jnp.zeros_like(q2)
            for half, (qm, k, bias, sink) in enumerate(((jnp.where(lo, q2, zero), unit.ka(), unit.bias_a(), unit.sink_a),
                                                        (jnp.where(lo, zero, q2), unit.kb(), unit.bias_b(), unit.sink_b))):
                s = lax.dot_general(qm, k, nt, preferred_element_type=_F32) + bias
                m = jnp.max(s, axis=-1, keepdims=True)
                if sink is not None:
                    m = jnp.maximum(m, sink)
                s_scr[2 * p + half] = s
                m_scr[2 * p + half] = m
        return carry

    def values(_, carry):
        for p, unit in enumerate(pairs):
            outs = []
            for half, (v, sink) in enumerate(((unit.va(), unit.sink_a), (unit.vb(), unit.sink_b))):
                m = m_scr[2 * p + half]
                e = jnp.exp(s_scr[2 * p + half] - m)
                den = jnp.sum(e, axis=-1, keepdims=True)
                if sink is not None:
                    den = den + jnp.exp(sink - m)
                o = jnp.dot(e.astype(_BF16), v, preferred_element_type=_F32) * (1.0 / den)
                outs.append((o, m + jnp.log(den)))
            unit.emit(jnp.where(lo, outs[0][0], outs[1][0]), outs[0][1], outs[1][1])
        return carry

    lax.fori_loop(0, one, scores, 0)
    lax.fori_loop(0, one, values, 0)


def _t5_bucket(rel):
    half = T5_BUCKETS // 2
    exact = half // 2
    n = np.abs(rel)
    big = exact + (np.log(np.maximum(n, 1) / exact) / math.log(T5_MAX_DIST / exact) * (half - exact)).astype(np.int32)
    return ((rel > 0) * half + np.where(n < exact, n, np.minimum(big, half - 1))).astype(np.int32)


def _band_bias(t5_table, blk, window, dil):
    q = np.arange(blk)[:, None]
    k = np.arange(3 * blk)[None, :]
    out = []
    for shift in (blk, 0, 2 * blk):
        rel = k - shift - q
        b = jnp.take(t5_table, jnp.asarray(_t5_bucket(rel * dil)), axis=0).astype(_F32)
        b = jnp.where(jnp.asarray(np.abs(rel) <= window)[..., None], b, NEG_INF)
        out.append(jnp.moveaxis(b, -1, 0))
    return jnp.stack(out, axis=1)


def _band_window(gb, nblk, blk, length):
    start = pl.multiple_of(jnp.clip((gb - 1) * blk, 0, length - 3 * blk), blk)
    var = jnp.where(gb == 0, 1, jnp.where(gb == nblk - 1, 2, 0))
    return start, var


def _attn_a_kernel(sink_ref, one_ref, q_ref, k_ref, v_ref, bias_ref, o_ref, kr_ref, vr_ref, s_scr, m_scr, *, seq):
    kvb = pl.program_id(0)
    qt = pl.program_id(2)
    one = one_ref[0]

    @pl.when(qt == 0)
    def _():
        kr_ref[...] = _swap_halves(k_ref[0])
        vr_ref[...] = _swap_halves(v_ref[0])

    nsub = A_TQ // A_BLOCK
    nblk = seq // A_BLOCK
    npair = q_ref.shape[2] // PAIR
    hbase = kvb * (2 * npair)

    def body(si, carry):
        gb = qt * nsub + si
        start, var = _band_window(gb, nblk, A_BLOCK, seq)
        rows = pl.ds(pl.multiple_of(si * A_BLOCK, A_BLOCK), A_BLOCK)
        band = pl.ds(start, 3 * A_BLOCK)
        plain = (lambda: k_ref[0, band, :]), (lambda: v_ref[0, band, :])
        swapped = (lambda: kr_ref[band, :]), (lambda: vr_ref[band, :])
        pairs = []
        for jj in range(npair):
            hk = jj // (npair // 2)
            cols = slice(jj * PAIR, (jj + 1) * PAIR)
            (ka, va), (kb, vb) = (plain if hk == 0 else swapped), (plain if hk == 1 else swapped)

            def emit(o, lse_a, lse_b, cols=cols):
                o_ref[0, rows, cols] = o.astype(o_ref.dtype)

            pairs.append(_HeadPair(
                q2=lambda cols=cols: q_ref[0, rows, cols], ka=ka, kb=kb, va=va, vb=vb,
                bias_a=lambda jj=jj: bias_ref[2 * jj, var], bias_b=lambda jj=jj: bias_ref[2 * jj + 1, var],
                emit=emit, sink_a=sink_ref[hbase + 2 * jj], sink_b=sink_ref[hbase + 2 * jj + 1]))
        _attend_pairs(pairs, s_scr, m_scr, one)
        return carry

    lax.fori_loop(0, nsub, body, 0)


def _mixer_a(q, k, v, sink, bias, bsz, seq):
    nkvb = k.shape[2] // PAIR
    qw = q.shape[2] // nkvb
    heads_per = N_HEADS // nkvb
    return pl.pallas_call(
        functools.partial(_attn_a_kernel, seq=seq),
        out_shape=jax.ShapeDtypeStruct(q.shape, _BF16),
        grid=(nkvb, bsz, seq // A_TQ),
        in_specs=[pl.BlockSpec(memory_space=pltpu.SMEM), pl.BlockSpec(memory_space=pltpu.SMEM),
                  pl.BlockSpec((1, A_TQ, qw), lambda c, b, t: (b, t, c)),
                  pl.BlockSpec((1, seq, PAIR), lambda c, b, t: (b, 0, c)),
                  pl.BlockSpec((1, seq, PAIR), lambda c, b, t: (b, 0, c)),
                  pl.BlockSpec((heads_per, 3, A_BLOCK, 3 * A_BLOCK), lambda c, b, t: (c, 0, 0, 0))],
        out_specs=pl.BlockSpec((1, A_TQ, qw), lambda c, b, t: (b, t, c)),
        scratch_shapes=[pltpu.VMEM((seq, PAIR), _BF16), pltpu.VMEM((seq, PAIR), _BF16),
                        pltpu.VMEM((heads_per, A_BLOCK, 3 * A_BLOCK), _F32),
                        pltpu.VMEM((heads_per, A_BLOCK, 1), _F32)],
        compiler_params=_params(("arbitrary", "arbitrary", "arbitrary")),
        name="mixer_a",
    )(sink, _one(), q, k, v, bias)


def _attn_band_kernel(one_ref, q_ref, k_ref, v_ref, bias_ref, o_ref, lse_ref, s_scr, m_scr, *, length, blk, npairs):
    col = pl.program_id(2)
    one = one_ref[0]

    @pl.when(col == 0)
    def _():
        lse_ref[...] = jnp.zeros_like(lse_ref)

    nblk = length // blk
    qb_per_pass = PAIRS_PER_PASS // npairs
    lane = lax.broadcasted_iota(jnp.int32, (blk, LANES), 1)

    def body(g, carry):
        pairs = []
        for j in range(qb_per_pass):
            gb = g * qb_per_pass + j
            start, var = _band_window(gb, nblk, blk, length)
            rows = pl.ds(pl.multiple_of(gb * blk, blk), blk)
            band = pl.ds(start, 3 * blk)
            for p in range(npairs):
                cols = slice(p * PAIR, (p + 1) * PAIR)
                keys = lambda band=band, cols=cols: k_ref[0, band, cols]
                vals = lambda band=band, cols=cols: v_ref[0, band, cols]

                def emit(o, lse_a, lse_b, rows=rows, cols=cols, p=p):
                    o_ref[0, rows, cols] = o.astype(o_ref.dtype)
                    ha = col * (2 * npairs) + 2 * p
                    lse_ref[0, rows, :] = jnp.where(lane == ha, lse_a,
                                                    jnp.where(lane == ha + 1, lse_b, lse_ref[0, rows, :]))

                pairs.append(_HeadPair(
                    q2=lambda rows=rows, cols=cols: q_ref[0, rows, cols], ka=keys, kb=keys, va=vals, vb=vals,
                    bias_a=lambda p=p, var=var: bias_ref[2 * p, var],
                    bias_b=lambda p=p, var=var: bias_ref[2 * p + 1, var], emit=emit))
        _attend_pairs(pairs, s_scr, m_scr, one)
        return carry

    lax.fori_loop(0, nblk // qb_per_pass, body, 0)


def _mixer_b_branch(q, k, v, bias, bsz, seq, dil, blk):
    length = seq // dil
    width = q.shape[2]
    npairs = (width // PAIR) if length * width * 2 <= 2 * 1024 * 1024 else 1
    ncol = width // (npairs * PAIR)
    bw = npairs * PAIR

    def view(t):
        return t.reshape(bsz, length, dil * width)

    def cmap(b, r, c):
        return (b, 0, r * ncol + c)

    o, lse = pl.pallas_call(
        functools.partial(_attn_band_kernel, length=length, blk=blk, npairs=npairs),
        out_shape=[jax.ShapeDtypeStruct((bsz, length, dil * width), _BF16),
                   jax.ShapeDtypeStruct((bsz, length, dil * LANES), _F32)],
        grid=(bsz, dil, ncol),
        in_specs=[pl.BlockSpec(memory_space=pltpu.SMEM),
                  pl.BlockSpec((1, length, bw), cmap),
                  pl.BlockSpec((1, length, bw), cmap),
                  pl.BlockSpec((1, length, bw), cmap),
                  pl.BlockSpec((2 * npairs, 3, blk, 3 * blk), lambda b, r, c: (c, 0, 0, 0))],
        out_specs=[pl.BlockSpec((1, length, bw), cmap),
                   pl.BlockSpec((1, length, LANES), lambda b, r, c: (b, 0, r))],
        scratch_shapes=[pltpu.VMEM((2 * PAIRS_PER_PASS, blk, 3 * blk), _F32),
                        pltpu.VMEM((2 * PAIRS_PER_PASS, blk, 1), _F32)],
        compiler_params=_params(("arbitrary", "arbitrary", "arbitrary")),
        name=f"mixer_b_dil{dil}",
    )(_one(), view(q), view(k), view(v), bias)
    return o.reshape(bsz, seq, width), lse.reshape(bsz, seq, LANES)


def _na_bias(rpb):
    qc = np.arange(GRID_W)[:, None]
    kc = np.arange(GRID_W)[None, :]
    ws = np.clip(qc - NA_COLS // 2, 0, GRID_W - NA_COLS)
    col_ok = (kc >= ws) & (kc < ws + NA_COLS)
    col_idx = np.clip(kc - qc, 1 - NA_COLS, NA_COLS - 1) + NA_COLS - 1
    u = np.arange(NA_ROWS)[:, None]
    kr = np.arange(NA_ROWS)[None, :]
    row_idx = kr - u + NA_ROWS - 1
    b = rpb.astype(_F32)[:, jnp.asarray(row_idx)][:, :, :, jnp.asarray(col_idx)]
    b = jnp.where(jnp.asarray(col_ok)[None, None, None], b, NEG_INF)
    b = b.transpose(0, 1, 3, 2, 4)
    return b.reshape(rpb.shape[0], NA_ROWS, GRID_W, NA_ROWS * GRID_W)


def _attn_c_kernel(one_ref, q_ref, k_ref, v_ref, bias_ref, o_ref, s_scr, m_scr, *, rows):
    rt = pl.program_id(2)
    pairs = []
    for ri in range(C_ROWS_PER_STEP):
        i = rt * C_ROWS_PER_STEP + ri
        start = jnp.clip(i - NA_ROWS // 2, 0, rows - NA_ROWS)
        var = i - start
        keys = pl.ds(pl.multiple_of(start * GRID_W, GRID_W), NA_ROWS * GRID_W)
        qrows = pl.ds(ri * GRID_W, GRID_W)
        kk = lambda keys=keys: k_ref[0, keys, :]
        vv = lambda keys=keys: v_ref[0, keys, :]

        def emit(o, lse_a, lse_b, qrows=qrows):
            o_ref[0, qrows, :] = o.astype(o_ref.dtype)

        pairs.append(_HeadPair(q2=lambda qrows=qrows: q_ref[0, qrows, :], ka=kk, kb=kk, va=vv, vb=vv,
                               bias_a=lambda var=var: bias_ref[0, var], bias_b=lambda var=var: bias_ref[1, var],
                               emit=emit))
    _attend_pairs(pairs, s_scr, m_scr, one_ref[0])


def _mixer_c(q, k, v, bias, bsz, seq):
    rows = seq // GRID_W
    tq = C_ROWS_PER_STEP * GRID_W
    npair = q.shape[2] // PAIR
    return pl.pallas_call(
        functools.partial(_attn_c_kernel, rows=rows),
        out_shape=jax.ShapeDtypeStruct(q.shape, _BF16),
        grid=(npair, bsz, seq // tq),
        in_specs=[pl.BlockSpec(memory_space=pltpu.SMEM),
                  pl.BlockSpec((1, tq, PAIR), lambda c, b, t: (b, t, c)),
                  pl.BlockSpec((1, seq, PAIR), lambda c, b, t: (b, 0, c)),
                  pl.BlockSpec((1, seq, PAIR), lambda c, b, t: (b, 0, c)),
                  pl.BlockSpec((2, NA_ROWS, GRID_W, NA_ROWS * GRID_W), lambda c, b, t: (c, 0, 0, 0))],
        out_specs=pl.BlockSpec((1, tq, PAIR), lambda c, b, t: (b, t, c)),
        scratch_shapes=[pltpu.VMEM((2 * C_ROWS_PER_STEP, GRID_W, NA_ROWS * GRID_W), _F32),
                        pltpu.VMEM((2 * C_ROWS_PER_STEP, GRID_W, 1), _F32)],
        compiler_params=_params(("arbitrary", "arbitrary", "arbitrary")),
        name="mixer_c",
    )(_one(), q, k, v, bias)


def _layer_norm(z, g, b):
    mu = jnp.mean(z, axis=-1, keepdims=True)
    zc = z - mu
    var = jnp.mean(zc * zc, axis=-1, keepdims=True)
    return zc * lax.rsqrt(var + LN_EPS) * g + b


def _split_bf16(x):
    hi = x.astype(_BF16)
    return hi, (x - hi.astype(_F32)).astype(_BF16)


def _route(hn, rw_ref, rb_ref, tri_ref, carry_ref):
    tm = hn.shape[0]
    h_hi, h_lo = _split_bf16(hn)
    w_hi, w_lo = _split_bf16(rw_ref[...])
    logits = (jnp.dot(h_hi, w_hi, preferred_element_type=_F32)
              + (jnp.dot(h_hi, w_lo, preferred_element_type=_F32)
                 + jnp.dot(h_lo, w_hi, preferred_element_type=_F32))) + rb_ref[...]
    lane = lax.broadcasted_iota(jnp.int32, (tm, LANES), 1).astype(_F32)
    ninf = -jnp.inf
    is_grp = lane < N_GROUPS
    lg = jnp.where(is_grp, logits, ninf)
    gmax = jnp.max(lg, axis=-1, keepdims=True)
    grp = jnp.min(jnp.where(lg == gmax, lane, float(LANES)), axis=-1, keepdims=True)
    p_grp = 1.0 / jnp.sum(jnp.where(is_grp, jnp.exp(lg - gmax), 0.0), axis=-1, keepdims=True)
    lo_lane = ROUTE_LANE0 + EXPERTS_PER_GROUP * grp
    in_grp = (lane >= lo_lane) & (lane < lo_lane + EXPERTS_PER_GROUP)
    le = jnp.where(in_grp, logits, ninf)
    v1 = jnp.max(le, axis=-1, keepdims=True)
    i1 = jnp.min(jnp.where(le == v1, lane, float(LANES)), axis=-1, keepdims=True)
    le2 = jnp.where(lane == i1, ninf, le)
    v2 = jnp.max(le2, axis=-1, keepdims=True)
    i2 = jnp.min(jnp.where(le2 == v2, lane, float(LANES)), axis=-1, keepdims=True)
    t = jnp.exp(v2 - v1)
    s1 = 1.0 / (1.0 + t)
    g1 = p_grp * s1
    g2 = p_grp * (t * s1)
    sel1 = lane == i1
    sel2 = lane == i2
    onehot = jnp.where(sel1 | sel2, 1.0, 0.0)
    before = jnp.dot(tri_ref[...], onehot.astype(_BF16), preferred_element_type=_F32) + carry_ref[...]
    rank1 = jnp.sum(jnp.where(sel1, before, 0.0), axis=-1, keepdims=True)
    rank2 = jnp.sum(jnp.where(sel2, before, 0.0), axis=-1, keepdims=True)
    carry_ref[...] = carry_ref[...] + jnp.sum(onehot, axis=0, keepdims=True)
    out = jnp.zeros((tm, LANES), _F32)
    for idx, val in enumerate((i1 - ROUTE_LANE0, i2 - ROUTE_LANE0, g1, g2, rank1, rank2)):
        out = jnp.where(lane == idx, val, out)
    return out


def _post_attn_kernel(*refs, n_branch):
    attn_refs = refs[:n_branch]
    pos = n_branch
    if n_branch > 1:
        lse_refs = refs[pos:pos + n_branch]
        expand_ref = refs[pos + n_branch]
        pos += n_branch + 1
    wo_ref, h_ref, g_ref, b_ref, rw_ref, rb_ref, tri_ref = refs[pos:pos + 7]
    h1_ref, route_ref, counts_ref, carry_ref = refs[pos + 7:pos + 11]

    @pl.when(pl.program_id(0) == 0)
    def _():
        carry_ref[...] = jnp.zeros_like(carry_ref)

    if n_branch == 1:
        a = attn_refs[0][...]
    else:
        lses = [r[...] for r in lse_refs]
        mx = functools.reduce(jnp.maximum, lses)
        es = [jnp.exp(l - mx) for l in lses]
        inv = 1.0 / functools.reduce(jnp.add, es)
        mixed = None
        for e, a_ref in zip(es, attn_refs):
            w_hi, w_lo = _split_bf16(e * inv)
            w_full = (jnp.dot(w_hi, expand_ref[...], preferred_element_type=_F32)
                      + jnp.dot(w_lo, expand_ref[...], preferred_element_type=_F32))
            term = w_full * a_ref[...].astype(_F32)
            mixed = term if mixed is None else mixed + term
        a = mixed.astype(_BF16)
    y = jnp.dot(a, wo_ref[...], preferred_element_type=_F32)
    hn = _layer_norm(ALPHA * h_ref[...] + y, g_ref[...], b_ref[...])
    h1_ref[...] = hn
    route_ref[...] = _route(hn, rw_ref, rb_ref, tri_ref, carry_ref)
    counts_ref[...] = carry_ref[...]


def _post_attn(attn, lses, h2, w_o, g, b, rw, rb):
    n, d = h2.shape
    tm = POST_TM
    nb = len(attn)
    tri = jnp.asarray(np.tril(np.ones((tm, tm), np.float32), -1), _BF16)
    row = lambda i: (i, 0)
    const = lambda i: (0, 0)
    args = list(attn)
    specs = [pl.BlockSpec((tm, d), row) for _ in attn]
    if nb > 1:
        expand = np.zeros((LANES, d), np.float32)
        for hd in range(N_HEADS):
            expand[hd, hd * HEAD_DIM:(hd + 1) * HEAD_DIM] = 1.0
        args += list(lses) + [jnp.asarray(expand, _BF16)]
        specs += [pl.BlockSpec((tm, LANES), row) for _ in lses] + [pl.BlockSpec((LANES, d), const)]
    args += [w_o, h2, g, b, rw, rb, tri]
    specs += [pl.BlockSpec((d, d), const), pl.BlockSpec((tm, d), row), pl.BlockSpec((1, d), const),
              pl.BlockSpec((1, d), const), pl.BlockSpec((d, LANES), const), pl.BlockSpec((1, LANES), const),
              pl.BlockSpec((tm, tm), const)]
    return pl.pallas_call(
        functools.partial(_post_attn_kernel, n_branch=nb),
        out_shape=[jax.ShapeDtypeStruct((n, d), _F32), jax.ShapeDtypeStruct((n, LANES), _F32),
                   jax.ShapeDtypeStruct((1, LANES), _F32)],
        grid=(n // tm,),
        in_specs=specs,
        out_specs=[pl.BlockSpec((tm, d), row), pl.BlockSpec((tm, LANES), row), pl.BlockSpec((1, LANES), const)],
        scratch_shapes=[pltpu.VMEM((1, LANES), _F32)],
        compiler_params=_params(("arbitrary",)),
        name="post_attn_router",
    )(*args)


def _dest_kernel(route_ref, ps_ref, o_ref):
    r = route_ref[...]
    lane = lax.broadcasted_iota(jnp.int32, r.shape, 1)
    lanef = lane.astype(_F32)
    ps = ps_ref[...]
    d = []
    for k in range(2):
        e = r[:, k:k + 1] + float(ROUTE_LANE0)
        d.append(jnp.sum(jnp.where(lanef == e, ps, 0.0), axis=-1, keepdims=True) + r[:, 4 + k:5 + k])
    o_ref[...] = jnp.where(lane == 0, d[0], jnp.where(lane == 1, d[1], 0.0)).astype(jnp.int32)


def _dest(route, pad_start_row):
    n = route.shape[0]
    tm = 1024
    return pl.pallas_call(
        _dest_kernel,
        out_shape=jax.ShapeDtypeStruct((n, LANES), jnp.int32),
        grid=(n // tm,),
        in_specs=[pl.BlockSpec((tm, LANES), lambda i: (i, 0)), pl.BlockSpec((1, LANES), lambda i: (0, 0))],
        out_specs=pl.BlockSpec((tm, LANES), lambda i: (i, 0)),
        compiler_params=_params(("arbitrary",)),
        name="moe_dest",
    )(route, pad_start_row)


def _dispatch_kernel(seg_ref, dest_ref, x_ref, xs_hbm, zero_ref, zsem, sem):
    i = pl.program_id(0)
    ntok = TOK_TM

    def zero_fill(e):
        first = pl.multiple_of(seg_ref[e] + seg_ref[N_EXPERTS + e] - MOE_BLOCK, MOE_BLOCK)
        return pltpu.make_async_copy(zero_ref, xs_hbm.at[pl.ds(first, MOE_BLOCK)], zsem)

    def has_pad(e):
        return seg_ref[N_EXPERTS + e] > seg_ref[2 * N_EXPERTS + e]

    @pl.when(i == 0)
    def _():
        zero_ref[...] = jnp.zeros_like(zero_ref)

        def start(e, c):
            @pl.when(has_pad(e))
            def _():
                zero_fill(e).start()
            return c

        def wait(e, c):
            @pl.when(has_pad(e))
            def _():
                zero_fill(e).wait()
            return c

        lax.fori_loop(0, N_EXPERTS, start, 0)
        lax.fori_loop(0, N_EXPERTS, wait, 0)

        nvalid = seg_ref[3 * N_EXPERTS]
        nblk = xs_hbm.shape[0] // MOE_BLOCK

        def tail(j):
            first = pl.multiple_of((nvalid + j) * MOE_BLOCK, MOE_BLOCK)
            return pltpu.make_async_copy(zero_ref, xs_hbm.at[pl.ds(first, MOE_BLOCK)], zsem)

        def tail_start(j, c):
            @pl.when(nvalid + j < nblk)
            def _():
                tail(j).start()
            return c

        def tail_wait(j, c):
            @pl.when(nvalid + j < nblk)
            def _():
                tail(j).wait()
            return c

        lax.fori_loop(0, N_EXPERTS, tail_start, 0)
        lax.fori_loop(0, N_EXPERTS, tail_wait, 0)

    def row_copy(j, slot):
        return pltpu.make_async_copy(x_ref.at[pl.ds(j, 1)], xs_hbm.at[pl.ds(slot, 1)], sem)

    def issue(j, c):
        for k in range(2):
            row_copy(j, dest_ref[0, 0, 2 * j + k]).start()
        return c

    def drain(j, c):
        row_copy(0, 0).wait()
        return c

    lax.fori_loop(0, ntok, issue, 0, unroll=4)
    lax.fori_loop(0, 2 * ntok, drain, 0, unroll=8)


def _dispatch(x2, dest_flat, seg, cap):
    n, d = x2.shape
    tm = TOK_TM
    nt = n // tm
    return pl.pallas_call(
        _dispatch_kernel,
        out_shape=jax.ShapeDtypeStruct((cap, d), x2.dtype),
        grid_spec=pltpu.PrefetchScalarGridSpec(
            num_scalar_prefetch=1,
            grid=(nt,),
            in_specs=[pl.BlockSpec((1, 1, 2 * tm), lambda i, seg: (i, 0, 0), memory_space=pltpu.SMEM),
                      pl.BlockSpec((tm, d), lambda i, seg: (i, 0))],
            out_specs=pl.BlockSpec(memory_space=pl.ANY),
            scratch_shapes=[pltpu.VMEM((MOE_BLOCK, d), x2.dtype), pltpu.SemaphoreType.DMA,
                            pltpu.SemaphoreType.DMA]),
        compiler_params=_params(("arbitrary",)),
        name="moe_dispatch",
    )(seg, dest_flat.reshape(nt, 1, 2 * tm), x2)


def _expert_kernel(blk_ref, nvalid_ref, xs_ref, wg_ref, wu_ref, wd_ref, ys_ref, wgb, wub, wdb):
    i = pl.program_id(0)
    prev = blk_ref[jnp.maximum(i - 1, 0)]
    valid = i < nvalid_ref[0]

    @pl.when(valid & ((i == 0) | (blk_ref[i] != prev)))
    def _():
        wgb[...] = wg_ref[0].astype(_BF16)
        wub[...] = wu_ref[0].astype(_BF16)
        wdb[...] = wd_ref[0].astype(_BF16)

    @pl.when(valid)
    def _():
        xb = xs_ref[...].astype(_BF16)
        gate = jnp.dot(xb, wgb[...], preferred_element_type=_F32)
        up = jnp.dot(xb, wub[...], preferred_element_type=_F32)
        mid = (gate * jax.nn.sigmoid(gate)) * up
        ys_ref[...] = jnp.dot(mid.astype(_BF16), wdb[...], preferred_element_type=_F32)

    @pl.when(jnp.logical_not(valid))
    def _():
        ys_ref[...] = jnp.zeros_like(ys_ref)


def _experts(xs, blk_exp, nvalid, w_gate, w_up, w_down):
    cap, d = xs.shape
    de = w_gate.shape[2]
    nblk = cap // MOE_BLOCK

    def xmap(i, blk, nv):
        return (jnp.minimum(i, nv[0] - 1), 0)

    def wmap(i, blk, nv):
        return (blk[i], 0, 0)

    return pl.pallas_call(
        _expert_kernel,
        out_shape=jax.ShapeDtypeStruct((cap, d), _F32),
        grid_spec=pltpu.PrefetchScalarGridSpec(
            num_scalar_prefetch=2,
            grid=(nblk,),
            in_specs=[pl.BlockSpec((MOE_BLOCK, d), xmap),
                      pl.BlockSpec((1, d, de), wmap), pl.BlockSpec((1, d, de), wmap),
                      pl.BlockSpec((1, de, d), wmap)],
            out_specs=pl.BlockSpec((MOE_BLOCK, d), lambda i, blk, nv: (i, 0)),
            scratch_shapes=[pltpu.VMEM((d, de), _BF16), pltpu.VMEM((d, de), _BF16), pltpu.VMEM((de, d), _BF16)]),
        compiler_params=_params(("arbitrary",)),
        name="moe_experts",
    )(blk_exp, nvalid, xs, w_gate, w_up, w_down)


def _combine_kernel(dcur_ref, dnxt_ref, ys_hbm, h_ref, route_ref, g_ref, b_ref, o_ref, buf, sem):
    i = pl.program_id(0)
    n = pl.num_programs(0)
    tm = TOK_TM

    def gather(dref, slot):
        def body(j, c):
            for k in range(2):
                pltpu.make_async_copy(ys_hbm.at[pl.ds(dref[0, 0, 2 * j + k], 1)],
                                      buf.at[slot, k, pl.ds(j, 1)], sem.at[slot]).start()
            return c
        lax.fori_loop(0, tm, body, 0, unroll=4)

    @pl.when(i == 0)
    def _():
        gather(dcur_ref, 0)

    @pl.when(i + 1 < n)
    def _():
        gather(dnxt_ref, (i + 1) % 2)

    slot = i % 2

    def drain(j, c):
        pltpu.make_async_copy(ys_hbm.at[pl.ds(0, 1)], buf.at[slot, 0, pl.ds(0, 1)], sem.at[slot]).wait()
        return c

    lax.fori_loop(0, 2 * tm, drain, 0, unroll=8)
    r = route_ref[...]
    y = r[:, 2:3] * buf[slot, 0] + r[:, 3:4] * buf[slot, 1]
    o_ref[...] = _layer_norm(ALPHA * h_ref[...] + y, g_ref[...], b_ref[...])


def _combine(ys, dest_flat, h1, route, g, b):
    n, d = h1.shape
    tm = TOK_TM
    nt = n // tm
    dest3 = dest_flat.reshape(nt, 1, 2 * tm)
    row = lambda i: (i, 0)
    const = lambda i: (0, 0)
    return pl.pallas_call(
        _combine_kernel,
        out_shape=jax.ShapeDtypeStruct((n, d), _F32),
        grid=(nt,),
        in_specs=[pl.BlockSpec((1, 1, 2 * tm), lambda i: (i, 0, 0), memory_space=pltpu.SMEM),
                  pl.BlockSpec((1, 1, 2 * tm), lambda i: (jnp.minimum(i + 1, nt - 1), 0, 0),
                               memory_space=pltpu.SMEM),
                  pl.BlockSpec(memory_space=pl.ANY),
                  pl.BlockSpec((tm, d), row), pl.BlockSpec((tm, LANES), row),
                  pl.BlockSpec((1, d), const), pl.BlockSpec((1, d), const)],
        out_specs=pl.BlockSpec((tm, d), row),
        scratch_shapes=[pltpu.VMEM((2, 2, tm, d), _F32), pltpu.SemaphoreType.DMA((2,))],
        compiler_params=_params(("arbitrary",)),
        name="moe_combine",
    )(dest3, dest3, ys, h1, route, g, b)


def _moe(h1, route, counts_row, w_gate, w_up, w_down, g, b):
    n, d = h1.shape
    cap = 2 * n + N_EXPERTS * MOE_BLOCK
    counts = counts_row[0, ROUTE_LANE0:ROUTE_LANE0 + N_EXPERTS].astype(jnp.int32)
    padded = (counts + MOE_BLOCK - 1) // MOE_BLOCK * MOE_BLOCK
    ends = jnp.cumsum(padded)
    pad_start = ends - padded
    nblk = cap // MOE_BLOCK
    blk_first = jnp.arange(nblk, dtype=jnp.int32) * MOE_BLOCK
    blk_exp = jnp.minimum(jnp.sum(blk_first[:, None] >= ends[None, :], axis=1), N_EXPERTS - 1).astype(jnp.int32)
    nvalid = (ends[-1:] // MOE_BLOCK).astype(jnp.int32)
    ps_row = jnp.zeros((1, LANES), _F32).at[0, ROUTE_LANE0:ROUTE_LANE0 + N_EXPERTS].set(pad_start.astype(_F32))
    dest_flat = _dest(route, ps_row)[:, :2].reshape(2 * n)
    seg = jnp.concatenate([pad_start, padded, counts, nvalid]).astype(jnp.int32)
    xs = _dispatch(h1, dest_flat, seg, cap)
    ys = _experts(xs, blk_exp, nvalid, w_gate, w_up, w_down)
    return _combine(ys, dest_flat, h1, route, g, b)


def _router_params(rg_w, rg_b, re_w, re_b):
    d = rg_w.shape[0]
    pad = LANES - N_GROUPS - N_EXPERTS
    rw = jnp.concatenate([rg_w, re_w, jnp.zeros((d, pad), _F32)], axis=1)
    rb = jnp.concatenate([rg_b, re_b.reshape(-1), jnp.zeros((pad,), _F32)]).reshape(1, LANES)
    return rw, rb


def kernel(x, t5_table, l0_w_qkv, l0_sink, l0_w_o, l0_ln1_g, l0_ln1_b, l0_rg_w, l0_rg_b, l0_re_w, l0_re_b, l0_w_gate, l0_w_up, l0_w_down, l0_ln2_g, l0_ln2_b, l1_w_qkv, l1_w_o, l1_ln1_g, l1_ln1_b, l1_rg_w, l1_rg_b, l1_re_w, l1_re_b, l1_w_gate, l1_w_up, l1_w_down, l1_ln2_g, l1_ln2_b, l2_w_qkv, l2_rpb, l2_w_o, l2_ln1_g, l2_ln1_b, l2_rg_w, l2_rg_b, l2_re_w, l2_re_b, l2_w_gate, l2_w_up, l2_w_down, l2_ln2_g, l2_ln2_b, l3_w_qkv, l3_sink, l3_w_o, l3_ln1_g, l3_ln1_b, l3_rg_w, l3_rg_b, l3_re_w, l3_re_b, l3_w_gate, l3_w_up, l3_w_down, l3_ln2_g, l3_ln2_b):
    mixers = [(l0_w_qkv, l0_sink, l0_w_o), (l1_w_qkv, None, l1_w_o), (l2_w_qkv, l2_rpb, l2_w_o),
              (l3_w_qkv, l3_sink, l3_w_o)]
    norm1 = [(l0_ln1_g, l0_ln1_b), (l1_ln1_g, l1_ln1_b), (l2_ln1_g, l2_ln1_b), (l3_ln1_g, l3_ln1_b)]
    routers = [(l0_rg_w, l0_rg_b, l0_re_w, l0_re_b), (l1_rg_w, l1_rg_b, l1_re_w, l1_re_b),
               (l2_rg_w, l2_rg_b, l2_re_w, l2_re_b), (l3_rg_w, l3_rg_b, l3_re_w, l3_re_b)]
    experts = [(l0_w_gate, l0_w_up, l0_w_down), (l1_w_gate, l1_w_up, l1_w_down),
               (l2_w_gate, l2_w_up, l2_w_down), (l3_w_gate, l3_w_up, l3_w_down)]
    norm2 = [(l0_ln2_g, l0_ln2_b), (l1_ln2_g, l1_ln2_b), (l2_ln2_g, l2_ln2_b), (l3_ln2_g, l3_ln2_b)]

    bsz, seq, d = x.shape
    n = bsz * seq
    inner = N_HEADS * HEAD_DIM
    scale = HEAD_DIM ** -0.5
    h = x.reshape(n, d)
    bias_a = _band_bias(t5_table, A_BLOCK, A_WINDOW, 1)
    for i in range(DEPTH):
        kind = i % N_MIXERS
        w_qkv, extra, w_o = mixers[i]
        kvw = A_KV_HEADS * HEAD_DIM if kind == 0 else inner
        q, k, v = _project(h, w_qkv.astype(_BF16), (inner, kvw, kvw), (scale, 1.0, 1.0))
        q, k, v = (t.reshape(bsz, seq, t.shape[1]) for t in (q, k, v))
        lses = ()
        if kind == 0:
            attn = [_mixer_a(q, k, v, extra.astype(_F32), bias_a, bsz, seq)]
        elif kind == 1:
            attn, lses = [], []
            for window, dil in B_BRANCHES:
                half = window // 2 // dil
                o, lse = _mixer_b_branch(q, k, v, _band_bias(t5_table, half, half, dil), bsz, seq, dil, half)
                attn.append(o)
                lses.append(lse.reshape(n, LANES))
        else:
            attn = [_mixer_c(q, k, v, _na_bias(extra), bsz, seq)]
        attn = [a.reshape(n, inner) for a in attn]
        g1, b1 = norm1[i]
        rw, rb = _router_params(*routers[i])
        h1, route, counts = _post_attn(attn, lses, h, w_o.astype(_BF16), g1.reshape(1, d), b1.reshape(1, d), rw, rb)
        g2, b2 = norm2[i]
        h = _moe(h1, route, counts, *experts[i], g2.reshape(1, d), b2.reshape(1, d))
    return h.reshape(bsz, seq, d)
```

```python
import functools
import math

import jax
import jax.numpy as jnp
import numpy as np
from jax import lax
from jax.experimental import pallas as pl
from jax.experimental.pallas import tpu as pltpu

HEAD_DIM = 64
N_HEADS = 16
A_KV_HEADS = 4
A_WINDOW = 128
A_BLOCK = 128
B_BRANCHES = ((128, 1), (512, 4), (2048, 16))
GRID_W = 64
NA_ROWS = 8
NA_COLS = 16
T5_BUCKETS = 32
T5_MAX_DIST = 1024
N_GROUPS = 4
EXPERTS_PER_GROUP = 8
N_EXPERTS = N_GROUPS * EXPERTS_PER_GROUP
DEPTH = 4
N_MIXERS = 3
ALPHA = (2 * DEPTH) ** 0.25
LN_EPS = 1e-5
NEG_INF = -1e30

LANES = 128
PAIR = 2 * HEAD_DIM
VMEM_LIMIT = 56 * 1024 * 1024

PROJ_TM = 512
PROJ_CHUNK = 512
POST_TM = 256
TOK_TM = 256
MOE_BLOCK = 512
A_TQ = 512
A_BAND = 3
A_SUB_PER_PASS = 2
B_BLOCK = 128
UNITS_PER_PASS = 8
C_UNIT_ROWS = 2
C_BAND_ROWS = 10
C_ROWS_PER_STEP = 16
ROUTE_LANE0 = N_GROUPS

_F32 = jnp.float32
_BF16 = jnp.bfloat16


def _params(sem, vmem=VMEM_LIMIT, flags=None):
    return pltpu.CompilerParams(dimension_semantics=sem, vmem_limit_bytes=vmem, flags=flags)


def _proj_kernel(x_ref, w_ref, *o_refs, widths, scales):
    xb = x_ref[...].astype(_BF16)
    col = 0
    for o_ref, width, scale in zip(o_refs, widths, scales):
        for c in range(0, width, PROJ_CHUNK):
            cw = min(PROJ_CHUNK, width - c)
            acc = jnp.dot(xb, w_ref[:, col + c:col + c + cw], preferred_element_type=_F32)
            if scale != 1.0:
                acc = acc * scale
            o_ref[:, c:c + cw] = acc.astype(o_ref.dtype)
        col += width


def _project(x2, w_bf16, widths, scales):
    n, d = x2.shape
    ctot = sum(widths)
    return pl.pallas_call(
        functools.partial(_proj_kernel, widths=tuple(widths), scales=tuple(scales)),
        out_shape=[jax.ShapeDtypeStruct((n, w), _BF16) for w in widths],
        grid=(n // PROJ_TM,),
        in_specs=[pl.BlockSpec((PROJ_TM, d), lambda i: (i, 0)),
                  pl.BlockSpec((d, ctot), lambda i: (0, 0))],
        out_specs=[pl.BlockSpec((PROJ_TM, w), lambda i: (i, 0)) for w in widths],
        compiler_params=_params(("arbitrary",)),
        name="qkv_proj",
    )(x2, w_bf16)


def _one():
    return jnp.ones((1,), jnp.int32)


def _lane_lo(rows=1):
    return lax.broadcasted_iota(jnp.int32, (rows, PAIR), 1) < HEAD_DIM


def _swap_halves(x):
    return pltpu.roll(x.astype(_F32), HEAD_DIM, axis=1).astype(_BF16)


class _PairUnit:
    def __init__(self, q2, k, vt, bias, emit, sink_a=None, sink_b=None):
        self.q2, self.k, self.vt, self.bias, self.emit = q2, k, vt, bias, emit
        self.sink_a, self.sink_b = sink_a, sink_b


def _attend_units(units, s_scr, m_scr, one):
    lo = _lane_lo()
    nt = (((1,), (1,)), ((), ()))

    def sink_row(unit, m2):
        lane = lax.broadcasted_iota(jnp.int32, (1, m2), 1)
        return jnp.where(lane < m2 // 2, unit.sink_a, unit.sink_b)

    def scores(_, carry):
        for u, unit in enumerate(units):
            q2 = unit.q2()
            zero = jnp.zeros_like(q2)
            qq = jnp.concatenate([jnp.where(lo, q2, zero), jnp.where(lo, zero, q2)], axis=0)
            s = lax.dot_general(unit.k(), qq, nt, preferred_element_type=_F32) + unit.bias()
            m = jnp.max(s, axis=0, keepdims=True)
            if unit.sink_a is not None:
                m = jnp.maximum(m, sink_row(unit, s.shape[1]))
            s_scr[u] = s
            m_scr[u] = m
        return carry

    def values(_, carry):
        for u, unit in enumerate(units):
            m = m_scr[u]
            e = jnp.exp(s_scr[u] - m)
            den = jnp.sum(e, axis=0, keepdims=True)
            if unit.sink_a is not None:
                den = den + jnp.exp(sink_row(unit, e.shape[1]) - m)
            ot = jnp.dot(unit.vt(), e.astype(_BF16), preferred_element_type=_F32) * (1.0 / den)
            mq = ot.shape[1] // 2
            top = lax.broadcasted_iota(jnp.int32, (PAIR, 1), 0) < HEAD_DIM
            pair_t = jnp.where(top, ot[:, :mq], ot[:, mq:])
            unit.emit(pair_t.T, m + jnp.log(den))
        return carry

    lax.fori_loop(0, one, scores, 0)
    lax.fori_loop(0, one, values, 0)


def _transpose_blocks(v_ref_rows, nblk, store):
    def body(j, carry):
        rows = pl.ds(pl.multiple_of(j * LANES, LANES), LANES)
        store(j, v_ref_rows(rows).astype(_F32).T.astype(_BF16))
        return carry
    lax.fori_loop(0, nblk, body, 0)


def _t5_bucket(rel):
    half = T5_BUCKETS // 2
    exact = half // 2
    n = np.abs(rel)
    big = exact + (np.log(np.maximum(n, 1) / exact) / math.log(T5_MAX_DIST / exact) * (half - exact)).astype(np.int32)
    return ((rel > 0) * half + np.where(n < exact, n, np.minimum(big, half - 1))).astype(np.int32)


def _band_bias_t(t5_table, blk, window, dil, nband):
    tk = nband * blk
    off = tk - 1
    rel = np.arange(-off, tk)
    vec = jnp.take(t5_table, jnp.asarray(_t5_bucket(rel * dil)), axis=0).astype(_F32)
    vec = jnp.where(jnp.asarray(np.abs(rel) <= window)[:, None], vec, NEG_INF).T
    h = vec.shape[0]
    vec = jnp.pad(vec, ((0, 0), (0, 1)))
    w = 2 * tk - 1
    skew = jnp.tile(vec, (1, blk))[:, :blk * w].reshape(h, blk, w)
    t = jnp.stack([skew[:, :, off - v * blk:off - v * blk + tk] for v in range(nband)], axis=1)
    t = t.reshape(h // 2, 2, nband, blk, tk).transpose(0, 2, 4, 1, 3)
    return t.reshape(h // 2, nband, tk, 2 * blk)


def _band_window(gb, blk, length, nband):
    first = jnp.clip(gb - 1, 0, length // blk - nband)
    return first, gb - first


def _attn_a_kernel(sink_ref, one_ref, q_ref, k_ref, v_ref, bias_ref, o_ref, kd_ref, vdt_ref, s_scr, m_scr, *, seq):
    kvb = pl.program_id(0)
    qt = pl.program_id(2)
    one = one_ref[0]
    nblk = seq // A_BLOCK

    @pl.when(qt == 0)
    def _():
        lo = _lane_lo()

        def fill(j, carry):
            rows = pl.ds(pl.multiple_of(j * A_BLOCK, A_BLOCK), A_BLOCK)
            kk = k_ref[0, rows, :]
            ks = _swap_halves(kk)
            kd_ref[0, rows, :] = jnp.where(lo, kk, ks)
            kd_ref[1, rows, :] = jnp.where(lo, ks, kk)
            vt = v_ref[0, rows, :].astype(_F32).T.astype(_BF16)
            vdt_ref[0, j] = jnp.concatenate([vt[:HEAD_DIM], vt[:HEAD_DIM]], axis=0)
            vdt_ref[1, j] = jnp.concatenate([vt[HEAD_DIM:], vt[HEAD_DIM:]], axis=0)
            return carry

        lax.fori_loop(0, nblk, fill, 0)

    nsub = A_TQ // A_BLOCK
    npair = q_ref.shape[2] // PAIR
    hbase = kvb * (2 * npair)

    def body(g, carry):
        units = []
        for j in range(A_SUB_PER_PASS):
            si = g * A_SUB_PER_PASS + j
            first, var = _band_window(qt * nsub + si, A_BLOCK, seq, A_BAND)
            rows = pl.ds(pl.multiple_of(si * A_BLOCK, A_BLOCK), A_BLOCK)
            band = pl.ds(pl.multiple_of(first * A_BLOCK, A_BLOCK), A_BAND * A_BLOCK)
            for jj in range(npair):
                hk = jj // (npair // 2)
                cols = slice(jj * PAIR, (jj + 1) * PAIR)

                def emit(o, lse, rows=rows, cols=cols):
                    o_ref[0, rows, cols] = o.astype(o_ref.dtype)

                units.append(_PairUnit(
                    q2=lambda rows=rows, cols=cols: q_ref[0, rows, cols],
                    k=lambda hk=hk, band=band: kd_ref[hk, band, :],
                    vt=lambda hk=hk, first=first: jnp.concatenate(
                        [vdt_ref[hk, first + t] for t in range(A_BAND)], axis=1),
                    bias=lambda jj=jj, var=var: bias_ref[jj, var],
                    emit=emit, sink_a=sink_ref[hbase + 2 * jj], sink_b=sink_ref[hbase + 2 * jj + 1]))
        _attend_units(units, s_scr, m_scr, one)
        return carry

    lax.fori_loop(0, nsub // A_SUB_PER_PASS, body, 0)


def _mixer_a(q, k, v, sink, bias, bsz, seq):
    nkvb = k.shape[2] // PAIR
    qw = q.shape[2] // nkvb
    npair = qw // PAIR
    nunits = A_SUB_PER_PASS * npair
    tk = A_BAND * A_BLOCK
    return pl.pallas_call(
        functools.partial(_attn_a_kernel, seq=seq),
        out_shape=jax.ShapeDtypeStruct(q.shape, _BF16),
        grid=(nkvb, bsz, seq // A_TQ),
        in_specs=[pl.BlockSpec(memory_space=pltpu.SMEM), pl.BlockSpec(memory_space=pltpu.SMEM),
                  pl.BlockSpec((1, A_TQ, qw), lambda c, b, t: (b, t, c)),
                  pl.BlockSpec((1, seq, PAIR), lambda c, b, t: (b, 0, c)),
                  pl.BlockSpec((1, seq, PAIR), lambda c, b, t: (b, 0, c)),
                  pl.BlockSpec((npair, A_BAND, tk, 2 * A_BLOCK), lambda c, b, t: (c, 0, 0, 0))],
        out_specs=pl.BlockSpec((1, A_TQ, qw), lambda c, b, t: (b, t, c)),
        scratch_shapes=[pltpu.VMEM((2, seq, PAIR), _BF16), pltpu.VMEM((2, seq // A_BLOCK, PAIR, A_BLOCK), _BF16),
                        pltpu.VMEM((nunits, tk, 2 * A_BLOCK), _F32),
                        pltpu.VMEM((nunits, 1, 2 * A_BLOCK), _F32)],
        compiler_params=_params(("arbitrary", "arbitrary", "arbitrary")),
        name="mixer_a",
    )(sink, _one(), q, k, v, bias)


def _attn_band_kernel(one_ref, q_ref, k_ref, v_ref, bias_ref, o_ref, lse_ref, vt_ref, s_scr, m_scr,
                      *, length, npairs, nband):
    col = pl.program_id(2)
    one = one_ref[0]
    nblk = length // B_BLOCK

    for p in range(npairs):
        def store(j, blk, p=p):
            vt_ref[p, j] = blk
        _transpose_blocks(lambda rows, p=p: v_ref[0, rows, p * PAIR:(p + 1) * PAIR], nblk, store)

    qb_per_pass = UNITS_PER_PASS // npairs

    def body(g, carry):
        units = []
        for j in range(qb_per_pass):
            gb = g * qb_per_pass + j
            first, var = _band_window(gb, B_BLOCK, length, nband)
            rows = pl.ds(pl.multiple_of(gb * B_BLOCK, B_BLOCK), B_BLOCK)
            band = pl.ds(pl.multiple_of(first * B_BLOCK, B_BLOCK), nband * B_BLOCK)
            for p in range(npairs):
                cols = slice(p * PAIR, (p + 1) * PAIR)

                def emit(o, lse, rows=rows, cols=cols, p=p, gb=gb):
                    o_ref[0, rows, cols] = o.astype(o_ref.dtype)
                    ha = col * (2 * npairs) + 2 * p
                    lse_ref[0, 0, gb, pl.ds(ha, 1), :] = lse[:, :B_BLOCK]
                    lse_ref[0, 0, gb, pl.ds(ha + 1, 1), :] = lse[:, B_BLOCK:]

                units.append(_PairUnit(
                    q2=lambda rows=rows, cols=cols: q_ref[0, rows, cols],
                    k=lambda band=band, cols=cols: k_ref[0, band, cols],
                    vt=lambda p=p, first=first: jnp.concatenate(
                        [vt_ref[p, first + t] for t in range(nband)], axis=1),
                    bias=lambda p=p, var=var: bias_ref[p, var], emit=emit))
        _attend_units(units, s_scr, m_scr, one)
        return carry

    lax.fori_loop(0, nblk // qb_per_pass, body, 0)


def _mixer_b_branch(q, k, v, t5_table, bsz, seq, window, dil):
    length = seq // dil
    width = q.shape[2]
    nblk = length // B_BLOCK
    nband = min(3, nblk)
    tk = nband * B_BLOCK
    npairs = (width // PAIR) if length * width * 2 <= 2 * 1024 * 1024 else 1
    ncol = width // (npairs * PAIR)
    bw = npairs * PAIR
    bias = _band_bias_t(t5_table, B_BLOCK, window // 2 // dil, dil, nband)

    def view(t):
        return t.reshape(bsz, length, dil * width)

    def cmap(b, r, c):
        return (b, 0, r * ncol + c)

    o, lse = pl.pallas_call(
        functools.partial(_attn_band_kernel, length=length, npairs=npairs, nband=nband),
        out_shape=[jax.ShapeDtypeStruct((bsz, length, dil * width), _BF16),
                   jax.ShapeDtypeStruct((bsz, dil, nblk, N_HEADS, B_BLOCK), _F32)],
        grid=(bsz, dil, ncol),
        in_specs=[pl.BlockSpec(memory_space=pltpu.SMEM),
                  pl.BlockSpec((1, length, bw), cmap),
                  pl.BlockSpec((1, length, bw), cmap),
                  pl.BlockSpec((1, length, bw), cmap),
                  pl.BlockSpec((npairs, nband, tk, 2 * B_BLOCK), lambda b, r, c: (c, 0, 0, 0))],
        out_specs=[pl.BlockSpec((1, length, bw), cmap),
                   pl.BlockSpec((1, 1, nblk, N_HEADS, B_BLOCK), lambda b, r, c: (b, r, 0, 0, 0))],
        scratch_shapes=[pltpu.VMEM((npairs, nblk, PAIR, B_BLOCK), _BF16),
                        pltpu.VMEM((UNITS_PER_PASS, tk, 2 * B_BLOCK), _F32),
                        pltpu.VMEM((UNITS_PER_PASS, 1, 2 * B_BLOCK), _F32)],
        compiler_params=_params(("arbitrary", "arbitrary", "arbitrary")),
        name=f"mixer_b_dil{dil}",
    )(_one(), view(q), view(k), view(v), bias)
    lse = lse.transpose(0, 2, 4, 1, 3).reshape(bsz, seq, N_HEADS)
    return o.reshape(bsz, seq, width), lse


def _na_bias_t(rpb, rows):
    canon = (0, 2, 4, rows - 4, rows - 2)
    h = rpb.shape[0]
    kc = np.arange(GRID_W)[:, None]
    qc = np.arange(GRID_W)[None, :]
    ws = np.clip(qc - NA_COLS // 2, 0, GRID_W - NA_COLS)
    col_ok = (kc >= ws) & (kc < ws + NA_COLS)
    col_idx = np.clip(kc - qc, 1 - NA_COLS, NA_COLS - 1) + NA_COLS - 1
    colpart = jnp.where(jnp.asarray(col_ok), rpb.astype(_F32)[:, :, jnp.asarray(col_idx)], NEG_INF)
    masked = jnp.full((h, GRID_W, GRID_W), NEG_INF, _F32)
    variants = []
    for i in canon:
        start = int(np.clip(i - NA_ROWS // 2, 0, rows - C_BAND_ROWS))
        per_key_row = []
        for kr in range(C_BAND_ROWS):
            per_query_row = []
            for qr in range(C_UNIT_ROWS):
                iq, kabs = i + qr, start + kr
                first_key_row = int(np.clip(iq - NA_ROWS // 2, 0, rows - NA_ROWS))
                inside = first_key_row <= kabs < first_key_row + NA_ROWS
                per_query_row.append(colpart[:, kabs - iq + NA_ROWS - 1] if inside else masked)
            per_key_row.append(jnp.stack(per_query_row, axis=2))
        variants.append(jnp.stack(per_key_row, axis=1))
    b = jnp.stack(variants, axis=1)
    nq, nk = C_UNIT_ROWS * GRID_W, C_BAND_ROWS * GRID_W
    b = b.reshape(h // 2, 2, len(canon), nk, nq)
    return b.transpose(0, 2, 3, 1, 4).reshape(h // 2, len(canon), nk, 2 * nq)


def _attn_c_kernel(one_ref, q_ref, k_ref, v_ref, bias_ref, o_ref, vt_ref, s_scr, m_scr, *, rows):
    rt = pl.program_id(2)
    nq = C_UNIT_ROWS * GRID_W
    band_blocks = C_BAND_ROWS * GRID_W // LANES

    @pl.when(rt == 0)
    def _():
        def store(j, blk):
            vt_ref[j] = blk
        _transpose_blocks(lambda r: v_ref[0, r, :], rows * GRID_W // LANES, store)

    units = []
    for u in range(C_ROWS_PER_STEP // C_UNIT_ROWS):
        i = rt * C_ROWS_PER_STEP + C_UNIT_ROWS * u
        start = jnp.clip(i - NA_ROWS // 2, 0, rows - C_BAND_ROWS)
        var = lax.shift_right_logical(i - start, 1)
        first = lax.shift_right_logical(start, 1)
        keys = pl.ds(pl.multiple_of(first * LANES, LANES), C_BAND_ROWS * GRID_W)
        qrows = pl.ds(u * nq, nq)

        def emit(o, lse, qrows=qrows):
            o_ref[0, qrows, :] = o.astype(o_ref.dtype)

        units.append(_PairUnit(
            q2=lambda qrows=qrows: q_ref[0, qrows, :], k=lambda keys=keys: k_ref[0, keys, :],
            vt=lambda first=first: jnp.concatenate([vt_ref[first + t] for t in range(band_blocks)], axis=1),
            bias=lambda var=var: bias_ref[0, var], emit=emit))
    _attend_units(units, s_scr, m_scr, one_ref[0])


def _mixer_c(q, k, v, bias, bsz, seq):
    rows = seq // GRID_W
    tq = C_ROWS_PER_STEP * GRID_W
    npair = q.shape[2] // PAIR
    nunits = C_ROWS_PER_STEP // C_UNIT_ROWS
    nq, nk = C_UNIT_ROWS * GRID_W, C_BAND_ROWS * GRID_W
    return pl.pallas_call(
        functools.partial(_attn_c_kernel, rows=rows),
        out_shape=jax.ShapeDtypeStruct(q.shape, _BF16),
        grid=(npair, bsz, seq // tq),
        in_specs=[pl.BlockSpec(memory_space=pltpu.SMEM),
                  pl.BlockSpec((1, tq, PAIR), lambda c, b, t: (b, t, c)),
                  pl.BlockSpec((1, seq, PAIR), lambda c, b, t: (b, 0, c)),
                  pl.BlockSpec((1, seq, PAIR), lambda c, b, t: (b, 0, c)),
                  pl.BlockSpec((1, bias.shape[1], nk, 2 * nq), lambda c, b, t: (c, 0, 0, 0))],
        out_specs=pl.BlockSpec((1, tq, PAIR), lambda c, b, t: (b, t, c)),
        scratch_shapes=[pltpu.VMEM((seq // LANES, PAIR, LANES), _BF16),
                        pltpu.VMEM((nunits, nk, 2 * nq), _F32),
                        pltpu.VMEM((nunits, 1, 2 * nq), _F32)],
        compiler_params=_params(("arbitrary", "arbitrary", "arbitrary")),
        name="mixer_c",
    )(_one(), q, k, v, bias)


def _layer_norm(z, g, b):
    mu = jnp.mean(z, axis=-1, keepdims=True)
    zc = z - mu
    var = jnp.mean(zc * zc, axis=-1, keepdims=True)
    return zc * lax.rsqrt(var + LN_EPS) * g + b


def _split_bf16(x):
    hi = x.astype(_BF16)
    return hi, (x - hi.astype(_F32)).astype(_BF16)


def _route(hn, rw_ref, rb_ref, tri_ref, carry_ref):
    tm = hn.shape[0]
    h_hi, h_lo = _split_bf16(hn)
    w_hi, w_lo = _split_bf16(rw_ref[...])
    logits = (jnp.dot(h_hi, w_hi, preferred_element_type=_F32)
              + (jnp.dot(h_hi, w_lo, preferred_element_type=_F32)
                 + jnp.dot(h_lo, w_hi, preferred_element_type=_F32))) + rb_ref[...]
    lane = lax.broadcasted_iota(jnp.int32, (tm, LANES), 1).astype(_F32)
    ninf = -jnp.inf
    is_grp = lane < N_GROUPS
    lg = jnp.where(is_grp, logits, ninf)
    gmax = jnp.max(lg, axis=-1, keepdims=True)
    grp = jnp.min(jnp.where(lg == gmax, lane, float(LANES)), axis=-1, keepdims=True)
    p_grp = 1.0 / jnp.sum(jnp.where(is_grp, jnp.exp(lg - gmax), 0.0), axis=-1, keepdims=True)
    lo_lane = ROUTE_LANE0 + EXPERTS_PER_GROUP * grp
    in_grp = (lane >= lo_lane) & (lane < lo_lane + EXPERTS_PER_GROUP)
    le = jnp.where(in_grp, logits, ninf)
    v1 = jnp.max(le, axis=-1, keepdims=True)
    i1 = jnp.min(jnp.where(le == v1, lane, float(LANES)), axis=-1, keepdims=True)
    le2 = jnp.where(lane == i1, ninf, le)
    v2 = jnp.max(le2, axis=-1, keepdims=True)
    i2 = jnp.min(jnp.where(le2 == v2, lane, float(LANES)), axis=-1, keepdims=True)
    t = jnp.exp(v2 - v1)
    s1 = 1.0 / (1.0 + t)
    g1 = p_grp * s1
    g2 = p_grp * (t * s1)
    sel1 = lane == i1
    sel2 = lane == i2
    onehot = jnp.where(sel1 | sel2, 1.0, 0.0)
    before = jnp.dot(tri_ref[...], onehot.astype(_BF16), preferred_element_type=_F32) + carry_ref[...]
    rank1 = jnp.sum(jnp.where(sel1, before, 0.0), axis=-1, keepdims=True)
    rank2 = jnp.sum(jnp.where(sel2, before, 0.0), axis=-1, keepdims=True)
    carry_ref[...] = carry_ref[...] + jnp.sum(onehot, axis=0, keepdims=True)
    out = jnp.zeros((tm, LANES), _F32)
    for idx, val in enumerate((i1 - ROUTE_LANE0, i2 - ROUTE_LANE0, g1, g2, rank1, rank2)):
        out = jnp.where(lane == idx, val, out)
    return out


def _post_attn_kernel(*refs, n_branch):
    attn_refs = refs[:n_branch]
    pos = n_branch
    if n_branch > 1:
        lse_refs = refs[pos:pos + n_branch]
        expand_ref = refs[pos + n_branch]
        pos += n_branch + 1
    wo_ref, h_ref, g_ref, b_ref, rw_ref, rb_ref, tri_ref = refs[pos:pos + 7]
    h1_ref, route_ref, counts_ref, carry_ref = refs[pos + 7:pos + 11]

    @pl.when(pl.program_id(0) == 0)
    def _():
        carry_ref[...] = jnp.zeros_like(carry_ref)

    if n_branch == 1:
        a = attn_refs[0][...]
    else:
        lses = [r[...] for r in lse_refs]
        mx = functools.reduce(jnp.maximum, lses)
        es = [jnp.exp(l - mx) for l in lses]
        inv = 1.0 / functools.reduce(jnp.add, es)
        mixed = None
        for e, a_ref in zip(es, attn_refs):
            w_hi, w_lo = _split_bf16(e * inv)
            w_full = (jnp.dot(w_hi, expand_ref[...], preferred_element_type=_F32)
                      + jnp.dot(w_lo, expand_ref[...], preferred_element_type=_F32))
            term = w_full * a_ref[...].astype(_F32)
            mixed = term if mixed is None else mixed + term
        a = mixed.astype(_BF16)
    y = jnp.dot(a, wo_ref[...], preferred_element_type=_F32)
    hn = _layer_norm(ALPHA * h_ref[...] + y, g_ref[...], b_ref[...])
    h1_ref[...] = hn
    route_ref[...] = _route(hn, rw_ref, rb_ref, tri_ref, carry_ref)
    counts_ref[...] = carry_ref[...]


def _post_attn(attn, lses, h2, w_o, g, b, rw, rb):
    n, d = h2.shape
    tm = POST_TM
    nb = len(attn)
    tri = jnp.asarray(np.tril(np.ones((tm, tm), np.float32), -1), _BF16)
    row = lambda i: (i, 0)
    const = lambda i: (0, 0)
    args = list(attn)
    specs = [pl.BlockSpec((tm, d), row) for _ in attn]
    if nb > 1:
        expand = np.zeros((LANES, d), np.float32)
        for hd in range(N_HEADS):
            expand[hd, hd * HEAD_DIM:(hd + 1) * HEAD_DIM] = 1.0
        args += list(lses) + [jnp.asarray(expand, _BF16)]
        specs += [pl.BlockSpec((tm, LANES), row) for _ in lses] + [pl.BlockSpec((LANES, d), const)]
    args += [w_o, h2, g, b, rw, rb, tri]
    specs += [pl.BlockSpec((d, d), const), pl.BlockSpec((tm, d), row), pl.BlockSpec((1, d), const),
              pl.BlockSpec((1, d), const), pl.BlockSpec((d, LANES), const), pl.BlockSpec((1, LANES), const),
              pl.BlockSpec((tm, tm), const)]
    return pl.pallas_call(
        functools.partial(_post_attn_kernel, n_branch=nb),
        out_shape=[jax.ShapeDtypeStruct((n, d), _F32), jax.ShapeDtypeStruct((n, LANES), _F32),
                   jax.ShapeDtypeStruct((1, LANES), _F32)],
        grid=(n // tm,),
        in_specs=specs,
        out_specs=[pl.BlockSpec((tm, d), row), pl.BlockSpec((tm, LANES), row), pl.BlockSpec((1, LANES), const)],
        scratch_shapes=[pltpu.VMEM((1, LANES), _F32)],
        compiler_params=_params(("arbitrary",)),
        name="post_attn_router",
    )(*args)


def _dest_kernel(route_ref, ps_ref, o_ref):
    r = route_ref[...]
    lane = lax.broadcasted_iota(jnp.int32, r.shape, 1)
    lanef = lane.astype(_F32)
    ps = ps_ref[...]
    d = []
    for k in range(2):
        e = r[:, k:k + 1] + float(ROUTE_LANE0)
        d.append(jnp.sum(jnp.where(lanef == e, ps, 0.0), axis=-1, keepdims=True) + r[:, 4 + k:5 + k])
    o_ref[...] = jnp.where(lane == 0, d[0], jnp.where(lane == 1, d[1], 0.0)).astype(jnp.int32)


def _dest(route, pad_start_row):
    n = route.shape[0]
    tm = 1024
    return pl.pallas_call(
        _dest_kernel,
        out_shape=jax.ShapeDtypeStruct((n, LANES), jnp.int32),
        grid=(n // tm,),
        in_specs=[pl.BlockSpec((tm, LANES), lambda i: (i, 0)), pl.BlockSpec((1, LANES), lambda i: (0, 0))],
        out_specs=pl.BlockSpec((tm, LANES), lambda i: (i, 0)),
        compiler_params=_params(("arbitrary",)),
        name="moe_dest",
    )(route, pad_start_row)


def _dispatch_kernel(seg_ref, dest_ref, x_ref, xs_hbm, zero_ref, zsem, sem):
    i = pl.program_id(0)
    ntok = TOK_TM

    def zero_fill(e):
        first = pl.multiple_of(seg_ref[e] + seg_ref[N_EXPERTS + e] - MOE_BLOCK, MOE_BLOCK)
        return pltpu.make_async_copy(zero_ref, xs_hbm.at[pl.ds(first, MOE_BLOCK)], zsem)

    def has_pad(e):
        return seg_ref[N_EXPERTS + e] > seg_ref[2 * N_EXPERTS + e]

    @pl.when(i == 0)
    def _():
        zero_ref[...] = jnp.zeros_like(zero_ref)

        def start(e, c):
            @pl.when(has_pad(e))
            def _():
                zero_fill(e).start()
            return c

        def wait(e, c):
            @pl.when(has_pad(e))
            def _():
                zero_fill(e).wait()
            return c

        lax.fori_loop(0, N_EXPERTS, start, 0)
        lax.fori_loop(0, N_EXPERTS, wait, 0)

        nvalid = seg_ref[3 * N_EXPERTS]
        nblk = xs_hbm.shape[0] // MOE_BLOCK

        def tail(j):
            first = pl.multiple_of((nvalid + j) * MOE_BLOCK, MOE_BLOCK)
            return pltpu.make_async_copy(zero_ref, xs_hbm.at[pl.ds(first, MOE_BLOCK)], zsem)

        def tail_start(j, c):
            @pl.when(nvalid + j < nblk)
            def _():
                tail(j).start()
            return c

        def tail_wait(j, c):
            @pl.when(nvalid + j < nblk)
            def _():
                tail(j).wait()
            return c

        lax.fori_loop(0, N_EXPERTS, tail_start, 0)
        lax.fori_loop(0, N_EXPERTS, tail_wait, 0)

    def row_copy(j, slot):
        return pltpu.make_async_copy(x_ref.at[pl.ds(j, 1)], xs_hbm.at[pl.ds(slot, 1)], sem)

    def issue(j, c):
        for k in range(2):
            row_copy(j, dest_ref[0, 0, 2 * j + k]).start()
        return c

    def drain(j, c):
        row_copy(0, 0).wait()
        return c

    lax.fori_loop(0, ntok, issue, 0, unroll=4)
    lax.fori_loop(0, 2 * ntok, drain, 0, unroll=8)


def _dispatch(x2, dest_flat, seg, cap):
    n, d = x2.shape
    tm = TOK_TM
    nt = n // tm
    return pl.pallas_call(
        _dispatch_kernel,
        out_shape=jax.ShapeDtypeStruct((cap, d), x2.dtype),
        grid_spec=pltpu.PrefetchScalarGridSpec(
            num_scalar_prefetch=1,
            grid=(nt,),
            in_specs=[pl.BlockSpec((1, 1, 2 * tm), lambda i, seg: (i, 0, 0), memory_space=pltpu.SMEM),
                      pl.BlockSpec((tm, d), lambda i, seg: (i, 0))],
            out_specs=pl.BlockSpec(memory_space=pl.ANY),
            scratch_shapes=[pltpu.VMEM((MOE_BLOCK, d), x2.dtype), pltpu.SemaphoreType.DMA,
                            pltpu.SemaphoreType.DMA]),
        compiler_params=_params(("arbitrary",)),
        name="moe_dispatch",
    )(seg, dest_flat.reshape(nt, 1, 2 * tm), x2)


def _expert_kernel(blk_ref, nvalid_ref, xs_ref, wg_ref, wu_ref, wd_ref, ys_ref, wgb, wub, wdb):
    i = pl.program_id(0)
    prev = blk_ref[jnp.maximum(i - 1, 0)]
    valid = i < nvalid_ref[0]

    @pl.when(valid & ((i == 0) | (blk_ref[i] != prev)))
    def _():
        wgb[...] = wg_ref[0].astype(_BF16)
        wub[...] = wu_ref[0].astype(_BF16)
        wdb[...] = wd_ref[0].astype(_BF16)

    @pl.when(valid)
    def _():
        xb = xs_ref[...].astype(_BF16)
        gate = jnp.dot(xb, wgb[...], preferred_element_type=_F32)
        up = jnp.dot(xb, wub[...], preferred_element_type=_F32)
        mid = (gate * jax.nn.sigmoid(gate)) * up
        ys_ref[...] = jnp.dot(mid.astype(_BF16), wdb[...], preferred_element_type=_F32)

    @pl.when(jnp.logical_not(valid))
    def _():
        ys_ref[...] = jnp.zeros_like(ys_ref)


def _experts(xs, blk_exp, nvalid, w_gate, w_up, w_down):
    cap, d = xs.shape
    de = w_gate.shape[2]
    nblk = cap // MOE_BLOCK

    def xmap(i, blk, nv):
        return (jnp.minimum(i, nv[0] - 1), 0)

    def wmap(i, blk, nv):
        return (blk[i], 0, 0)

    return pl.pallas_call(
        _expert_kernel,
        out_shape=jax.ShapeDtypeStruct((cap, d), _F32),
        grid_spec=pltpu.PrefetchScalarGridSpec(
            num_scalar_prefetch=2,
            grid=(nblk,),
            in_specs=[pl.BlockSpec((MOE_BLOCK, d), xmap),
                      pl.BlockSpec((1, d, de), wmap), pl.BlockSpec((1, d, de), wmap),
                      pl.BlockSpec((1, de, d), wmap)],
            out_specs=pl.BlockSpec((MOE_BLOCK, d), lambda i, blk, nv: (i, 0)),
            scratch_shapes=[pltpu.VMEM((d, de), _BF16), pltpu.VMEM((d, de), _BF16), pltpu.VMEM((de, d), _BF16)]),
        compiler_params=_params(("arbitrary",)),
        name="moe_experts",
    )(blk_exp, nvalid, xs, w_gate, w_up, w_down)


def _combine_kernel(dcur_ref, dnxt_ref, ys_hbm, h_ref, route_ref, g_ref, b_ref, o_ref, buf, sem):
    i = pl.program_id(0)
    n = pl.num_programs(0)
    tm = TOK_TM

    def gather(dref, slot):
        def body(j, c):
            for k in range(2):
                pltpu.make_async_copy(ys_hbm.at[pl.ds(dref[0, 0, 2 * j + k], 1)],
                                      buf.at[slot, k, pl.ds(j, 1)], sem.at[slot]).start()
            return c
        lax.fori_loop(0, tm, body, 0, unroll=4)

    @pl.when(i == 0)
    def _():
        gather(dcur_ref, 0)

    @pl.when(i + 1 < n)
    def _():
        gather(dnxt_ref, (i + 1) % 2)

    slot = i % 2

    def drain(j, c):
        pltpu.make_async_copy(ys_hbm.at[pl.ds(0, 1)], buf.at[slot, 0, pl.ds(0, 1)], sem.at[slot]).wait()
        return c

    lax.fori_loop(0, 2 * tm, drain, 0, unroll=8)
    r = route_ref[...]
    y = r[:, 2:3] * buf[slot, 0] + r[:, 3:4] * buf[slot, 1]
    o_ref[...] = _layer_norm(ALPHA * h_ref[...] + y, g_ref[...], b_ref[...])


def _combine(ys, dest_flat, h1, route, g, b):
    n, d = h1.shape
    tm = TOK_TM
    nt = n // tm
    dest3 = dest_flat.reshape(nt, 1, 2 * tm)
    row = lambda i: (i, 0)
    const = lambda i: (0, 0)
    return pl.pallas_call(
        _combine_kernel,
        out_shape=jax.ShapeDtypeStruct((n, d), _F32),
        grid=(nt,),
        in_specs=[pl.BlockSpec((1, 1, 2 * tm), lambda i: (i, 0, 0), memory_space=pltpu.SMEM),
                  pl.BlockSpec((1, 1, 2 * tm), lambda i: (jnp.minimum(i + 1, nt - 1), 0, 0),
                               memory_space=pltpu.SMEM),
                  pl.BlockSpec(memory_space=pl.ANY),
                  pl.BlockSpec((tm, d), row), pl.BlockSpec((tm, LANES), row),
                  pl.BlockSpec((1, d), const), pl.BlockSpec((1, d), const)],
        out_specs=pl.BlockSpec((tm, d), row),
        scratch_shapes=[pltpu.VMEM((2, 2, tm, d), _F32), pltpu.SemaphoreType.DMA((2,))],
        compiler_params=_params(("arbitrary",)),
        name="moe_combine",
    )(dest3, dest3, ys, h1, route, g, b)


def _moe(h1, route, counts_row, w_gate, w_up, w_down, g, b):
    n, d = h1.shape
    cap = 2 * n + N_EXPERTS * MOE_BLOCK
    counts = counts_row[0, ROUTE_LANE0:ROUTE_LANE0 + N_EXPERTS].astype(jnp.int32)
    padded = (counts + MOE_BLOCK - 1) // MOE_BLOCK * MOE_BLOCK
    ends = jnp.cumsum(padded)
    pad_start = ends - padded
    nblk = cap // MOE_BLOCK
    blk_first = jnp.arange(nblk, dtype=jnp.int32) * MOE_BLOCK
    blk_exp = jnp.minimum(jnp.sum(blk_first[:, None] >= ends[None, :], axis=1), N_EXPERTS - 1).astype(jnp.int32)
    nvalid = (ends[-1:] // MOE_BLOCK).astype(jnp.int32)
    ps_row = jnp.zeros((1, LANES), _F32).at[0, ROUTE_LANE0:ROUTE_LANE0 + N_EXPERTS].set(pad_start.astype(_F32))
    dest_flat = _dest(route, ps_row)[:, :2].reshape(2 * n)
    seg = jnp.concatenate([pad_start, padded, counts, nvalid]).astype(jnp.int32)
    xs = _dispatch(h1, dest_flat, seg, cap)
    ys = _experts(xs, blk_exp, nvalid, w_gate, w_up, w_down)
    return _combine(ys, dest_flat, h1, route, g, b)


def _router_params(rg_w, rg_b, re_w, re_b):
    d = rg_w.shape[0]
    pad = LANES - N_GROUPS - N_EXPERTS
    rw = jnp.concatenate([rg_w, re_w, jnp.zeros((d, pad), _F32)], axis=1)
    rb = jnp.concatenate([rg_b, re_b.reshape(-1), jnp.zeros((pad,), _F32)]).reshape(1, LANES)
    return rw, rb


def kernel(x, t5_table, l0_w_qkv, l0_sink, l0_w_o, l0_ln1_g, l0_ln1_b, l0_rg_w, l0_rg_b, l0_re_w, l0_re_b, l0_w_gate, l0_w_up, l0_w_down, l0_ln2_g, l0_ln2_b, l1_w_qkv, l1_w_o, l1_ln1_g, l1_ln1_b, l1_rg_w, l1_rg_b, l1_re_w, l1_re_b, l1_w_gate, l1_w_up, l1_w_down, l1_ln2_g, l1_ln2_b, l2_w_qkv, l2_rpb, l2_w_o, l2_ln1_g, l2_ln1_b, l2_rg_w, l2_rg_b, l2_re_w, l2_re_b, l2_w_gate, l2_w_up, l2_w_down, l2_ln2_g, l2_ln2_b, l3_w_qkv, l3_sink, l3_w_o, l3_ln1_g, l3_ln1_b, l3_rg_w, l3_rg_b, l3_re_w, l3_re_b, l3_w_gate, l3_w_up, l3_w_down, l3_ln2_g, l3_ln2_b):
    mixers = [(l0_w_qkv, l0_sink, l0_w_o), (l1_w_qkv, None, l1_w_o), (l2_w_qkv, l2_rpb, l2_w_o),
              (l3_w_qkv, l3_sink, l3_w_o)]
    norm1 = [(l0_ln1_g, l0_ln1_b), (l1_ln1_g, l1_ln1_b), (l2_ln1_g, l2_ln1_b), (l3_ln1_g, l3_ln1_b)]
    routers = [(l0_rg_w, l0_rg_b, l0_re_w, l0_re_b), (l1_rg_w, l1_rg_b, l1_re_w, l1_re_b),
               (l2_rg_w, l2_rg_b, l2_re_w, l2_re_b), (l3_rg_w, l3_rg_b, l3_re_w, l3_re_b)]
    experts = [(l0_w_gate, l0_w_up, l0_w_down), (l1_w_gate, l1_w_up, l1_w_down),
               (l2_w_gate, l2_w_up, l2_w_down), (l3_w_gate, l3_w_up, l3_w_down)]
    norm2 = [(l0_ln2_g, l0_ln2_b), (l1_ln2_g, l1_ln2_b), (l2_ln2_g, l2_ln2_b), (l3_ln2_g, l3_ln2_b)]

    bsz, seq, d = x.shape
    n = bsz * seq
    inner = N_HEADS * HEAD_DIM
    scale = HEAD_DIM ** -0.5
    h = x.reshape(n, d)
    bias_a = _band_bias_t(t5_table, A_BLOCK, A_WINDOW, 1, A_BAND)
    for i in range(DEPTH):
        kind = i % N_MIXERS
        w_qkv, extra, w_o = mixers[i]
        kvw = A_KV_HEADS * HEAD_DIM if kind == 0 else inner
        q, k, v = _project(h, w_qkv.astype(_BF16), (inner, kvw, kvw), (scale, 1.0, 1.0))
        q, k, v = (t.reshape(bsz, seq, t.shape[1]) for t in (q, k, v))
        lses = ()
        if kind == 0:
            attn = [_mixer_a(q, k, v, extra.astype(_F32), bias_a, bsz, seq)]
        elif kind == 1:
            attn, lses = [], []
            for window, dil in B_BRANCHES:
                o, lse = _mixer_b_branch(q, k, v, t5_table, bsz, seq, window, dil)
                attn.append(o)
                lses.append(jnp.pad(lse.reshape(n, N_HEADS), ((0, 0), (0, LANES - N_HEADS))))
        else:
            attn = [_mixer_c(q, k, v, _na_bias_t(extra, seq // GRID_W), bsz, seq)]
        attn = [a.reshape(n, inner) for a in attn]
        g1, b1 = norm1[i]
        rw, rb = _router_params(*routers[i])
        h1, route, counts = _post_attn(attn, lses, h, w_o.astype(_BF16), g1.reshape(1, d), b1.reshape(1, d), rw, rb)
        g2, b2 = norm2[i]
        h = _moe(h1, route, counts, *experts[i], g2.reshape(1, d), b2.reshape(1, d))
    return h.reshape(bsz, seq, d)
```

```python
import functools
import math

import jax
import jax.numpy as jnp
import numpy as np
from jax import lax
from jax.experimental import pallas as pl
from jax.experimental.pallas import tpu as pltpu

HEAD_DIM = 64
N_HEADS = 16
A_KV_HEADS = 4
A_WINDOW = 128
A_BLOCK = 128
B_BRANCHES = ((128, 1), (512, 4), (2048, 16))
GRID_W = 64
NA_ROWS = 8
NA_COLS = 16
T5_BUCKETS = 32
T5_MAX_DIST = 1024
N_GROUPS = 4
EXPERTS_PER_GROUP = 8
N_EXPERTS = N_GROUPS * EXPERTS_PER_GROUP
DEPTH = 4
N_MIXERS = 3
ALPHA = (2 * DEPTH) ** 0.25
LN_EPS = 1e-5
NEG_INF = -1e30

LANES = 128
PAIR = 2 * HEAD_DIM
BF16_SUBLANES = 16
VT_ROWS = PAIR + BF16_SUBLANES
LOG2E = math.log2(math.e)
LN2 = math.log(2.0)
VMEM_LIMIT = 56 * 1024 * 1024

PROJ_TM = 512
PROJ_CHUNK = 512
POST_TM = 256
TOK_TM = 256
MOE_BLOCK = 512
A_TQ = 1024
A_BAND = 3
A_SUB_PER_GROUP = 2
B_BLOCK = 128
UNITS_PER_GROUP = 8
C_UNIT_ROWS = 2
C_BAND_ROWS = 10
C_UNITS_PER_GROUP = 4
C_ROWS_PER_STEP = 32
ROUTE_LANE0 = N_GROUPS

_F32 = jnp.float32
_BF16 = jnp.bfloat16


def _params(sem, vmem=VMEM_LIMIT, flags=None):
    return pltpu.CompilerParams(dimension_semantics=sem, vmem_limit_bytes=vmem, flags=flags)


def _proj_kernel(x_ref, w_ref, *o_refs, widths, scales):
    xb = x_ref[...].astype(_BF16)
    col = 0
    for o_ref, width, scale in zip(o_refs, widths, scales):
        for c in range(0, width, PROJ_CHUNK):
            cw = min(PROJ_CHUNK, width - c)
            acc = jnp.dot(xb, w_ref[:, col + c:col + c + cw], preferred_element_type=_F32)
            if scale != 1.0:
                acc = acc * scale
            o_ref[:, c:c + cw] = acc.astype(o_ref.dtype)
        col += width


def _project(x2, w_bf16, widths, scales):
    n, d = x2.shape
    ctot = sum(widths)
    return pl.pallas_call(
        functools.partial(_proj_kernel, widths=tuple(widths), scales=tuple(scales)),
        out_shape=[jax.ShapeDtypeStruct((n, w), _BF16) for w in widths],
        grid=(n // PROJ_TM,),
        in_specs=[pl.BlockSpec((PROJ_TM, d), lambda i: (i, 0)),
                  pl.BlockSpec((d, ctot), lambda i: (0, 0))],
        out_specs=[pl.BlockSpec((PROJ_TM, w), lambda i: (i, 0)) for w in widths],
        compiler_params=_params(("arbitrary",)),
        name="qkv_proj",
    )(x2, w_bf16)


def _one():
    return jnp.ones((1,), jnp.int32)


def _lane_lo(rows=1):
    return lax.broadcasted_iota(jnp.int32, (rows, PAIR), 1) < HEAD_DIM


def _swap_halves(x):
    return pltpu.roll(x.astype(_F32), HEAD_DIM, axis=1).astype(_BF16)


class _PairUnit:
    def __init__(self, q2, k, vt, bias, emit, sink_a=None, sink_b=None):
        self.q2, self.k, self.vt, self.bias, self.emit = q2, k, vt, bias, emit
        self.sink_a, self.sink_b = sink_a, sink_b


def _attend_groups(ngroups, units_of, scratch, one):
    assert ngroups % 2 == 0
    lo = _lane_lo()
    nt = (((1,), (1,)), ((), ()))

    def sink_row(unit, m2):
        lane = lax.broadcasted_iota(jnp.int32, (1, m2), 1)
        return jnp.where(lane < m2 // 2, unit.sink_a, unit.sink_b)

    def scores(g, parity):
        s_scr, m_scr = scratch[parity]
        for u, unit in enumerate(units_of(g, parity)):
            q2 = unit.q2()
            zero = jnp.zeros_like(q2)
            qq = jnp.concatenate([jnp.where(lo, q2, zero), jnp.where(lo, zero, q2)], axis=0)
            s = lax.dot_general(unit.k(), qq, nt, preferred_element_type=_F32) + unit.bias()
            m = jnp.max(s, axis=0, keepdims=True)
            if unit.sink_a is not None:
                m = jnp.maximum(m, sink_row(unit, s.shape[1]))
            s_scr[u] = s
            m_scr[u] = m

    def values(g, parity):
        s_scr, m_scr = scratch[parity]
        for u, unit in enumerate(units_of(g, parity)):
            m = m_scr[u]
            e = jnp.exp2(s_scr[u] - m)
            ot = jnp.dot(unit.vt(), e.astype(_BF16), preferred_element_type=_F32)
            den = ot[PAIR:PAIR + 1, :]
            if unit.sink_a is not None:
                den = den + jnp.exp2(sink_row(unit, e.shape[1]) - m)
            ot = ot[:PAIR] * (1.0 / den)
            mq = ot.shape[1] // 2
            top = lax.broadcasted_iota(jnp.int32, (PAIR, 1), 0) < HEAD_DIM
            pair_t = jnp.where(top, ot[:, :mq], ot[:, mq:])
            unit.emit(pair_t.T, (m + jnp.log2(den)) * LN2)

    def region(*work):
        def body(_, carry):
            for fn, g, parity in work:
                fn(g, parity)
            return carry
        lax.fori_loop(0, one, body, 0)

    region((scores, 0, 0))

    def steady(i, carry):
        g = 2 * i
        region((values, g, 0), (scores, g + 1, 1))
        region((values, g + 1, 1), (scores, g + 2, 0))
        return carry

    lax.fori_loop(0, ngroups // 2 - 1, steady, 0)
    region((values, ngroups - 2, 0), (scores, ngroups - 1, 1))
    region((values, ngroups - 1, 1))


def _transpose_blocks(v_ref_rows, nblk, store):
    def body(j, carry):
        rows = pl.ds(pl.multiple_of(j * LANES, LANES), LANES)
        vt = v_ref_rows(rows).astype(_F32).T.astype(_BF16)
        store(j, jnp.concatenate([vt, jnp.ones((VT_ROWS - PAIR, LANES), _BF16)], axis=0))
        return carry
    lax.fori_loop(0, nblk, body, 0)


def _t5_bucket(rel):
    half = T5_BUCKETS // 2
    exact = half // 2
    n = np.abs(rel)
    big = exact + (np.log(np.maximum(n, 1) / exact) / math.log(T5_MAX_DIST / exact) * (half - exact)).astype(np.int32)
    return ((rel > 0) * half + np.where(n < exact, n, np.minimum(big, half - 1))).astype(np.int32)


def _band_bias_t(t5_table, blk, window, dil, nband):
    tk = nband * blk
    off = tk - 1
    rel = np.arange(-off, tk)
    vec = jnp.take(t5_table, jnp.asarray(_t5_bucket(rel * dil)), axis=0).astype(_F32)
    vec = jnp.where(jnp.asarray(np.abs(rel) <= window)[:, None], vec * LOG2E, NEG_INF).T
    h = vec.shape[0]
    vec = jnp.pad(vec, ((0, 0), (0, 1)))
    w = 2 * tk - 1
    skew = jnp.tile(vec, (1, blk))[:, :blk * w].reshape(h, blk, w)
    t = jnp.stack([skew[:, :, off - v * blk:off - v * blk + tk] for v in range(nband)], axis=1)
    t = t.reshape(h // 2, 2, nband, blk, tk).transpose(0, 2, 4, 1, 3)
    return t.reshape(h // 2, nband, tk, 2 * blk)


def _band_window(gb, blk, length, nband):
    first = jnp.clip(gb - 1, 0, length // blk - nband)
    return first, gb - first


def _attn_a_kernel(sink_ref, one_ref, q_ref, k_ref, v_ref, bias_ref, o_ref, kd_ref, vdt_ref,
                   s0_scr, m0_scr, s1_scr, m1_scr, *, seq):
    kvb = pl.program_id(0)
    qt = pl.program_id(2)
    one = one_ref[0]
    nblk = seq // A_BLOCK

    @pl.when(qt == 0)
    def _():
        lo = _lane_lo()

        def fill(j, carry):
            rows = pl.ds(pl.multiple_of(j * A_BLOCK, A_BLOCK), A_BLOCK)
            kk = k_ref[0, rows, :]
            ks = _swap_halves(kk)
            kd_ref[0, rows, :] = jnp.where(lo, kk, ks)
            kd_ref[1, rows, :] = jnp.where(lo, ks, kk)
            vt = v_ref[0, rows, :].astype(_F32).T.astype(_BF16)
            ones = jnp.ones((VT_ROWS - PAIR, A_BLOCK), _BF16)
            vdt_ref[0, j] = jnp.concatenate([vt[:HEAD_DIM], vt[:HEAD_DIM], ones], axis=0)
            vdt_ref[1, j] = jnp.concatenate([vt[HEAD_DIM:], vt[HEAD_DIM:], ones], axis=0)
            return carry

        lax.fori_loop(0, nblk, fill, 0)

    nsub = A_TQ // A_BLOCK
    npair = q_ref.shape[2] // PAIR
    hbase = kvb * (2 * npair)

    def units_of(g, parity):
        units = []
        for j in range(A_SUB_PER_GROUP):
            si = g * A_SUB_PER_GROUP + j
            first, var = _band_window(qt * nsub + si, A_BLOCK, seq, A_BAND)
            rows = pl.ds(pl.multiple_of(si * A_BLOCK, A_BLOCK), A_BLOCK)
            band = pl.ds(pl.multiple_of(first * A_BLOCK, A_BLOCK), A_BAND * A_BLOCK)
            for jj in range(npair):
                hk = jj // (npair // 2)
                cols = slice(jj * PAIR, (jj + 1) * PAIR)

                def emit(o, lse, rows=rows, cols=cols):
                    o_ref[0, rows, cols] = o.astype(o_ref.dtype)

                units.append(_PairUnit(
                    q2=lambda rows=rows, cols=cols: q_ref[0, rows, cols],
                    k=lambda hk=hk, band=band: kd_ref[hk, band, :],
                    vt=lambda hk=hk, first=first: jnp.concatenate(
                        [vdt_ref[hk, first + t] for t in range(A_BAND)], axis=1),
                    bias=lambda jj=jj, var=var: bias_ref[jj, var],
                    emit=emit, sink_a=sink_ref[hbase + 2 * jj], sink_b=sink_ref[hbase + 2 * jj + 1]))
        return units

    _attend_groups(nsub // A_SUB_PER_GROUP, units_of, ((s0_scr, m0_scr), (s1_scr, m1_scr)), one)


def _mixer_a(q, k, v, sink, bias, bsz, seq):
    nkvb = k.shape[2] // PAIR
    qw = q.shape[2] // nkvb
    npair = qw // PAIR
    tk = A_BAND * A_BLOCK
    nunits = A_SUB_PER_GROUP * npair
    group_scratch = [pltpu.VMEM((nunits, tk, 2 * A_BLOCK), _F32), pltpu.VMEM((nunits, 1, 2 * A_BLOCK), _F32)]
    return pl.pallas_call(
        functools.partial(_attn_a_kernel, seq=seq),
        out_shape=jax.ShapeDtypeStruct(q.shape, _BF16),
        grid=(nkvb, bsz, seq // A_TQ),
        in_specs=[pl.BlockSpec(memory_space=pltpu.SMEM), pl.BlockSpec(memory_space=pltpu.SMEM),
                  pl.BlockSpec((1, A_TQ, qw), lambda c, b, t: (b, t, c)),
                  pl.BlockSpec((1, seq, PAIR), lambda c, b, t: (b, 0, c)),
                  pl.BlockSpec((1, seq, PAIR), lambda c, b, t: (b, 0, c)),
                  pl.BlockSpec((npair, A_BAND, tk, 2 * A_BLOCK), lambda c, b, t: (c, 0, 0, 0))],
        out_specs=pl.BlockSpec((1, A_TQ, qw), lambda c, b, t: (b, t, c)),
        scratch_shapes=[pltpu.VMEM((2, seq, PAIR), _BF16), pltpu.VMEM((2, seq // A_BLOCK, VT_ROWS, A_BLOCK), _BF16)]
        + group_scratch + group_scratch,
        compiler_params=_params(("arbitrary", "arbitrary", "arbitrary")),
        name="mixer_a",
    )(sink, _one(), q, k, v, bias)


def _attn_band_kernel(one_ref, q_ref, k_ref, v_ref, bias_ref, o_ref, lse_ref, vt_ref,
                      s0_scr, m0_scr, s1_scr, m1_scr, *, length, npairs, nband):
    col = pl.program_id(2)
    one = one_ref[0]
    nblk = length // B_BLOCK

    for p in range(npairs):
        def store(j, blk, p=p):
            vt_ref[p, j] = blk
        _transpose_blocks(lambda rows, p=p: v_ref[0, rows, p * PAIR:(p + 1) * PAIR], nblk, store)

    def unit(gb, p):
        first, var = _band_window(gb, B_BLOCK, length, nband)
        rows = pl.ds(pl.multiple_of(gb * B_BLOCK, B_BLOCK), B_BLOCK)
        band = pl.ds(pl.multiple_of(first * B_BLOCK, B_BLOCK), nband * B_BLOCK)
        cols = slice(p * PAIR, (p + 1) * PAIR)

        def emit(o, lse):
            o_ref[0, rows, cols] = o.astype(o_ref.dtype)
            ha = col * (2 * npairs) + 2 * p
            lse_ref[0, 0, gb, pl.ds(ha, 1), :] = lse[:, :B_BLOCK]
            lse_ref[0, 0, gb, pl.ds(ha + 1, 1), :] = lse[:, B_BLOCK:]

        return _PairUnit(
            q2=lambda: q_ref[0, rows, cols], k=lambda: k_ref[0, band, cols],
            vt=lambda: jnp.concatenate([vt_ref[p, first + t] for t in range(nband)], axis=1),
            bias=lambda: bias_ref[p, var], emit=emit)

    if npairs == 1:
        ngroups = nblk // UNITS_PER_GROUP
        units_of = lambda g, parity: [unit(g * UNITS_PER_GROUP + j, 0) for j in range(UNITS_PER_GROUP)]
    else:
        assert npairs == UNITS_PER_GROUP
        ngroups = nblk
        units_of = lambda g, parity: [unit(g, p) for p in range(npairs)]
    _attend_groups(ngroups, units_of, ((s0_scr, m0_scr), (s1_scr, m1_scr)), one)


def _mixer_b_branch(q, k, v, t5_table, bsz, seq, window, dil):
    length = seq // dil
    width = q.shape[2]
    nblk = length // B_BLOCK
    nband = min(3, nblk)
    tk = nband * B_BLOCK
    npairs = (width // PAIR) if length * width * 2 <= 2 * 1024 * 1024 else 1
    ncol = width // (npairs * PAIR)
    bw = npairs * PAIR
    bias = _band_bias_t(t5_table, B_BLOCK, window // 2 // dil, dil, nband)

    def view(t):
        return t.reshape(bsz, length, dil * width)

    def cmap(b, r, c):
        return (b, 0, r * ncol + c)

    o, lse = pl.pallas_call(
        functools.partial(_attn_band_kernel, length=length, npairs=npairs, nband=nband),
        out_shape=[jax.ShapeDtypeStruct((bsz, length, dil * width), _BF16),
                   jax.ShapeDtypeStruct((bsz, dil, nblk, N_HEADS, B_BLOCK), _F32)],
        grid=(bsz, dil, ncol),
        in_specs=[pl.BlockSpec(memory_space=pltpu.SMEM),
                  pl.BlockSpec((1, length, bw), cmap),
                  pl.BlockSpec((1, length, bw), cmap),
                  pl.BlockSpec((1, length, bw), cmap),
                  pl.BlockSpec((npairs, nband, tk, 2 * B_BLOCK), lambda b, r, c: (c, 0, 0, 0))],
        out_specs=[pl.BlockSpec((1, length, bw), cmap),
                   pl.BlockSpec((1, 1, nblk, N_HEADS, B_BLOCK), lambda b, r, c: (b, r, 0, 0, 0))],
        scratch_shapes=[pltpu.VMEM((npairs, nblk, VT_ROWS, B_BLOCK), _BF16)] + 2 * [
            pltpu.VMEM((UNITS_PER_GROUP, tk, 2 * B_BLOCK), _F32), pltpu.VMEM((UNITS_PER_GROUP, 1, 2 * B_BLOCK), _F32)],
        compiler_params=_params(("arbitrary", "arbitrary", "arbitrary")),
        name=f"mixer_b_dil{dil}",
    )(_one(), view(q), view(k), view(v), bias)
    lse = lse.transpose(0, 2, 4, 1, 3).reshape(bsz, seq, N_HEADS)
    return o.reshape(bsz, seq, width), lse


def _na_bias_t(rpb, rows):
    canon = (0, 2, 4, rows - 4, rows - 2)
    h = rpb.shape[0]
    kc = np.arange(GRID_W)[:, None]
    qc = np.arange(GRID_W)[None, :]
    ws = np.clip(qc - NA_COLS // 2, 0, GRID_W - NA_COLS)
    col_ok = (kc >= ws) & (kc < ws + NA_COLS)
    col_idx = np.clip(kc - qc, 1 - NA_COLS, NA_COLS - 1) + NA_COLS - 1
    colpart = jnp.where(jnp.asarray(col_ok), rpb.astype(_F32)[:, :, jnp.asarray(col_idx)] * LOG2E, NEG_INF)
    masked = jnp.full((h, GRID_W, GRID_W), NEG_INF, _F32)
    variants = []
    for i in canon:
        start = int(np.clip(i - NA_ROWS // 2, 0, rows - C_BAND_ROWS))
        per_key_row = []
        for kr in range(C_BAND_ROWS):
            per_query_row = []
            for qr in range(C_UNIT_ROWS):
                iq, kabs = i + qr, start + kr
                first_key_row = int(np.clip(iq - NA_ROWS // 2, 0, rows - NA_ROWS))
                inside = first_key_row <= kabs < first_key_row + NA_ROWS
                per_query_row.append(colpart[:, kabs - iq + NA_ROWS - 1] if inside else masked)
            per_key_row.append(jnp.stack(per_query_row, axis=2))
        variants.append(jnp.stack(per_key_row, axis=1))
    b = jnp.stack(variants, axis=1)
    nq, nk = C_UNIT_ROWS * GRID_W, C_BAND_ROWS * GRID_W
    b = b.reshape(h // 2, 2, len(canon), nk, nq)
    return b.transpose(0, 2, 3, 1, 4).reshape(h // 2, len(canon), nk, 2 * nq)


def _attn_c_kernel(one_ref, q_ref, k_ref, v_ref, bias_ref, o_ref, vt_ref, s0_scr, m0_scr, s1_scr, m1_scr, *, rows):
    rt = pl.program_id(2)
    nq = C_UNIT_ROWS * GRID_W
    band_blocks = C_BAND_ROWS * GRID_W // LANES

    @pl.when(rt == 0)
    def _():
        def store(j, blk):
            vt_ref[j] = blk
        _transpose_blocks(lambda r: v_ref[0, r, :], rows * GRID_W // LANES, store)

    def unit(u):
        i = rt * C_ROWS_PER_STEP + C_UNIT_ROWS * u
        start = jnp.clip(i - NA_ROWS // 2, 0, rows - C_BAND_ROWS)
        var = lax.shift_right_logical(i - start, 1)
        first = lax.shift_right_logical(start, 1)
        keys = pl.ds(pl.multiple_of(first * LANES, LANES), C_BAND_ROWS * GRID_W)
        qrows = pl.ds(pl.multiple_of(u * nq, nq), nq)

        def emit(o, lse):
            o_ref[0, qrows, :] = o.astype(o_ref.dtype)

        return _PairUnit(
            q2=lambda: q_ref[0, qrows, :], k=lambda: k_ref[0, keys, :],
            vt=lambda: jnp.concatenate([vt_ref[first + t] for t in range(band_blocks)], axis=1),
            bias=lambda: bias_ref[0, var], emit=emit)

    ngroups = C_ROWS_PER_STEP // C_UNIT_ROWS // C_UNITS_PER_GROUP
    units_of = lambda g, parity: [unit(g * C_UNITS_PER_GROUP + j) for j in range(C_UNITS_PER_GROUP)]
    _attend_groups(ngroups, units_of, ((s0_scr, m0_scr), (s1_scr, m1_scr)), one_ref[0])


def _mixer_c(q, k, v, bias, bsz, seq):
    rows = seq // GRID_W
    tq = C_ROWS_PER_STEP * GRID_W
    npair = q.shape[2] // PAIR
    nq, nk = C_UNIT_ROWS * GRID_W, C_BAND_ROWS * GRID_W
    return pl.pallas_call(
        functools.partial(_attn_c_kernel, rows=rows),
        out_shape=jax.ShapeDtypeStruct(q.shape, _BF16),
        grid=(npair, bsz, seq // tq),
        in_specs=[pl.BlockSpec(memory_space=pltpu.SMEM),
                  pl.BlockSpec((1, tq, PAIR), lambda c, b, t: (b, t, c)),
                  pl.BlockSpec((1, seq, PAIR), lambda c, b, t: (b, 0, c)),
                  pl.BlockSpec((1, seq, PAIR), lambda c, b, t: (b, 0, c)),
                  pl.BlockSpec((1, bias.shape[1], nk, 2 * nq), lambda c, b, t: (c, 0, 0, 0))],
        out_specs=pl.BlockSpec((1, tq, PAIR), lambda c, b, t: (b, t, c)),
        scratch_shapes=[pltpu.VMEM((seq // LANES, VT_ROWS, LANES), _BF16)] + 2 * [
            pltpu.VMEM((C_UNITS_PER_GROUP, nk, 2 * nq), _F32), pltpu.VMEM((C_UNITS_PER_GROUP, 1, 2 * nq), _F32)],
        compiler_params=_params(("arbitrary", "arbitrary", "arbitrary")),
        name="mixer_c",
    )(_one(), q, k, v, bias)


def _layer_norm(z, g, b):
    mu = jnp.mean(z, axis=-1, keepdims=True)
    zc = z - mu
    var = jnp.mean(zc * zc, axis=-1, keepdims=True)
    return zc * lax.rsqrt(var + LN_EPS) * g + b


def _rows_to_tiles(ref, x):
    nrows, d = x.shape
    nsl = d // LANES
    for s in range(nsl):
        ref[pl.ds(s, nrows, stride=nsl), :] = x[:, s * LANES:(s + 1) * LANES]


def _rows_from_tiles(ref, nrows, d):
    nsl = d // LANES
    return jnp.concatenate([ref[pl.ds(s, nrows, stride=nsl), :] for s in range(nsl)], axis=1)


def _split_bf16(x):
    hi = x.astype(_BF16)
    return hi, (x - hi.astype(_F32)).astype(_BF16)


def _route(hn, rw_ref, rb_ref, tri_ref, carry_ref):
    tm = hn.shape[0]
    h_hi, h_lo = _split_bf16(hn)
    w_hi, w_lo = _split_bf16(rw_ref[...])
    logits = (jnp.dot(h_hi, w_hi, preferred_element_type=_F32)
              + (jnp.dot(h_hi, w_lo, preferred_element_type=_F32)
                 + jnp.dot(h_lo, w_hi, preferred_element_type=_F32))) + rb_ref[...]
    lane = lax.broadcasted_iota(jnp.int32, (tm, LANES), 1).astype(_F32)
    ninf = -jnp.inf
    is_grp = lane < N_GROUPS
    lg = jnp.where(is_grp, logits, ninf)
    gmax = jnp.max(lg, axis=-1, keepdims=True)
    grp = jnp.min(jnp.where(lg == gmax, lane, float(LANES)), axis=-1, keepdims=True)
    p_grp = 1.0 / jnp.sum(jnp.where(is_grp, jnp.exp(lg - gmax), 0.0), axis=-1, keepdims=True)
    lo_lane = ROUTE_LANE0 + EXPERTS_PER_GROUP * grp
    in_grp = (lane >= lo_lane) & (lane < lo_lane + EXPERTS_PER_GROUP)
    le = jnp.where(in_grp, logits, ninf)
    v1 = jnp.max(le, axis=-1, keepdims=True)
    i1 = jnp.min(jnp.where(le == v1, lane, float(LANES)), axis=-1, keepdims=True)
    le2 = jnp.where(lane == i1, ninf, le)
    v2 = jnp.max(le2, axis=-1, keepdims=True)
    i2 = jnp.min(jnp.where(le2 == v2, lane, float(LANES)), axis=-1, keepdims=True)
    t = jnp.exp(v2 - v1)
    s1 = 1.0 / (1.0 + t)
    g1 = p_grp * s1
    g2 = p_grp * (t * s1)
    sel1 = lane == i1
    sel2 = lane == i2
    onehot = jnp.where(sel1 | sel2, 1.0, 0.0)
    before = jnp.dot(tri_ref[...], onehot.astype(_BF16), preferred_element_type=_F32) + carry_ref[...]
    rank1 = jnp.sum(jnp.where(sel1, before, 0.0), axis=-1, keepdims=True)
    rank2 = jnp.sum(jnp.where(sel2, before, 0.0), axis=-1, keepdims=True)
    carry_ref[...] = carry_ref[...] + jnp.sum(onehot, axis=0, keepdims=True)
    out = jnp.zeros((tm, LANES), _F32)
    for idx, val in enumerate((i1 - ROUTE_LANE0, i2 - ROUTE_LANE0, g1, g2, rank1, rank2)):
        out = jnp.where(lane == idx, val, out)
    return out


def _post_attn_kernel(*refs, n_branch):
    attn_refs = refs[:n_branch]
    pos = n_branch
    if n_branch > 1:
        lse_refs = refs[pos:pos + n_branch]
        expand_ref = refs[pos + n_branch]
        pos += n_branch + 1
    wo_ref, h_ref, g_ref, b_ref, rw_ref, rb_ref, tri_ref = refs[pos:pos + 7]
    h1_ref, route_ref, counts_ref, carry_ref = refs[pos + 7:pos + 11]

    @pl.when(pl.program_id(0) == 0)
    def _():
        carry_ref[...] = jnp.zeros_like(carry_ref)

    if n_branch == 1:
        a = attn_refs[0][...]
    else:
        lses = [r[...] for r in lse_refs]
        mx = functools.reduce(jnp.maximum, lses)
        es = [jnp.exp(l - mx) for l in lses]
        inv = 1.0 / functools.reduce(jnp.add, es)
        mixed = None
        for e, a_ref in zip(es, attn_refs):
            w_hi, w_lo = _split_bf16(e * inv)
            w_full = (jnp.dot(w_hi, expand_ref[...], preferred_element_type=_F32)
                      + jnp.dot(w_lo, expand_ref[...], preferred_element_type=_F32))
            term = w_full * a_ref[...].astype(_F32)
            mixed = term if mixed is None else mixed + term
        a = mixed.astype(_BF16)
    y = jnp.dot(a, wo_ref[...], preferred_element_type=_F32)
    hn = _layer_norm(ALPHA * h_ref[...] + y, g_ref[...], b_ref[...])
    _rows_to_tiles(h1_ref, hn)
    route_ref[...] = _route(hn, rw_ref, rb_ref, tri_ref, carry_ref)
    counts_ref[...] = carry_ref[...]


def _post_attn(attn, lses, h2, w_o, g, b, rw, rb):
    n, d = h2.shape
    tm = POST_TM
    nb = len(attn)
    tri = jnp.asarray(np.tril(np.ones((tm, tm), np.float32), -1), _BF16)
    row = lambda i: (i, 0)
    const = lambda i: (0, 0)
    args = list(attn)
    specs = [pl.BlockSpec((tm, d), row) for _ in attn]
    if nb > 1:
        expand = np.zeros((LANES, d), np.float32)
        for hd in range(N_HEADS):
            expand[hd, hd * HEAD_DIM:(hd + 1) * HEAD_DIM] = 1.0
        args += list(lses) + [jnp.asarray(expand, _BF16)]
        specs += [pl.BlockSpec((tm, LANES), row) for _ in lses] + [pl.BlockSpec((LANES, d), const)]
    args += [w_o, h2, g, b, rw, rb, tri]
    specs += [pl.BlockSpec((d, d), const), pl.BlockSpec((tm, d), row), pl.BlockSpec((1, d), const),
              pl.BlockSpec((1, d), const), pl.BlockSpec((d, LANES), const), pl.BlockSpec((1, LANES), const),
              pl.BlockSpec((tm, tm), const)]
    return pl.pallas_call(
        functools.partial(_post_attn_kernel, n_branch=nb),
        out_shape=[jax.ShapeDtypeStruct((n * d // LANES, LANES), _F32), jax.ShapeDtypeStruct((n, LANES), _F32),
                   jax.ShapeDtypeStruct((1, LANES), _F32)],
        grid=(n // tm,),
        in_specs=specs,
        out_specs=[pl.BlockSpec((tm * d // LANES, LANES), row), pl.BlockSpec((tm, LANES), row),
                   pl.BlockSpec((1, LANES), const)],
        scratch_shapes=[pltpu.VMEM((1, LANES), _F32)],
        compiler_params=_params(("arbitrary",)),
        name="post_attn_router",
    )(*args)


def _dest_kernel(route_ref, ps_ref, o_ref):
    r = route_ref[...]
    lane = lax.broadcasted_iota(jnp.int32, r.shape, 1)
    lanef = lane.astype(_F32)
    ps = ps_ref[...]
    d = []
    for k in range(2):
        e = r[:, k:k + 1] + float(ROUTE_LANE0)
        d.append(jnp.sum(jnp.where(lanef == e, ps, 0.0), axis=-1, keepdims=True) + r[:, 4 + k:5 + k])
    o_ref[...] = jnp.where(lane == 0, d[0], jnp.where(lane == 1, d[1], 0.0)).astype(jnp.int32)


def _dest(route, pad_start_row):
    n = route.shape[0]
    tm = 1024
    return pl.pallas_call(
        _dest_kernel,
        out_shape=jax.ShapeDtypeStruct((n, LANES), jnp.int32),
        grid=(n // tm,),
        in_specs=[pl.BlockSpec((tm, LANES), lambda i: (i, 0)), pl.BlockSpec((1, LANES), lambda i: (0, 0))],
        out_specs=pl.BlockSpec((tm, LANES), lambda i: (i, 0)),
        compiler_params=_params(("arbitrary",)),
        name="moe_dest",
    )(route, pad_start_row)


def _dispatch_kernel(seg_ref, dest_ref, x_ref, xs_hbm, zero_ref, zsem, sem):
    i = pl.program_id(0)
    ntok = TOK_TM
    nsl = x_ref.shape[0] // ntok
    blk_rows = MOE_BLOCK * nsl

    def zero_fill(e):
        first = pl.multiple_of((seg_ref[e] + seg_ref[N_EXPERTS + e] - MOE_BLOCK) * nsl, blk_rows)
        return pltpu.make_async_copy(zero_ref, xs_hbm.at[pl.ds(first, blk_rows)], zsem)

    def has_pad(e):
        return seg_ref[N_EXPERTS + e] > seg_ref[2 * N_EXPERTS + e]

    @pl.when(i == 0)
    def _():
        zero_ref[...] = jnp.zeros_like(zero_ref)

        def start(e, c):
            @pl.when(has_pad(e))
            def _():
                zero_fill(e).start()
            return c

        def wait(e, c):
            @pl.when(has_pad(e))
            def _():
                zero_fill(e).wait()
            return c

        lax.fori_loop(0, N_EXPERTS, start, 0)
        lax.fori_loop(0, N_EXPERTS, wait, 0)

        nvalid = seg_ref[3 * N_EXPERTS]
        nblk = xs_hbm.shape[0] // blk_rows

        def tail(j):
            first = pl.multiple_of((nvalid + j) * blk_rows, blk_rows)
            return pltpu.make_async_copy(zero_ref, xs_hbm.at[pl.ds(first, blk_rows)], zsem)

        def tail_start(j, c):
            @pl.when(nvalid + j < nblk)
            def _():
                tail(j).start()
            return c

        def tail_wait(j, c):
            @pl.when(nvalid + j < nblk)
            def _():
                tail(j).wait()
            return c

        lax.fori_loop(0, N_EXPERTS, tail_start, 0)
        lax.fori_loop(0, N_EXPERTS, tail_wait, 0)

    def row_copy(j, slot):
        return pltpu.make_async_copy(x_ref.at[pl.ds(pl.multiple_of(j * nsl, nsl), nsl)],
                                     xs_hbm.at[pl.ds(pl.multiple_of(slot * nsl, nsl), nsl)], sem)

    def issue(j, c):
        for k in range(2):
            row_copy(j, dest_ref[0, 0, 2 * j + k]).start()
        return c

    def drain(j, c):
        row_copy(0, 0).wait()
        return c

    lax.fori_loop(0, ntok, issue, 0, unroll=4)
    lax.fori_loop(0, 2 * ntok, drain, 0, unroll=8)


def _dispatch(x_tiles, dest_flat, seg, cap, n):
    nsl = x_tiles.shape[0] // n
    tm = TOK_TM
    nt = n // tm
    return pl.pallas_call(
        _dispatch_kernel,
        out_shape=jax.ShapeDtypeStruct((cap * nsl, LANES), x_tiles.dtype),
        grid_spec=pltpu.PrefetchScalarGridSpec(
            num_scalar_prefetch=1,
            grid=(nt,),
            in_specs=[pl.BlockSpec((1, 1, 2 * tm), lambda i, seg: (i, 0, 0), memory_space=pltpu.SMEM),
                      pl.BlockSpec((tm * nsl, LANES), lambda i, seg: (i, 0))],
            out_specs=pl.BlockSpec(memory_space=pl.ANY),
            scratch_shapes=[pltpu.VMEM((MOE_BLOCK * nsl, LANES), x_tiles.dtype), pltpu.SemaphoreType.DMA,
                            pltpu.SemaphoreType.DMA]),
        compiler_params=_params(("arbitrary",)),
        name="moe_dispatch",
    )(seg, dest_flat.reshape(nt, 1, 2 * tm), x_tiles)


def _expert_kernel(blk_ref, nvalid_ref, xs_ref, wg_ref, wu_ref, wd_ref, ys_ref, wgb, wub, wdb):
    i = pl.program_id(0)
    prev = blk_ref[jnp.maximum(i - 1, 0)]
    valid = i < nvalid_ref[0]

    @pl.when(valid & ((i == 0) | (blk_ref[i] != prev)))
    def _():
        wgb[...] = wg_ref[0].astype(_BF16)
        wub[...] = wu_ref[0].astype(_BF16)
        wdb[...] = wd_ref[0].astype(_BF16)

    @pl.when(valid)
    def _():
        d = wgb.shape[0]
        xb = _rows_from_tiles(xs_ref, MOE_BLOCK, d).astype(_BF16)
        gate = jnp.dot(xb, wgb[...], preferred_element_type=_F32)
        up = jnp.dot(xb, wub[...], preferred_element_type=_F32)
        mid = (gate * jax.nn.sigmoid(gate)) * up
        _rows_to_tiles(ys_ref, jnp.dot(mid.astype(_BF16), wdb[...], preferred_element_type=_F32))

    @pl.when(jnp.logical_not(valid))
    def _():
        ys_ref[...] = jnp.zeros_like(ys_ref)


def _experts(xs, blk_exp, nvalid, w_gate, w_up, w_down):
    d, de = w_gate.shape[1], w_gate.shape[2]
    nsl = d // LANES
    cap = xs.shape[0] // nsl
    nblk = cap // MOE_BLOCK

    def xmap(i, blk, nv):
        return (jnp.minimum(i, nv[0] - 1), 0)

    def wmap(i, blk, nv):
        return (blk[i], 0, 0)

    return pl.pallas_call(
        _expert_kernel,
        out_shape=jax.ShapeDtypeStruct((cap * nsl, LANES), _F32),
        grid_spec=pltpu.PrefetchScalarGridSpec(
            num_scalar_prefetch=2,
            grid=(nblk,),
            in_specs=[pl.BlockSpec((MOE_BLOCK * nsl, LANES), xmap),
                      pl.BlockSpec((1, d, de), wmap), pl.BlockSpec((1, d, de), wmap),
                      pl.BlockSpec((1, de, d), wmap)],
            out_specs=pl.BlockSpec((MOE_BLOCK * nsl, LANES), lambda i, blk, nv: (i, 0)),
            scratch_shapes=[pltpu.VMEM((d, de), _BF16), pltpu.VMEM((d, de), _BF16), pltpu.VMEM((de, d), _BF16)]),
        compiler_params=_params(("arbitrary",)),
        name="moe_experts",
    )(blk_exp, nvalid, xs, w_gate, w_up, w_down)


def _combine_kernel(dcur_ref, dnxt_ref, ys_hbm, h_ref, route_ref, g_ref, b_ref, o_ref, buf, sem):
    i = pl.program_id(0)
    n = pl.num_programs(0)
    tm = TOK_TM

    d = o_ref.shape[1]
    nsl = d // LANES

    def tile_copy(src_slot, slot, k, j):
        return pltpu.make_async_copy(ys_hbm.at[pl.ds(pl.multiple_of(src_slot * nsl, nsl), nsl)],
                                     buf.at[slot, k, pl.ds(pl.multiple_of(j * nsl, nsl), nsl)], sem.at[slot])

    def gather(dref, slot):
        def body(j, c):
            for k in range(2):
                tile_copy(dref[0, 0, 2 * j + k], slot, k, j).start()
            return c
        lax.fori_loop(0, tm, body, 0, unroll=4)

    @pl.when(i == 0)
    def _():
        gather(dcur_ref, 0)

    @pl.when(i + 1 < n)
    def _():
        gather(dnxt_ref, (i + 1) % 2)

    slot = i % 2

    def drain(j, c):
        tile_copy(0, slot, 0, 0).wait()
        return c

    lax.fori_loop(0, 2 * tm, drain, 0, unroll=8)
    r = route_ref[...]
    y = (r[:, 2:3] * _rows_from_tiles(buf.at[slot, 0], tm, d)
         + r[:, 3:4] * _rows_from_tiles(buf.at[slot, 1], tm, d))
    o_ref[...] = _layer_norm(ALPHA * _rows_from_tiles(h_ref, tm, d) + y, g_ref[...], b_ref[...])


def _combine(ys, dest_flat, h1_tiles, route, g, b):
    n = route.shape[0]
    d = g.shape[1]
    nsl = d // LANES
    tm = TOK_TM
    nt = n // tm
    dest3 = dest_flat.reshape(nt, 1, 2 * tm)
    row = lambda i: (i, 0)
    const = lambda i: (0, 0)
    return pl.pallas_call(
        _combine_kernel,
        out_shape=jax.ShapeDtypeStruct((n, d), _F32),
        grid=(nt,),
        in_specs=[pl.BlockSpec((1, 1, 2 * tm), lambda i: (i, 0, 0), memory_space=pltpu.SMEM),
                  pl.BlockSpec((1, 1, 2 * tm), lambda i: (jnp.minimum(i + 1, nt - 1), 0, 0),
                               memory_space=pltpu.SMEM),
                  pl.BlockSpec(memory_space=pl.ANY),
                  pl.BlockSpec((tm * nsl, LANES), row), pl.BlockSpec((tm, LANES), row),
                  pl.BlockSpec((1, d), const), pl.BlockSpec((1, d), const)],
        out_specs=pl.BlockSpec((tm, d), row),
        scratch_shapes=[pltpu.VMEM((2, 2, tm * nsl, LANES), _F32), pltpu.SemaphoreType.DMA((2,))],
        compiler_params=_params(("arbitrary",)),
        name="moe_combine",
    )(dest3, dest3, ys, h1_tiles, route, g, b)


def _moe(h1_tiles, route, counts_row, w_gate, w_up, w_down, g, b):
    n = route.shape[0]
    cap = 2 * n + N_EXPERTS * MOE_BLOCK
    counts = counts_row[0, ROUTE_LANE0:ROUTE_LANE0 + N_EXPERTS].astype(jnp.int32)
    padded = (counts + MOE_BLOCK - 1) // MOE_BLOCK * MOE_BLOCK
    ends = jnp.cumsum(padded)
    pad_start = ends - padded
    nblk = cap // MOE_BLOCK
    blk_first = jnp.arange(nblk, dtype=jnp.int32) * MOE_BLOCK
    blk_exp = jnp.minimum(jnp.sum(blk_first[:, None] >= ends[None, :], axis=1), N_EXPERTS - 1).astype(jnp.int32)
    nvalid = (ends[-1:] // MOE_BLOCK).astype(jnp.int32)
    ps_row = jnp.zeros((1, LANES), _F32).at[0, ROUTE_LANE0:ROUTE_LANE0 + N_EXPERTS].set(pad_start.astype(_F32))
    dest_flat = _dest(route, ps_row)[:, :2].reshape(2 * n)
    seg = jnp.concatenate([pad_start, padded, counts, nvalid]).astype(jnp.int32)
    xs = _dispatch(h1_tiles, dest_flat, seg, cap, n)
    ys = _experts(xs, blk_exp, nvalid, w_gate, w_up, w_down)
    return _combine(ys, dest_flat, h1_tiles, route, g, b)


def _router_params(rg_w, rg_b, re_w, re_b):
    d = rg_w.shape[0]
    pad = LANES - N_GROUPS - N_EXPERTS
    rw = jnp.concatenate([rg_w, re_w, jnp.zeros((d, pad), _F32)], axis=1)
    rb = jnp.concatenate([rg_b, re_b.reshape(-1), jnp.zeros((pad,), _F32)]).reshape(1, LANES)
    return rw, rb


def kernel(x, t5_table, l0_w_qkv, l0_sink, l0_w_o, l0_ln1_g, l0_ln1_b, l0_rg_w, l0_rg_b, l0_re_w, l0_re_b, l0_w_gate, l0_w_up, l0_w_down, l0_ln2_g, l0_ln2_b, l1_w_qkv, l1_w_o, l1_ln1_g, l1_ln1_b, l1_rg_w, l1_rg_b, l1_re_w, l1_re_b, l1_w_gate, l1_w_up, l1_w_down, l1_ln2_g, l1_ln2_b, l2_w_qkv, l2_rpb, l2_w_o, l2_ln1_g, l2_ln1_b, l2_rg_w, l2_rg_b, l2_re_w, l2_re_b, l2_w_gate, l2_w_up, l2_w_down, l2_ln2_g, l2_ln2_b, l3_w_qkv, l3_sink, l3_w_o, l3_ln1_g, l3_ln1_b, l3_rg_w, l3_rg_b, l3_re_w, l3_re_b, l3_w_gate, l3_w_up, l3_w_down, l3_ln2_g, l3_ln2_b):
    mixers = [(l0_w_qkv, l0_sink, l0_w_o), (l1_w_qkv, None, l1_w_o), (l2_w_qkv, l2_rpb, l2_w_o),
              (l3_w_qkv, l3_sink, l3_w_o)]
    norm1 = [(l0_ln1_g, l0_ln1_b), (l1_ln1_g, l1_ln1_b), (l2_ln1_g, l2_ln1_b), (l3_ln1_g, l3_ln1_b)]
    routers = [(l0_rg_w, l0_rg_b, l0_re_w, l0_re_b), (l1_rg_w, l1_rg_b, l1_re_w, l1_re_b),
               (l2_rg_w, l2_rg_b, l2_re_w, l2_re_b), (l3_rg_w, l3_rg_b, l3_re_w, l3_re_b)]
    experts = [(l0_w_gate, l0_w_up, l0_w_down), (l1_w_gate, l1_w_up, l1_w_down),
               (l2_w_gate, l2_w_up, l2_w_down), (l3_w_gate, l3_w_up, l3_w_down)]
    norm2 = [(l0_ln2_g, l0_ln2_b), (l1_ln2_g, l1_ln2_b), (l2_ln2_g, l2_ln2_b), (l3_ln2_g, l3_ln2_b)]

    bsz, seq, d = x.shape
    n = bsz * seq
    inner = N_HEADS * HEAD_DIM
    scale = HEAD_DIM ** -0.5 * LOG2E
    h = x.reshape(n, d)
    bias_a = _band_bias_t(t5_table, A_BLOCK, A_WINDOW, 1, A_BAND)
    for i in range(DEPTH):
        kind = i % N_MIXERS
        w_qkv, extra, w_o = mixers[i]
        kvw = A_KV_HEADS * HEAD_DIM if kind == 0 else inner
        q, k, v = _project(h, w_qkv.astype(_BF16), (inner, kvw, kvw), (scale, 1.0, 1.0))
        q, k, v = (t.reshape(bsz, seq, t.shape[1]) for t in (q, k, v))
        lses = ()
        if kind == 0:
            attn = [_mixer_a(q, k, v, extra.astype(_F32) * LOG2E, bias_a, bsz, seq)]
        elif kind == 1:
            attn, lses = [], []
            for window, dil in B_BRANCHES:
                o, lse = _mixer_b_branch(q, k, v, t5_table, bsz, seq, window, dil)
                attn.append(o)
                lses.append(jnp.pad(lse.reshape(n, N_HEADS), ((0, 0), (0, LANES - N_HEADS))))
        else:
            attn = [_mixer_c(q, k, v, _na_bias_t(extra, seq // GRID_W), bsz, seq)]
        attn = [a.reshape(n, inner) for a in attn]
        g1, b1 = norm1[i]
        rw, rb = _router_params(*routers[i])
        h1, route, counts = _post_attn(attn, lses, h, w_o.astype(_BF16), g1.reshape(1, d), b1.reshape(1, d), rw, rb)
        g2, b2 = norm2[i]
        h = _moe(h1, route, counts, *experts[i], g2.reshape(1, d), b2.reshape(1, d))
    return h.reshape(bsz, seq, d)
```

```python
import functools
import math

import jax
import jax.numpy as jnp
import numpy as np
from jax import lax
from jax.experimental import pallas as pl
from jax.experimental.pallas import tpu as pltpu

HEAD_DIM = 64
N_HEADS = 16
A_KV_HEADS = 4
A_WINDOW = 128
A_BLOCK = 128
B_BRANCHES = ((128, 1), (512, 4), (2048, 16))
GRID_W = 64
NA_ROWS = 8
NA_COLS = 16
T5_BUCKETS = 32
T5_MAX_DIST = 1024
N_GROUPS = 4
EXPERTS_PER_GROUP = 8
N_EXPERTS = N_GROUPS * EXPERTS_PER_GROUP
DEPTH = 4
N_MIXERS = 3
ALPHA = (2 * DEPTH) ** 0.25
LN_EPS = 1e-5
NEG_INF = -1e30

LANES = 128
PAIR = 2 * HEAD_DIM
BF16_SUBLANES = 16
VT_ROWS = PAIR + BF16_SUBLANES
LOG2E = math.log2(math.e)
LN2 = math.log(2.0)
VMEM_LIMIT = 56 * 1024 * 1024

PROJ_TM = 512
PROJ_CHUNK = 512
POST_TM = 1024
POST_SUB = 256
TOK_TM = 256
MOE_BLOCK = 512
A_TQ = 1024
A_BAND = 3
A_SUB_PER_GROUP = 2
B_BLOCK = 128
UNITS_PER_GROUP = 8
C_UNIT_ROWS = 2
C_BAND_ROWS = 10
C_UNITS_PER_GROUP = 4
C_ROWS_PER_STEP = 32
ROUTE_LANE0 = N_GROUPS

_F32 = jnp.float32
_BF16 = jnp.bfloat16


def _params(sem, vmem=VMEM_LIMIT, flags=None):
    return pltpu.CompilerParams(dimension_semantics=sem, vmem_limit_bytes=vmem, flags=flags)


def _proj_kernel(x_ref, w_ref, *o_refs, widths, scales):
    xb = x_ref[...].astype(_BF16)
    col = 0
    for o_ref, width, scale in zip(o_refs, widths, scales):
        for c in range(0, width, PROJ_CHUNK):
            cw = min(PROJ_CHUNK, width - c)
            acc = jnp.dot(xb, w_ref[:, col + c:col + c + cw], preferred_element_type=_F32)
            if scale != 1.0:
                acc = acc * scale
            o_ref[:, c:c + cw] = acc.astype(o_ref.dtype)
        col += width


def _project(x2, w_bf16, widths, scales):
    n, d = x2.shape
    ctot = sum(widths)
    return pl.pallas_call(
        functools.partial(_proj_kernel, widths=tuple(widths), scales=tuple(scales)),
        out_shape=[jax.ShapeDtypeStruct((n, w), _BF16) for w in widths],
        grid=(n // PROJ_TM,),
        in_specs=[pl.BlockSpec((PROJ_TM, d), lambda i: (i, 0)),
                  pl.BlockSpec((d, ctot), lambda i: (0, 0))],
        out_specs=[pl.BlockSpec((PROJ_TM, w), lambda i: (i, 0)) for w in widths],
        compiler_params=_params(("arbitrary",)),
        name="qkv_proj",
    )(x2, w_bf16)


def _one():
    return jnp.ones((1,), jnp.int32)


def _lane_lo(rows=1):
    return lax.broadcasted_iota(jnp.int32, (rows, PAIR), 1) < HEAD_DIM


def _swap_halves(x):
    return pltpu.roll(x.astype(_F32), HEAD_DIM, axis=1).astype(_BF16)


class _PairUnit:
    def __init__(self, q2, k, vt, bias, emit, sink_a=None, sink_b=None):
        self.q2, self.k, self.vt, self.bias, self.emit = q2, k, vt, bias, emit
        self.sink_a, self.sink_b = sink_a, sink_b


def _attend_groups(ngroups, units_of, scratch, one):
    assert ngroups % 2 == 0
    lo = _lane_lo()
    nt = (((1,), (1,)), ((), ()))

    def sink_row(unit, m2):
        lane = lax.broadcasted_iota(jnp.int32, (1, m2), 1)
        return jnp.where(lane < m2 // 2, unit.sink_a, unit.sink_b)

    def scores(g, parity):
        s_scr, m_scr = scratch[parity]
        for u, unit in enumerate(units_of(g, parity)):
            q2 = unit.q2()
            zero = jnp.zeros_like(q2)
            qq = jnp.concatenate([jnp.where(lo, q2, zero), jnp.where(lo, zero, q2)], axis=0)
            s = lax.dot_general(unit.k(), qq, nt, preferred_element_type=_F32) + unit.bias()
            m = jnp.max(s, axis=0, keepdims=True)
            if unit.sink_a is not None:
                m = jnp.maximum(m, sink_row(unit, s.shape[1]))
            s_scr[u] = s
            m_scr[u] = m

    def values(g, parity):
        s_scr, m_scr = scratch[parity]
        for u, unit in enumerate(units_of(g, parity)):
            m = m_scr[u]
            e = jnp.exp2(s_scr[u] - m)
            ot = jnp.dot(unit.vt(), e.astype(_BF16), preferred_element_type=_F32)
            den = ot[PAIR:PAIR + 1, :]
            if unit.sink_a is not None:
                den = den + jnp.exp2(sink_row(unit, e.shape[1]) - m)
            ot = ot[:PAIR] * (1.0 / den)
            mq = ot.shape[1] // 2
            top = lax.broadcasted_iota(jnp.int32, (PAIR, 1), 0) < HEAD_DIM
            pair_t = jnp.where(top, ot[:, :mq], ot[:, mq:])
            unit.emit(pair_t.T, (m + jnp.log2(den)) * LN2)

    def region(*work):
        def body(_, carry):
            for fn, g, parity in work:
                fn(g, parity)
            return carry
        lax.fori_loop(0, one, body, 0)

    region((scores, 0, 0))

    def steady(i, carry):
        g = 2 * i
        region((values, g, 0), (scores, g + 1, 1))
        region((values, g + 1, 1), (scores, g + 2, 0))
        return carry

    lax.fori_loop(0, ngroups // 2 - 1, steady, 0)
    region((values, ngroups - 2, 0), (scores, ngroups - 1, 1))
    region((values, ngroups - 1, 1))


def _transpose_blocks(v_ref_rows, nblk, store):
    def body(j, carry):
        rows = pl.ds(pl.multiple_of(j * LANES, LANES), LANES)
        vt = v_ref_rows(rows).astype(_F32).T.astype(_BF16)
        store(j, jnp.concatenate([vt, jnp.ones((VT_ROWS - PAIR, LANES), _BF16)], axis=0))
        return carry
    lax.fori_loop(0, nblk, body, 0)


def _t5_bucket(rel):
    half = T5_BUCKETS // 2
    exact = half // 2
    n = np.abs(rel)
    big = exact + (np.log(np.maximum(n, 1) / exact) / math.log(T5_MAX_DIST / exact) * (half - exact)).astype(np.int32)
    return ((rel > 0) * half + np.where(n < exact, n, np.minimum(big, half - 1))).astype(np.int32)


def _band_bias_t(t5_table, blk, window, dil, nband):
    tk = nband * blk
    off = tk - 1
    rel = np.arange(-off, tk)
    vec = jnp.take(t5_table, jnp.asarray(_t5_bucket(rel * dil)), axis=0).astype(_F32)
    vec = jnp.where(jnp.asarray(np.abs(rel) <= window)[:, None], vec * LOG2E, NEG_INF).T
    h = vec.shape[0]
    vec = jnp.pad(vec, ((0, 0), (0, 1)))
    w = 2 * tk - 1
    skew = jnp.tile(vec, (1, blk))[:, :blk * w].reshape(h, blk, w)
    t = jnp.stack([skew[:, :, off - v * blk:off - v * blk + tk] for v in range(nband)], axis=1)
    t = t.reshape(h // 2, 2, nband, blk, tk).transpose(0, 2, 4, 1, 3)
    return t.reshape(h // 2, nband, tk, 2 * blk)


def _band_window(gb, blk, length, nband):
    first = jnp.clip(gb - 1, 0, length // blk - nband)
    return first, gb - first


def _attn_a_kernel(sink_ref, one_ref, q_ref, k_ref, v_ref, bias_ref, o_ref, kd_ref, vdt_ref,
                   s0_scr, m0_scr, s1_scr, m1_scr, *, seq):
    kvb = pl.program_id(0)
    qt = pl.program_id(2)
    one = one_ref[0]
    nblk = seq // A_BLOCK

    @pl.when(qt == 0)
    def _():
        lo = _lane_lo()

        def fill(j, carry):
            rows = pl.ds(pl.multiple_of(j * A_BLOCK, A_BLOCK), A_BLOCK)
            kk = k_ref[0, rows, :]
            ks = _swap_halves(kk)
            kd_ref[0, rows, :] = jnp.where(lo, kk, ks)
            kd_ref[1, rows, :] = jnp.where(lo, ks, kk)
            vt = v_ref[0, rows, :].astype(_F32).T.astype(_BF16)
            ones = jnp.ones((VT_ROWS - PAIR, A_BLOCK), _BF16)
            vdt_ref[0, j] = jnp.concatenate([vt[:HEAD_DIM], vt[:HEAD_DIM], ones], axis=0)
            vdt_ref[1, j] = jnp.concatenate([vt[HEAD_DIM:], vt[HEAD_DIM:], ones], axis=0)
            return carry

        lax.fori_loop(0, nblk, fill, 0)

    nsub = A_TQ // A_BLOCK
    npair = q_ref.shape[2] // PAIR
    hbase = kvb * (2 * npair)

    def units_of(g, parity):
        units = []
        for j in range(A_SUB_PER_GROUP):
            si = g * A_SUB_PER_GROUP + j
            first, var = _band_window(qt * nsub + si, A_BLOCK, seq, A_BAND)
            rows = pl.ds(pl.multiple_of(si * A_BLOCK, A_BLOCK), A_BLOCK)
            band = pl.ds(pl.multiple_of(first * A_BLOCK, A_BLOCK), A_BAND * A_BLOCK)
            for jj in range(npair):
                hk = jj // (npair // 2)
                cols = slice(jj * PAIR, (jj + 1) * PAIR)

                def emit(o, lse, rows=rows, cols=cols):
                    o_ref[0, rows, cols] = o.astype(o_ref.dtype)

                units.append(_PairUnit(
                    q2=lambda rows=rows, cols=cols: q_ref[0, rows, cols],
                    k=lambda hk=hk, band=band: kd_ref[hk, band, :],
                    vt=lambda hk=hk, first=first: jnp.concatenate(
                        [vdt_ref[hk, first + t] for t in range(A_BAND)], axis=1),
                    bias=lambda jj=jj, var=var: bias_ref[jj, var],
                    emit=emit, sink_a=sink_ref[hbase + 2 * jj], sink_b=sink_ref[hbase + 2 * jj + 1]))
        return units

    _attend_groups(nsub // A_SUB_PER_GROUP, units_of, ((s0_scr, m0_scr), (s1_scr, m1_scr)), one)


def _mixer_a(q, k, v, sink, bias, bsz, seq):
    nkvb = k.shape[2] // PAIR
    qw = q.shape[2] // nkvb
    npair = qw // PAIR
    tk = A_BAND * A_BLOCK
    nunits = A_SUB_PER_GROUP * npair
    group_scratch = [pltpu.VMEM((nunits, tk, 2 * A_BLOCK), _F32), pltpu.VMEM((nunits, 1, 2 * A_BLOCK), _F32)]
    return pl.pallas_call(
        functools.partial(_attn_a_kernel, seq=seq),
        out_shape=jax.ShapeDtypeStruct(q.shape, _BF16),
        grid=(nkvb, bsz, seq // A_TQ),
        in_specs=[pl.BlockSpec(memory_space=pltpu.SMEM), pl.BlockSpec(memory_space=pltpu.SMEM),
                  pl.BlockSpec((1, A_TQ, qw), lambda c, b, t: (b, t, c)),
                  pl.BlockSpec((1, seq, PAIR), lambda c, b, t: (b, 0, c)),
                  pl.BlockSpec((1, seq, PAIR), lambda c, b, t: (b, 0, c)),
                  pl.BlockSpec((npair, A_BAND, tk, 2 * A_BLOCK), lambda c, b, t: (c, 0, 0, 0))],
        out_specs=pl.BlockSpec((1, A_TQ, qw), lambda c, b, t: (b, t, c)),
        scratch_shapes=[pltpu.VMEM((2, seq, PAIR), _BF16), pltpu.VMEM((2, seq // A_BLOCK, VT_ROWS, A_BLOCK), _BF16)]
        + group_scratch + group_scratch,
        compiler_params=_params(("arbitrary", "arbitrary", "arbitrary")),
        name="mixer_a",
    )(sink, _one(), q, k, v, bias)


def _attn_band_kernel(one_ref, q_ref, k_ref, v_ref, bias_ref, o_ref, lse_ref, vt_ref,
                      s0_scr, m0_scr, s1_scr, m1_scr, *, length, npairs, nband):
    col = pl.program_id(2)
    one = one_ref[0]
    nblk = length // B_BLOCK

    for p in range(npairs):
        def store(j, blk, p=p):
            vt_ref[p, j] = blk
        _transpose_blocks(lambda rows, p=p: v_ref[0, rows, p * PAIR:(p + 1) * PAIR], nblk, store)

    def unit(gb, p):
        first, var = _band_window(gb, B_BLOCK, length, nband)
        rows = pl.ds(pl.multiple_of(gb * B_BLOCK, B_BLOCK), B_BLOCK)
        band = pl.ds(pl.multiple_of(first * B_BLOCK, B_BLOCK), nband * B_BLOCK)
        cols = slice(p * PAIR, (p + 1) * PAIR)

        def emit(o, lse):
            o_ref[0, rows, cols] = o.astype(o_ref.dtype)
            ha = col * (2 * npairs) + 2 * p
            lse_ref[0, 0, gb, pl.ds(ha, 1), :] = lse[:, :B_BLOCK]
            lse_ref[0, 0, gb, pl.ds(ha + 1, 1), :] = lse[:, B_BLOCK:]

        return _PairUnit(
            q2=lambda: q_ref[0, rows, cols], k=lambda: k_ref[0, band, cols],
            vt=lambda: jnp.concatenate([vt_ref[p, first + t] for t in range(nband)], axis=1),
            bias=lambda: bias_ref[p, var], emit=emit)

    if npairs == 1:
        ngroups = nblk // UNITS_PER_GROUP
        units_of = lambda g, parity: [unit(g * UNITS_PER_GROUP + j, 0) for j in range(UNITS_PER_GROUP)]
    else:
        assert npairs == UNITS_PER_GROUP
        ngroups = nblk
        units_of = lambda g, parity: [unit(g, p) for p in range(npairs)]
    _attend_groups(ngroups, units_of, ((s0_scr, m0_scr), (s1_scr, m1_scr)), one)


def _mixer_b_branch(q, k, v, t5_table, bsz, seq, window, dil):
    length = seq // dil
    width = q.shape[2]
    nblk = length // B_BLOCK
    nband = min(3, nblk)
    tk = nband * B_BLOCK
    npairs = (width // PAIR) if length * width * 2 <= 2 * 1024 * 1024 else 1
    ncol = width // (npairs * PAIR)
    bw = npairs * PAIR
    bias = _band_bias_t(t5_table, B_BLOCK, window // 2 // dil, dil, nband)

    def view(t):
        return t.reshape(bsz, length, dil * width)

    def cmap(b, r, c):
        return (b, 0, r * ncol + c)

    o, lse = pl.pallas_call(
        functools.partial(_attn_band_kernel, length=length, npairs=npairs, nband=nband),
        out_shape=[jax.ShapeDtypeStruct((bsz, length, dil * width), _BF16),
                   jax.ShapeDtypeStruct((bsz, dil, nblk, N_HEADS, B_BLOCK), _F32)],
        grid=(bsz, dil, ncol),
        in_specs=[pl.BlockSpec(memory_space=pltpu.SMEM),
                  pl.BlockSpec((1, length, bw), cmap),
                  pl.BlockSpec((1, length, bw), cmap),
                  pl.BlockSpec((1, length, bw), cmap),
                  pl.BlockSpec((npairs, nband, tk, 2 * B_BLOCK), lambda b, r, c: (c, 0, 0, 0))],
        out_specs=[pl.BlockSpec((1, length, bw), cmap),
                   pl.BlockSpec((1, 1, nblk, N_HEADS, B_BLOCK), lambda b, r, c: (b, r, 0, 0, 0))],
        scratch_shapes=[pltpu.VMEM((npairs, nblk, VT_ROWS, B_BLOCK), _BF16)] + 2 * [
            pltpu.VMEM((UNITS_PER_GROUP, tk, 2 * B_BLOCK), _F32), pltpu.VMEM((UNITS_PER_GROUP, 1, 2 * B_BLOCK), _F32)],
        compiler_params=_params(("arbitrary", "arbitrary", "arbitrary")),
        name=f"mixer_b_dil{dil}",
    )(_one(), view(q), view(k), view(v), bias)
    lse = lse.transpose(0, 2, 4, 1, 3).reshape(bsz, seq, N_HEADS)
    return o.reshape(bsz, seq, width), lse


def _na_bias_t(rpb, rows):
    canon = (0, 2, 4, rows - 4, rows - 2)
    h = rpb.shape[0]
    kc = np.arange(GRID_W)[:, None]
    qc = np.arange(GRID_W)[None, :]
    ws = np.clip(qc - NA_COLS // 2, 0, GRID_W - NA_COLS)
    col_ok = (kc >= ws) & (kc < ws + NA_COLS)
    col_idx = np.clip(kc - qc, 1 - NA_COLS, NA_COLS - 1) + NA_COLS - 1
    colpart = jnp.where(jnp.asarray(col_ok), rpb.astype(_F32)[:, :, jnp.asarray(col_idx)] * LOG2E, NEG_INF)
    masked = jnp.full((h, GRID_W, GRID_W), NEG_INF, _F32)
    variants = []
    for i in canon:
        start = int(np.clip(i - NA_ROWS // 2, 0, rows - C_BAND_ROWS))
        per_key_row = []
        for kr in range(C_BAND_ROWS):
            per_query_row = []
            for qr in range(C_UNIT_ROWS):
                iq, kabs = i + qr, start + kr
                first_key_row = int(np.clip(iq - NA_ROWS // 2, 0, rows - NA_ROWS))
                inside = first_key_row <= kabs < first_key_row + NA_ROWS
                per_query_row.append(colpart[:, kabs - iq + NA_ROWS - 1] if inside else masked)
            per_key_row.append(jnp.stack(per_query_row, axis=2))
        variants.append(jnp.stack(per_key_row, axis=1))
    b = jnp.stack(variants, axis=1)
    nq, nk = C_UNIT_ROWS * GRID_W, C_BAND_ROWS * GRID_W
    b = b.reshape(h // 2, 2, len(canon), nk, nq)
    return b.transpose(0, 2, 3, 1, 4).reshape(h // 2, len(canon), nk, 2 * nq)


def _attn_c_kernel(one_ref, q_ref, k_ref, v_ref, bias_ref, o_ref, vt_ref, s0_scr, m0_scr, s1_scr, m1_scr, *, rows):
    rt = pl.program_id(2)
    nq = C_UNIT_ROWS * GRID_W
    band_blocks = C_BAND_ROWS * GRID_W // LANES

    @pl.when(rt == 0)
    def _():
        def store(j, blk):
            vt_ref[j] = blk
        _transpose_blocks(lambda r: v_ref[0, r, :], rows * GRID_W // LANES, store)

    def unit(u):
        i = rt * C_ROWS_PER_STEP + C_UNIT_ROWS * u
        start = jnp.clip(i - NA_ROWS // 2, 0, rows - C_BAND_ROWS)
        var = lax.shift_right_logical(i - start, 1)
        first = lax.shift_right_logical(start, 1)
        keys = pl.ds(pl.multiple_of(first * LANES, LANES), C_BAND_ROWS * GRID_W)
        qrows = pl.ds(pl.multiple_of(u * nq, nq), nq)

        def emit(o, lse):
            o_ref[0, qrows, :] = o.astype(o_ref.dtype)

        return _PairUnit(
            q2=lambda: q_ref[0, qrows, :], k=lambda: k_ref[0, keys, :],
            vt=lambda: jnp.concatenate([vt_ref[first + t] for t in range(band_blocks)], axis=1),
            bias=lambda: bias_ref[0, var], emit=emit)

    ngroups = C_ROWS_PER_STEP // C_UNIT_ROWS // C_UNITS_PER_GROUP
    units_of = lambda g, parity: [unit(g * C_UNITS_PER_GROUP + j) for j in range(C_UNITS_PER_GROUP)]
    _attend_groups(ngroups, units_of, ((s0_scr, m0_scr), (s1_scr, m1_scr)), one_ref[0])


def _mixer_c(q, k, v, bias, bsz, seq):
    rows = seq // GRID_W
    tq = C_ROWS_PER_STEP * GRID_W
    npair = q.shape[2] // PAIR
    nq, nk = C_UNIT_ROWS * GRID_W, C_BAND_ROWS * GRID_W
    return pl.pallas_call(
        functools.partial(_attn_c_kernel, rows=rows),
        out_shape=jax.ShapeDtypeStruct(q.shape, _BF16),
        grid=(npair, bsz, seq // tq),
        in_specs=[pl.BlockSpec(memory_space=pltpu.SMEM),
                  pl.BlockSpec((1, tq, PAIR), lambda c, b, t: (b, t, c)),
                  pl.BlockSpec((1, seq, PAIR), lambda c, b, t: (b, 0, c)),
                  pl.BlockSpec((1, seq, PAIR), lambda c, b, t: (b, 0, c)),
                  pl.BlockSpec((1, bias.shape[1], nk, 2 * nq), lambda c, b, t: (c, 0, 0, 0))],
        out_specs=pl.BlockSpec((1, tq, PAIR), lambda c, b, t: (b, t, c)),
        scratch_shapes=[pltpu.VMEM((seq // LANES, VT_ROWS, LANES), _BF16)] + 2 * [
            pltpu.VMEM((C_UNITS_PER_GROUP, nk, 2 * nq), _F32), pltpu.VMEM((C_UNITS_PER_GROUP, 1, 2 * nq), _F32)],
        compiler_params=_params(("arbitrary", "arbitrary", "arbitrary")),
        name="mixer_c",
    )(_one(), q, k, v, bias)


def _layer_norm(z, g, b):
    mu = jnp.mean(z, axis=-1, keepdims=True)
    zc = z - mu
    var = jnp.mean(zc * zc, axis=-1, keepdims=True)
    return zc * lax.rsqrt(var + LN_EPS) * g + b


def _rows_to_tiles(ref, x):
    nrows, d = x.shape
    nsl = d // LANES
    for s in range(nsl):
        ref[pl.ds(s, nrows, stride=nsl), :] = x[:, s * LANES:(s + 1) * LANES]


def _rows_from_tiles(ref, nrows, d):
    nsl = d // LANES
    return jnp.concatenate([ref[pl.ds(s, nrows, stride=nsl), :] for s in range(nsl)], axis=1)


def _split_bf16(x):
    hi = x.astype(_BF16)
    return hi, (x - hi.astype(_F32)).astype(_BF16)


def _router_logits(hn, rw_ref, rb_ref):
    h_hi, h_lo = _split_bf16(hn)
    w_hi, w_lo = _split_bf16(rw_ref[...])
    return (jnp.dot(h_hi, w_hi, preferred_element_type=_F32)
            + (jnp.dot(h_hi, w_lo, preferred_element_type=_F32)
               + jnp.dot(h_lo, w_hi, preferred_element_type=_F32))) + rb_ref[...]


def _route(logits, tri_ref, carry_ref):
    tm = logits.shape[0]
    lane = lax.broadcasted_iota(jnp.int32, (tm, LANES), 1).astype(_F32)
    ninf = -jnp.inf
    is_grp = lane < N_GROUPS
    lg = jnp.where(is_grp, logits, ninf)
    gmax = jnp.max(lg, axis=-1, keepdims=True)
    grp = jnp.min(jnp.where(lg == gmax, lane, float(LANES)), axis=-1, keepdims=True)
    p_grp = 1.0 / jnp.sum(jnp.where(is_grp, jnp.exp(lg - gmax), 0.0), axis=-1, keepdims=True)
    lo_lane = ROUTE_LANE0 + EXPERTS_PER_GROUP * grp
    in_grp = (lane >= lo_lane) & (lane < lo_lane + EXPERTS_PER_GROUP)
    le = jnp.where(in_grp, logits, ninf)
    v1 = jnp.max(le, axis=-1, keepdims=True)
    i1 = jnp.min(jnp.where(le == v1, lane, float(LANES)), axis=-1, keepdims=True)
    le2 = jnp.where(lane == i1, ninf, le)
    v2 = jnp.max(le2, axis=-1, keepdims=True)
    i2 = jnp.min(jnp.where(le2 == v2, lane, float(LANES)), axis=-1, keepdims=True)
    t = jnp.exp(v2 - v1)
    s1 = 1.0 / (1.0 + t)
    g1 = p_grp * s1
    g2 = p_grp * (t * s1)
    sel1 = lane == i1
    sel2 = lane == i2
    onehot = jnp.where(sel1 | sel2, 1.0, 0.0)
    sub = tri_ref.shape[0]
    carry = carry_ref[...]
    parts = []
    for r0 in range(0, tm, sub):
        part = onehot[r0:r0 + sub]
        parts.append(jnp.dot(tri_ref[...], part.astype(_BF16), preferred_element_type=_F32) + carry)
        carry = carry + jnp.sum(part, axis=0, keepdims=True)
    carry_ref[...] = carry
    before = jnp.concatenate(parts, axis=0) if len(parts) > 1 else parts[0]
    rank1 = jnp.sum(jnp.where(sel1, before, 0.0), axis=-1, keepdims=True)
    rank2 = jnp.sum(jnp.where(sel2, before, 0.0), axis=-1, keepdims=True)
    out = jnp.zeros((tm, LANES), _F32)
    for idx, val in enumerate((i1 - ROUTE_LANE0, i2 - ROUTE_LANE0, g1, g2, rank1, rank2)):
        out = jnp.where(lane == idx, val, out)
    return out


def _post_attn_kernel(*refs, n_branch):
    attn_refs = refs[:n_branch]
    pos = n_branch
    if n_branch > 1:
        lse_refs = refs[pos:pos + n_branch]
        expand_ref = refs[pos + n_branch]
        pos += n_branch + 1
    wo_ref, h_ref, g_ref, b_ref, rw_ref, rb_ref, tri_ref = refs[pos:pos + 7]
    h1_ref, route_ref, counts_ref, carry_ref = refs[pos + 7:pos + 11]

    @pl.when(pl.program_id(0) == 0)
    def _():
        carry_ref[...] = jnp.zeros_like(carry_ref)

    d = h_ref.shape[1]
    logits = []
    for sub in range(h_ref.shape[0] // POST_SUB):
        rows = slice(sub * POST_SUB, (sub + 1) * POST_SUB)
        if n_branch == 1:
            a = attn_refs[0][rows, :]
        else:
            lses = [r[rows, :] for r in lse_refs]
            mx = functools.reduce(jnp.maximum, lses)
            es = [jnp.exp(l - mx) for l in lses]
            inv = 1.0 / functools.reduce(jnp.add, es)
            mixed = None
            for e, a_ref in zip(es, attn_refs):
                w_hi, w_lo = _split_bf16(e * inv)
                w_full = (jnp.dot(w_hi, expand_ref[...], preferred_element_type=_F32)
                          + jnp.dot(w_lo, expand_ref[...], preferred_element_type=_F32))
                term = w_full * a_ref[rows, :].astype(_F32)
                mixed = term if mixed is None else mixed + term
            a = mixed.astype(_BF16)
        y = jnp.dot(a, wo_ref[...], preferred_element_type=_F32)
        hn = _layer_norm(ALPHA * h_ref[rows, :] + y, g_ref[...], b_ref[...])
        nsl = d // LANES
        _rows_to_tiles(h1_ref.at[pl.ds(sub * POST_SUB * nsl, POST_SUB * nsl)], hn)
        logits.append(_router_logits(hn, rw_ref, rb_ref))
    route_ref[...] = _route(jnp.concatenate(logits, axis=0), tri_ref, carry_ref)
    counts_ref[...] = carry_ref[...]


def _post_attn(attn, lses, h2, w_o, g, b, rw, rb):
    n, d = h2.shape
    tm = POST_TM
    nb = len(attn)
    tri = jnp.asarray(np.tril(np.ones((POST_SUB, POST_SUB), np.float32), -1), _BF16)
    row = lambda i: (i, 0)
    const = lambda i: (0, 0)
    args = list(attn)
    specs = [pl.BlockSpec((tm, d), row) for _ in attn]
    if nb > 1:
        expand = np.zeros((LANES, d), np.float32)
        for hd in range(N_HEADS):
            expand[hd, hd * HEAD_DIM:(hd + 1) * HEAD_DIM] = 1.0
        args += list(lses) + [jnp.asarray(expand, _BF16)]
        specs += [pl.BlockSpec((tm, LANES), row) for _ in lses] + [pl.BlockSpec((LANES, d), const)]
    args += [w_o, h2, g, b, rw, rb, tri]
    specs += [pl.BlockSpec((d, d), const), pl.BlockSpec((tm, d), row), pl.BlockSpec((1, d), const),
              pl.BlockSpec((1, d), const), pl.BlockSpec((d, LANES), const), pl.BlockSpec((1, LANES), const),
              pl.BlockSpec((POST_SUB, POST_SUB), const)]
    return pl.pallas_call(
        functools.partial(_post_attn_kernel, n_branch=nb),
        out_shape=[jax.ShapeDtypeStruct((n * d // LANES, LANES), _F32), jax.ShapeDtypeStruct((n, LANES), _F32),
                   jax.ShapeDtypeStruct((1, LANES), _F32)],
        grid=(n // tm,),
        in_specs=specs,
        out_specs=[pl.BlockSpec((tm * d // LANES, LANES), row), pl.BlockSpec((tm, LANES), row),
                   pl.BlockSpec((1, LANES), const)],
        scratch_shapes=[pltpu.VMEM((1, LANES), _F32)],
        compiler_params=_params(("arbitrary",)),
        name="post_attn_router",
    )(*args)


def _dest_kernel(route_ref, ps_ref, o_ref):
    r = route_ref[...]
    lane = lax.broadcasted_iota(jnp.int32, r.shape, 1)
    lanef = lane.astype(_F32)
    ps = ps_ref[...]
    d = []
    for k in range(2):
        e = r[:, k:k + 1] + float(ROUTE_LANE0)
        d.append(jnp.sum(jnp.where(lanef == e, ps, 0.0), axis=-1, keepdims=True) + r[:, 4 + k:5 + k])
    o_ref[...] = jnp.where(lane == 0, d[0], jnp.where(lane == 1, d[1], 0.0)).astype(jnp.int32)


def _dest(route, pad_start_row):
    n = route.shape[0]
    tm = 1024
    return pl.pallas_call(
        _dest_kernel,
        out_shape=jax.ShapeDtypeStruct((n, LANES), jnp.int32),
        grid=(n // tm,),
        in_specs=[pl.BlockSpec((tm, LANES), lambda i: (i, 0)), pl.BlockSpec((1, LANES), lambda i: (0, 0))],
        out_specs=pl.BlockSpec((tm, LANES), lambda i: (i, 0)),
        compiler_params=_params(("arbitrary",)),
        name="moe_dest",
    )(route, pad_start_row)


def _dispatch_kernel(seg_ref, dest_ref, x_ref, xs_hbm, zero_ref, zsem, sem):
    i = pl.program_id(0)
    ntok = TOK_TM
    nsl = x_ref.shape[0] // ntok
    blk_rows = MOE_BLOCK * nsl

    def zero_fill(e):
        first = pl.multiple_of((seg_ref[e] + seg_ref[N_EXPERTS + e] - MOE_BLOCK) * nsl, blk_rows)
        return pltpu.make_async_copy(zero_ref, xs_hbm.at[pl.ds(first, blk_rows)], zsem)

    def has_pad(e):
        return seg_ref[N_EXPERTS + e] > seg_ref[2 * N_EXPERTS + e]

    @pl.when(i == 0)
    def _():
        zero_ref[...] = jnp.zeros_like(zero_ref)

        def start(e, c):
            @pl.when(has_pad(e))
            def _():
                zero_fill(e).start()
            return c

        def wait(e, c):
            @pl.when(has_pad(e))
            def _():
                zero_fill(e).wait()
            return c

        lax.fori_loop(0, N_EXPERTS, start, 0)
        lax.fori_loop(0, N_EXPERTS, wait, 0)

        nvalid = seg_ref[3 * N_EXPERTS]
        nblk = xs_hbm.shape[0] // blk_rows

        def tail(j):
            first = pl.multiple_of((nvalid + j) * blk_rows, blk_rows)
            return pltpu.make_async_copy(zero_ref, xs_hbm.at[pl.ds(first, blk_rows)], zsem)

        def tail_start(j, c):
            @pl.when(nvalid + j < nblk)
            def _():
                tail(j).start()
            return c

        def tail_wait(j, c):
            @pl.when(nvalid + j < nblk)
            def _():
                tail(j).wait()
            return c

        lax.fori_loop(0, N_EXPERTS, tail_start, 0)
        lax.fori_loop(0, N_EXPERTS, tail_wait, 0)

    def row_copy(j, slot):
        return pltpu.make_async_copy(x_ref.at[pl.ds(pl.multiple_of(j * nsl, nsl), nsl)],
                                     xs_hbm.at[pl.ds(pl.multiple_of(slot * nsl, nsl), nsl)], sem)

    def issue(j, c):
        for k in range(2):
            row_copy(j, dest_ref[0, 0, 2 * j + k]).start(priority=k)
        return c

    def drain(j, c):
        row_copy(0, 0).wait()
        return c

    lax.fori_loop(0, ntok, issue, 0, unroll=4)
    lax.fori_loop(0, 2 * ntok, drain, 0, unroll=8)


def _dispatch(x_tiles, dest_flat, seg, cap, n):
    nsl = x_tiles.shape[0] // n
    tm = TOK_TM
    nt = n // tm
    return pl.pallas_call(
        _dispatch_kernel,
        out_shape=jax.ShapeDtypeStruct((cap * nsl, LANES), x_tiles.dtype),
        grid_spec=pltpu.PrefetchScalarGridSpec(
            num_scalar_prefetch=1,
            grid=(nt,),
            in_specs=[pl.BlockSpec((1, 1, 2 * tm), lambda i, seg: (i, 0, 0), memory_space=pltpu.SMEM),
                      pl.BlockSpec((tm * nsl, LANES), lambda i, seg: (i, 0))],
            out_specs=pl.BlockSpec(memory_space=pl.ANY),
            scratch_shapes=[pltpu.VMEM((MOE_BLOCK * nsl, LANES), x_tiles.dtype), pltpu.SemaphoreType.DMA,
                            pltpu.SemaphoreType.DMA]),
        compiler_params=_params(("arbitrary",)),
        name="moe_dispatch",
    )(seg, dest_flat.reshape(nt, 1, 2 * tm), x_tiles)


def _expert_kernel(blk_ref, nvalid_ref, xs_ref, wg_ref, wu_ref, wd_ref, ys_ref, wgb, wub, wdb):
    i = pl.program_id(0)
    prev = blk_ref[jnp.maximum(i - 1, 0)]
    valid = i < nvalid_ref[0]

    @pl.when(valid & ((i == 0) | (blk_ref[i] != prev)))
    def _():
        wgb[...] = wg_ref[0].astype(_BF16)
        wub[...] = wu_ref[0].astype(_BF16)
        wdb[...] = wd_ref[0].astype(_BF16)

    @pl.when(valid)
    def _():
        d = wgb.shape[0]
        xb = _rows_from_tiles(xs_ref, MOE_BLOCK, d).astype(_BF16)
        gate = jnp.dot(xb, wgb[...], preferred_element_type=_F32)
        up = jnp.dot(xb, wub[...], preferred_element_type=_F32)
        mid = (gate * jax.nn.sigmoid(gate)) * up
        _rows_to_tiles(ys_ref, jnp.dot(mid.astype(_BF16), wdb[...], preferred_element_type=_F32))

    @pl.when(jnp.logical_not(valid))
    def _():
        ys_ref[...] = jnp.zeros_like(ys_ref)


def _experts(xs, blk_exp, nvalid, w_gate, w_up, w_down):
    d, de = w_gate.shape[1], w_gate.shape[2]
    nsl = d // LANES
    cap = xs.shape[0] // nsl
    nblk = cap // MOE_BLOCK

    def xmap(i, blk, nv):
        return (jnp.minimum(i, nv[0] - 1), 0)

    def wmap(i, blk, nv):
        return (blk[i], 0, 0)

    return pl.pallas_call(
        _expert_kernel,
        out_shape=jax.ShapeDtypeStruct((cap * nsl, LANES), _F32),
        grid_spec=pltpu.PrefetchScalarGridSpec(
            num_scalar_prefetch=2,
            grid=(nblk,),
            in_specs=[pl.BlockSpec((MOE_BLOCK * nsl, LANES), xmap),
                      pl.BlockSpec((1, d, de), wmap), pl.BlockSpec((1, d, de), wmap),
                      pl.BlockSpec((1, de, d), wmap)],
            out_specs=pl.BlockSpec((MOE_BLOCK * nsl, LANES), lambda i, blk, nv: (i, 0)),
            scratch_shapes=[pltpu.VMEM((d, de), _BF16), pltpu.VMEM((d, de), _BF16), pltpu.VMEM((de, d), _BF16)]),
        compiler_params=_params(("arbitrary",)),
        name="moe_experts",
    )(blk_exp, nvalid, xs, w_gate, w_up, w_down)


def _combine_kernel(dcur_ref, dnxt_ref, ys_hbm, h_ref, route_ref, g_ref, b_ref, o_ref, buf, sem):
    i = pl.program_id(0)
    n = pl.num_programs(0)
    tm = TOK_TM

    d = o_ref.shape[1]
    nsl = d // LANES

    def tile_copy(src_slot, slot, k, j):
        return pltpu.make_async_copy(ys_hbm.at[pl.ds(pl.multiple_of(src_slot * nsl, nsl), nsl)],
                                     buf.at[slot, k, pl.ds(pl.multiple_of(j * nsl, nsl), nsl)], sem.at[slot])

    def gather(dref, slot):
        def body(j, c):
            for k in range(2):
                tile_copy(dref[0, 0, 2 * j + k], slot, k, j).start(priority=k)
            return c
        lax.fori_loop(0, tm, body, 0, unroll=4)

    @pl.when(i == 0)
    def _():
        gather(dcur_ref, 0)

    @pl.when(i + 1 < n)
    def _():
        gather(dnxt_ref, (i + 1) % 2)

    slot = i % 2

    def drain(j, c):
        tile_copy(0, slot, 0, 0).wait()
        return c

    lax.fori_loop(0, 2 * tm, drain, 0, unroll=8)
    r = route_ref[...]
    y = (r[:, 2:3] * _rows_from_tiles(buf.at[slot, 0], tm, d)
         + r[:, 3:4] * _rows_from_tiles(buf.at[slot, 1], tm, d))
    o_ref[...] = _layer_norm(ALPHA * _rows_from_tiles(h_ref, tm, d) + y, g_ref[...], b_ref[...])


def _combine(ys, dest_flat, h1_tiles, route, g, b):
    n = route.shape[0]
    d = g.shape[1]
    nsl = d // LANES
    tm = TOK_TM
    nt = n // tm
    dest3 = dest_flat.reshape(nt, 1, 2 * tm)
    row = lambda i: (i, 0)
    const = lambda i: (0, 0)
    return pl.pallas_call(
        _combine_kernel,
        out_shape=jax.ShapeDtypeStruct((n, d), _F32),
        grid=(nt,),
        in_specs=[pl.BlockSpec((1, 1, 2 * tm), lambda i: (i, 0, 0), memory_space=pltpu.SMEM),
                  pl.BlockSpec((1, 1, 2 * tm), lambda i: (jnp.minimum(i + 1, nt - 1), 0, 0),
                               memory_space=pltpu.SMEM),
                  pl.BlockSpec(memory_space=pl.ANY),
                  pl.BlockSpec((tm * nsl, LANES), row), pl.BlockSpec((tm, LANES), row),
                  pl.BlockSpec((1, d), const), pl.BlockSpec((1, d), const)],
        out_specs=pl.BlockSpec((tm, d), row),
        scratch_shapes=[pltpu.VMEM((2, 2, tm * nsl, LANES), _F32), pltpu.SemaphoreType.DMA((2,))],
        compiler_params=_params(("arbitrary",)),
        name="moe_combine",
    )(dest3, dest3, ys, h1_tiles, route, g, b)


def _moe(h1_tiles, route, counts_row, w_gate, w_up, w_down, g, b):
    n = route.shape[0]
    cap = 2 * n + N_EXPERTS * MOE_BLOCK
    counts = counts_row[0, ROUTE_LANE0:ROUTE_LANE0 + N_EXPERTS].astype(jnp.int32)
    padded = (counts + MOE_BLOCK - 1) // MOE_BLOCK * MOE_BLOCK
    ends = jnp.cumsum(padded)
    pad_start = ends - padded
    nblk = cap // MOE_BLOCK
    blk_first = jnp.arange(nblk, dtype=jnp.int32) * MOE_BLOCK
    blk_exp = jnp.minimum(jnp.sum(blk_first[:, None] >= ends[None, :], axis=1), N_EXPERTS - 1).astype(jnp.int32)
    nvalid = (ends[-1:] // MOE_BLOCK).astype(jnp.int32)
    ps_row = jnp.zeros((1, LANES), _F32).at[0, ROUTE_LANE0:ROUTE_LANE0 + N_EXPERTS].set(pad_start.astype(_F32))
    dest_flat = _dest(route, ps_row)[:, :2].reshape(2 * n)
    seg = jnp.concatenate([pad_start, padded, counts, nvalid]).astype(jnp.int32)
    xs = _dispatch(h1_tiles, dest_flat, seg, cap, n)
    ys = _experts(xs, blk_exp, nvalid, w_gate, w_up, w_down)
    return _combine(ys, dest_flat, h1_tiles, route, g, b)


def _router_params(rg_w, rg_b, re_w, re_b):
    d = rg_w.shape[0]
    pad = LANES - N_GROUPS - N_EXPERTS
    rw = jnp.concatenate([rg_w, re_w, jnp.zeros((d, pad), _F32)], axis=1)
    rb = jnp.concatenate([rg_b, re_b.reshape(-1), jnp.zeros((pad,), _F32)]).reshape(1, LANES)
    return rw, rb


def kernel(x, t5_table, l0_w_qkv, l0_sink, l0_w_o, l0_ln1_g, l0_ln1_b, l0_rg_w, l0_rg_b, l0_re_w, l0_re_b, l0_w_gate, l0_w_up, l0_w_down, l0_ln2_g, l0_ln2_b, l1_w_qkv, l1_w_o, l1_ln1_g, l1_ln1_b, l1_rg_w, l1_rg_b, l1_re_w, l1_re_b, l1_w_gate, l1_w_up, l1_w_down, l1_ln2_g, l1_ln2_b, l2_w_qkv, l2_rpb, l2_w_o, l2_ln1_g, l2_ln1_b, l2_rg_w, l2_rg_b, l2_re_w, l2_re_b, l2_w_gate, l2_w_up, l2_w_down, l2_ln2_g, l2_ln2_b, l3_w_qkv, l3_sink, l3_w_o, l3_ln1_g, l3_ln1_b, l3_rg_w, l3_rg_b, l3_re_w, l3_re_b, l3_w_gate, l3_w_up, l3_w_down, l3_ln2_g, l3_ln2_b):
    mixers = [(l0_w_qkv, l0_sink, l0_w_o), (l1_w_qkv, None, l1_w_o), (l2_w_qkv, l2_rpb, l2_w_o),
              (l3_w_qkv, l3_sink, l3_w_o)]
    norm1 = [(l0_ln1_g, l0_ln1_b), (l1_ln1_g, l1_ln1_b), (l2_ln1_g, l2_ln1_b), (l3_ln1_g, l3_ln1_b)]
    routers = [(l0_rg_w, l0_rg_b, l0_re_w, l0_re_b), (l1_rg_w, l1_rg_b, l1_re_w, l1_re_b),
               (l2_rg_w, l2_rg_b, l2_re_w, l2_re_b), (l3_rg_w, l3_rg_b, l3_re_w, l3_re_b)]
    experts = [(l0_w_gate, l0_w_up, l0_w_down), (l1_w_gate, l1_w_up, l1_w_down),
               (l2_w_gate, l2_w_up, l2_w_down), (l3_w_gate, l3_w_up, l3_w_down)]
    norm2 = [(l0_ln2_g, l0_ln2_b), (l1_ln2_g, l1_ln2_b), (l2_ln2_g, l2_ln2_b), (l3_ln2_g, l3_ln2_b)]

    bsz, seq, d = x.shape
    n = bsz * seq
    inner = N_HEADS * HEAD_DIM
    scale = HEAD_DIM ** -0.5 * LOG2E
    h = x.reshape(n, d)
    bias_a = _band_bias_t(t5_table, A_BLOCK, A_WINDOW, 1, A_BAND)
    for i in range(DEPTH):
        kind = i % N_MIXERS
        w_qkv, extra, w_o = mixers[i]
        kvw = A_KV_HEADS * HEAD_DIM if kind == 0 else inner
        q, k, v = _project(h, w_qkv.astype(_BF16), (inner, kvw, kvw), (scale, 1.0, 1.0))
        q, k, v = (t.reshape(bsz, seq, t.shape[1]) for t in (q, k, v))
        lses = ()
        if kind == 0:
            attn = [_mixer_a(q, k, v, extra.astype(_F32) * LOG2E, bias_a, bsz, seq)]
        elif kind == 1:
            attn, lses = [], []
            for window, dil in B_BRANCHES:
                o, lse = _mixer_b_branch(q, k, v, t5_table, bsz, seq, window, dil)
                attn.append(o)
                lses.append(jnp.pad(lse.reshape(n, N_HEADS), ((0, 0), (0, LANES - N_HEADS))))
        else:
            attn = [_mixer_c(q, k, v, _na_bias_t(extra, seq // GRID_W), bsz, seq)]
        attn = [a.reshape(n, inner) for a in attn]
        g1, b1 = norm1[i]
        rw, rb = _router_params(*routers[i])
        h1, route, counts = _post_attn(attn, lses, h, w_o.astype(_BF16), g1.reshape(1, d), b1.reshape(1, d), rw, rb)
        g2, b2 = norm2[i]
        h = _moe(h1, route, counts, *experts[i], g2.reshape(1, d), b2.reshape(1, d))
    return h.reshape(bsz, seq, d)
```

```python
import functools
import math

import jax
import jax.numpy as jnp
import numpy as np
from jax import lax
from jax.experimental import pallas as pl
from jax.experimental.pallas import tpu as pltpu

HEAD_DIM = 64
N_HEADS = 16
A_KV_HEADS = 4
A_WINDOW = 128
A_BLOCK = 128
B_BRANCHES = ((128, 1), (512, 4), (2048, 16))
GRID_W = 64
NA_ROWS = 8
NA_COLS = 16
T5_BUCKETS = 32
T5_MAX_DIST = 1024
N_GROUPS = 4
EXPERTS_PER_GROUP = 8
N_EXPERTS = N_GROUPS * EXPERTS_PER_GROUP
DEPTH = 4
N_MIXERS = 3
ALPHA = (2 * DEPTH) ** 0.25
LN_EPS = 1e-5
NEG_INF = -1e30

LANES = 128
PAIR = 2 * HEAD_DIM
BF16_SUBLANES = 16
VT_ROWS = PAIR + BF16_SUBLANES
LOG2E = math.log2(math.e)
LN2 = math.log(2.0)
VMEM_LIMIT = 56 * 1024 * 1024

PROJ_TM = 512
PROJ_CHUNK = 512
POST_TM = 1024
POST_SUB = 256
TOK_TM = 256
MOE_BLOCK = 512
A_TQ = 1024
A_BAND = 3
A_SUB_PER_GROUP = 2
B_BLOCK = 128
RESIDUE_RUN = BF16_SUBLANES
PERM_CHUNK = 256
UNITS_PER_GROUP = 8
C_UNIT_ROWS = 2
C_BAND_ROWS = 10
C_UNITS_PER_GROUP = 4
C_ROWS_PER_STEP = 32
ROUTE_LANE0 = N_GROUPS

_F32 = jnp.float32
_BF16 = jnp.bfloat16


def _params(sem, vmem=VMEM_LIMIT, flags=None):
    return pltpu.CompilerParams(dimension_semantics=sem, vmem_limit_bytes=vmem, flags=flags)


def _proj_kernel(x_ref, w_ref, *refs, widths, scales, nperm):
    perm_ref = refs[0] if nperm else None
    o_refs = refs[1:] if nperm else refs
    nout = len(widths)
    xb = x_ref[...].astype(_BF16)
    col = 0
    for t, (width, scale) in enumerate(zip(widths, scales)):
        for c in range(0, width, PROJ_CHUNK):
            cw = min(PROJ_CHUNK, width - c)
            acc = jnp.dot(xb, w_ref[:, col + c:col + c + cw], preferred_element_type=_F32)
            if scale != 1.0:
                acc = acc * scale
            ob = acc.astype(_BF16)
            o_refs[t][:, c:c + cw] = ob
            for p in range(nperm):
                for r0 in range(0, ob.shape[0], PERM_CHUNK):
                    moved = jnp.dot(perm_ref[p], ob[r0:r0 + PERM_CHUNK], preferred_element_type=_F32)
                    o_refs[(p + 1) * nout + t][r0:r0 + PERM_CHUNK, c:c + cw] = moved.astype(_BF16)
        col += width


def _project(x2, w_bf16, widths, scales, perms=None):
    n, d = x2.shape
    ctot = sum(widths)
    nperm = 0 if perms is None else perms.shape[0]
    args = [x2, w_bf16] + ([perms] if nperm else [])
    specs = [pl.BlockSpec((PROJ_TM, d), lambda i: (i, 0)), pl.BlockSpec((d, ctot), lambda i: (0, 0))]
    if nperm:
        specs.append(pl.BlockSpec(perms.shape, lambda i: (0, 0, 0)))
    return pl.pallas_call(
        functools.partial(_proj_kernel, widths=tuple(widths), scales=tuple(scales), nperm=nperm),
        out_shape=[jax.ShapeDtypeStruct((n, w), _BF16) for w in widths] * (nperm + 1),
        grid=(n // PROJ_TM,),
        in_specs=specs,
        out_specs=[pl.BlockSpec((PROJ_TM, w), lambda i: (i, 0)) for w in widths] * (nperm + 1),
        compiler_params=_params(("arbitrary",)),
        name="qkv_proj",
    )(*args)


def _residue_perm(dil):
    span = dil * RESIDUE_RUN
    p = np.zeros((PERM_CHUNK, PERM_CHUNK), np.float32)
    for base in range(0, PERM_CHUNK, span):
        for r in range(dil):
            for u in range(RESIDUE_RUN):
                p[base + r * RESIDUE_RUN + u, base + u * dil + r] = 1.0
    return p


def _one():
    return jnp.ones((1,), jnp.int32)


def _lane_lo(rows=1):
    return lax.broadcasted_iota(jnp.int32, (rows, PAIR), 1) < HEAD_DIM


def _swap_halves(x):
    return pltpu.roll(x.astype(_F32), HEAD_DIM, axis=1).astype(_BF16)


class _PairUnit:
    def __init__(self, q2, k, vt, bias, emit, sink_a=None, sink_b=None):
        self.q2, self.k, self.vt, self.bias, self.emit = q2, k, vt, bias, emit
        self.sink_a, self.sink_b = sink_a, sink_b


def _attend_groups(ngroups, units_of, scratch, one):
    assert ngroups % 2 == 0
    lo = _lane_lo()
    nt = (((1,), (1,)), ((), ()))

    def sink_row(unit, m2):
        lane = lax.broadcasted_iota(jnp.int32, (1, m2), 1)
        return jnp.where(lane < m2 // 2, unit.sink_a, unit.sink_b)

    def scores(g, parity):
        s_scr, m_scr = scratch[parity]
        for u, unit in enumerate(units_of(g, parity)):
            q2 = unit.q2()
            zero = jnp.zeros_like(q2)
            qq = jnp.concatenate([jnp.where(lo, q2, zero), jnp.where(lo, zero, q2)], axis=0)
            s = lax.dot_general(unit.k(), qq, nt, preferred_element_type=_F32) + unit.bias()
            m = jnp.max(s, axis=0, keepdims=True)
            if unit.sink_a is not None:
                m = jnp.maximum(m, sink_row(unit, s.shape[1]))
            s_scr[u] = s
            m_scr[u] = m

    def values(g, parity):
        s_scr, m_scr = scratch[parity]
        for u, unit in enumerate(units_of(g, parity)):
            m = m_scr[u]
            e = jnp.exp2(s_scr[u] - m)
            ot = jnp.dot(unit.vt(), e.astype(_BF16), preferred_element_type=_F32)
            den = ot[PAIR:PAIR + 1, :]
            if unit.sink_a is not None:
                den = den + jnp.exp2(sink_row(unit, e.shape[1]) - m)
            ot = ot[:PAIR] * (1.0 / den)
            mq = ot.shape[1] // 2
            top = lax.broadcasted_iota(jnp.int32, (PAIR, 1), 0) < HEAD_DIM
            pair_t = jnp.where(top, ot[:, :mq], ot[:, mq:])
            unit.emit(pair_t.T, (m + jnp.log2(den)) * LN2)

    def region(*work):
        def body(_, carry):
            for fn, g, parity in work:
                fn(g, parity)
            return carry
        lax.fori_loop(0, one, body, 0)

    region((scores, 0, 0))

    def steady(i, carry):
        g = 2 * i
        region((values, g, 0), (scores, g + 1, 1))
        region((values, g + 1, 1), (scores, g + 2, 0))
        return carry

    lax.fori_loop(0, ngroups // 2 - 1, steady, 0)
    region((values, ngroups - 2, 0), (scores, ngroups - 1, 1))
    region((values, ngroups - 1, 1))


def _transpose_blocks(v_ref_rows, nblk, store):
    def body(j, carry):
        rows = pl.ds(pl.multiple_of(j * LANES, LANES), LANES)
        vt = v_ref_rows(rows).astype(_F32).T.astype(_BF16)
        store(j, jnp.concatenate([vt, jnp.ones((VT_ROWS - PAIR, LANES), _BF16)], axis=0))
        return carry
    lax.fori_loop(0, nblk, body, 0)


def _t5_bucket(rel):
    half = T5_BUCKETS // 2
    exact = half // 2
    n = np.abs(rel)
    big = exact + (np.log(np.maximum(n, 1) / exact) / math.log(T5_MAX_DIST / exact) * (half - exact)).astype(np.int32)
    return ((rel > 0) * half + np.where(n < exact, n, np.minimum(big, half - 1))).astype(np.int32)


def _band_bias_t(t5_table, blk, window, dil, nband):
    tk = nband * blk
    off = tk - 1
    rel = np.arange(-off, tk)
    vec = jnp.take(t5_table, jnp.asarray(_t5_bucket(rel * dil)), axis=0).astype(_F32)
    vec = jnp.where(jnp.asarray(np.abs(rel) <= window)[:, None], vec * LOG2E, NEG_INF).T
    h = vec.shape[0]
    vec = jnp.pad(vec, ((0, 0), (0, 1)))
    w = 2 * tk - 1
    skew = jnp.tile(vec, (1, blk))[:, :blk * w].reshape(h, blk, w)
    t = jnp.stack([skew[:, :, off - v * blk:off - v * blk + tk] for v in range(nband)], axis=1)
    t = t.reshape(h // 2, 2, nband, blk, tk).transpose(0, 2, 4, 1, 3)
    return t.reshape(h // 2, nband, tk, 2 * blk)


def _band_window(gb, blk, length, nband):
    first = jnp.clip(gb - 1, 0, length // blk - nband)
    return first, gb - first


def _attn_a_kernel(sink_ref, one_ref, q_ref, k_ref, v_ref, bias_ref, o_ref, kd_ref, vdt_ref,
                   s0_scr, m0_scr, s1_scr, m1_scr, *, seq):
    kvb = pl.program_id(0)
    qt = pl.program_id(2)
    one = one_ref[0]
    nblk = seq // A_BLOCK

    @pl.when(qt == 0)
    def _():
        lo = _lane_lo()

        def fill(j, carry):
            rows = pl.ds(pl.multiple_of(j * A_BLOCK, A_BLOCK), A_BLOCK)
            kk = k_ref[0, rows, :]
            ks = _swap_halves(kk)
            kd_ref[0, rows, :] = jnp.where(lo, kk, ks)
            kd_ref[1, rows, :] = jnp.where(lo, ks, kk)
            vt = v_ref[0, rows, :].astype(_F32).T.astype(_BF16)
            ones = jnp.ones((VT_ROWS - PAIR, A_BLOCK), _BF16)
            vdt_ref[0, j] = jnp.concatenate([vt[:HEAD_DIM], vt[:HEAD_DIM], ones], axis=0)
            vdt_ref[1, j] = jnp.concatenate([vt[HEAD_DIM:], vt[HEAD_DIM:], ones], axis=0)
            return carry

        lax.fori_loop(0, nblk, fill, 0)

    nsub = A_TQ // A_BLOCK
    npair = q_ref.shape[2] // PAIR
    hbase = kvb * (2 * npair)

    def units_of(g, parity):
        units = []
        for j in range(A_SUB_PER_GROUP):
            si = g * A_SUB_PER_GROUP + j
            first, var = _band_window(qt * nsub + si, A_BLOCK, seq, A_BAND)
            rows = pl.ds(pl.multiple_of(si * A_BLOCK, A_BLOCK), A_BLOCK)
            band = pl.ds(pl.multiple_of(first * A_BLOCK, A_BLOCK), A_BAND * A_BLOCK)
            for jj in range(npair):
                hk = jj // (npair // 2)
                cols = slice(jj * PAIR, (jj + 1) * PAIR)

                def emit(o, lse, rows=rows, cols=cols):
                    o_ref[0, rows, cols] = o.astype(o_ref.dtype)

                units.append(_PairUnit(
                    q2=lambda rows=rows, cols=cols: q_ref[0, rows, cols],
                    k=lambda hk=hk, band=band: kd_ref[hk, band, :],
                    vt=lambda hk=hk, first=first: jnp.concatenate(
                        [vdt_ref[hk, first + t] for t in range(A_BAND)], axis=1),
                    bias=lambda jj=jj, var=var: bias_ref[jj, var],
                    emit=emit, sink_a=sink_ref[hbase + 2 * jj], sink_b=sink_ref[hbase + 2 * jj + 1]))
        return units

    _attend_groups(nsub // A_SUB_PER_GROUP, units_of, ((s0_scr, m0_scr), (s1_scr, m1_scr)), one)


def _mixer_a(q, k, v, sink, bias, bsz, seq):
    nkvb = k.shape[2] // PAIR
    qw = q.shape[2] // nkvb
    npair = qw // PAIR
    tk = A_BAND * A_BLOCK
    nunits = A_SUB_PER_GROUP * npair
    group_scratch = [pltpu.VMEM((nunits, tk, 2 * A_BLOCK), _F32), pltpu.VMEM((nunits, 1, 2 * A_BLOCK), _F32)]
    return pl.pallas_call(
        functools.partial(_attn_a_kernel, seq=seq),
        out_shape=jax.ShapeDtypeStruct(q.shape, _BF16),
        grid=(nkvb, bsz, seq // A_TQ),
        in_specs=[pl.BlockSpec(memory_space=pltpu.SMEM), pl.BlockSpec(memory_space=pltpu.SMEM),
                  pl.BlockSpec((1, A_TQ, qw), lambda c, b, t: (b, t, c)),
                  pl.BlockSpec((1, seq, PAIR), lambda c, b, t: (b, 0, c)),
                  pl.BlockSpec((1, seq, PAIR), lambda c, b, t: (b, 0, c)),
                  pl.BlockSpec((npair, A_BAND, tk, 2 * A_BLOCK), lambda c, b, t: (c, 0, 0, 0))],
        out_specs=pl.BlockSpec((1, A_TQ, qw), lambda c, b, t: (b, t, c)),
        scratch_shapes=[pltpu.VMEM((2, seq, PAIR), _BF16), pltpu.VMEM((2, seq // A_BLOCK, VT_ROWS, A_BLOCK), _BF16)]
        + group_scratch + group_scratch,
        compiler_params=_params(("arbitrary", "arbitrary", "arbitrary")),
        name="mixer_a",
    )(sink, _one(), q, k, v, bias)


def _attn_band_kernel(one_ref, q_ref, k_ref, v_ref, bias_ref, o_ref, lse_ref, vt_ref,
                      s0_scr, m0_scr, s1_scr, m1_scr, *, length, npairs, nband, chunked):
    col = pl.program_id(2)
    one = one_ref[0]
    nblk = length // B_BLOCK
    runs = B_BLOCK // RESIDUE_RUN

    def load(ref, blk0, nblocks, cols):
        if chunked:
            r0 = pl.multiple_of(blk0 * runs, runs)
            return ref[pl.ds(r0, nblocks * runs), :, cols].reshape(nblocks * B_BLOCK, PAIR)
        return ref[0, pl.ds(pl.multiple_of(blk0 * B_BLOCK, B_BLOCK), nblocks * B_BLOCK), cols]

    def store_out(blk0, cols, val):
        if chunked:
            o_ref[pl.ds(pl.multiple_of(blk0 * runs, runs), runs), :, cols] = val.reshape(runs, RESIDUE_RUN, PAIR)
        else:
            o_ref[0, pl.ds(pl.multiple_of(blk0 * B_BLOCK, B_BLOCK), B_BLOCK), cols] = val

    for p in range(npairs):
        pcols = slice(p * PAIR, (p + 1) * PAIR)

        def fill(j, carry, p=p, pcols=pcols):
            vt = load(v_ref, j, 1, pcols).astype(_F32).T.astype(_BF16)
            vt_ref[p, j] = jnp.concatenate([vt, jnp.ones((VT_ROWS - PAIR, B_BLOCK), _BF16)], axis=0)
            return carry

        lax.fori_loop(0, nblk, fill, 0)

    def unit(gb, p):
        first, var = _band_window(gb, B_BLOCK, length, nband)
        cols = slice(p * PAIR, (p + 1) * PAIR)

        def emit(o, lse):
            store_out(gb, cols, o.astype(o_ref.dtype))
            ha = col * (2 * npairs) + 2 * p
            lse_ref[0, 0, gb, pl.ds(ha, 1), :] = lse[:, :B_BLOCK]
            lse_ref[0, 0, gb, pl.ds(ha + 1, 1), :] = lse[:, B_BLOCK:]

        return _PairUnit(
            q2=lambda: load(q_ref, gb, 1, cols), k=lambda: load(k_ref, first, nband, cols),
            vt=lambda: jnp.concatenate([vt_ref[p, first + t] for t in range(nband)], axis=1),
            bias=lambda: bias_ref[p, var], emit=emit)

    if npairs == 1:
        ngroups = nblk // UNITS_PER_GROUP
        units_of = lambda g, parity: [unit(g * UNITS_PER_GROUP + j, 0) for j in range(UNITS_PER_GROUP)]
    else:
        assert npairs == UNITS_PER_GROUP
        ngroups = nblk
        units_of = lambda g, parity: [unit(g, p) for p in range(npairs)]
    _attend_groups(ngroups, units_of, ((s0_scr, m0_scr), (s1_scr, m1_scr)), one)


def _mixer_b_branch(q, k, v, t5_table, bsz, seq, window, dil):
    length = seq // dil
    width = q.shape[1]
    nblk = length // B_BLOCK
    nband = min(3, nblk)
    tk = nband * B_BLOCK
    npairs = (width // PAIR) if length * width * 2 <= 2 * 1024 * 1024 else 1
    ncol = width // (npairs * PAIR)
    bw = npairs * PAIR
    bias = _band_bias_t(t5_table, B_BLOCK, window // 2 // dil, dil, nband)
    chunked = dil > 1
    if chunked:
        nrun = length // RESIDUE_RUN
        shape = (bsz * nrun, dil, RESIDUE_RUN, width)
        spec = pl.BlockSpec((nrun, None, RESIDUE_RUN, bw), lambda b, r, c: (b, r, 0, c))
    else:
        shape = (bsz, seq, width)
        spec = pl.BlockSpec((1, seq, bw), lambda b, r, c: (b, 0, c))

    o, lse = pl.pallas_call(
        functools.partial(_attn_band_kernel, length=length, npairs=npairs, nband=nband, chunked=chunked),
        out_shape=[jax.ShapeDtypeStruct(shape, _BF16),
                   jax.ShapeDtypeStruct((bsz, dil, nblk, N_HEADS, B_BLOCK), _F32)],
        grid=(bsz, dil, ncol),
        in_specs=[pl.BlockSpec(memory_space=pltpu.SMEM), spec, spec, spec,
                  pl.BlockSpec((npairs, nband, tk, 2 * B_BLOCK), lambda b, r, c: (c, 0, 0, 0))],
        out_specs=[spec, pl.BlockSpec((1, 1, nblk, N_HEADS, B_BLOCK), lambda b, r, c: (b, r, 0, 0, 0))],
        scratch_shapes=[pltpu.VMEM((npairs, nblk, VT_ROWS, B_BLOCK), _BF16)] + 2 * [
            pltpu.VMEM((UNITS_PER_GROUP, tk, 2 * B_BLOCK), _F32), pltpu.VMEM((UNITS_PER_GROUP, 1, 2 * B_BLOCK), _F32)],
        compiler_params=_params(("arbitrary", "arbitrary", "arbitrary")),
        name=f"mixer_b_dil{dil}",
    )(_one(), q.reshape(shape), k.reshape(shape), v.reshape(shape), bias)
    lse = lse.transpose(0, 2, 4, 1, 3).reshape(bsz, seq, N_HEADS)
    return o.reshape(bsz * seq, width), lse


def _na_bias_t(rpb, rows):
    canon = (0, 2, 4, rows - 4, rows - 2)
    h = rpb.shape[0]
    kc = np.arange(GRID_W)[:, None]
    qc = np.arange(GRID_W)[None, :]
    ws = np.clip(qc - NA_COLS // 2, 0, GRID_W - NA_COLS)
    col_ok = (kc >= ws) & (kc < ws + NA_COLS)
    col_idx = np.clip(kc - qc, 1 - NA_COLS, NA_COLS - 1) + NA_COLS - 1
    colpart = jnp.where(jnp.asarray(col_ok), rpb.astype(_F32)[:, :, jnp.asarray(col_idx)] * LOG2E, NEG_INF)
    masked = jnp.full((h, GRID_W, GRID_W), NEG_INF, _F32)
    variants = []
    for i in canon:
        start = int(np.clip(i - NA_ROWS // 2, 0, rows - C_BAND_ROWS))
        per_key_row = []
        for kr in range(C_BAND_ROWS):
            per_query_row = []
            for qr in range(C_UNIT_ROWS):
                iq, kabs = i + qr, start + kr
                first_key_row = int(np.clip(iq - NA_ROWS // 2, 0, rows - NA_ROWS))
                inside = first_key_row <= kabs < first_key_row + NA_ROWS
                per_query_row.append(colpart[:, kabs - iq + NA_ROWS - 1] if inside else masked)
            per_key_row.append(jnp.stack(per_query_row, axis=2))
        variants.append(jnp.stack(per_key_row, axis=1))
    b = jnp.stack(variants, axis=1)
    nq, nk = C_UNIT_ROWS * GRID_W, C_BAND_ROWS * GRID_W
    b = b.reshape(h // 2, 2, len(canon), nk, nq)
    return b.transpose(0, 2, 3, 1, 4).reshape(h // 2, len(canon), nk, 2 * nq)


def _attn_c_kernel(one_ref, q_ref, k_ref, v_ref, bias_ref, o_ref, vt_ref, s0_scr, m0_scr, s1_scr, m1_scr, *, rows):
    rt = pl.program_id(2)
    nq = C_UNIT_ROWS * GRID_W
    band_blocks = C_BAND_ROWS * GRID_W // LANES

    @pl.when(rt == 0)
    def _():
        def store(j, blk):
            vt_ref[j] = blk
        _transpose_blocks(lambda r: v_ref[0, r, :], rows * GRID_W // LANES, store)

    def unit(u):
        i = rt * C_ROWS_PER_STEP + C_UNIT_ROWS * u
        start = jnp.clip(i - NA_ROWS // 2, 0, rows - C_BAND_ROWS)
        var = lax.shift_right_logical(i - start, 1)
        first = lax.shift_right_logical(start, 1)
        keys = pl.ds(pl.multiple_of(first * LANES, LANES), C_BAND_ROWS * GRID_W)
        qrows = pl.ds(pl.multiple_of(u * nq, nq), nq)

        def emit(o, lse):
            o_ref[0, qrows, :] = o.astype(o_ref.dtype)

        return _PairUnit(
            q2=lambda: q_ref[0, qrows, :], k=lambda: k_ref[0, keys, :],
            vt=lambda: jnp.concatenate([vt_ref[first + t] for t in range(band_blocks)], axis=1),
            bias=lambda: bias_ref[0, var], emit=emit)

    ngroups = C_ROWS_PER_STEP // C_UNIT_ROWS // C_UNITS_PER_GROUP
    units_of = lambda g, parity: [unit(g * C_UNITS_PER_GROUP + j) for j in range(C_UNITS_PER_GROUP)]
    _attend_groups(ngroups, units_of, ((s0_scr, m0_scr), (s1_scr, m1_scr)), one_ref[0])


def _mixer_c(q, k, v, bias, bsz, seq):
    rows = seq // GRID_W
    tq = C_ROWS_PER_STEP * GRID_W
    npair = q.shape[2] // PAIR
    nq, nk = C_UNIT_ROWS * GRID_W, C_BAND_ROWS * GRID_W
    return pl.pallas_call(
        functools.partial(_attn_c_kernel, rows=rows),
        out_shape=jax.ShapeDtypeStruct(q.shape, _BF16),
        grid=(npair, bsz, seq // tq),
        in_specs=[pl.BlockSpec(memory_space=pltpu.SMEM),
                  pl.BlockSpec((1, tq, PAIR), lambda c, b, t: (b, t, c)),
                  pl.BlockSpec((1, seq, PAIR), lambda c, b, t: (b, 0, c)),
                  pl.BlockSpec((1, seq, PAIR), lambda c, b, t: (b, 0, c)),
                  pl.BlockSpec((1, bias.shape[1], nk, 2 * nq), lambda c, b, t: (c, 0, 0, 0))],
        out_specs=pl.BlockSpec((1, tq, PAIR), lambda c, b, t: (b, t, c)),
        scratch_shapes=[pltpu.VMEM((seq // LANES, VT_ROWS, LANES), _BF16)] + 2 * [
            pltpu.VMEM((C_UNITS_PER_GROUP, nk, 2 * nq), _F32), pltpu.VMEM((C_UNITS_PER_GROUP, 1, 2 * nq), _F32)],
        compiler_params=_params(("arbitrary", "arbitrary", "arbitrary")),
        name="mixer_c",
    )(_one(), q, k, v, bias)


def _layer_norm(z, g, b):
    mu = jnp.mean(z, axis=-1, keepdims=True)
    zc = z - mu
    var = jnp.mean(zc * zc, axis=-1, keepdims=True)
    return zc * lax.rsqrt(var + LN_EPS) * g + b


def _rows_to_tiles(ref, x):
    nrows, d = x.shape
    nsl = d // LANES
    for s in range(nsl):
        ref[pl.ds(s, nrows, stride=nsl), :] = x[:, s * LANES:(s + 1) * LANES]


def _rows_from_tiles(ref, nrows, d):
    nsl = d // LANES
    return jnp.concatenate([ref[pl.ds(s, nrows, stride=nsl), :] for s in range(nsl)], axis=1)


def _split_bf16(x):
    hi = x.astype(_BF16)
    return hi, (x - hi.astype(_F32)).astype(_BF16)


def _router_logits(hn, rw_ref, rb_ref):
    h_hi, h_lo = _split_bf16(hn)
    w_hi, w_lo = _split_bf16(rw_ref[...])
    return (jnp.dot(h_hi, w_hi, preferred_element_type=_F32)
            + (jnp.dot(h_hi, w_lo, preferred_element_type=_F32)
               + jnp.dot(h_lo, w_hi, preferred_element_type=_F32))) + rb_ref[...]


def _route(logits, tri_ref, carry_ref):
    tm = logits.shape[0]
    lane = lax.broadcasted_iota(jnp.int32, (tm, LANES), 1).astype(_F32)
    ninf = -jnp.inf
    is_grp = lane < N_GROUPS
    lg = jnp.where(is_grp, logits, ninf)
    gmax = jnp.max(lg, axis=-1, keepdims=True)
    grp = jnp.min(jnp.where(lg == gmax, lane, float(LANES)), axis=-1, keepdims=True)
    p_grp = 1.0 / jnp.sum(jnp.where(is_grp, jnp.exp(lg - gmax), 0.0), axis=-1, keepdims=True)
    lo_lane = ROUTE_LANE0 + EXPERTS_PER_GROUP * grp
    in_grp = (lane >= lo_lane) & (lane < lo_lane + EXPERTS_PER_GROUP)
    le = jnp.where(in_grp, logits, ninf)
    v1 = jnp.max(le, axis=-1, keepdims=True)
    i1 = jnp.min(jnp.where(le == v1, lane, float(LANES)), axis=-1, keepdims=True)
    le2 = jnp.where(lane == i1, ninf, le)
    v2 = jnp.max(le2, axis=-1, keepdims=True)
    i2 = jnp.min(jnp.where(le2 == v2, lane, float(LANES)), axis=-1, keepdims=True)
    t = jnp.exp(v2 - v1)
    s1 = 1.0 / (1.0 + t)
    g1 = p_grp * s1
    g2 = p_grp * (t * s1)
    sel1 = lane == i1
    sel2 = lane == i2
    onehot = jnp.where(sel1 | sel2, 1.0, 0.0)
    sub = tri_ref.shape[0]
    carry = carry_ref[...]
    parts = []
    for r0 in range(0, tm, sub):
        part = onehot[r0:r0 + sub]
        parts.append(jnp.dot(tri_ref[...], part.astype(_BF16), preferred_element_type=_F32) + carry)
        carry = carry + jnp.sum(part, axis=0, keepdims=True)
    carry_ref[...] = carry
    before = jnp.concatenate(parts, axis=0) if len(parts) > 1 else parts[0]
    rank1 = jnp.sum(jnp.where(sel1, before, 0.0), axis=-1, keepdims=True)
    rank2 = jnp.sum(jnp.where(sel2, before, 0.0), axis=-1, keepdims=True)
    out = jnp.zeros((tm, LANES), _F32)
    for idx, val in enumerate((i1 - ROUTE_LANE0, i2 - ROUTE_LANE0, g1, g2, rank1, rank2)):
        out = jnp.where(lane == idx, val, out)
    return out


def _post_attn_kernel(*refs, n_branch):
    attn_refs = refs[:n_branch]
    pos = n_branch
    if n_branch > 1:
        lse_refs = refs[pos:pos + n_branch]
        expand_ref, unperm_ref = refs[pos + n_branch:pos + n_branch + 2]
        pos += n_branch + 2
    wo_ref, h_ref, g_ref, b_ref, rw_ref, rb_ref, tri_ref = refs[pos:pos + 7]
    h1_ref, route_ref, counts_ref, carry_ref = refs[pos + 7:pos + 11]

    @pl.when(pl.program_id(0) == 0)
    def _():
        carry_ref[...] = jnp.zeros_like(carry_ref)

    d = h_ref.shape[1]
    logits = []
    for sub in range(h_ref.shape[0] // POST_SUB):
        rows = slice(sub * POST_SUB, (sub + 1) * POST_SUB)
        if n_branch == 1:
            a = attn_refs[0][rows, :]
        else:
            lses = [r[rows, :] for r in lse_refs]
            mx = functools.reduce(jnp.maximum, lses)
            es = [jnp.exp(l - mx) for l in lses]
            inv = 1.0 / functools.reduce(jnp.add, es)
            mixed = None
            for bi, (e, a_ref) in enumerate(zip(es, attn_refs)):
                w_hi, w_lo = _split_bf16(e * inv)
                w_full = (jnp.dot(w_hi, expand_ref[...], preferred_element_type=_F32)
                          + jnp.dot(w_lo, expand_ref[...], preferred_element_type=_F32))
                if bi == 0:
                    a_nat = a_ref[rows, :].astype(_F32)
                else:
                    a_nat = jnp.dot(unperm_ref[bi - 1], a_ref[rows, :], preferred_element_type=_F32)
                term = w_full * a_nat
                mixed = term if mixed is None else mixed + term
            a = mixed.astype(_BF16)
        y = jnp.dot(a, wo_ref[...], preferred_element_type=_F32)
        hn = _layer_norm(ALPHA * h_ref[rows, :] + y, g_ref[...], b_ref[...])
        nsl = d // LANES
        _rows_to_tiles(h1_ref.at[pl.ds(sub * POST_SUB * nsl, POST_SUB * nsl)], hn)
        logits.append(_router_logits(hn, rw_ref, rb_ref))
    route_ref[...] = _route(jnp.concatenate(logits, axis=0), tri_ref, carry_ref)
    counts_ref[...] = carry_ref[...]


def _post_attn(attn, lses, h2, w_o, g, b, rw, rb, unperm=None):
    n, d = h2.shape
    tm = POST_TM
    nb = len(attn)
    tri = jnp.asarray(np.tril(np.ones((POST_SUB, POST_SUB), np.float32), -1), _BF16)
    row = lambda i: (i, 0)
    const = lambda i: (0, 0)
    args = list(attn)
    specs = [pl.BlockSpec((tm, d), row) for _ in attn]
    if nb > 1:
        expand = np.zeros((LANES, d), np.float32)
        for hd in range(N_HEADS):
            expand[hd, hd * HEAD_DIM:(hd + 1) * HEAD_DIM] = 1.0
        args += list(lses) + [jnp.asarray(expand, _BF16), unperm]
        specs += [pl.BlockSpec((tm, LANES), row) for _ in lses] + [
            pl.BlockSpec((LANES, d), const), pl.BlockSpec(unperm.shape, lambda i: (0, 0, 0))]
    args += [w_o, h2, g, b, rw, rb, tri]
    specs += [pl.BlockSpec((d, d), const), pl.BlockSpec((tm, d), row), pl.BlockSpec((1, d), const),
              pl.BlockSpec((1, d), const), pl.BlockSpec((d, LANES), const), pl.BlockSpec((1, LANES), const),
              pl.BlockSpec((POST_SUB, POST_SUB), const)]
    return pl.pallas_call(
        functools.partial(_post_attn_kernel, n_branch=nb),
        out_shape=[jax.ShapeDtypeStruct((n * d // LANES, LANES), _F32), jax.ShapeDtypeStruct((n, LANES), _F32),
                   jax.ShapeDtypeStruct((1, LANES), _F32)],
        grid=(n // tm,),
        in_specs=specs,
        out_specs=[pl.BlockSpec((tm * d // LANES, LANES), row), pl.BlockSpec((tm, LANES), row),
                   pl.BlockSpec((1, LANES), const)],
        scratch_shapes=[pltpu.VMEM((1, LANES), _F32)],
        compiler_params=_params(("arbitrary",)),
        name="post_attn_router",
    )(*args)


def _dest_kernel(route_ref, ps_ref, o_ref):
    r = route_ref[...]
    lane = lax.broadcasted_iota(jnp.int32, r.shape, 1)
    lanef = lane.astype(_F32)
    ps = ps_ref[...]
    d = []
    for k in range(2):
        e = r[:, k:k + 1] + float(ROUTE_LANE0)
        d.append(jnp.sum(jnp.where(lanef == e, ps, 0.0), axis=-1, keepdims=True) + r[:, 4 + k:5 + k])
    o_ref[...] = jnp.where(lane == 0, d[0], jnp.where(lane == 1, d[1], 0.0)).astype(jnp.int32)


def _dest(route, pad_start_row):
    n = route.shape[0]
    tm = 1024
    return pl.pallas_call(
        _dest_kernel,
        out_shape=jax.ShapeDtypeStruct((n, LANES), jnp.int32),
        grid=(n // tm,),
        in_specs=[pl.BlockSpec((tm, LANES), lambda i: (i, 0)), pl.BlockSpec((1, LANES), lambda i: (0, 0))],
        out_specs=pl.BlockSpec((tm, LANES), lambda i: (i, 0)),
        compiler_params=_params(("arbitrary",)),
        name="moe_dest",
    )(route, pad_start_row)


def _dispatch_kernel(seg_ref, dest_ref, x_ref, xs_hbm, zero_ref, zsem, sem):
    i = pl.program_id(0)
    ntok = TOK_TM
    nsl = x_ref.shape[0] // ntok
    blk_rows = MOE_BLOCK * nsl

    def zero_fill(e):
        first = pl.multiple_of((seg_ref[e] + seg_ref[N_EXPERTS + e] - MOE_BLOCK) * nsl, blk_rows)
        return pltpu.make_async_copy(zero_ref, xs_hbm.at[pl.ds(first, blk_rows)], zsem)

    def has_pad(e):
        return seg_ref[N_EXPERTS + e] > seg_ref[2 * N_EXPERTS + e]

    @pl.when(i == 0)
    def _():
        zero_ref[...] = jnp.zeros_like(zero_ref)

        def start(e, c):
            @pl.when(has_pad(e))
            def _():
                zero_fill(e).start()
            return c

        def wait(e, c):
            @pl.when(has_pad(e))
            def _():
                zero_fill(e).wait()
            return c

        lax.fori_loop(0, N_EXPERTS, start, 0)
        lax.fori_loop(0, N_EXPERTS, wait, 0)

        nvalid = seg_ref[3 * N_EXPERTS]
        nblk = xs_hbm.shape[0] // blk_rows

        def tail(j):
            first = pl.multiple_of((nvalid + j) * blk_rows, blk_rows)
            return pltpu.make_async_copy(zero_ref, xs_hbm.at[pl.ds(first, blk_rows)], zsem)

        def tail_start(j, c):
            @pl.when(nvalid + j < nblk)
            def _():
                tail(j).start()
            return c

        def tail_wait(j, c):
            @pl.when(nvalid + j < nblk)
            def _():
                tail(j).wait()
            return c

        lax.fori_loop(0, N_EXPERTS, tail_start, 0)
        lax.fori_loop(0, N_EXPERTS, tail_wait, 0)

    def row_copy(j, slot):
        return pltpu.make_async_copy(x_ref.at[pl.ds(pl.multiple_of(j * nsl, nsl), nsl)],
                                     xs_hbm.at[pl.ds(pl.multiple_of(slot * nsl, nsl), nsl)], sem)

    def issue(j, c):
        for k in range(2):
            row_copy(j, dest_ref[0, 0, 2 * j + k]).start(priority=k)
        return c

    def drain(j, c):
        row_copy(0, 0).wait()
        return c

    lax.fori_loop(0, ntok, issue, 0, unroll=4)
    lax.fori_loop(0, 2 * ntok, drain, 0, unroll=8)


def _dispatch(x_tiles, dest_flat, seg, cap, n):
    nsl = x_tiles.shape[0] // n
    tm = TOK_TM
    nt = n // tm
    return pl.pallas_call(
        _dispatch_kernel,
        out_shape=jax.ShapeDtypeStruct((cap * nsl, LANES), x_tiles.dtype),
        grid_spec=pltpu.PrefetchScalarGridSpec(
            num_scalar_prefetch=1,
            grid=(nt,),
            in_specs=[pl.BlockSpec((1, 1, 2 * tm), lambda i, seg: (i, 0, 0), memory_space=pltpu.SMEM),
                      pl.BlockSpec((tm * nsl, LANES), lambda i, seg: (i, 0))],
            out_specs=pl.BlockSpec(memory_space=pl.ANY),
            scratch_shapes=[pltpu.VMEM((MOE_BLOCK * nsl, LANES), x_tiles.dtype), pltpu.SemaphoreType.DMA,
                            pltpu.SemaphoreType.DMA]),
        compiler_params=_params(("arbitrary",)),
        name="moe_dispatch",
    )(seg, dest_flat.reshape(nt, 1, 2 * tm), x_tiles)


def _expert_kernel(blk_ref, nvalid_ref, xs_ref, wg_ref, wu_ref, wd_ref, ys_ref, wgb, wub, wdb):
    i = pl.program_id(0)
    prev = blk_ref[jnp.maximum(i - 1, 0)]
    valid = i < nvalid_ref[0]

    @pl.when(valid & ((i == 0) | (blk_ref[i] != prev)))
    def _():
        wgb[...] = wg_ref[0].astype(_BF16)
        wub[...] = wu_ref[0].astype(_BF16)
        wdb[...] = wd_ref[0].astype(_BF16)

    @pl.when(valid)
    def _():
        d = wgb.shape[0]
        xb = _rows_from_tiles(xs_ref, MOE_BLOCK, d).astype(_BF16)
        gate = jnp.dot(xb, wgb[...], preferred_element_type=_F32)
        up = jnp.dot(xb, wub[...], preferred_element_type=_F32)
        mid = (gate * jax.nn.sigmoid(gate)) * up
        _rows_to_tiles(ys_ref, jnp.dot(mid.astype(_BF16), wdb[...], preferred_element_type=_F32))

    @pl.when(jnp.logical_not(valid))
    def _():
        ys_ref[...] = jnp.zeros_like(ys_ref)


def _experts(xs, blk_exp, nvalid, w_gate, w_up, w_down):
    d, de = w_gate.shape[1], w_gate.shape[2]
    nsl = d // LANES
    cap = xs.shape[0] // nsl
    nblk = cap // MOE_BLOCK

    def xmap(i, blk, nv):
        return (jnp.minimum(i, nv[0] - 1), 0)

    def wmap(i, blk, nv):
        return (blk[i], 0, 0)

    return pl.pallas_call(
        _expert_kernel,
        out_shape=jax.ShapeDtypeStruct((cap * nsl, LANES), _F32),
        grid_spec=pltpu.PrefetchScalarGridSpec(
            num_scalar_prefetch=2,
            grid=(nblk,),
            in_specs=[pl.BlockSpec((MOE_BLOCK * nsl, LANES), xmap),
                      pl.BlockSpec((1, d, de), wmap), pl.BlockSpec((1, d, de), wmap),
                      pl.BlockSpec((1, de, d), wmap)],
            out_specs=pl.BlockSpec((MOE_BLOCK * nsl, LANES), lambda i, blk, nv: (i, 0)),
            scratch_shapes=[pltpu.VMEM((d, de), _BF16), pltpu.VMEM((d, de), _BF16), pltpu.VMEM((de, d), _BF16)]),
        compiler_params=_params(("arbitrary",)),
        name="moe_experts",
    )(blk_exp, nvalid, xs, w_gate, w_up, w_down)


def _combine_kernel(dcur_ref, dnxt_ref, ys_hbm, h_ref, route_ref, g_ref, b_ref, o_ref, buf, sem):
    i = pl.program_id(0)
    n = pl.num_programs(0)
    tm = TOK_TM

    d = o_ref.shape[1]
    nsl = d // LANES

    def tile_copy(src_slot, slot, k, j):
        return pltpu.make_async_copy(ys_hbm.at[pl.ds(pl.multiple_of(src_slot * nsl, nsl), nsl)],
                                     buf.at[slot, k, pl.ds(pl.multiple_of(j * nsl, nsl), nsl)], sem.at[slot])

    def gather(dref, slot):
        def body(j, c):
            for k in range(2):
                tile_copy(dref[0, 0, 2 * j + k], slot, k, j).start(priority=k)
            return c
        lax.fori_loop(0, tm, body, 0, unroll=4)

    @pl.when(i == 0)
    def _():
        gather(dcur_ref, 0)

    @pl.when(i + 1 < n)
    def _():
        gather(dnxt_ref, (i + 1) % 2)

    slot = i % 2

    def drain(j, c):
        tile_copy(0, slot, 0, 0).wait()
        return c

    lax.fori_loop(0, 2 * tm, drain, 0, unroll=8)
    r = route_ref[...]
    y = (r[:, 2:3] * _rows_from_tiles(buf.at[slot, 0], tm, d)
         + r[:, 3:4] * _rows_from_tiles(buf.at[slot, 1], tm, d))
    o_ref[...] = _layer_norm(ALPHA * _rows_from_tiles(h_ref, tm, d) + y, g_ref[...], b_ref[...])


def _combine(ys, dest_flat, h1_tiles, route, g, b):
    n = route.shape[0]
    d = g.shape[1]
    nsl = d // LANES
    tm = TOK_TM
    nt = n // tm
    dest3 = dest_flat.reshape(nt, 1, 2 * tm)
    row = lambda i: (i, 0)
    const = lambda i: (0, 0)
    return pl.pallas_call(
        _combine_kernel,
        out_shape=jax.ShapeDtypeStruct((n, d), _F32),
        grid=(nt,),
        in_specs=[pl.BlockSpec((1, 1, 2 * tm), lambda i: (i, 0, 0), memory_space=pltpu.SMEM),
                  pl.BlockSpec((1, 1, 2 * tm), lambda i: (jnp.minimum(i + 1, nt - 1), 0, 0),
                               memory_space=pltpu.SMEM),
                  pl.BlockSpec(memory_space=pl.ANY),
                  pl.BlockSpec((tm * nsl, LANES), row), pl.BlockSpec((tm, LANES), row),
                  pl.BlockSpec((1, d), const), pl.BlockSpec((1, d), const)],
        out_specs=pl.BlockSpec((tm, d), row),
        scratch_shapes=[pltpu.VMEM((2, 2, tm * nsl, LANES), _F32), pltpu.SemaphoreType.DMA((2,))],
        compiler_params=_params(("arbitrary",)),
        name="moe_combine",
    )(dest3, dest3, ys, h1_tiles, route, g, b)


def _moe(h1_tiles, route, counts_row, w_gate, w_up, w_down, g, b):
    n = route.shape[0]
    cap = 2 * n + N_EXPERTS * MOE_BLOCK
    counts = counts_row[0, ROUTE_LANE0:ROUTE_LANE0 + N_EXPERTS].astype(jnp.int32)
    padded = (counts + MOE_BLOCK - 1) // MOE_BLOCK * MOE_BLOCK
    ends = jnp.cumsum(padded)
    pad_start = ends - padded
    nblk = cap // MOE_BLOCK
    blk_first = jnp.arange(nblk, dtype=jnp.int32) * MOE_BLOCK
    blk_exp = jnp.minimum(jnp.sum(blk_first[:, None] >= ends[None, :], axis=1), N_EXPERTS - 1).astype(jnp.int32)
    nvalid = (ends[-1:] // MOE_BLOCK).astype(jnp.int32)
    ps_row = jnp.zeros((1, LANES), _F32).at[0, ROUTE_LANE0:ROUTE_LANE0 + N_EXPERTS].set(pad_start.astype(_F32))
    dest_flat = _dest(route, ps_row)[:, :2].reshape(2 * n)
    seg = jnp.concatenate([pad_start, padded, counts, nvalid]).astype(jnp.int32)
    xs = _dispatch(h1_tiles, dest_flat, seg, cap, n)
    ys = _experts(xs, blk_exp, nvalid, w_gate, w_up, w_down)
    return _combine(ys, dest_flat, h1_tiles, route, g, b)


def _router_params(rg_w, rg_b, re_w, re_b):
    d = rg_w.shape[0]
    pad = LANES - N_GROUPS - N_EXPERTS
    rw = jnp.concatenate([rg_w, re_w, jnp.zeros((d, pad), _F32)], axis=1)
    rb = jnp.concatenate([rg_b, re_b.reshape(-1), jnp.zeros((pad,), _F32)]).reshape(1, LANES)
    return rw, rb


def kernel(x, t5_table, l0_w_qkv, l0_sink, l0_w_o, l0_ln1_g, l0_ln1_b, l0_rg_w, l0_rg_b, l0_re_w, l0_re_b, l0_w_gate, l0_w_up, l0_w_down, l0_ln2_g, l0_ln2_b, l1_w_qkv, l1_w_o, l1_ln1_g, l1_ln1_b, l1_rg_w, l1_rg_b, l1_re_w, l1_re_b, l1_w_gate, l1_w_up, l1_w_down, l1_ln2_g, l1_ln2_b, l2_w_qkv, l2_rpb, l2_w_o, l2_ln1_g, l2_ln1_b, l2_rg_w, l2_rg_b, l2_re_w, l2_re_b, l2_w_gate, l2_w_up, l2_w_down, l2_ln2_g, l2_ln2_b, l3_w_qkv, l3_sink, l3_w_o, l3_ln1_g, l3_ln1_b, l3_rg_w, l3_rg_b, l3_re_w, l3_re_b, l3_w_gate, l3_w_up, l3_w_down, l3_ln2_g, l3_ln2_b):
    mixers = [(l0_w_qkv, l0_sink, l0_w_o), (l1_w_qkv, None, l1_w_o), (l2_w_qkv, l2_rpb, l2_w_o),
              (l3_w_qkv, l3_sink, l3_w_o)]
    norm1 = [(l0_ln1_g, l0_ln1_b), (l1_ln1_g, l1_ln1_b), (l2_ln1_g, l2_ln1_b), (l3_ln1_g, l3_ln1_b)]
    routers = [(l0_rg_w, l0_rg_b, l0_re_w, l0_re_b), (l1_rg_w, l1_rg_b, l1_re_w, l1_re_b),
               (l2_rg_w, l2_rg_b, l2_re_w, l2_re_b), (l3_rg_w, l3_rg_b, l3_re_w, l3_re_b)]
    experts = [(l0_w_gate, l0_w_up, l0_w_down), (l1_w_gate, l1_w_up, l1_w_down),
               (l2_w_gate, l2_w_up, l2_w_down), (l3_w_gate, l3_w_up, l3_w_down)]
    norm2 = [(l0_ln2_g, l0_ln2_b), (l1_ln2_g, l1_ln2_b), (l2_ln2_g, l2_ln2_b), (l3_ln2_g, l3_ln2_b)]

    bsz, seq, d = x.shape
    n = bsz * seq
    inner = N_HEADS * HEAD_DIM
    scale = HEAD_DIM ** -0.5 * LOG2E
    h = x.reshape(n, d)
    bias_a = _band_bias_t(t5_table, A_BLOCK, A_WINDOW, 1, A_BAND)
    for i in range(DEPTH):
        kind = i % N_MIXERS
        w_qkv, extra, w_o = mixers[i]
        kvw = A_KV_HEADS * HEAD_DIM if kind == 0 else inner
        lses, unperm = (), None
        if kind == 1:
            perms = np.stack([_residue_perm(dil) for _, dil in B_BRANCHES if dil > 1])
            outs = _project(h, w_qkv.astype(_BF16), (inner, kvw, kvw), (scale, 1.0, 1.0), jnp.asarray(perms, _BF16))
            unperm = jnp.asarray(perms.transpose(0, 2, 1), _BF16)
            attn, lses = [], []
            for bi, (window, dil) in enumerate(B_BRANCHES):
                q, k, v = outs[3 * bi:3 * bi + 3]
                o, lse = _mixer_b_branch(q, k, v, t5_table, bsz, seq, window, dil)
                attn.append(o)
                lses.append(jnp.pad(lse.reshape(n, N_HEADS), ((0, 0), (0, LANES - N_HEADS))))
        else:
            q, k, v = _project(h, w_qkv.astype(_BF16), (inner, kvw, kvw), (scale, 1.0, 1.0))
            q, k, v = (t.reshape(bsz, seq, t.shape[1]) for t in (q, k, v))
            if kind == 0:
                attn = [_mixer_a(q, k, v, extra.astype(_F32) * LOG2E, bias_a, bsz, seq)]
            else:
                attn = [_mixer_c(q, k, v, _na_bias_t(extra, seq // GRID_W), bsz, seq)]
        attn = [a.reshape(n, inner) for a in attn]
        g1, b1 = norm1[i]
        rw, rb = _router_params(*routers[i])
        h1, route, counts = _post_attn(attn, lses, h, w_o.astype(_BF16), g1.reshape(1, d), b1.reshape(1, d), rw, rb,
                                       unperm)
        g2, b2 = norm2[i]
        h = _moe(h1, route, counts, *experts[i], g2.reshape(1, d), b2.reshape(1, d))
    return h.reshape(bsz, seq, d)
```

```python
import functools
import math

import jax
import jax.numpy as jnp
import numpy as np
from jax import lax
from jax.experimental import pallas as pl
from jax.experimental.pallas import tpu as pltpu

HEAD_DIM = 64
N_HEADS = 16
A_KV_HEADS = 4
A_WINDOW = 128
A_BLOCK = 128
B_BRANCHES = ((128, 1), (512, 4), (2048, 16))
GRID_W = 64
NA_ROWS = 8
NA_COLS = 16
T5_BUCKETS = 32
T5_MAX_DIST = 1024
N_GROUPS = 4
EXPERTS_PER_GROUP = 8
N_EXPERTS = N_GROUPS * EXPERTS_PER_GROUP
TOP_K = 2
DEPTH = 4
N_MIXERS = 3
ALPHA = (2 * DEPTH) ** 0.25
LN_EPS = 1e-5
NEG_INF = -1e30

LANES = 128
PAIR = 2 * HEAD_DIM
BF16_SUBLANES = 16
VT_ROWS = PAIR + BF16_SUBLANES
LOG2E = math.log2(math.e)
LN2 = math.log(2.0)
VMEM_LIMIT = 56 * 1024 * 1024

PROJ_TM = 512
PROJ_CHUNK = 512
POST_TM = 1024
POST_SUB = 256
TOK_TM = 256
DEST_TM = 4096
MOE_BLOCK = 512
A_TQ = 1024
A_BAND = 3
A_SUB_PER_GROUP = 2
B_BLOCK = 128
RESIDUE_RUN = BF16_SUBLANES
PERM_CHUNK = 256
TRANSPOSE_UNROLL = 4
UNITS_PER_GROUP = 8
C_UNIT_ROWS = 2
C_BAND_ROWS = 10
C_UNITS_PER_GROUP = 4
C_ROWS_PER_STEP = 32
ROUTE_LANE0 = N_GROUPS

_F32 = jnp.float32
_BF16 = jnp.bfloat16


def _params(sem, vmem=VMEM_LIMIT, flags=None):
    return pltpu.CompilerParams(dimension_semantics=sem, vmem_limit_bytes=vmem, flags=flags)


def _proj_kernel(x_ref, w_ref, *refs, widths, scales, nperm):
    perm_ref = refs[0] if nperm else None
    o_refs = refs[1:] if nperm else refs
    nout = len(widths)
    xb = x_ref[...].astype(_BF16)
    col = 0
    for t, (width, scale) in enumerate(zip(widths, scales)):
        for c in range(0, width, PROJ_CHUNK):
            cw = min(PROJ_CHUNK, width - c)
            acc = jnp.dot(xb, w_ref[:, col + c:col + c + cw], preferred_element_type=_F32)
            if scale != 1.0:
                acc = acc * scale
            ob = acc.astype(_BF16)
            o_refs[t][:, c:c + cw] = ob
            for p in range(nperm):
                for r0 in range(0, ob.shape[0], PERM_CHUNK):
                    moved = jnp.dot(perm_ref[p], ob[r0:r0 + PERM_CHUNK], preferred_element_type=_F32)
                    o_refs[(p + 1) * nout + t][r0:r0 + PERM_CHUNK, c:c + cw] = moved.astype(_BF16)
        col += width


def _project(x2, w_bf16, widths, scales, perms=None):
    n, d = x2.shape
    ctot = sum(widths)
    nperm = 0 if perms is None else perms.shape[0]
    args = [x2, w_bf16] + ([perms] if nperm else [])
    specs = [pl.BlockSpec((PROJ_TM, d), lambda i: (i, 0)), pl.BlockSpec((d, ctot), lambda i: (0, 0))]
    if nperm:
        specs.append(pl.BlockSpec(perms.shape, lambda i: (0, 0, 0)))
    return pl.pallas_call(
        functools.partial(_proj_kernel, widths=tuple(widths), scales=tuple(scales), nperm=nperm),
        out_shape=[jax.ShapeDtypeStruct((n, w), _BF16) for w in widths] * (nperm + 1),
        grid=(n // PROJ_TM,),
        in_specs=specs,
        out_specs=[pl.BlockSpec((PROJ_TM, w), lambda i: (i, 0)) for w in widths] * (nperm + 1),
        compiler_params=_params(("arbitrary",)),
        name="qkv_proj",
    )(*args)


def _residue_perm(dil):
    span = dil * RESIDUE_RUN
    p = np.zeros((PERM_CHUNK, PERM_CHUNK), np.float32)
    for base in range(0, PERM_CHUNK, span):
        for r in range(dil):
            for u in range(RESIDUE_RUN):
                p[base + r * RESIDUE_RUN + u, base + u * dil + r] = 1.0
    return p


def _one():
    return jnp.ones((1,), jnp.int32)


def _lane_lo(rows=1):
    return lax.broadcasted_iota(jnp.int32, (rows, PAIR), 1) < HEAD_DIM


def _swap_halves(x):
    return pltpu.roll(x.astype(_F32), HEAD_DIM, axis=1).astype(_BF16)


class _PairUnit:
    def __init__(self, q2, k, vt, bias, emit, sink_a=None, sink_b=None):
        self.q2, self.k, self.vt, self.bias, self.emit = q2, k, vt, bias, emit
        self.sink_a, self.sink_b = sink_a, sink_b


def _attend_groups(ngroups, units_of, scratch, one):
    assert ngroups % 2 == 0
    lo = _lane_lo()
    nt = (((1,), (1,)), ((), ()))

    def sink_row(unit, m2):
        lane = lax.broadcasted_iota(jnp.int32, (1, m2), 1)
        return jnp.where(lane < m2 // 2, unit.sink_a, unit.sink_b)

    def scores(g, parity):
        s_scr, m_scr = scratch[parity]
        for u, unit in enumerate(units_of(g, parity)):
            q2 = unit.q2()
            zero = jnp.zeros_like(q2)
            qq = jnp.concatenate([jnp.where(lo, q2, zero), jnp.where(lo, zero, q2)], axis=0)
            s = lax.dot_general(unit.k(), qq, nt, preferred_element_type=_F32) + unit.bias()
            m = jnp.max(s, axis=0, keepdims=True)
            if unit.sink_a is not None:
                m = jnp.maximum(m, sink_row(unit, s.shape[1]))
            s_scr[u] = s
            m_scr[u] = m

    def values(g, parity):
        s_scr, m_scr = scratch[parity]
        for u, unit in enumerate(units_of(g, parity)):
            m = m_scr[u]
            e = jnp.exp2(s_scr[u] - m)
            ot = jnp.dot(unit.vt(), e.astype(_BF16), preferred_element_type=_F32)
            den = ot[PAIR:PAIR + 1, :]
            if unit.sink_a is not None:
                den = den + jnp.exp2(sink_row(unit, e.shape[1]) - m)
            ot = ot[:PAIR] * (1.0 / den)
            mq = ot.shape[1] // 2
            top = lax.broadcasted_iota(jnp.int32, (PAIR, 1), 0) < HEAD_DIM
            pair_t = jnp.where(top, ot[:, :mq], ot[:, mq:])
            unit.emit(pair_t.T, (m + jnp.log2(den)) * LN2)

    def region(*work):
        def body(_, carry):
            for fn, g, parity in work:
                fn(g, parity)
            return carry
        lax.fori_loop(0, one, body, 0)

    region((scores, 0, 0))

    def steady(i, carry):
        g = 2 * i
        region((values, g, 0), (scores, g + 1, 1))
        region((values, g + 1, 1), (scores, g + 2, 0))
        return carry

    lax.fori_loop(0, ngroups // 2 - 1, steady, 0)
    region((values, ngroups - 2, 0), (scores, ngroups - 1, 1))
    region((values, ngroups - 1, 1))


def _transpose_blocks(v_ref_rows, nblk, store):
    def body(j, carry):
        rows = pl.ds(pl.multiple_of(j * LANES, LANES), LANES)
        vt = v_ref_rows(rows).astype(_F32).T.astype(_BF16)
        store(j, jnp.concatenate([vt, jnp.ones((VT_ROWS - PAIR, LANES), _BF16)], axis=0))
        return carry
    lax.fori_loop(0, nblk, body, 0, unroll=TRANSPOSE_UNROLL)


def _t5_bucket(rel):
    half = T5_BUCKETS // 2
    exact = half // 2
    n = np.abs(rel)
    big = exact + (np.log(np.maximum(n, 1) / exact) / math.log(T5_MAX_DIST / exact) * (half - exact)).astype(np.int32)
    return ((rel > 0) * half + np.where(n < exact, n, np.minimum(big, half - 1))).astype(np.int32)


def _band_bias_t(t5_table, blk, window, dil, nband):
    tk = nband * blk
    off = tk - 1
    rel = np.arange(-off, tk)
    vec = jnp.take(t5_table, jnp.asarray(_t5_bucket(rel * dil)), axis=0).astype(_F32)
    vec = jnp.where(jnp.asarray(np.abs(rel) <= window)[:, None], vec * LOG2E, NEG_INF).T
    h = vec.shape[0]
    vec = jnp.pad(vec, ((0, 0), (0, 1)))
    w = 2 * tk - 1
    skew = jnp.tile(vec, (1, blk))[:, :blk * w].reshape(h, blk, w)
    t = jnp.stack([skew[:, :, off - v * blk:off - v * blk + tk] for v in range(nband)], axis=1)
    t = t.reshape(h // 2, 2, nband, blk, tk).transpose(0, 2, 4, 1, 3)
    return t.reshape(h // 2, nband, tk, 2 * blk)


def _band_window(gb, blk, length, nband):
    first = jnp.clip(gb - 1, 0, length // blk - nband)
    return first, gb - first


def _attn_a_kernel(sink_ref, one_ref, q_ref, k_ref, v_ref, bias_ref, o_ref, kd_ref, vdt_ref,
                   s0_scr, m0_scr, s1_scr, m1_scr, *, seq):
    kvb = pl.program_id(0)
    qt = pl.program_id(2)
    one = one_ref[0]
    nblk = seq // A_BLOCK

    @pl.when(qt == 0)
    def _():
        lo = _lane_lo()

        def fill(j, carry):
            rows = pl.ds(pl.multiple_of(j * A_BLOCK, A_BLOCK), A_BLOCK)
            kk = k_ref[0, rows, :]
            ks = _swap_halves(kk)
            kd_ref[0, rows, :] = jnp.where(lo, kk, ks)
            kd_ref[1, rows, :] = jnp.where(lo, ks, kk)
            vt = v_ref[0, rows, :].astype(_F32).T.astype(_BF16)
            ones = jnp.ones((VT_ROWS - PAIR, A_BLOCK), _BF16)
            vdt_ref[0, j] = jnp.concatenate([vt[:HEAD_DIM], vt[:HEAD_DIM], ones], axis=0)
            vdt_ref[1, j] = jnp.concatenate([vt[HEAD_DIM:], vt[HEAD_DIM:], ones], axis=0)
            return carry

        lax.fori_loop(0, nblk, fill, 0, unroll=TRANSPOSE_UNROLL)

    nsub = A_TQ // A_BLOCK
    npair = q_ref.shape[2] // PAIR
    hbase = kvb * (2 * npair)

    def units_of(g, parity):
        units = []
        for j in range(A_SUB_PER_GROUP):
            si = g * A_SUB_PER_GROUP + j
            first, var = _band_window(qt * nsub + si, A_BLOCK, seq, A_BAND)
            rows = pl.ds(pl.multiple_of(si * A_BLOCK, A_BLOCK), A_BLOCK)
            band = pl.ds(pl.multiple_of(first * A_BLOCK, A_BLOCK), A_BAND * A_BLOCK)
            for jj in range(npair):
                hk = jj // (npair // 2)
                cols = slice(jj * PAIR, (jj + 1) * PAIR)

                def emit(o, lse, rows=rows, cols=cols):
                    o_ref[0, rows, cols] = o.astype(o_ref.dtype)

                units.append(_PairUnit(
                    q2=lambda rows=rows, cols=cols: q_ref[0, rows, cols],
                    k=lambda hk=hk, band=band: kd_ref[hk, band, :],
                    vt=lambda hk=hk, first=first: jnp.concatenate(
                        [vdt_ref[hk, first + t] for t in range(A_BAND)], axis=1),
                    bias=lambda jj=jj, var=var: bias_ref[jj, var],
                    emit=emit, sink_a=sink_ref[hbase + 2 * jj], sink_b=sink_ref[hbase + 2 * jj + 1]))
        return units

    _attend_groups(nsub // A_SUB_PER_GROUP, units_of, ((s0_scr, m0_scr), (s1_scr, m1_scr)), one)


def _mixer_a(q, k, v, sink, bias, bsz, seq):
    nkvb = k.shape[2] // PAIR
    qw = q.shape[2] // nkvb
    npair = qw // PAIR
    tk = A_BAND * A_BLOCK
    nunits = A_SUB_PER_GROUP * npair
    group_scratch = [pltpu.VMEM((nunits, tk, 2 * A_BLOCK), _F32), pltpu.VMEM((nunits, 1, 2 * A_BLOCK), _F32)]
    return pl.pallas_call(
        functools.partial(_attn_a_kernel, seq=seq),
        out_shape=jax.ShapeDtypeStruct(q.shape, _BF16),
        grid=(nkvb, bsz, seq // A_TQ),
        in_specs=[pl.BlockSpec(memory_space=pltpu.SMEM), pl.BlockSpec(memory_space=pltpu.SMEM),
                  pl.BlockSpec((1, A_TQ, qw), lambda c, b, t: (b, t, c)),
                  pl.BlockSpec((1, seq, PAIR), lambda c, b, t: (b, 0, c)),
                  pl.BlockSpec((1, seq, PAIR), lambda c, b, t: (b, 0, c)),
                  pl.BlockSpec((npair, A_BAND, tk, 2 * A_BLOCK), lambda c, b, t: (c, 0, 0, 0))],
        out_specs=pl.BlockSpec((1, A_TQ, qw), lambda c, b, t: (b, t, c)),
        scratch_shapes=[pltpu.VMEM((2, seq, PAIR), _BF16), pltpu.VMEM((2, seq // A_BLOCK, VT_ROWS, A_BLOCK), _BF16)]
        + group_scratch + group_scratch,
        compiler_params=_params(("arbitrary", "arbitrary", "arbitrary")),
        name="mixer_a",
    )(sink, _one(), q, k, v, bias)


def _attn_band_kernel(one_ref, q_ref, k_ref, v_ref, bias_ref, o_ref, lse_ref, vt_ref,
                      s0_scr, m0_scr, s1_scr, m1_scr, *, length, npairs, nband, chunked):
    col = pl.program_id(2)
    one = one_ref[0]
    nblk = length // B_BLOCK
    runs = B_BLOCK // RESIDUE_RUN

    def load(ref, blk0, nblocks, cols):
        if chunked:
            r0 = pl.multiple_of(blk0 * runs, runs)
            return ref[pl.ds(r0, nblocks * runs), :, cols].reshape(nblocks * B_BLOCK, PAIR)
        return ref[0, pl.ds(pl.multiple_of(blk0 * B_BLOCK, B_BLOCK), nblocks * B_BLOCK), cols]

    def store_out(blk0, cols, val):
        if chunked:
            o_ref[pl.ds(pl.multiple_of(blk0 * runs, runs), runs), :, cols] = val.reshape(runs, RESIDUE_RUN, PAIR)
        else:
            o_ref[0, pl.ds(pl.multiple_of(blk0 * B_BLOCK, B_BLOCK), B_BLOCK), cols] = val

    for p in range(npairs):
        pcols = slice(p * PAIR, (p + 1) * PAIR)

        def fill(j, carry, p=p, pcols=pcols):
            vt = load(v_ref, j, 1, pcols).astype(_F32).T.astype(_BF16)
            vt_ref[p, j] = jnp.concatenate([vt, jnp.ones((VT_ROWS - PAIR, B_BLOCK), _BF16)], axis=0)
            return carry

        lax.fori_loop(0, nblk, fill, 0, unroll=min(TRANSPOSE_UNROLL, nblk))

    def unit(gb, p):
        first, var = _band_window(gb, B_BLOCK, length, nband)
        cols = slice(p * PAIR, (p + 1) * PAIR)

        def emit(o, lse):
            store_out(gb, cols, o.astype(o_ref.dtype))
            ha = col * (2 * npairs) + 2 * p
            lse_ref[0, 0, gb, pl.ds(ha, 1), :] = lse[:, :B_BLOCK]
            lse_ref[0, 0, gb, pl.ds(ha + 1, 1), :] = lse[:, B_BLOCK:]

        return _PairUnit(
            q2=lambda: load(q_ref, gb, 1, cols), k=lambda: load(k_ref, first, nband, cols),
            vt=lambda: jnp.concatenate([vt_ref[p, first + t] for t in range(nband)], axis=1),
            bias=lambda: bias_ref[p, var], emit=emit)

    if npairs == 1:
        ngroups = nblk // UNITS_PER_GROUP
        units_of = lambda g, parity: [unit(g * UNITS_PER_GROUP + j, 0) for j in range(UNITS_PER_GROUP)]
    else:
        assert npairs == UNITS_PER_GROUP
        ngroups = nblk
        units_of = lambda g, parity: [unit(g, p) for p in range(npairs)]
    _attend_groups(ngroups, units_of, ((s0_scr, m0_scr), (s1_scr, m1_scr)), one)


def _mixer_b_branch(q, k, v, t5_table, bsz, seq, window, dil):
    length = seq // dil
    width = q.shape[1]
    nblk = length // B_BLOCK
    nband = min(3, nblk)
    tk = nband * B_BLOCK
    npairs = (width // PAIR) if length * width * 2 <= 2 * 1024 * 1024 else 1
    ncol = width // (npairs * PAIR)
    bw = npairs * PAIR
    bias = _band_bias_t(t5_table, B_BLOCK, window // 2 // dil, dil, nband)
    chunked = dil > 1
    if chunked:
        nrun = length // RESIDUE_RUN
        shape = (bsz * nrun, dil, RESIDUE_RUN, width)
        spec = pl.BlockSpec((nrun, None, RESIDUE_RUN, bw), lambda b, r, c: (b, r, 0, c))
    else:
        shape = (bsz, seq, width)
        spec = pl.BlockSpec((1, seq, bw), lambda b, r, c: (b, 0, c))

    o, lse = pl.pallas_call(
        functools.partial(_attn_band_kernel, length=length, npairs=npairs, nband=nband, chunked=chunked),
        out_shape=[jax.ShapeDtypeStruct(shape, _BF16),
                   jax.ShapeDtypeStruct((bsz, dil, nblk, N_HEADS, B_BLOCK), _F32)],
        grid=(bsz, dil, ncol),
        in_specs=[pl.BlockSpec(memory_space=pltpu.SMEM), spec, spec, spec,
                  pl.BlockSpec((npairs, nband, tk, 2 * B_BLOCK), lambda b, r, c: (c, 0, 0, 0))],
        out_specs=[spec, pl.BlockSpec((1, 1, nblk, N_HEADS, B_BLOCK), lambda b, r, c: (b, r, 0, 0, 0))],
        scratch_shapes=[pltpu.VMEM((npairs, nblk, VT_ROWS, B_BLOCK), _BF16)] + 2 * [
            pltpu.VMEM((UNITS_PER_GROUP, tk, 2 * B_BLOCK), _F32), pltpu.VMEM((UNITS_PER_GROUP, 1, 2 * B_BLOCK), _F32)],
        compiler_params=_params(("arbitrary", "arbitrary", "arbitrary")),
        name=f"mixer_b_dil{dil}",
    )(_one(), q.reshape(shape), k.reshape(shape), v.reshape(shape), bias)
    lse = lse.transpose(0, 2, 4, 1, 3).reshape(bsz, seq, N_HEADS)
    return o.reshape(bsz * seq, width), lse


def _na_bias_t(rpb, rows):
    canon = (0, 2, 4, rows - 4, rows - 2)
    h = rpb.shape[0]
    kc = np.arange(GRID_W)[:, None]
    qc = np.arange(GRID_W)[None, :]
    ws = np.clip(qc - NA_COLS // 2, 0, GRID_W - NA_COLS)
    col_ok = (kc >= ws) & (kc < ws + NA_COLS)
    col_idx = np.clip(kc - qc, 1 - NA_COLS, NA_COLS - 1) + NA_COLS - 1
    colpart = jnp.where(jnp.asarray(col_ok), rpb.astype(_F32)[:, :, jnp.asarray(col_idx)] * LOG2E, NEG_INF)
    masked = jnp.full((h, GRID_W, GRID_W), NEG_INF, _F32)
    variants = []
    for i in canon:
        start = int(np.clip(i - NA_ROWS // 2, 0, rows - C_BAND_ROWS))
        per_key_row = []
        for kr in range(C_BAND_ROWS):
            per_query_row = []
            for qr in range(C_UNIT_ROWS):
                iq, kabs = i + qr, start + kr
                first_key_row = int(np.clip(iq - NA_ROWS // 2, 0, rows - NA_ROWS))
                inside = first_key_row <= kabs < first_key_row + NA_ROWS
                per_query_row.append(colpart[:, kabs - iq + NA_ROWS - 1] if inside else masked)
            per_key_row.append(jnp.stack(per_query_row, axis=2))
        variants.append(jnp.stack(per_key_row, axis=1))
    b = jnp.stack(variants, axis=1)
    nq, nk = C_UNIT_ROWS * GRID_W, C_BAND_ROWS * GRID_W
    b = b.reshape(h // 2, 2, len(canon), nk, nq)
    return b.transpose(0, 2, 3, 1, 4).reshape(h // 2, len(canon), nk, 2 * nq)


def _attn_c_kernel(one_ref, q_ref, k_ref, v_ref, bias_ref, o_ref, vt_ref, s0_scr, m0_scr, s1_scr, m1_scr, *, rows):
    rt = pl.program_id(2)
    nq = C_UNIT_ROWS * GRID_W
    band_blocks = C_BAND_ROWS * GRID_W // LANES

    @pl.when(rt == 0)
    def _():
        def store(j, blk):
            vt_ref[j] = blk
        _transpose_blocks(lambda r: v_ref[0, r, :], rows * GRID_W // LANES, store)

    def unit(u):
        i = rt * C_ROWS_PER_STEP + C_UNIT_ROWS * u
        start = jnp.clip(i - NA_ROWS // 2, 0, rows - C_BAND_ROWS)
        var = lax.shift_right_logical(i - start, 1)
        first = lax.shift_right_logical(start, 1)
        keys = pl.ds(pl.multiple_of(first * LANES, LANES), C_BAND_ROWS * GRID_W)
        qrows = pl.ds(pl.multiple_of(u * nq, nq), nq)

        def emit(o, lse):
            o_ref[0, qrows, :] = o.astype(o_ref.dtype)

        return _PairUnit(
            q2=lambda: q_ref[0, qrows, :], k=lambda: k_ref[0, keys, :],
            vt=lambda: jnp.concatenate([vt_ref[first + t] for t in range(band_blocks)], axis=1),
            bias=lambda: bias_ref[0, var], emit=emit)

    ngroups = C_ROWS_PER_STEP // C_UNIT_ROWS // C_UNITS_PER_GROUP
    units_of = lambda g, parity: [unit(g * C_UNITS_PER_GROUP + j) for j in range(C_UNITS_PER_GROUP)]
    _attend_groups(ngroups, units_of, ((s0_scr, m0_scr), (s1_scr, m1_scr)), one_ref[0])


def _mixer_c(q, k, v, bias, bsz, seq):
    rows = seq // GRID_W
    tq = C_ROWS_PER_STEP * GRID_W
    npair = q.shape[2] // PAIR
    nq, nk = C_UNIT_ROWS * GRID_W, C_BAND_ROWS * GRID_W
    return pl.pallas_call(
        functools.partial(_attn_c_kernel, rows=rows),
        out_shape=jax.ShapeDtypeStruct(q.shape, _BF16),
        grid=(npair, bsz, seq // tq),
        in_specs=[pl.BlockSpec(memory_space=pltpu.SMEM),
                  pl.BlockSpec((1, tq, PAIR), lambda c, b, t: (b, t, c)),
                  pl.BlockSpec((1, seq, PAIR), lambda c, b, t: (b, 0, c)),
                  pl.BlockSpec((1, seq, PAIR), lambda c, b, t: (b, 0, c)),
                  pl.BlockSpec((1, bias.shape[1], nk, 2 * nq), lambda c, b, t: (c, 0, 0, 0))],
        out_specs=pl.BlockSpec((1, tq, PAIR), lambda c, b, t: (b, t, c)),
        scratch_shapes=[pltpu.VMEM((seq // LANES, VT_ROWS, LANES), _BF16)] + 2 * [
            pltpu.VMEM((C_UNITS_PER_GROUP, nk, 2 * nq), _F32), pltpu.VMEM((C_UNITS_PER_GROUP, 1, 2 * nq), _F32)],
        compiler_params=_params(("arbitrary", "arbitrary", "arbitrary")),
        name="mixer_c",
    )(_one(), q, k, v, bias)


def _layer_norm(z, g, b):
    mu = jnp.mean(z, axis=-1, keepdims=True)
    zc = z - mu
    var = jnp.mean(zc * zc, axis=-1, keepdims=True)
    return zc * lax.rsqrt(var + LN_EPS) * g + b


def _rows_to_tiles(ref, x):
    nrows, d = x.shape
    nsl = d // LANES
    for s in range(nsl):
        ref[pl.ds(s, nrows, stride=nsl), :] = x[:, s * LANES:(s + 1) * LANES]


def _rows_from_tiles(ref, nrows, d):
    nsl = d // LANES
    return jnp.concatenate([ref[pl.ds(s, nrows, stride=nsl), :] for s in range(nsl)], axis=1)


def _split_bf16(x):
    hi = x.astype(_BF16)
    return hi, (x - hi.astype(_F32)).astype(_BF16)


def _router_logits(hn, rw_ref, rb_ref):
    h_hi, h_lo = _split_bf16(hn)
    w_hi, w_lo = _split_bf16(rw_ref[...])
    return (jnp.dot(h_hi, w_hi, preferred_element_type=_F32)
            + (jnp.dot(h_hi, w_lo, preferred_element_type=_F32)
               + jnp.dot(h_lo, w_hi, preferred_element_type=_F32))) + rb_ref[...]


def _route(logits, tri_ref, carry_ref):
    tm = logits.shape[0]
    lane = lax.broadcasted_iota(jnp.int32, (tm, LANES), 1).astype(_F32)
    ninf = -jnp.inf
    is_grp = lane < N_GROUPS
    lg = jnp.where(is_grp, logits, ninf)
    gmax = jnp.max(lg, axis=-1, keepdims=True)
    grp = jnp.min(jnp.where(lg == gmax, lane, float(LANES)), axis=-1, keepdims=True)
    p_grp = 1.0 / jnp.sum(jnp.where(is_grp, jnp.exp(lg - gmax), 0.0), axis=-1, keepdims=True)
    lo_lane = ROUTE_LANE0 + EXPERTS_PER_GROUP * grp
    in_grp = (lane >= lo_lane) & (lane < lo_lane + EXPERTS_PER_GROUP)
    le = jnp.where(in_grp, logits, ninf)
    v1 = jnp.max(le, axis=-1, keepdims=True)
    i1 = jnp.min(jnp.where(le == v1, lane, float(LANES)), axis=-1, keepdims=True)
    le2 = jnp.where(lane == i1, ninf, le)
    v2 = jnp.max(le2, axis=-1, keepdims=True)
    i2 = jnp.min(jnp.where(le2 == v2, lane, float(LANES)), axis=-1, keepdims=True)
    t = jnp.exp(v2 - v1)
    s1 = 1.0 / (1.0 + t)
    g1 = p_grp * s1
    g2 = p_grp * (t * s1)
    sel1 = lane == i1
    sel2 = lane == i2
    onehot = jnp.where(sel1 | sel2, 1.0, 0.0)
    sub = tri_ref.shape[0]
    carry = carry_ref[...]
    parts = []
    for r0 in range(0, tm, sub):
        part = onehot[r0:r0 + sub]
        parts.append(jnp.dot(tri_ref[...], part.astype(_BF16), preferred_element_type=_F32) + carry)
        carry = carry + jnp.sum(part, axis=0, keepdims=True)
    carry_ref[...] = carry
    before = jnp.concatenate(parts, axis=0) if len(parts) > 1 else parts[0]
    rank1 = jnp.sum(jnp.where(sel1, before, 0.0), axis=-1, keepdims=True)
    rank2 = jnp.sum(jnp.where(sel2, before, 0.0), axis=-1, keepdims=True)
    out = jnp.zeros((tm, LANES), _F32)
    for idx, val in enumerate((i1 - ROUTE_LANE0, i2 - ROUTE_LANE0, g1, g2, rank1, rank2)):
        out = jnp.where(lane == idx, val, out)
    return out


def _post_attn_kernel(*refs, n_branch):
    attn_refs = refs[:n_branch]
    pos = n_branch
    if n_branch > 1:
        lse_refs = refs[pos:pos + n_branch]
        expand_ref, unperm_ref = refs[pos + n_branch:pos + n_branch + 2]
        pos += n_branch + 2
    wo_ref, h_ref, g_ref, b_ref, rw_ref, rb_ref, tri_ref = refs[pos:pos + 7]
    h1_ref, route_ref, counts_ref, carry_ref = refs[pos + 7:pos + 11]

    @pl.when(pl.program_id(0) == 0)
    def _():
        carry_ref[...] = jnp.zeros_like(carry_ref)

    d = h_ref.shape[1]
    logits = []
    for sub in range(h_ref.shape[0] // POST_SUB):
        rows = slice(sub * POST_SUB, (sub + 1) * POST_SUB)
        if n_branch == 1:
            a = attn_refs[0][rows, :]
        else:
            lses = [r[rows, :] for r in lse_refs]
            mx = functools.reduce(jnp.maximum, lses)
            es = [jnp.exp(l - mx) for l in lses]
            inv = 1.0 / functools.reduce(jnp.add, es)
            mixed = None
            for bi, (e, a_ref) in enumerate(zip(es, attn_refs)):
                w_hi, w_lo = _split_bf16(e * inv)
                w_full = (jnp.dot(w_hi, expand_ref[...], preferred_element_type=_F32)
                          + jnp.dot(w_lo, expand_ref[...], preferred_element_type=_F32))
                if bi == 0:
                    a_nat = a_ref[rows, :].astype(_F32)
                else:
                    a_nat = jnp.dot(unperm_ref[bi - 1], a_ref[rows, :], preferred_element_type=_F32)
                term = w_full * a_nat
                mixed = term if mixed is None else mixed + term
            a = mixed.astype(_BF16)
        y = jnp.dot(a, wo_ref[...], preferred_element_type=_F32)
        hn = _layer_norm(ALPHA * h_ref[rows, :] + y, g_ref[...], b_ref[...])
        nsl = d // LANES
        _rows_to_tiles(h1_ref.at[pl.ds(sub * POST_SUB * nsl, POST_SUB * nsl)], hn)
        logits.append(_router_logits(hn, rw_ref, rb_ref))
    route_ref[...] = _route(jnp.concatenate(logits, axis=0), tri_ref, carry_ref)
    counts_ref[...] = carry_ref[...]


def _post_attn(attn, lses, h2, w_o, g, b, rw, rb, unperm=None):
    n, d = h2.shape
    tm = POST_TM
    nb = len(attn)
    tri = jnp.asarray(np.tril(np.ones((POST_SUB, POST_SUB), np.float32), -1), _BF16)
    row = lambda i: (i, 0)
    const = lambda i: (0, 0)
    args = list(attn)
    specs = [pl.BlockSpec((tm, d), row) for _ in attn]
    if nb > 1:
        expand = np.zeros((LANES, d), np.float32)
        for hd in range(N_HEADS):
            expand[hd, hd * HEAD_DIM:(hd + 1) * HEAD_DIM] = 1.0
        args += list(lses) + [jnp.asarray(expand, _BF16), unperm]
        specs += [pl.BlockSpec((tm, LANES), row) for _ in lses] + [
            pl.BlockSpec((LANES, d), const), pl.BlockSpec(unperm.shape, lambda i: (0, 0, 0))]
    args += [w_o, h2, g, b, rw, rb, tri]
    specs += [pl.BlockSpec((d, d), const), pl.BlockSpec((tm, d), row), pl.BlockSpec((1, d), const),
              pl.BlockSpec((1, d), const), pl.BlockSpec((d, LANES), const), pl.BlockSpec((1, LANES), const),
              pl.BlockSpec((POST_SUB, POST_SUB), const)]
    return pl.pallas_call(
        functools.partial(_post_attn_kernel, n_branch=nb),
        out_shape=[jax.ShapeDtypeStruct((n * d // LANES, LANES), _F32), jax.ShapeDtypeStruct((n, LANES), _F32),
                   jax.ShapeDtypeStruct((1, LANES), _F32)],
        grid=(n // tm,),
        in_specs=specs,
        out_specs=[pl.BlockSpec((tm * d // LANES, LANES), row), pl.BlockSpec((tm, LANES), row),
                   pl.BlockSpec((1, LANES), const)],
        scratch_shapes=[pltpu.VMEM((1, LANES), _F32)],
        compiler_params=_params(("arbitrary",)),
        name="post_attn_router",
    )(*args)


def _dest_kernel(route_ref, ps_ref, o_ref):
    r = route_ref[...]
    lane = lax.broadcasted_iota(jnp.int32, r.shape, 1)
    lanef = lane.astype(_F32)
    ps = ps_ref[...]
    d = []
    for k in range(2):
        e = r[:, k:k + 1] + float(ROUTE_LANE0)
        d.append(jnp.sum(jnp.where(lanef == e, ps, 0.0), axis=-1, keepdims=True) + r[:, 4 + k:5 + k])
    cols = jnp.where(lane == 0, d[0], jnp.where(lane == 1, d[1], 0.0))
    o_ref[...] = cols.T[:TOP_K].astype(jnp.int32)


def _dest(route, pad_start_row):
    n = route.shape[0]
    tm = min(DEST_TM, n)
    return pl.pallas_call(
        _dest_kernel,
        out_shape=jax.ShapeDtypeStruct((TOP_K, n), jnp.int32),
        grid=(n // tm,),
        in_specs=[pl.BlockSpec((tm, LANES), lambda i: (i, 0)), pl.BlockSpec((1, LANES), lambda i: (0, 0))],
        out_specs=pl.BlockSpec((TOP_K, tm), lambda i: (0, i)),
        compiler_params=_params(("arbitrary",)),
        name="moe_dest",
    )(route, pad_start_row)


def _dispatch_kernel(seg_ref, dest_ref, x_ref, xs_hbm, zero_ref, zsem, sem):
    i = pl.program_id(0)
    ntok = TOK_TM
    nsl = x_ref.shape[0] // ntok
    blk_rows = MOE_BLOCK * nsl

    def zero_fill(e):
        first = pl.multiple_of((seg_ref[e] + seg_ref[N_EXPERTS + e] - MOE_BLOCK) * nsl, blk_rows)
        return pltpu.make_async_copy(zero_ref, xs_hbm.at[pl.ds(first, blk_rows)], zsem)

    def has_pad(e):
        return seg_ref[N_EXPERTS + e] > seg_ref[2 * N_EXPERTS + e]

    @pl.when(i == 0)
    def _():
        zero_ref[...] = jnp.zeros_like(zero_ref)

        def start(e, c):
            @pl.when(has_pad(e))
            def _():
                zero_fill(e).start()
            return c

        def wait(e, c):
            @pl.when(has_pad(e))
            def _():
                zero_fill(e).wait()
            return c

        lax.fori_loop(0, N_EXPERTS, start, 0)
        lax.fori_loop(0, N_EXPERTS, wait, 0)

        nvalid = seg_ref[3 * N_EXPERTS]
        nblk = xs_hbm.shape[0] // blk_rows

        def tail(j):
            first = pl.multiple_of((nvalid + j) * blk_rows, blk_rows)
            return pltpu.make_async_copy(zero_ref, xs_hbm.at[pl.ds(first, blk_rows)], zsem)

        def tail_start(j, c):
            @pl.when(nvalid + j < nblk)
            def _():
                tail(j).start()
            return c

        def tail_wait(j, c):
            @pl.when(nvalid + j < nblk)
            def _():
                tail(j).wait()
            return c

        lax.fori_loop(0, N_EXPERTS, tail_start, 0)
        lax.fori_loop(0, N_EXPERTS, tail_wait, 0)

    def row_copy(j, slot):
        return pltpu.make_async_copy(x_ref.at[pl.ds(pl.multiple_of(j * nsl, nsl), nsl)],
                                     xs_hbm.at[pl.ds(pl.multiple_of(slot * nsl, nsl), nsl)], sem)

    def issue(j, c):
        for k in range(2):
            row_copy(j, dest_ref[k, j]).start(priority=k)
        return c

    def drain(j, c):
        row_copy(0, 0).wait()
        return c

    lax.fori_loop(0, ntok, issue, 0, unroll=4)
    lax.fori_loop(0, 2 * ntok, drain, 0, unroll=8)


def _dispatch(x_tiles, dest, seg, cap, n):
    nsl = x_tiles.shape[0] // n
    tm = TOK_TM
    nt = n // tm
    return pl.pallas_call(
        _dispatch_kernel,
        out_shape=jax.ShapeDtypeStruct((cap * nsl, LANES), x_tiles.dtype),
        grid_spec=pltpu.PrefetchScalarGridSpec(
            num_scalar_prefetch=1,
            grid=(nt,),
            in_specs=[pl.BlockSpec((TOP_K, tm), lambda i, seg: (0, i), memory_space=pltpu.SMEM),
                      pl.BlockSpec((tm * nsl, LANES), lambda i, seg: (i, 0))],
            out_specs=pl.BlockSpec(memory_space=pl.ANY),
            scratch_shapes=[pltpu.VMEM((MOE_BLOCK * nsl, LANES), x_tiles.dtype), pltpu.SemaphoreType.DMA,
                            pltpu.SemaphoreType.DMA]),
        compiler_params=_params(("arbitrary",)),
        name="moe_dispatch",
    )(seg, dest, x_tiles)


def _expert_kernel(blk_ref, nvalid_ref, xs_ref, wg_ref, wu_ref, wd_ref, ys_ref, wgb, wub, wdb):
    i = pl.program_id(0)
    prev = blk_ref[jnp.maximum(i - 1, 0)]
    valid = i < nvalid_ref[0]

    @pl.when(valid & ((i == 0) | (blk_ref[i] != prev)))
    def _():
        wgb[...] = wg_ref[0].astype(_BF16)
        wub[...] = wu_ref[0].astype(_BF16)
        wdb[...] = wd_ref[0].astype(_BF16)

    @pl.when(valid)
    def _():
        d = wgb.shape[0]
        xb = _rows_from_tiles(xs_ref, MOE_BLOCK, d).astype(_BF16)
        gate = jnp.dot(xb, wgb[...], preferred_element_type=_F32)
        up = jnp.dot(xb, wub[...], preferred_element_type=_F32)
        mid = (gate * jax.nn.sigmoid(gate)) * up
        _rows_to_tiles(ys_ref, jnp.dot(mid.astype(_BF16), wdb[...], preferred_element_type=_F32))

    @pl.when(jnp.logical_not(valid))
    def _():
        ys_ref[...] = jnp.zeros_like(ys_ref)


def _experts(xs, blk_exp, nvalid, w_gate, w_up, w_down):
    d, de = w_gate.shape[1], w_gate.shape[2]
    nsl = d // LANES
    cap = xs.shape[0] // nsl
    nblk = cap // MOE_BLOCK

    def xmap(i, blk, nv):
        return (jnp.minimum(i, nv[0] - 1), 0)

    def wmap(i, blk, nv):
        return (blk[i], 0, 0)

    return pl.pallas_call(
        _expert_kernel,
        out_shape=jax.ShapeDtypeStruct((cap * nsl, LANES), _F32),
        grid_spec=pltpu.PrefetchScalarGridSpec(
            num_scalar_prefetch=2,
            grid=(nblk,),
            in_specs=[pl.BlockSpec((MOE_BLOCK * nsl, LANES), xmap),
                      pl.BlockSpec((1, d, de), wmap), pl.BlockSpec((1, d, de), wmap),
                      pl.BlockSpec((1, de, d), wmap)],
            out_specs=pl.BlockSpec((MOE_BLOCK * nsl, LANES), lambda i, blk, nv: (i, 0)),
            scratch_shapes=[pltpu.VMEM((d, de), _BF16), pltpu.VMEM((d, de), _BF16), pltpu.VMEM((de, d), _BF16)]),
        compiler_params=_params(("arbitrary",)),
        name="moe_experts",
    )(blk_exp, nvalid, xs, w_gate, w_up, w_down)


def _combine_kernel(dcur_ref, dnxt_ref, ys_hbm, h_ref, route_ref, g_ref, b_ref, o_ref, buf, sem):
    i = pl.program_id(0)
    n = pl.num_programs(0)
    tm = TOK_TM

    d = o_ref.shape[1]
    nsl = d // LANES

    def tile_copy(src_slot, slot, k, j):
        return pltpu.make_async_copy(ys_hbm.at[pl.ds(pl.multiple_of(src_slot * nsl, nsl), nsl)],
                                     buf.at[slot, k, pl.ds(pl.multiple_of(j * nsl, nsl), nsl)], sem.at[slot])

    def gather(dref, slot):
        def body(j, c):
            for k in range(2):
                tile_copy(dref[k, j], slot, k, j).start(priority=k)
            return c
        lax.fori_loop(0, tm, body, 0, unroll=4)

    @pl.when(i == 0)
    def _():
        gather(dcur_ref, 0)

    @pl.when(i + 1 < n)
    def _():
        gather(dnxt_ref, (i + 1) % 2)

    slot = i % 2

    def drain(j, c):
        tile_copy(0, slot, 0, 0).wait()
        return c

    lax.fori_loop(0, 2 * tm, drain, 0, unroll=8)
    r = route_ref[...]
    y = (r[:, 2:3] * _rows_from_tiles(buf.at[slot, 0], tm, d)
         + r[:, 3:4] * _rows_from_tiles(buf.at[slot, 1], tm, d))
    o_ref[...] = _layer_norm(ALPHA * _rows_from_tiles(h_ref, tm, d) + y, g_ref[...], b_ref[...])


def _combine(ys, dest, h1_tiles, route, g, b):
    n = route.shape[0]
    d = g.shape[1]
    nsl = d // LANES
    tm = TOK_TM
    nt = n // tm
    row = lambda i: (i, 0)
    const = lambda i: (0, 0)
    return pl.pallas_call(
        _combine_kernel,
        out_shape=jax.ShapeDtypeStruct((n, d), _F32),
        grid=(nt,),
        in_specs=[pl.BlockSpec((TOP_K, tm), lambda i: (0, i), memory_space=pltpu.SMEM),
                  pl.BlockSpec((TOP_K, tm), lambda i: (0, jnp.minimum(i + 1, nt - 1)), memory_space=pltpu.SMEM),
                  pl.BlockSpec(memory_space=pl.ANY),
                  pl.BlockSpec((tm * nsl, LANES), row), pl.BlockSpec((tm, LANES), row),
                  pl.BlockSpec((1, d), const), pl.BlockSpec((1, d), const)],
        out_specs=pl.BlockSpec((tm, d), row),
        scratch_shapes=[pltpu.VMEM((2, 2, tm * nsl, LANES), _F32), pltpu.SemaphoreType.DMA((2,))],
        compiler_params=_params(("arbitrary",)),
        name="moe_combine",
    )(dest, dest, ys, h1_tiles, route, g, b)


def _moe(h1_tiles, route, counts_row, w_gate, w_up, w_down, g, b):
    n = route.shape[0]
    cap = 2 * n + N_EXPERTS * MOE_BLOCK
    counts = counts_row[0, ROUTE_LANE0:ROUTE_LANE0 + N_EXPERTS].astype(jnp.int32)
    padded = (counts + MOE_BLOCK - 1) // MOE_BLOCK * MOE_BLOCK
    ends = jnp.cumsum(padded)
    pad_start = ends - padded
    nblk = cap // MOE_BLOCK
    blk_first = jnp.arange(nblk, dtype=jnp.int32) * MOE_BLOCK
    blk_exp = jnp.minimum(jnp.sum(blk_first[:, None] >= ends[None, :], axis=1), N_EXPERTS - 1).astype(jnp.int32)
    nvalid = (ends[-1:] // MOE_BLOCK).astype(jnp.int32)
    ps_row = jnp.zeros((1, LANES), _F32).at[0, ROUTE_LANE0:ROUTE_LANE0 + N_EXPERTS].set(pad_start.astype(_F32))
    dest = _dest(route, ps_row)
    seg = jnp.concatenate([pad_start, padded, counts, nvalid]).astype(jnp.int32)
    xs = _dispatch(h1_tiles, dest, seg, cap, n)
    ys = _experts(xs, blk_exp, nvalid, w_gate, w_up, w_down)
    return _combine(ys, dest, h1_tiles, route, g, b)


def _router_params(rg_w, rg_b, re_w, re_b):
    d = rg_w.shape[0]
    pad = LANES - N_GROUPS - N_EXPERTS
    rw = jnp.concatenate([rg_w, re_w, jnp.zeros((d, pad), _F32)], axis=1)
    rb = jnp.concatenate([rg_b, re_b.reshape(-1), jnp.zeros((pad,), _F32)]).reshape(1, LANES)
    return rw, rb


def kernel(x, t5_table, l0_w_qkv, l0_sink, l0_w_o, l0_ln1_g, l0_ln1_b, l0_rg_w, l0_rg_b, l0_re_w, l0_re_b, l0_w_gate, l0_w_up, l0_w_down, l0_ln2_g, l0_ln2_b, l1_w_qkv, l1_w_o, l1_ln1_g, l1_ln1_b, l1_rg_w, l1_rg_b, l1_re_w, l1_re_b, l1_w_gate, l1_w_up, l1_w_down, l1_ln2_g, l1_ln2_b, l2_w_qkv, l2_rpb, l2_w_o, l2_ln1_g, l2_ln1_b, l2_rg_w, l2_rg_b, l2_re_w, l2_re_b, l2_w_gate, l2_w_up, l2_w_down, l2_ln2_g, l2_ln2_b, l3_w_qkv, l3_sink, l3_w_o, l3_ln1_g, l3_ln1_b, l3_rg_w, l3_rg_b, l3_re_w, l3_re_b, l3_w_gate, l3_w_up, l3_w_down, l3_ln2_g, l3_ln2_b):
    mixers = [(l0_w_qkv, l0_sink, l0_w_o), (l1_w_qkv, None, l1_w_o), (l2_w_qkv, l2_rpb, l2_w_o),
              (l3_w_qkv, l3_sink, l3_w_o)]
    norm1 = [(l0_ln1_g, l0_ln1_b), (l1_ln1_g, l1_ln1_b), (l2_ln1_g, l2_ln1_b), (l3_ln1_g, l3_ln1_b)]
    routers = [(l0_rg_w, l0_rg_b, l0_re_w, l0_re_b), (l1_rg_w, l1_rg_b, l1_re_w, l1_re_b),
               (l2_rg_w, l2_rg_b, l2_re_w, l2_re_b), (l3_rg_w, l3_rg_b, l3_re_w, l3_re_b)]
    experts = [(l0_w_gate, l0_w_up, l0_w_down), (l1_w_gate, l1_w_up, l1_w_down),
               (l2_w_gate, l2_w_up, l2_w_down), (l3_w_gate, l3_w_up, l3_w_down)]
    norm2 = [(l0_ln2_g, l0_ln2_b), (l1_ln2_g, l1_ln2_b), (l2_ln2_g, l2_ln2_b), (l3_ln2_g, l3_ln2_b)]

    bsz, seq, d = x.shape
    n = bsz * seq
    inner = N_HEADS * HEAD_DIM
    scale = HEAD_DIM ** -0.5 * LOG2E
    h = x.reshape(n, d)
    bias_a = _band_bias_t(t5_table, A_BLOCK, A_WINDOW, 1, A_BAND)
    for i in range(DEPTH):
        kind = i % N_MIXERS
        w_qkv, extra, w_o = mixers[i]
        kvw = A_KV_HEADS * HEAD_DIM if kind == 0 else inner
        lses, unperm = (), None
        if kind == 1:
            perms = np.stack([_residue_perm(dil) for _, dil in B_BRANCHES if dil > 1])
            outs = _project(h, w_qkv.astype(_BF16), (inner, kvw, kvw), (scale, 1.0, 1.0), jnp.asarray(perms, _BF16))
            unperm = jnp.asarray(perms.transpose(0, 2, 1), _BF16)
            attn, lses = [], []
            for bi, (window, dil) in enumerate(B_BRANCHES):
                q, k, v = outs[3 * bi:3 * bi + 3]
                o, lse = _mixer_b_branch(q, k, v, t5_table, bsz, seq, window, dil)
                attn.append(o)
                lses.append(jnp.pad(lse.reshape(n, N_HEADS), ((0, 0), (0, LANES - N_HEADS))))
        else:
            q, k, v = _project(h, w_qkv.astype(_BF16), (inner, kvw, kvw), (scale, 1.0, 1.0))
            q, k, v = (t.reshape(bsz, seq, t.shape[1]) for t in (q, k, v))
            if kind == 0:
                attn = [_mixer_a(q, k, v, extra.astype(_F32) * LOG2E, bias_a, bsz, seq)]
            else:
                attn = [_mixer_c(q, k, v, _na_bias_t(extra, seq // GRID_W), bsz, seq)]
        attn = [a.reshape(n, inner) for a in attn]
        g1, b1 = norm1[i]
        rw, rb = _router_params(*routers[i])
        h1, route, counts = _post_attn(attn, lses, h, w_o.astype(_BF16), g1.reshape(1, d), b1.reshape(1, d), rw, rb,
                                       unperm)
        g2, b2 = norm2[i]
        h = _moe(h1, route, counts, *experts[i], g2.reshape(1, d), b2.reshape(1, d))
    return h.reshape(bsz, seq, d)
```

```python
import functools
import math

import jax
import jax.numpy as jnp
import numpy as np
from jax import lax
from jax.experimental import pallas as pl
from jax.experimental.pallas import tpu as pltpu

HEAD_DIM = 64
N_HEADS = 16
A_KV_HEADS = 4
A_WINDOW = 128
A_BLOCK = 128
B_BRANCHES = ((128, 1), (512, 4), (2048, 16))
GRID_W = 64
NA_ROWS = 8
NA_COLS = 16
T5_BUCKETS = 32
T5_MAX_DIST = 1024
N_GROUPS = 4
EXPERTS_PER_GROUP = 8
N_EXPERTS = N_GROUPS * EXPERTS_PER_GROUP
TOP_K = 2
DEPTH = 4
N_MIXERS = 3
ALPHA = (2 * DEPTH) ** 0.25
LN_EPS = 1e-5
NEG_INF = -1e30

LANES = 128
PAIR = 2 * HEAD_DIM
BF16_SUBLANES = 16
VT_ROWS = PAIR + BF16_SUBLANES
LOG2E = math.log2(math.e)
LN2 = math.log(2.0)
VMEM_LIMIT = 56 * 1024 * 1024

PROJ_TM = 512
PROJ_CHUNK = 512
POST_TM = 1024
POST_SUB = 256
TOK_TM = 256
DEST_TM = 4096
MOE_BLOCK = 512
A_TQ = 1024
A_BAND = 3
A_SUB_PER_GROUP = 2
B_BLOCK = 128
RESIDUE_RUN = BF16_SUBLANES
PERM_CHUNK = 256
TRANSPOSE_UNROLL = 4
UNITS_PER_GROUP = 8
C_UNIT_ROWS = 2
C_BAND_ROWS = 10
C_UNITS_PER_GROUP = 4
C_ROWS_PER_STEP = 32
ROUTE_LANE0 = N_GROUPS

_F32 = jnp.float32
_BF16 = jnp.bfloat16


def _params(sem, vmem=VMEM_LIMIT, flags=None):
    return pltpu.CompilerParams(dimension_semantics=sem, vmem_limit_bytes=vmem, flags=flags)


def _proj_kernel(x_ref, w_ref, *refs, widths, scales, nperm):
    perm_ref = refs[0] if nperm else None
    o_refs = refs[1:] if nperm else refs
    nout = len(widths)
    xb = x_ref[...].astype(_BF16)
    col = 0
    for t, (width, scale) in enumerate(zip(widths, scales)):
        for c in range(0, width, PROJ_CHUNK):
            cw = min(PROJ_CHUNK, width - c)
            acc = jnp.dot(xb, w_ref[:, col + c:col + c + cw], preferred_element_type=_F32)
            if scale != 1.0:
                acc = acc * scale
            ob = acc.astype(_BF16)
            o_refs[t][:, c:c + cw] = ob
            for p in range(nperm):
                for r0 in range(0, ob.shape[0], PERM_CHUNK):
                    moved = jnp.dot(perm_ref[p], ob[r0:r0 + PERM_CHUNK], preferred_element_type=_F32)
                    o_refs[(p + 1) * nout + t][r0:r0 + PERM_CHUNK, c:c + cw] = moved.astype(_BF16)
        col += width


def _project(x2, w_bf16, widths, scales, perms=None):
    n, d = x2.shape
    ctot = sum(widths)
    nperm = 0 if perms is None else perms.shape[0]
    args = [x2, w_bf16] + ([perms] if nperm else [])
    specs = [pl.BlockSpec((PROJ_TM, d), lambda i: (i, 0)), pl.BlockSpec((d, ctot), lambda i: (0, 0))]
    if nperm:
        specs.append(pl.BlockSpec(perms.shape, lambda i: (0, 0, 0)))
    return pl.pallas_call(
        functools.partial(_proj_kernel, widths=tuple(widths), scales=tuple(scales), nperm=nperm),
        out_shape=[jax.ShapeDtypeStruct((n, w), _BF16) for w in widths] * (nperm + 1),
        grid=(n // PROJ_TM,),
        in_specs=specs,
        out_specs=[pl.BlockSpec((PROJ_TM, w), lambda i: (i, 0)) for w in widths] * (nperm + 1),
        compiler_params=_params(("arbitrary",)),
        name="qkv_proj",
    )(*args)


def _residue_perm(dil):
    span = dil * RESIDUE_RUN
    p = np.zeros((PERM_CHUNK, PERM_CHUNK), np.float32)
    for base in range(0, PERM_CHUNK, span):
        for r in range(dil):
            for u in range(RESIDUE_RUN):
                p[base + r * RESIDUE_RUN + u, base + u * dil + r] = 1.0
    return p


def _one():
    return jnp.ones((1,), jnp.int32)


def _lane_lo(rows=1):
    return lax.broadcasted_iota(jnp.int32, (rows, PAIR), 1) < HEAD_DIM


def _swap_halves(x):
    return pltpu.roll(x.astype(_F32), HEAD_DIM, axis=1).astype(_BF16)


class _PairUnit:
    def __init__(self, q2, k, vt, bias, emit, sink_a=None, sink_b=None):
        self.q2, self.k, self.vt, self.bias, self.emit = q2, k, vt, bias, emit
        self.sink_a, self.sink_b = sink_a, sink_b


def _attend_groups(ngroups, units_of, scratch, one):
    assert ngroups % 2 == 0
    lo = _lane_lo()
    nt = (((1,), (1,)), ((), ()))

    def sink_row(unit, m2):
        lane = lax.broadcasted_iota(jnp.int32, (1, m2), 1)
        return jnp.where(lane < m2 // 2, unit.sink_a, unit.sink_b)

    def scores(g, parity):
        s_scr, m_scr = scratch[parity]
        for u, unit in enumerate(units_of(g, parity)):
            q2 = unit.q2()
            zero = jnp.zeros_like(q2)
            qq = jnp.concatenate([jnp.where(lo, q2, zero), jnp.where(lo, zero, q2)], axis=0)
            s = lax.dot_general(unit.k(), qq, nt, preferred_element_type=_F32) + unit.bias()
            m = jnp.max(s, axis=0, keepdims=True)
            if unit.sink_a is not None:
                m = jnp.maximum(m, sink_row(unit, s.shape[1]))
            s_scr[u] = s
            m_scr[u] = m

    def values(g, parity):
        s_scr, m_scr = scratch[parity]
        for u, unit in enumerate(units_of(g, parity)):
            m = m_scr[u]
            e = jnp.exp2(s_scr[u] - m)
            ot = jnp.dot(unit.vt(), e.astype(_BF16), preferred_element_type=_F32)
            den = ot[PAIR:PAIR + 1, :]
            if unit.sink_a is not None:
                den = den + jnp.exp2(sink_row(unit, e.shape[1]) - m)
            ot = ot[:PAIR] * (1.0 / den)
            mq = ot.shape[1] // 2
            top = lax.broadcasted_iota(jnp.int32, (PAIR, 1), 0) < HEAD_DIM
            pair_t = jnp.where(top, ot[:, :mq], ot[:, mq:])
            unit.emit(pair_t.T, (m + jnp.log2(den)) * LN2)

    def region(*work):
        def body(_, carry):
            for fn, g, parity in work:
                fn(g, parity)
            return carry
        lax.fori_loop(0, one, body, 0)

    region((scores, 0, 0))

    def steady(i, carry):
        g = 2 * i
        region((values, g, 0), (scores, g + 1, 1))
        region((values, g + 1, 1), (scores, g + 2, 0))
        return carry

    lax.fori_loop(0, ngroups // 2 - 1, steady, 0)
    region((values, ngroups - 2, 0), (scores, ngroups - 1, 1))
    region((values, ngroups - 1, 1))


def _transpose_blocks(v_ref_rows, nblk, store):
    def body(j, carry):
        rows = pl.ds(pl.multiple_of(j * LANES, LANES), LANES)
        vt = v_ref_rows(rows).T
        store(j, jnp.concatenate([vt, jnp.ones((VT_ROWS - PAIR, LANES), _BF16)], axis=0))
        return carry
    lax.fori_loop(0, nblk, body, 0, unroll=TRANSPOSE_UNROLL)


def _t5_bucket(rel):
    half = T5_BUCKETS // 2
    exact = half // 2
    n = np.abs(rel)
    big = exact + (np.log(np.maximum(n, 1) / exact) / math.log(T5_MAX_DIST / exact) * (half - exact)).astype(np.int32)
    return ((rel > 0) * half + np.where(n < exact, n, np.minimum(big, half - 1))).astype(np.int32)


def _band_bias_t(t5_table, blk, window, dil, nband):
    tk = nband * blk
    off = tk - 1
    rel = np.arange(-off, tk)
    vec = jnp.take(t5_table, jnp.asarray(_t5_bucket(rel * dil)), axis=0).astype(_F32)
    vec = jnp.where(jnp.asarray(np.abs(rel) <= window)[:, None], vec * LOG2E, NEG_INF).T
    h = vec.shape[0]
    vec = jnp.pad(vec, ((0, 0), (0, 1)))
    w = 2 * tk - 1
    skew = jnp.tile(vec, (1, blk))[:, :blk * w].reshape(h, blk, w)
    t = jnp.stack([skew[:, :, off - v * blk:off - v * blk + tk] for v in range(nband)], axis=1)
    t = t.reshape(h // 2, 2, nband, blk, tk).transpose(0, 2, 4, 1, 3)
    return t.reshape(h // 2, nband, tk, 2 * blk)


def _band_window(gb, blk, length, nband):
    first = jnp.clip(gb - 1, 0, length // blk - nband)
    return first, gb - first


def _attn_a_kernel(sink_ref, one_ref, q_ref, k_ref, v_ref, bias_ref, o_ref, kd_ref, vdt_ref,
                   s0_scr, m0_scr, s1_scr, m1_scr, *, seq):
    kvb = pl.program_id(0)
    qt = pl.program_id(2)
    one = one_ref[0]
    nblk = seq // A_BLOCK

    @pl.when(qt == 0)
    def _():
        lo = _lane_lo()

        def fill(j, carry):
            rows = pl.ds(pl.multiple_of(j * A_BLOCK, A_BLOCK), A_BLOCK)
            kk = k_ref[0, rows, :]
            ks = _swap_halves(kk)
            kd_ref[0, rows, :] = jnp.where(lo, kk, ks)
            kd_ref[1, rows, :] = jnp.where(lo, ks, kk)
            vt = v_ref[0, rows, :].T
            ones = jnp.ones((VT_ROWS - PAIR, A_BLOCK), _BF16)
            vdt_ref[0, j] = jnp.concatenate([vt[:HEAD_DIM], vt[:HEAD_DIM], ones], axis=0)
            vdt_ref[1, j] = jnp.concatenate([vt[HEAD_DIM:], vt[HEAD_DIM:], ones], axis=0)
            return carry

        lax.fori_loop(0, nblk, fill, 0, unroll=TRANSPOSE_UNROLL)

    nsub = A_TQ // A_BLOCK
    npair = q_ref.shape[2] // PAIR
    hbase = kvb * (2 * npair)

    def units_of(g, parity):
        units = []
        for j in range(A_SUB_PER_GROUP):
            si = g * A_SUB_PER_GROUP + j
            first, var = _band_window(qt * nsub + si, A_BLOCK, seq, A_BAND)
            rows = pl.ds(pl.multiple_of(si * A_BLOCK, A_BLOCK), A_BLOCK)
            band = pl.ds(pl.multiple_of(first * A_BLOCK, A_BLOCK), A_BAND * A_BLOCK)
            for jj in range(npair):
                hk = jj // (npair // 2)
                cols = slice(jj * PAIR, (jj + 1) * PAIR)

                def emit(o, lse, rows=rows, cols=cols):
                    o_ref[0, rows, cols] = o.astype(o_ref.dtype)

                units.append(_PairUnit(
                    q2=lambda rows=rows, cols=cols: q_ref[0, rows, cols],
                    k=lambda hk=hk, band=band: kd_ref[hk, band, :],
                    vt=lambda hk=hk, first=first: jnp.concatenate(
                        [vdt_ref[hk, first + t] for t in range(A_BAND)], axis=1),
                    bias=lambda jj=jj, var=var: bias_ref[jj, var],
                    emit=emit, sink_a=sink_ref[hbase + 2 * jj], sink_b=sink_ref[hbase + 2 * jj + 1]))
        return units

    _attend_groups(nsub // A_SUB_PER_GROUP, units_of, ((s0_scr, m0_scr), (s1_scr, m1_scr)), one)


def _mixer_a(q, k, v, sink, bias, bsz, seq):
    nkvb = k.shape[2] // PAIR
    qw = q.shape[2] // nkvb
    npair = qw // PAIR
    tk = A_BAND * A_BLOCK
    nunits = A_SUB_PER_GROUP * npair
    group_scratch = [pltpu.VMEM((nunits, tk, 2 * A_BLOCK), _F32), pltpu.VMEM((nunits, 1, 2 * A_BLOCK), _F32)]
    return pl.pallas_call(
        functools.partial(_attn_a_kernel, seq=seq),
        out_shape=jax.ShapeDtypeStruct(q.shape, _BF16),
        grid=(nkvb, bsz, seq // A_TQ),
        in_specs=[pl.BlockSpec(memory_space=pltpu.SMEM), pl.BlockSpec(memory_space=pltpu.SMEM),
                  pl.BlockSpec((1, A_TQ, qw), lambda c, b, t: (b, t, c)),
                  pl.BlockSpec((1, seq, PAIR), lambda c, b, t: (b, 0, c)),
                  pl.BlockSpec((1, seq, PAIR), lambda c, b, t: (b, 0, c)),
                  pl.BlockSpec((npair, A_BAND, tk, 2 * A_BLOCK), lambda c, b, t: (c, 0, 0, 0))],
        out_specs=pl.BlockSpec((1, A_TQ, qw), lambda c, b, t: (b, t, c)),
        scratch_shapes=[pltpu.VMEM((2, seq, PAIR), _BF16), pltpu.VMEM((2, seq // A_BLOCK, VT_ROWS, A_BLOCK), _BF16)]
        + group_scratch + group_scratch,
        compiler_params=_params(("arbitrary", "arbitrary", "arbitrary")),
        name="mixer_a",
    )(sink, _one(), q, k, v, bias)


def _attn_band_kernel(one_ref, q_ref, k_ref, v_ref, bias_ref, o_ref, lse_ref, vt_ref,
                      s0_scr, m0_scr, s1_scr, m1_scr, *, length, npairs, nband, chunked):
    col = pl.program_id(2)
    one = one_ref[0]
    nblk = length // B_BLOCK
    runs = B_BLOCK // RESIDUE_RUN

    def load(ref, blk0, nblocks, cols):
        if chunked:
            r0 = pl.multiple_of(blk0 * runs, runs)
            return ref[pl.ds(r0, nblocks * runs), :, cols].reshape(nblocks * B_BLOCK, PAIR)
        return ref[0, pl.ds(pl.multiple_of(blk0 * B_BLOCK, B_BLOCK), nblocks * B_BLOCK), cols]

    def store_out(blk0, cols, val):
        if chunked:
            o_ref[pl.ds(pl.multiple_of(blk0 * runs, runs), runs), :, cols] = val.reshape(runs, RESIDUE_RUN, PAIR)
        else:
            o_ref[0, pl.ds(pl.multiple_of(blk0 * B_BLOCK, B_BLOCK), B_BLOCK), cols] = val

    for p in range(npairs):
        pcols = slice(p * PAIR, (p + 1) * PAIR)

        def fill(j, carry, p=p, pcols=pcols):
            vt = load(v_ref, j, 1, pcols).T
            vt_ref[p, j] = jnp.concatenate([vt, jnp.ones((VT_ROWS - PAIR, B_BLOCK), _BF16)], axis=0)
            return carry

        lax.fori_loop(0, nblk, fill, 0, unroll=min(TRANSPOSE_UNROLL, nblk))

    def unit(gb, p):
        first, var = _band_window(gb, B_BLOCK, length, nband)
        cols = slice(p * PAIR, (p + 1) * PAIR)

        def emit(o, lse):
            store_out(gb, cols, o.astype(o_ref.dtype))
            ha = col * (2 * npairs) + 2 * p
            lse_ref[0, 0, gb, pl.ds(ha, 1), :] = lse[:, :B_BLOCK]
            lse_ref[0, 0, gb, pl.ds(ha + 1, 1), :] = lse[:, B_BLOCK:]

        return _PairUnit(
            q2=lambda: load(q_ref, gb, 1, cols), k=lambda: load(k_ref, first, nband, cols),
            vt=lambda: jnp.concatenate([vt_ref[p, first + t] for t in range(nband)], axis=1),
            bias=lambda: bias_ref[p, var], emit=emit)

    if npairs == 1:
        ngroups = nblk // UNITS_PER_GROUP
        units_of = lambda g, parity: [unit(g * UNITS_PER_GROUP + j, 0) for j in range(UNITS_PER_GROUP)]
    else:
        assert npairs == UNITS_PER_GROUP
        ngroups = nblk
        units_of = lambda g, parity: [unit(g, p) for p in range(npairs)]
    _attend_groups(ngroups, units_of, ((s0_scr, m0_scr), (s1_scr, m1_scr)), one)


def _mixer_b_branch(q, k, v, t5_table, bsz, seq, window, dil):
    length = seq // dil
    width = q.shape[1]
    nblk = length // B_BLOCK
    nband = min(3, nblk)
    tk = nband * B_BLOCK
    npairs = (width // PAIR) if length * width * 2 <= 2 * 1024 * 1024 else 1
    ncol = width // (npairs * PAIR)
    bw = npairs * PAIR
    bias = _band_bias_t(t5_table, B_BLOCK, window // 2 // dil, dil, nband)
    chunked = dil > 1
    if chunked:
        nrun = length // RESIDUE_RUN
        shape = (bsz * nrun, dil, RESIDUE_RUN, width)
        spec = pl.BlockSpec((nrun, None, RESIDUE_RUN, bw), lambda b, r, c: (b, r, 0, c))
    else:
        shape = (bsz, seq, width)
        spec = pl.BlockSpec((1, seq, bw), lambda b, r, c: (b, 0, c))

    o, lse = pl.pallas_call(
        functools.partial(_attn_band_kernel, length=length, npairs=npairs, nband=nband, chunked=chunked),
        out_shape=[jax.ShapeDtypeStruct(shape, _BF16),
                   jax.ShapeDtypeStruct((bsz, dil, nblk, N_HEADS, B_BLOCK), _F32)],
        grid=(bsz, dil, ncol),
        in_specs=[pl.BlockSpec(memory_space=pltpu.SMEM), spec, spec, spec,
                  pl.BlockSpec((npairs, nband, tk, 2 * B_BLOCK), lambda b, r, c: (c, 0, 0, 0))],
        out_specs=[spec, pl.BlockSpec((1, 1, nblk, N_HEADS, B_BLOCK), lambda b, r, c: (b, r, 0, 0, 0))],
        scratch_shapes=[pltpu.VMEM((npairs, nblk, VT_ROWS, B_BLOCK), _BF16)] + 2 * [
            pltpu.VMEM((UNITS_PER_GROUP, tk, 2 * B_BLOCK), _F32), pltpu.VMEM((UNITS_PER_GROUP, 1, 2 * B_BLOCK), _F32)],
        compiler_params=_params(("arbitrary", "arbitrary", "arbitrary")),
        name=f"mixer_b_dil{dil}",
    )(_one(), q.reshape(shape), k.reshape(shape), v.reshape(shape), bias)
    lse = lse.transpose(0, 2, 4, 1, 3).reshape(bsz, seq, N_HEADS)
    return o.reshape(bsz * seq, width), lse


def _na_bias_t(rpb, rows):
    canon = (0, 2, 4, rows - 4, rows - 2)
    h = rpb.shape[0]
    kc = np.arange(GRID_W)[:, None]
    qc = np.arange(GRID_W)[None, :]
    ws = np.clip(qc - NA_COLS // 2, 0, GRID_W - NA_COLS)
    col_ok = (kc >= ws) & (kc < ws + NA_COLS)
    col_idx = np.clip(kc - qc, 1 - NA_COLS, NA_COLS - 1) + NA_COLS - 1
    colpart = jnp.where(jnp.asarray(col_ok), rpb.astype(_F32)[:, :, jnp.asarray(col_idx)] * LOG2E, NEG_INF)
    masked = jnp.full((h, GRID_W, GRID_W), NEG_INF, _F32)
    variants = []
    for i in canon:
        start = int(np.clip(i - NA_ROWS // 2, 0, rows - C_BAND_ROWS))
        per_key_row = []
        for kr in range(C_BAND_ROWS):
            per_query_row = []
            for qr in range(C_UNIT_ROWS):
                iq, kabs = i + qr, start + kr
                first_key_row = int(np.clip(iq - NA_ROWS // 2, 0, rows - NA_ROWS))
                inside = first_key_row <= kabs < first_key_row + NA_ROWS
                per_query_row.append(colpart[:, kabs - iq + NA_ROWS - 1] if inside else masked)
            per_key_row.append(jnp.stack(per_query_row, axis=2))
        variants.append(jnp.stack(per_key_row, axis=1))
    b = jnp.stack(variants, axis=1)
    nq, nk = C_UNIT_ROWS * GRID_W, C_BAND_ROWS * GRID_W
    b = b.reshape(h // 2, 2, len(canon), nk, nq)
    return b.transpose(0, 2, 3, 1, 4).reshape(h // 2, len(canon), nk, 2 * nq)


def _attn_c_kernel(one_ref, q_ref, k_ref, v_ref, bias_ref, o_ref, vt_ref, s0_scr, m0_scr, s1_scr, m1_scr, *, rows):
    rt = pl.program_id(2)
    nq = C_UNIT_ROWS * GRID_W
    band_blocks = C_BAND_ROWS * GRID_W // LANES

    @pl.when(rt == 0)
    def _():
        def store(j, blk):
            vt_ref[j] = blk
        _transpose_blocks(lambda r: v_ref[0, r, :], rows * GRID_W // LANES, store)

    def unit(u):
        i = rt * C_ROWS_PER_STEP + C_UNIT_ROWS * u
        start = jnp.clip(i - NA_ROWS // 2, 0, rows - C_BAND_ROWS)
        var = lax.shift_right_logical(i - start, 1)
        first = lax.shift_right_logical(start, 1)
        keys = pl.ds(pl.multiple_of(first * LANES, LANES), C_BAND_ROWS * GRID_W)
        qrows = pl.ds(pl.multiple_of(u * nq, nq), nq)

        def emit(o, lse):
            o_ref[0, qrows, :] = o.astype(o_ref.dtype)

        return _PairUnit(
            q2=lambda: q_ref[0, qrows, :], k=lambda: k_ref[0, keys, :],
            vt=lambda: jnp.concatenate([vt_ref[first + t] for t in range(band_blocks)], axis=1),
            bias=lambda: bias_ref[0, var], emit=emit)

    ngroups = C_ROWS_PER_STEP // C_UNIT_ROWS // C_UNITS_PER_GROUP
    units_of = lambda g, parity: [unit(g * C_UNITS_PER_GROUP + j) for j in range(C_UNITS_PER_GROUP)]
    _attend_groups(ngroups, units_of, ((s0_scr, m0_scr), (s1_scr, m1_scr)), one_ref[0])


def _mixer_c(q, k, v, bias, bsz, seq):
    rows = seq // GRID_W
    tq = C_ROWS_PER_STEP * GRID_W
    npair = q.shape[2] // PAIR
    nq, nk = C_UNIT_ROWS * GRID_W, C_BAND_ROWS * GRID_W
    return pl.pallas_call(
        functools.partial(_attn_c_kernel, rows=rows),
        out_shape=jax.ShapeDtypeStruct(q.shape, _BF16),
        grid=(npair, bsz, seq // tq),
        in_specs=[pl.BlockSpec(memory_space=pltpu.SMEM),
                  pl.BlockSpec((1, tq, PAIR), lambda c, b, t: (b, t, c)),
                  pl.BlockSpec((1, seq, PAIR), lambda c, b, t: (b, 0, c)),
                  pl.BlockSpec((1, seq, PAIR), lambda c, b, t: (b, 0, c)),
                  pl.BlockSpec((1, bias.shape[1], nk, 2 * nq), lambda c, b, t: (c, 0, 0, 0))],
        out_specs=pl.BlockSpec((1, tq, PAIR), lambda c, b, t: (b, t, c)),
        scratch_shapes=[pltpu.VMEM((seq // LANES, VT_ROWS, LANES), _BF16)] + 2 * [
            pltpu.VMEM((C_UNITS_PER_GROUP, nk, 2 * nq), _F32), pltpu.VMEM((C_UNITS_PER_GROUP, 1, 2 * nq), _F32)],
        compiler_params=_params(("arbitrary", "arbitrary", "arbitrary")),
        name="mixer_c",
    )(_one(), q, k, v, bias)


def _layer_norm(z, g, b):
    mu = jnp.mean(z, axis=-1, keepdims=True)
    zc = z - mu
    var = jnp.mean(zc * zc, axis=-1, keepdims=True)
    return zc * lax.rsqrt(var + LN_EPS) * g + b


def _rows_to_tiles(ref, x):
    nrows, d = x.shape
    nsl = d // LANES
    for s in range(nsl):
        ref[pl.ds(s, nrows, stride=nsl), :] = x[:, s * LANES:(s + 1) * LANES]


def _rows_from_tiles(ref, nrows, d):
    nsl = d // LANES
    return jnp.concatenate([ref[pl.ds(s, nrows, stride=nsl), :] for s in range(nsl)], axis=1)


def _split_bf16(x):
    hi = x.astype(_BF16)
    return hi, (x - hi.astype(_F32)).astype(_BF16)


def _router_logits(hn, rw_ref, rb_ref):
    h_hi, h_lo = _split_bf16(hn)
    w_hi, w_lo = _split_bf16(rw_ref[...])
    return (jnp.dot(h_hi, w_hi, preferred_element_type=_F32)
            + (jnp.dot(h_hi, w_lo, preferred_element_type=_F32)
               + jnp.dot(h_lo, w_hi, preferred_element_type=_F32))) + rb_ref[...]


def _route(logits, tri_ref, carry_ref):
    tm = logits.shape[0]
    lane = lax.broadcasted_iota(jnp.int32, (tm, LANES), 1).astype(_F32)
    ninf = -jnp.inf
    is_grp = lane < N_GROUPS
    lg = jnp.where(is_grp, logits, ninf)
    gmax = jnp.max(lg, axis=-1, keepdims=True)
    grp = jnp.min(jnp.where(lg == gmax, lane, float(LANES)), axis=-1, keepdims=True)
    p_grp = 1.0 / jnp.sum(jnp.where(is_grp, jnp.exp(lg - gmax), 0.0), axis=-1, keepdims=True)
    lo_lane = ROUTE_LANE0 + EXPERTS_PER_GROUP * grp
    in_grp = (lane >= lo_lane) & (lane < lo_lane + EXPERTS_PER_GROUP)
    le = jnp.where(in_grp, logits, ninf)
    v1 = jnp.max(le, axis=-1, keepdims=True)
    i1 = jnp.min(jnp.where(le == v1, lane, float(LANES)), axis=-1, keepdims=True)
    le2 = jnp.where(lane == i1, ninf, le)
    v2 = jnp.max(le2, axis=-1, keepdims=True)
    i2 = jnp.min(jnp.where(le2 == v2, lane, float(LANES)), axis=-1, keepdims=True)
    t = jnp.exp(v2 - v1)
    s1 = 1.0 / (1.0 + t)
    g1 = p_grp * s1
    g2 = p_grp * (t * s1)
    sel1 = lane == i1
    sel2 = lane == i2
    onehot = jnp.where(sel1 | sel2, 1.0, 0.0)
    sub = tri_ref.shape[0]
    carry = carry_ref[...]
    parts = []
    for r0 in range(0, tm, sub):
        part = onehot[r0:r0 + sub]
        parts.append(jnp.dot(tri_ref[...], part.astype(_BF16), preferred_element_type=_F32) + carry)
        carry = carry + jnp.sum(part, axis=0, keepdims=True)
    carry_ref[...] = carry
    before = jnp.concatenate(parts, axis=0) if len(parts) > 1 else parts[0]
    rank1 = jnp.sum(jnp.where(sel1, before, 0.0), axis=-1, keepdims=True)
    rank2 = jnp.sum(jnp.where(sel2, before, 0.0), axis=-1, keepdims=True)
    out = jnp.zeros((tm, LANES), _F32)
    for idx, val in enumerate((i1 - ROUTE_LANE0, i2 - ROUTE_LANE0, g1, g2, rank1, rank2)):
        out = jnp.where(lane == idx, val, out)
    return out


def _post_attn_kernel(*refs, n_branch):
    attn_refs = refs[:n_branch]
    pos = n_branch
    if n_branch > 1:
        lse_refs = refs[pos:pos + n_branch]
        expand_ref, unperm_ref = refs[pos + n_branch:pos + n_branch + 2]
        pos += n_branch + 2
    wo_ref, h_ref, g_ref, b_ref, rw_ref, rb_ref, tri_ref = refs[pos:pos + 7]
    h1_ref, route_ref, counts_ref, carry_ref = refs[pos + 7:pos + 11]

    @pl.when(pl.program_id(0) == 0)
    def _():
        carry_ref[...] = jnp.zeros_like(carry_ref)

    d = h_ref.shape[1]
    logits = []
    for sub in range(h_ref.shape[0] // POST_SUB):
        rows = slice(sub * POST_SUB, (sub + 1) * POST_SUB)
        if n_branch == 1:
            a = attn_refs[0][rows, :]
        else:
            lses = [r[rows, :] for r in lse_refs]
            mx = functools.reduce(jnp.maximum, lses)
            es = [jnp.exp(l - mx) for l in lses]
            inv = 1.0 / functools.reduce(jnp.add, es)
            mixed = None
            for bi, (e, a_ref) in enumerate(zip(es, attn_refs)):
                w_hi, w_lo = _split_bf16(e * inv)
                w_full = (jnp.dot(w_hi, expand_ref[...], preferred_element_type=_F32)
                          + jnp.dot(w_lo, expand_ref[...], preferred_element_type=_F32))
                if bi == 0:
                    a_nat = a_ref[rows, :].astype(_F32)
                else:
                    a_nat = jnp.dot(unperm_ref[bi - 1], a_ref[rows, :], preferred_element_type=_F32)
                term = w_full * a_nat
                mixed = term if mixed is None else mixed + term
            a = mixed.astype(_BF16)
        y = jnp.dot(a, wo_ref[...], preferred_element_type=_F32)
        hn = _layer_norm(ALPHA * h_ref[rows, :] + y, g_ref[...], b_ref[...])
        nsl = d // LANES
        _rows_to_tiles(h1_ref.at[pl.ds(sub * POST_SUB * nsl, POST_SUB * nsl)], hn)
        logits.append(_router_logits(hn, rw_ref, rb_ref))
    route_ref[...] = _route(jnp.concatenate(logits, axis=0), tri_ref, carry_ref)
    counts_ref[...] = carry_ref[...]


def _post_attn(attn, lses, h2, w_o, g, b, rw, rb, unperm=None):
    n, d = h2.shape
    tm = POST_TM
    nb = len(attn)
    tri = jnp.asarray(np.tril(np.ones((POST_SUB, POST_SUB), np.float32), -1), _BF16)
    row = lambda i: (i, 0)
    const = lambda i: (0, 0)
    args = list(attn)
    specs = [pl.BlockSpec((tm, d), row) for _ in attn]
    if nb > 1:
        expand = np.zeros((LANES, d), np.float32)
        for hd in range(N_HEADS):
            expand[hd, hd * HEAD_DIM:(hd + 1) * HEAD_DIM] = 1.0
        args += list(lses) + [jnp.asarray(expand, _BF16), unperm]
        specs += [pl.BlockSpec((tm, LANES), row) for _ in lses] + [
            pl.BlockSpec((LANES, d), const), pl.BlockSpec(unperm.shape, lambda i: (0, 0, 0))]
    args += [w_o, h2, g, b, rw, rb, tri]
    specs += [pl.BlockSpec((d, d), const), pl.BlockSpec((tm, d), row), pl.BlockSpec((1, d), const),
              pl.BlockSpec((1, d), const), pl.BlockSpec((d, LANES), const), pl.BlockSpec((1, LANES), const),
              pl.BlockSpec((POST_SUB, POST_SUB), const)]
    return pl.pallas_call(
        functools.partial(_post_attn_kernel, n_branch=nb),
        out_shape=[jax.ShapeDtypeStruct((n * d // LANES, LANES), _F32), jax.ShapeDtypeStruct((n, LANES), _F32),
                   jax.ShapeDtypeStruct((1, LANES), _F32)],
        grid=(n // tm,),
        in_specs=specs,
        out_specs=[pl.BlockSpec((tm * d // LANES, LANES), row), pl.BlockSpec((tm, LANES), row),
                   pl.BlockSpec((1, LANES), const)],
        scratch_shapes=[pltpu.VMEM((1, LANES), _F32)],
        compiler_params=_params(("arbitrary",)),
        name="post_attn_router",
    )(*args)


def _dest_kernel(route_ref, ps_ref, o_ref):
    r = route_ref[...]
    lane = lax.broadcasted_iota(jnp.int32, r.shape, 1)
    lanef = lane.astype(_F32)
    ps = ps_ref[...]
    d = []
    for k in range(2):
        e = r[:, k:k + 1] + float(ROUTE_LANE0)
        d.append(jnp.sum(jnp.where(lanef == e, ps, 0.0), axis=-1, keepdims=True) + r[:, 4 + k:5 + k])
    cols = jnp.where(lane == 0, d[0], jnp.where(lane == 1, d[1], 0.0))
    o_ref[...] = cols.T[:TOP_K].astype(jnp.int32)


def _dest(route, pad_start_row):
    n = route.shape[0]
    tm = min(DEST_TM, n)
    return pl.pallas_call(
        _dest_kernel,
        out_shape=jax.ShapeDtypeStruct((TOP_K, n), jnp.int32),
        grid=(n // tm,),
        in_specs=[pl.BlockSpec((tm, LANES), lambda i: (i, 0)), pl.BlockSpec((1, LANES), lambda i: (0, 0))],
        out_specs=pl.BlockSpec((TOP_K, tm), lambda i: (0, i)),
        compiler_params=_params(("arbitrary",)),
        name="moe_dest",
    )(route, pad_start_row)


def _dispatch_kernel(seg_ref, dest_ref, x_ref, xs_hbm, zero_ref, zsem, sem):
    i = pl.program_id(0)
    ntok = TOK_TM
    nsl = x_ref.shape[0] // ntok
    blk_rows = MOE_BLOCK * nsl

    def zero_fill(e):
        first = pl.multiple_of((seg_ref[e] + seg_ref[N_EXPERTS + e] - MOE_BLOCK) * nsl, blk_rows)
        return pltpu.make_async_copy(zero_ref, xs_hbm.at[pl.ds(first, blk_rows)], zsem)

    def has_pad(e):
        return seg_ref[N_EXPERTS + e] > seg_ref[2 * N_EXPERTS + e]

    @pl.when(i == 0)
    def _():
        zero_ref[...] = jnp.zeros_like(zero_ref)

        def start(e, c):
            @pl.when(has_pad(e))
            def _():
                zero_fill(e).start()
            return c

        def wait(e, c):
            @pl.when(has_pad(e))
            def _():
                zero_fill(e).wait()
            return c

        lax.fori_loop(0, N_EXPERTS, start, 0)
        lax.fori_loop(0, N_EXPERTS, wait, 0)

        nvalid = seg_ref[3 * N_EXPERTS]
        nblk = xs_hbm.shape[0] // blk_rows

        def tail(j):
            first = pl.multiple_of((nvalid + j) * blk_rows, blk_rows)
            return pltpu.make_async_copy(zero_ref, xs_hbm.at[pl.ds(first, blk_rows)], zsem)

        def tail_start(j, c):
            @pl.when(nvalid + j < nblk)
            def _():
                tail(j).start()
            return c

        def tail_wait(j, c):
            @pl.when(nvalid + j < nblk)
            def _():
                tail(j).wait()
            return c

        lax.fori_loop(0, N_EXPERTS, tail_start, 0)
        lax.fori_loop(0, N_EXPERTS, tail_wait, 0)

    def row_copy(j, slot):
        return pltpu.make_async_copy(x_ref.at[pl.ds(pl.multiple_of(j * nsl, nsl), nsl)],
                                     xs_hbm.at[pl.ds(pl.multiple_of(slot * nsl, nsl), nsl)], sem)

    def issue(j, c):
        for k in range(2):
            row_copy(j, dest_ref[0, 0, TOP_K * j + k]).start(priority=k)
        return c

    def drain(j, c):
        row_copy(0, 0).wait()
        return c

    lax.fori_loop(0, ntok, issue, 0, unroll=4)
    lax.fori_loop(0, 2 * ntok, drain, 0, unroll=8)


def _dispatch(x_tiles, dest, seg, cap, n):
    nsl = x_tiles.shape[0] // n
    tm = TOK_TM
    nt = n // tm
    return pl.pallas_call(
        _dispatch_kernel,
        out_shape=jax.ShapeDtypeStruct((cap * nsl, LANES), x_tiles.dtype),
        grid_spec=pltpu.PrefetchScalarGridSpec(
            num_scalar_prefetch=1,
            grid=(nt,),
            in_specs=[pl.BlockSpec((1, 1, TOP_K * tm), lambda i, seg: (i, 0, 0), memory_space=pltpu.SMEM),
                      pl.BlockSpec((tm * nsl, LANES), lambda i, seg: (i, 0))],
            out_specs=pl.BlockSpec(memory_space=pl.ANY),
            scratch_shapes=[pltpu.VMEM((MOE_BLOCK * nsl, LANES), x_tiles.dtype), pltpu.SemaphoreType.DMA,
                            pltpu.SemaphoreType.DMA]),
        compiler_params=_params(("arbitrary",)),
        name="moe_dispatch",
    )(seg, dest, x_tiles)


def _expert_kernel(blk_ref, nvalid_ref, xs_ref, wg_ref, wu_ref, wd_ref, ys_ref, wgb, wub, wdb):
    i = pl.program_id(0)
    prev = blk_ref[jnp.maximum(i - 1, 0)]
    valid = i < nvalid_ref[0]

    @pl.when(valid & ((i == 0) | (blk_ref[i] != prev)))
    def _():
        wgb[...] = wg_ref[0].astype(_BF16)
        wub[...] = wu_ref[0].astype(_BF16)
        wdb[...] = wd_ref[0].astype(_BF16)

    @pl.when(valid)
    def _():
        d = wgb.shape[0]
        xb = _rows_from_tiles(xs_ref, MOE_BLOCK, d).astype(_BF16)
        gate = jnp.dot(xb, wgb[...], preferred_element_type=_F32)
        up = jnp.dot(xb, wub[...], preferred_element_type=_F32)
        mid = (gate * jax.nn.sigmoid(gate)) * up
        _rows_to_tiles(ys_ref, jnp.dot(mid.astype(_BF16), wdb[...], preferred_element_type=_F32))

    @pl.when(jnp.logical_not(valid))
    def _():
        ys_ref[...] = jnp.zeros_like(ys_ref)


def _experts(xs, blk_exp, nvalid, w_gate, w_up, w_down):
    d, de = w_gate.shape[1], w_gate.shape[2]
    nsl = d // LANES
    cap = xs.shape[0] // nsl
    nblk = cap // MOE_BLOCK

    def xmap(i, blk, nv):
        return (jnp.minimum(i, nv[0] - 1), 0)

    def wmap(i, blk, nv):
        return (blk[i], 0, 0)

    return pl.pallas_call(
        _expert_kernel,
        out_shape=jax.ShapeDtypeStruct((cap * nsl, LANES), _F32),
        grid_spec=pltpu.PrefetchScalarGridSpec(
            num_scalar_prefetch=2,
            grid=(nblk,),
            in_specs=[pl.BlockSpec((MOE_BLOCK * nsl, LANES), xmap),
                      pl.BlockSpec((1, d, de), wmap), pl.BlockSpec((1, d, de), wmap),
                      pl.BlockSpec((1, de, d), wmap)],
            out_specs=pl.BlockSpec((MOE_BLOCK * nsl, LANES), lambda i, blk, nv: (i, 0)),
            scratch_shapes=[pltpu.VMEM((d, de), _BF16), pltpu.VMEM((d, de), _BF16), pltpu.VMEM((de, d), _BF16)]),
        compiler_params=_params(("arbitrary",)),
        name="moe_experts",
    )(blk_exp, nvalid, xs, w_gate, w_up, w_down)


def _combine_kernel(dcur_ref, dnxt_ref, ys_hbm, h_ref, route_ref, g_ref, b_ref, o_ref, buf, sem):
    i = pl.program_id(0)
    n = pl.num_programs(0)
    tm = TOK_TM

    d = o_ref.shape[1]
    nsl = d // LANES

    def tile_copy(src_slot, slot, k, j):
        return pltpu.make_async_copy(ys_hbm.at[pl.ds(pl.multiple_of(src_slot * nsl, nsl), nsl)],
                                     buf.at[slot, k, pl.ds(pl.multiple_of(j * nsl, nsl), nsl)], sem.at[slot])

    def gather(dref, slot):
        def body(j, c):
            for k in range(2):
                tile_copy(dref[0, 0, TOP_K * j + k], slot, k, j).start(priority=k)
            return c
        lax.fori_loop(0, tm, body, 0, unroll=4)

    @pl.when(i == 0)
    def _():
        gather(dcur_ref, 0)

    @pl.when(i + 1 < n)
    def _():
        gather(dnxt_ref, (i + 1) % 2)

    slot = i % 2

    def drain(j, c):
        tile_copy(0, slot, 0, 0).wait()
        return c

    lax.fori_loop(0, 2 * tm, drain, 0, unroll=8)
    r = route_ref[...]
    y = (r[:, 2:3] * _rows_from_tiles(buf.at[slot, 0], tm, d)
         + r[:, 3:4] * _rows_from_tiles(buf.at[slot, 1], tm, d))
    o_ref[...] = _layer_norm(ALPHA * _rows_from_tiles(h_ref, tm, d) + y, g_ref[...], b_ref[...])


def _combine(ys, dest, h1_tiles, route, g, b):
    n = route.shape[0]
    d = g.shape[1]
    nsl = d // LANES
    tm = TOK_TM
    nt = n // tm
    row = lambda i: (i, 0)
    const = lambda i: (0, 0)
    return pl.pallas_call(
        _combine_kernel,
        out_shape=jax.ShapeDtypeStruct((n, d), _F32),
        grid=(nt,),
        in_specs=[pl.BlockSpec((1, 1, TOP_K * tm), lambda i: (i, 0, 0), memory_space=pltpu.SMEM),
                  pl.BlockSpec((1, 1, TOP_K * tm), lambda i: (jnp.minimum(i + 1, nt - 1), 0, 0),
                               memory_space=pltpu.SMEM),
                  pl.BlockSpec(memory_space=pl.ANY),
                  pl.BlockSpec((tm * nsl, LANES), row), pl.BlockSpec((tm, LANES), row),
                  pl.BlockSpec((1, d), const), pl.BlockSpec((1, d), const)],
        out_specs=pl.BlockSpec((tm, d), row),
        scratch_shapes=[pltpu.VMEM((2, 2, tm * nsl, LANES), _F32), pltpu.SemaphoreType.DMA((2,))],
        compiler_params=_params(("arbitrary",)),
        name="moe_combine",
    )(dest, dest, ys, h1_tiles, route, g, b)


def _moe(h1_tiles, route, counts_row, w_gate, w_up, w_down, g, b):
    n = route.shape[0]
    cap = 2 * n + N_EXPERTS * MOE_BLOCK
    counts = counts_row[0, ROUTE_LANE0:ROUTE_LANE0 + N_EXPERTS].astype(jnp.int32)
    padded = (counts + MOE_BLOCK - 1) // MOE_BLOCK * MOE_BLOCK
    ends = jnp.cumsum(padded)
    pad_start = ends - padded
    nblk = cap // MOE_BLOCK
    blk_first = jnp.arange(nblk, dtype=jnp.int32) * MOE_BLOCK
    blk_exp = jnp.minimum(jnp.sum(blk_first[:, None] >= ends[None, :], axis=1), N_EXPERTS - 1).astype(jnp.int32)
    nvalid = (ends[-1:] // MOE_BLOCK).astype(jnp.int32)
    ps_row = jnp.zeros((1, LANES), _F32).at[0, ROUTE_LANE0:ROUTE_LANE0 + N_EXPERTS].set(pad_start.astype(_F32))
    dest = _dest(route, ps_row).T.reshape(n // TOK_TM, 1, TOP_K * TOK_TM)
    seg = jnp.concatenate([pad_start, padded, counts, nvalid]).astype(jnp.int32)
    xs = _dispatch(h1_tiles, dest, seg, cap, n)
    ys = _experts(xs, blk_exp, nvalid, w_gate, w_up, w_down)
    return _combine(ys, dest, h1_tiles, route, g, b)


def _router_params(rg_w, rg_b, re_w, re_b):
    d = rg_w.shape[0]
    pad = LANES - N_GROUPS - N_EXPERTS
    rw = jnp.concatenate([rg_w, re_w, jnp.zeros((d, pad), _F32)], axis=1)
    rb = jnp.concatenate([rg_b, re_b.reshape(-1), jnp.zeros((pad,), _F32)]).reshape(1, LANES)
    return rw, rb


def kernel(x, t5_table, l0_w_qkv, l0_sink, l0_w_o, l0_ln1_g, l0_ln1_b, l0_rg_w, l0_rg_b, l0_re_w, l0_re_b, l0_w_gate, l0_w_up, l0_w_down, l0_ln2_g, l0_ln2_b, l1_w_qkv, l1_w_o, l1_ln1_g, l1_ln1_b, l1_rg_w, l1_rg_b, l1_re_w, l1_re_b, l1_w_gate, l1_w_up, l1_w_down, l1_ln2_g, l1_ln2_b, l2_w_qkv, l2_rpb, l2_w_o, l2_ln1_g, l2_ln1_b, l2_rg_w, l2_rg_b, l2_re_w, l2_re_b, l2_w_gate, l2_w_up, l2_w_down, l2_ln2_g, l2_ln2_b, l3_w_qkv, l3_sink, l3_w_o, l3_ln1_g, l3_ln1_b, l3_rg_w, l3_rg_b, l3_re_w, l3_re_b, l3_w_gate, l3_w_up, l3_w_down, l3_ln2_g, l3_ln2_b):
    mixers = [(l0_w_qkv, l0_sink, l0_w_o), (l1_w_qkv, None, l1_w_o), (l2_w_qkv, l2_rpb, l2_w_o),
              (l3_w_qkv, l3_sink, l3_w_o)]
    norm1 = [(l0_ln1_g, l0_ln1_b), (l1_ln1_g, l1_ln1_b), (l2_ln1_g, l2_ln1_b), (l3_ln1_g, l3_ln1_b)]
    routers = [(l0_rg_w, l0_rg_b, l0_re_w, l0_re_b), (l1_rg_w, l1_rg_b, l1_re_w, l1_re_b),
               (l2_rg_w, l2_rg_b, l2_re_w, l2_re_b), (l3_rg_w, l3_rg_b, l3_re_w, l3_re_b)]
    experts = [(l0_w_gate, l0_w_up, l0_w_down), (l1_w_gate, l1_w_up, l1_w_down),
               (l2_w_gate, l2_w_up, l2_w_down), (l3_w_gate, l3_w_up, l3_w_down)]
    norm2 = [(l0_ln2_g, l0_ln2_b), (l1_ln2_g, l1_ln2_b), (l2_ln2_g, l2_ln2_b), (l3_ln2_g, l3_ln2_b)]

    bsz, seq, d = x.shape
    n = bsz * seq
    inner = N_HEADS * HEAD_DIM
    scale = HEAD_DIM ** -0.5 * LOG2E
    h = x.reshape(n, d)
    bias_a = _band_bias_t(t5_table, A_BLOCK, A_WINDOW, 1, A_BAND)
    for i in range(DEPTH):
        kind = i % N_MIXERS
        w_qkv, extra, w_o = mixers[i]
        kvw = A_KV_HEADS * HEAD_DIM if kind == 0 else inner
        lses, unperm = (), None
        if kind == 1:
            perms = np.stack([_residue_perm(dil) for _, dil in B_BRANCHES if dil > 1])
            outs = _project(h, w_qkv.astype(_BF16), (inner, kvw, kvw), (scale, 1.0, 1.0), jnp.asarray(perms, _BF16))
            unperm = jnp.asarray(perms.transpose(0, 2, 1), _BF16)
            attn, lses = [], []
            for bi, (window, dil) in enumerate(B_BRANCHES):
                q, k, v = outs[3 * bi:3 * bi + 3]
                o, lse = _mixer_b_branch(q, k, v, t5_table, bsz, seq, window, dil)
                attn.append(o)
                lses.append(jnp.pad(lse.reshape(n, N_HEADS), ((0, 0), (0, LANES - N_HEADS))))
        else:
            q, k, v = _project(h, w_qkv.astype(_BF16), (inner, kvw, kvw), (scale, 1.0, 1.0))
            q, k, v = (t.reshape(bsz, seq, t.shape[1]) for t in (q, k, v))
            if kind == 0:
                attn = [_mixer_a(q, k, v, extra.astype(_F32) * LOG2E, bias_a, bsz, seq)]
            else:
                attn = [_mixer_c(q, k, v, _na_bias_t(extra, seq // GRID_W), bsz, seq)]
        attn = [a.reshape(n, inner) for a in attn]
        g1, b1 = norm1[i]
        rw, rb = _router_params(*routers[i])
        h1, route, counts = _post_attn(attn, lses, h, w_o.astype(_BF16), g1.reshape(1, d), b1.reshape(1, d), rw, rb,
                                       unperm)
        g2, b2 = norm2[i]
        h = _moe(h1, route, counts, *experts[i], g2.reshape(1, d), b2.reshape(1, d))
    return h.reshape(bsz, seq, d)
```

```python
import functools
import math

import jax
import jax.numpy as jnp
import numpy as np
from jax import lax
from jax.experimental import pallas as pl
from jax.experimental.pallas import tpu as pltpu

HEAD_DIM = 64
N_HEADS = 16
A_KV_HEADS = 4
A_WINDOW = 128
A_BLOCK = 128
B_BRANCHES = ((128, 1), (512, 4), (2048, 16))
GRID_W = 64
NA_ROWS = 8
NA_COLS = 16
T5_BUCKETS = 32
T5_MAX_DIST = 1024
N_GROUPS = 4
EXPERTS_PER_GROUP = 8
N_EXPERTS = N_GROUPS * EXPERTS_PER_GROUP
TOP_K = 2
DEPTH = 4
N_MIXERS = 3
ALPHA = (2 * DEPTH) ** 0.25
LN_EPS = 1e-5
NEG_INF = -1e30

LANES = 128
PAIR = 2 * HEAD_DIM
BF16_SUBLANES = 16
VT_ROWS = PAIR + BF16_SUBLANES
LOG2E = math.log2(math.e)
LN2 = math.log(2.0)
VMEM_LIMIT = 56 * 1024 * 1024

PROJ_TM = 512
PROJ_CHUNK = 512
POST_TM = 1024
POST_SUB = 256
TOK_TM = 256
DEST_TM = 4096
MOE_BLOCK = 512
A_TQ = 2048
A_BAND = 3
A_SUB_PER_GROUP = 2
B_BLOCK = 128
RESIDUE_RUN = BF16_SUBLANES
PERM_CHUNK = 256
TRANSPOSE_UNROLL = 8
UNITS_PER_GROUP = 8
C_UNIT_ROWS = 2
C_BAND_ROWS = 10
C_UNITS_PER_GROUP = 4
C_ROWS_PER_STEP = 64
ROUTE_LANE0 = N_GROUPS

_F32 = jnp.float32
_BF16 = jnp.bfloat16


def _params(sem, vmem=VMEM_LIMIT, flags=None):
    return pltpu.CompilerParams(dimension_semantics=sem, vmem_limit_bytes=vmem, flags=flags)


def _proj_kernel(x_ref, w_ref, *refs, widths, scales, nperm):
    perm_ref = refs[0] if nperm else None
    o_refs = refs[1:] if nperm else refs
    nout = len(widths)
    xb = x_ref[...].astype(_BF16)
    col = 0
    for t, (width, scale) in enumerate(zip(widths, scales)):
        for c in range(0, width, PROJ_CHUNK):
            cw = min(PROJ_CHUNK, width - c)
            acc = jnp.dot(xb, w_ref[:, col + c:col + c + cw], preferred_element_type=_F32)
            if scale != 1.0:
                acc = acc * scale
            ob = acc.astype(_BF16)
            o_refs[t][:, c:c + cw] = ob
            for p in range(nperm):
                for r0 in range(0, ob.shape[0], PERM_CHUNK):
                    moved = jnp.dot(perm_ref[p], ob[r0:r0 + PERM_CHUNK], preferred_element_type=_F32)
                    o_refs[(p + 1) * nout + t][r0:r0 + PERM_CHUNK, c:c + cw] = moved.astype(_BF16)
        col += width


def _project(x2, w_bf16, widths, scales, perms=None):
    n, d = x2.shape
    ctot = sum(widths)
    nperm = 0 if perms is None else perms.shape[0]
    args = [x2, w_bf16] + ([perms] if nperm else [])
    specs = [pl.BlockSpec((PROJ_TM, d), lambda i: (i, 0)), pl.BlockSpec((d, ctot), lambda i: (0, 0))]
    if nperm:
        specs.append(pl.BlockSpec(perms.shape, lambda i: (0, 0, 0)))
    return pl.pallas_call(
        functools.partial(_proj_kernel, widths=tuple(widths), scales=tuple(scales), nperm=nperm),
        out_shape=[jax.ShapeDtypeStruct((n, w), _BF16) for w in widths] * (nperm + 1),
        grid=(n // PROJ_TM,),
        in_specs=specs,
        out_specs=[pl.BlockSpec((PROJ_TM, w), lambda i: (i, 0)) for w in widths] * (nperm + 1),
        compiler_params=_params(("arbitrary",)),
        name="qkv_proj",
    )(*args)


def _residue_perm(dil):
    span = dil * RESIDUE_RUN
    p = np.zeros((PERM_CHUNK, PERM_CHUNK), np.float32)
    for base in range(0, PERM_CHUNK, span):
        for r in range(dil):
            for u in range(RESIDUE_RUN):
                p[base + r * RESIDUE_RUN + u, base + u * dil + r] = 1.0
    return p


def _one():
    return jnp.ones((1,), jnp.int32)


def _lane_lo(rows=1):
    return lax.broadcasted_iota(jnp.int32, (rows, PAIR), 1) < HEAD_DIM


def _swap_halves(x):
    return pltpu.roll(x.astype(_F32), HEAD_DIM, axis=1).astype(_BF16)


class _PairUnit:
    def __init__(self, q2, k, vt, bias, emit, sink_a=None, sink_b=None):
        self.q2, self.k, self.vt, self.bias, self.emit = q2, k, vt, bias, emit
        self.sink_a, self.sink_b = sink_a, sink_b


def _attend_groups(ngroups, units_of, scratch, one):
    assert ngroups % 2 == 0
    lo = _lane_lo()
    nt = (((1,), (1,)), ((), ()))

    def sink_row(unit, m2):
        lane = lax.broadcasted_iota(jnp.int32, (1, m2), 1)
        return jnp.where(lane < m2 // 2, unit.sink_a, unit.sink_b)

    def scores(g, parity):
        s_scr, m_scr = scratch[parity]
        for u, unit in enumerate(units_of(g, parity)):
            q2 = unit.q2()
            zero = jnp.zeros_like(q2)
            qq = jnp.concatenate([jnp.where(lo, q2, zero), jnp.where(lo, zero, q2)], axis=0)
            s = lax.dot_general(unit.k(), qq, nt, preferred_element_type=_F32) + unit.bias()
            m = jnp.max(s, axis=0, keepdims=True)
            if unit.sink_a is not None:
                m = jnp.maximum(m, sink_row(unit, s.shape[1]))
            s_scr[u] = s
            m_scr[u] = m

    def values(g, parity):
        s_scr, m_scr = scratch[parity]
        for u, unit in enumerate(units_of(g, parity)):
            m = m_scr[u]
            e = jnp.exp2(s_scr[u] - m)
            ot = jnp.dot(unit.vt(), e.astype(_BF16), preferred_element_type=_F32)
            den = ot[PAIR:PAIR + 1, :]
            if unit.sink_a is not None:
                den = den + jnp.exp2(sink_row(unit, e.shape[1]) - m)
            ot = ot[:PAIR] * (1.0 / den)
            mq = ot.shape[1] // 2
            top = lax.broadcasted_iota(jnp.int32, (PAIR, 1), 0) < HEAD_DIM
            pair_t = jnp.where(top, ot[:, :mq], ot[:, mq:])
            unit.emit(pair_t.T, (m + jnp.log2(den)) * LN2)

    def region(*work):
        def body(_, carry):
            for fn, g, parity in work:
                fn(g, parity)
            return carry
        lax.fori_loop(0, one, body, 0)

    region((scores, 0, 0))

    def steady(i, carry):
        g = 2 * i
        region((values, g, 0), (scores, g + 1, 1))
        region((values, g + 1, 1), (scores, g + 2, 0))
        return carry

    lax.fori_loop(0, ngroups // 2 - 1, steady, 0)
    region((values, ngroups - 2, 0), (scores, ngroups - 1, 1))
    region((values, ngroups - 1, 1))


def _transpose_blocks(v_ref_rows, nblk, store):
    def body(j, carry):
        rows = pl.ds(pl.multiple_of(j * LANES, LANES), LANES)
        vt = v_ref_rows(rows).T
        store(j, jnp.concatenate([vt, jnp.ones((VT_ROWS - PAIR, LANES), _BF16)], axis=0))
        return carry
    lax.fori_loop(0, nblk, body, 0, unroll=TRANSPOSE_UNROLL)


def _t5_bucket(rel):
    half = T5_BUCKETS // 2
    exact = half // 2
    n = np.abs(rel)
    big = exact + (np.log(np.maximum(n, 1) / exact) / math.log(T5_MAX_DIST / exact) * (half - exact)).astype(np.int32)
    return ((rel > 0) * half + np.where(n < exact, n, np.minimum(big, half - 1))).astype(np.int32)


def _band_bias_t(t5_table, blk, window, dil, nband):
    tk = nband * blk
    off = tk - 1
    rel = np.arange(-off, tk)
    vec = jnp.take(t5_table, jnp.asarray(_t5_bucket(rel * dil)), axis=0).astype(_F32)
    vec = jnp.where(jnp.asarray(np.abs(rel) <= window)[:, None], vec * LOG2E, NEG_INF).T
    h = vec.shape[0]
    vec = jnp.pad(vec, ((0, 0), (0, 1)))
    w = 2 * tk - 1
    skew = jnp.tile(vec, (1, blk))[:, :blk * w].reshape(h, blk, w)
    t = jnp.stack([skew[:, :, off - v * blk:off - v * blk + tk] for v in range(nband)], axis=1)
    t = t.reshape(h // 2, 2, nband, blk, tk).transpose(0, 2, 4, 1, 3)
    return t.reshape(h // 2, nband, tk, 2 * blk)


def _band_window(gb, blk, length, nband):
    first = jnp.clip(gb - 1, 0, length // blk - nband)
    return first, gb - first


def _attn_a_kernel(sink_ref, one_ref, q_ref, k_ref, v_ref, bias_ref, o_ref, kd_ref, vdt_ref,
                   s0_scr, m0_scr, s1_scr, m1_scr, *, seq):
    kvb = pl.program_id(0)
    qt = pl.program_id(2)
    one = one_ref[0]
    nblk = seq // A_BLOCK

    @pl.when(qt == 0)
    def _():
        lo = _lane_lo()

        def fill(j, carry):
            rows = pl.ds(pl.multiple_of(j * A_BLOCK, A_BLOCK), A_BLOCK)
            kk = k_ref[0, rows, :]
            ks = _swap_halves(kk)
            kd_ref[0, rows, :] = jnp.where(lo, kk, ks)
            kd_ref[1, rows, :] = jnp.where(lo, ks, kk)
            vt = v_ref[0, rows, :].T
            ones = jnp.ones((VT_ROWS - PAIR, A_BLOCK), _BF16)
            vdt_ref[0, j] = jnp.concatenate([vt[:HEAD_DIM], vt[:HEAD_DIM], ones], axis=0)
            vdt_ref[1, j] = jnp.concatenate([vt[HEAD_DIM:], vt[HEAD_DIM:], ones], axis=0)
            return carry

        lax.fori_loop(0, nblk, fill, 0, unroll=TRANSPOSE_UNROLL)

    nsub = A_TQ // A_BLOCK
    npair = q_ref.shape[2] // PAIR
    hbase = kvb * (2 * npair)

    def units_of(g, parity):
        units = []
        for j in range(A_SUB_PER_GROUP):
            si = g * A_SUB_PER_GROUP + j
            first, var = _band_window(qt * nsub + si, A_BLOCK, seq, A_BAND)
            rows = pl.ds(pl.multiple_of(si * A_BLOCK, A_BLOCK), A_BLOCK)
            band = pl.ds(pl.multiple_of(first * A_BLOCK, A_BLOCK), A_BAND * A_BLOCK)
            for jj in range(npair):
                hk = jj // (npair // 2)
                cols = slice(jj * PAIR, (jj + 1) * PAIR)

                def emit(o, lse, rows=rows, cols=cols):
                    o_ref[0, rows, cols] = o.astype(o_ref.dtype)

                units.append(_PairUnit(
                    q2=lambda rows=rows, cols=cols: q_ref[0, rows, cols],
                    k=lambda hk=hk, band=band: kd_ref[hk, band, :],
                    vt=lambda hk=hk, first=first: jnp.concatenate(
                        [vdt_ref[hk, first + t] for t in range(A_BAND)], axis=1),
                    bias=lambda jj=jj, var=var: bias_ref[jj, var],
                    emit=emit, sink_a=sink_ref[hbase + 2 * jj], sink_b=sink_ref[hbase + 2 * jj + 1]))
        return units

    _attend_groups(nsub // A_SUB_PER_GROUP, units_of, ((s0_scr, m0_scr), (s1_scr, m1_scr)), one)


def _mixer_a(q, k, v, sink, bias, bsz, seq):
    nkvb = k.shape[2] // PAIR
    qw = q.shape[2] // nkvb
    npair = qw // PAIR
    tk = A_BAND * A_BLOCK
    nunits = A_SUB_PER_GROUP * npair
    group_scratch = [pltpu.VMEM((nunits, tk, 2 * A_BLOCK), _F32), pltpu.VMEM((nunits, 1, 2 * A_BLOCK), _F32)]
    return pl.pallas_call(
        functools.partial(_attn_a_kernel, seq=seq),
        out_shape=jax.ShapeDtypeStruct(q.shape, _BF16),
        grid=(nkvb, bsz, seq // A_TQ),
        in_specs=[pl.BlockSpec(memory_space=pltpu.SMEM), pl.BlockSpec(memory_space=pltpu.SMEM),
                  pl.BlockSpec((1, A_TQ, qw), lambda c, b, t: (b, t, c)),
                  pl.BlockSpec((1, seq, PAIR), lambda c, b, t: (b, 0, c)),
                  pl.BlockSpec((1, seq, PAIR), lambda c, b, t: (b, 0, c)),
                  pl.BlockSpec((npair, A_BAND, tk, 2 * A_BLOCK), lambda c, b, t: (c, 0, 0, 0))],
        out_specs=pl.BlockSpec((1, A_TQ, qw), lambda c, b, t: (b, t, c)),
        scratch_shapes=[pltpu.VMEM((2, seq, PAIR), _BF16), pltpu.VMEM((2, seq // A_BLOCK, VT_ROWS, A_BLOCK), _BF16)]
        + group_scratch + group_scratch,
        compiler_params=_params(("arbitrary", "arbitrary", "arbitrary")),
        name="mixer_a",
    )(sink, _one(), q, k, v, bias)


def _attn_band_kernel(one_ref, q_ref, k_ref, v_ref, bias_ref, o_ref, lse_ref, vt_ref,
                      s0_scr, m0_scr, s1_scr, m1_scr, *, length, npairs, nband, chunked):
    col = pl.program_id(2)
    one = one_ref[0]
    nblk = length // B_BLOCK
    runs = B_BLOCK // RESIDUE_RUN

    def load(ref, blk0, nblocks, cols):
        if chunked:
            r0 = pl.multiple_of(blk0 * runs, runs)
            return ref[pl.ds(r0, nblocks * runs), :, cols].reshape(nblocks * B_BLOCK, PAIR)
        return ref[0, pl.ds(pl.multiple_of(blk0 * B_BLOCK, B_BLOCK), nblocks * B_BLOCK), cols]

    def store_out(blk0, cols, val):
        if chunked:
            o_ref[pl.ds(pl.multiple_of(blk0 * runs, runs), runs), :, cols] = val.reshape(runs, RESIDUE_RUN, PAIR)
        else:
            o_ref[0, pl.ds(pl.multiple_of(blk0 * B_BLOCK, B_BLOCK), B_BLOCK), cols] = val

    for p in range(npairs):
        pcols = slice(p * PAIR, (p + 1) * PAIR)

        def fill(j, carry, p=p, pcols=pcols):
            vt = load(v_ref, j, 1, pcols).T
            vt_ref[p, j] = jnp.concatenate([vt, jnp.ones((VT_ROWS - PAIR, B_BLOCK), _BF16)], axis=0)
            return carry

        lax.fori_loop(0, nblk, fill, 0, unroll=min(TRANSPOSE_UNROLL, nblk))

    def unit(gb, p):
        first, var = _band_window(gb, B_BLOCK, length, nband)
        cols = slice(p * PAIR, (p + 1) * PAIR)

        def emit(o, lse):
            store_out(gb, cols, o.astype(o_ref.dtype))
            ha = col * (2 * npairs) + 2 * p
            lse_ref[0, 0, gb, pl.ds(ha, 1), :] = lse[:, :B_BLOCK]
            lse_ref[0, 0, gb, pl.ds(ha + 1, 1), :] = lse[:, B_BLOCK:]

        return _PairUnit(
            q2=lambda: load(q_ref, gb, 1, cols), k=lambda: load(k_ref, first, nband, cols),
            vt=lambda: jnp.concatenate([vt_ref[p, first + t] for t in range(nband)], axis=1),
            bias=lambda: bias_ref[p, var], emit=emit)

    if npairs == 1:
        ngroups = nblk // UNITS_PER_GROUP
        units_of = lambda g, parity: [unit(g * UNITS_PER_GROUP + j, 0) for j in range(UNITS_PER_GROUP)]
    else:
        assert npairs == UNITS_PER_GROUP
        ngroups = nblk
        units_of = lambda g, parity: [unit(g, p) for p in range(npairs)]
    _attend_groups(ngroups, units_of, ((s0_scr, m0_scr), (s1_scr, m1_scr)), one)


def _mixer_b_branch(q, k, v, t5_table, bsz, seq, window, dil):
    length = seq // dil
    width = q.shape[1]
    nblk = length // B_BLOCK
    nband = min(3, nblk)
    tk = nband * B_BLOCK
    npairs = (width // PAIR) if length * width * 2 <= 2 * 1024 * 1024 else 1
    ncol = width // (npairs * PAIR)
    bw = npairs * PAIR
    bias = _band_bias_t(t5_table, B_BLOCK, window // 2 // dil, dil, nband)
    chunked = dil > 1
    if chunked:
        nrun = length // RESIDUE_RUN
        shape = (bsz * nrun, dil, RESIDUE_RUN, width)
        spec = pl.BlockSpec((nrun, None, RESIDUE_RUN, bw), lambda b, r, c: (b, r, 0, c))
    else:
        shape = (bsz, seq, width)
        spec = pl.BlockSpec((1, seq, bw), lambda b, r, c: (b, 0, c))

    o, lse = pl.pallas_call(
        functools.partial(_attn_band_kernel, length=length, npairs=npairs, nband=nband, chunked=chunked),
        out_shape=[jax.ShapeDtypeStruct(shape, _BF16),
                   jax.ShapeDtypeStruct((bsz, dil, nblk, N_HEADS, B_BLOCK), _F32)],
        grid=(bsz, dil, ncol),
        in_specs=[pl.BlockSpec(memory_space=pltpu.SMEM), spec, spec, spec,
                  pl.BlockSpec((npairs, nband, tk, 2 * B_BLOCK), lambda b, r, c: (c, 0, 0, 0))],
        out_specs=[spec, pl.BlockSpec((1, 1, nblk, N_HEADS, B_BLOCK), lambda b, r, c: (b, r, 0, 0, 0))],
        scratch_shapes=[pltpu.VMEM((npairs, nblk, VT_ROWS, B_BLOCK), _BF16)] + 2 * [
            pltpu.VMEM((UNITS_PER_GROUP, tk, 2 * B_BLOCK), _F32), pltpu.VMEM((UNITS_PER_GROUP, 1, 2 * B_BLOCK), _F32)],
        compiler_params=_params(("arbitrary", "arbitrary", "arbitrary")),
        name=f"mixer_b_dil{dil}",
    )(_one(), q.reshape(shape), k.reshape(shape), v.reshape(shape), bias)
    lse = lse.transpose(0, 2, 4, 1, 3).reshape(bsz, seq, N_HEADS)
    return o.reshape(bsz * seq, width), lse


def _na_bias_t(rpb, rows):
    canon = (0, 2, 4, rows - 4, rows - 2)
    h = rpb.shape[0]
    kc = np.arange(GRID_W)[:, None]
    qc = np.arange(GRID_W)[None, :]
    ws = np.clip(qc - NA_COLS // 2, 0, GRID_W - NA_COLS)
    col_ok = (kc >= ws) & (kc < ws + NA_COLS)
    col_idx = np.clip(kc - qc, 1 - NA_COLS, NA_COLS - 1) + NA_COLS - 1
    colpart = jnp.where(jnp.asarray(col_ok), rpb.astype(_F32)[:, :, jnp.asarray(col_idx)] * LOG2E, NEG_INF)
    masked = jnp.full((h, GRID_W, GRID_W), NEG_INF, _F32)
    variants = []
    for i in canon:
        start = int(np.clip(i - NA_ROWS // 2, 0, rows - C_BAND_ROWS))
        per_key_row = []
        for kr in range(C_BAND_ROWS):
            per_query_row = []
            for qr in range(C_UNIT_ROWS):
                iq, kabs = i + qr, start + kr
                first_key_row = int(np.clip(iq - NA_ROWS // 2, 0, rows - NA_ROWS))
                inside = first_key_row <= kabs < first_key_row + NA_ROWS
                per_query_row.append(colpart[:, kabs - iq + NA_ROWS - 1] if inside else masked)
            per_key_row.append(jnp.stack(per_query_row, axis=2))
        variants.append(jnp.stack(per_key_row, axis=1))
    b = jnp.stack(variants, axis=1)
    nq, nk = C_UNIT_ROWS * GRID_W, C_BAND_ROWS * GRID_W
    b = b.reshape(h // 2, 2, len(canon), nk, nq)
    return b.transpose(0, 2, 3, 1, 4).reshape(h // 2, len(canon), nk, 2 * nq)


def _attn_c_kernel(one_ref, q_ref, k_ref, v_ref, bias_ref, o_ref, vt_ref, s0_scr, m0_scr, s1_scr, m1_scr, *, rows):
    rt = pl.program_id(2)
    nq = C_UNIT_ROWS * GRID_W
    band_blocks = C_BAND_ROWS * GRID_W // LANES

    @pl.when(rt == 0)
    def _():
        def store(j, blk):
            vt_ref[j] = blk
        _transpose_blocks(lambda r: v_ref[0, r, :], rows * GRID_W // LANES, store)

    def unit(u):
        i = rt * C_ROWS_PER_STEP + C_UNIT_ROWS * u
        start = jnp.clip(i - NA_ROWS // 2, 0, rows - C_BAND_ROWS)
        var = lax.shift_right_logical(i - start, 1)
        first = lax.shift_right_logical(start, 1)
        keys = pl.ds(pl.multiple_of(first * LANES, LANES), C_BAND_ROWS * GRID_W)
        qrows = pl.ds(pl.multiple_of(u * nq, nq), nq)

        def emit(o, lse):
            o_ref[0, qrows, :] = o.astype(o_ref.dtype)

        return _PairUnit(
            q2=lambda: q_ref[0, qrows, :], k=lambda: k_ref[0, keys, :],
            vt=lambda: jnp.concatenate([vt_ref[first + t] for t in range(band_blocks)], axis=1),
            bias=lambda: bias_ref[0, var], emit=emit)

    ngroups = C_ROWS_PER_STEP // C_UNIT_ROWS // C_UNITS_PER_GROUP
    units_of = lambda g, parity: [unit(g * C_UNITS_PER_GROUP + j) for j in range(C_UNITS_PER_GROUP)]
    _attend_groups(ngroups, units_of, ((s0_scr, m0_scr), (s1_scr, m1_scr)), one_ref[0])


def _mixer_c(q, k, v, bias, bsz, seq):
    rows = seq // GRID_W
    tq = C_ROWS_PER_STEP * GRID_W
    npair = q.shape[2] // PAIR
    nq, nk = C_UNIT_ROWS * GRID_W, C_BAND_ROWS * GRID_W
    return pl.pallas_call(
        functools.partial(_attn_c_kernel, rows=rows),
        out_shape=jax.ShapeDtypeStruct(q.shape, _BF16),
        grid=(npair, bsz, seq // tq),
        in_specs=[pl.BlockSpec(memory_space=pltpu.SMEM),
                  pl.BlockSpec((1, tq, PAIR), lambda c, b, t: (b, t, c)),
                  pl.BlockSpec((1, seq, PAIR), lambda c, b, t: (b, 0, c)),
                  pl.BlockSpec((1, seq, PAIR), lambda c, b, t: (b, 0, c)),
                  pl.BlockSpec((1, bias.shape[1], nk, 2 * nq), lambda c, b, t: (c, 0, 0, 0))],
        out_specs=pl.BlockSpec((1, tq, PAIR), lambda c, b, t: (b, t, c)),
        scratch_shapes=[pltpu.VMEM((seq // LANES, VT_ROWS, LANES), _BF16)] + 2 * [
            pltpu.VMEM((C_UNITS_PER_GROUP, nk, 2 * nq), _F32), pltpu.VMEM((C_UNITS_PER_GROUP, 1, 2 * nq), _F32)],
        compiler_params=_params(("arbitrary", "arbitrary", "arbitrary")),
        name="mixer_c",
    )(_one(), q, k, v, bias)


def _layer_norm(z, g, b):
    mu = jnp.mean(z, axis=-1, keepdims=True)
    zc = z - mu
    var = jnp.mean(zc * zc, axis=-1, keepdims=True)
    return zc * lax.rsqrt(var + LN_EPS) * g + b


def _rows_to_tiles(ref, x):
    nrows, d = x.shape
    nsl = d // LANES
    for s in range(nsl):
        ref[pl.ds(s, nrows, stride=nsl), :] = x[:, s * LANES:(s + 1) * LANES]


def _rows_from_tiles(ref, nrows, d):
    nsl = d // LANES
    return jnp.concatenate([ref[pl.ds(s, nrows, stride=nsl), :] for s in range(nsl)], axis=1)


def _split_bf16(x):
    hi = x.astype(_BF16)
    return hi, (x - hi.astype(_F32)).astype(_BF16)


def _router_logits(hn, rw_ref, rb_ref):
    h_hi, h_lo = _split_bf16(hn)
    w_hi, w_lo = _split_bf16(rw_ref[...])
    both = jnp.dot(h_hi, jnp.concatenate([w_hi, w_lo], axis=1), preferred_element_type=_F32)
    return (both[:, :LANES]
            + (both[:, LANES:] + jnp.dot(h_lo, w_hi, preferred_element_type=_F32))) + rb_ref[...]


def _route(logits, tri_ref, carry_ref):
    tm = logits.shape[0]
    lane = lax.broadcasted_iota(jnp.int32, (tm, LANES), 1).astype(_F32)
    ninf = -jnp.inf
    is_grp = lane < N_GROUPS
    lg = jnp.where(is_grp, logits, ninf)
    gmax = jnp.max(lg, axis=-1, keepdims=True)
    grp = jnp.min(jnp.where(lg == gmax, lane, float(LANES)), axis=-1, keepdims=True)
    p_grp = 1.0 / jnp.sum(jnp.where(is_grp, jnp.exp(lg - gmax), 0.0), axis=-1, keepdims=True)
    lo_lane = ROUTE_LANE0 + EXPERTS_PER_GROUP * grp
    in_grp = (lane >= lo_lane) & (lane < lo_lane + EXPERTS_PER_GROUP)
    le = jnp.where(in_grp, logits, ninf)
    v1 = jnp.max(le, axis=-1, keepdims=True)
    i1 = jnp.min(jnp.where(le == v1, lane, float(LANES)), axis=-1, keepdims=True)
    le2 = jnp.where(lane == i1, ninf, le)
    v2 = jnp.max(le2, axis=-1, keepdims=True)
    i2 = jnp.min(jnp.where(le2 == v2, lane, float(LANES)), axis=-1, keepdims=True)
    t = jnp.exp(v2 - v1)
    s1 = 1.0 / (1.0 + t)
    g1 = p_grp * s1
    g2 = p_grp * (t * s1)
    sel1 = lane == i1
    sel2 = lane == i2
    onehot = jnp.where(sel1 | sel2, 1.0, 0.0)
    sub = tri_ref.shape[0]
    carry = carry_ref[...]
    parts = []
    for r0 in range(0, tm, sub):
        part = onehot[r0:r0 + sub]
        parts.append(jnp.dot(tri_ref[...], part.astype(_BF16), preferred_element_type=_F32) + carry)
        carry = carry + jnp.sum(part, axis=0, keepdims=True)
    carry_ref[...] = carry
    before = jnp.concatenate(parts, axis=0) if len(parts) > 1 else parts[0]
    rank1 = jnp.sum(jnp.where(sel1, before, 0.0), axis=-1, keepdims=True)
    rank2 = jnp.sum(jnp.where(sel2, before, 0.0), axis=-1, keepdims=True)
    out = jnp.zeros((tm, LANES), _F32)
    for idx, val in enumerate((i1 - ROUTE_LANE0, i2 - ROUTE_LANE0, g1, g2, rank1, rank2)):
        out = jnp.where(lane == idx, val, out)
    return out


def _post_attn_kernel(*refs, n_branch):
    attn_refs = refs[:n_branch]
    pos = n_branch
    if n_branch > 1:
        lse_refs = refs[pos:pos + n_branch]
        expand_ref, unperm_ref = refs[pos + n_branch:pos + n_branch + 2]
        pos += n_branch + 2
    wo_ref, h_ref, g_ref, b_ref, rw_ref, rb_ref, tri_ref = refs[pos:pos + 7]
    h1_ref, route_ref, counts_ref, carry_ref = refs[pos + 7:pos + 11]

    @pl.when(pl.program_id(0) == 0)
    def _():
        carry_ref[...] = jnp.zeros_like(carry_ref)

    d = h_ref.shape[1]
    logits = []
    for sub in range(h_ref.shape[0] // POST_SUB):
        rows = slice(sub * POST_SUB, (sub + 1) * POST_SUB)
        if n_branch == 1:
            a = attn_refs[0][rows, :]
        else:
            lses = [r[rows, :] for r in lse_refs]
            mx = functools.reduce(jnp.maximum, lses)
            es = [jnp.exp(l - mx) for l in lses]
            inv = 1.0 / functools.reduce(jnp.add, es)
            mixed = None
            for bi, (e, a_ref) in enumerate(zip(es, attn_refs)):
                w_hi, w_lo = _split_bf16(e * inv)
                w_full = (jnp.dot(w_hi, expand_ref[...], preferred_element_type=_F32)
                          + jnp.dot(w_lo, expand_ref[...], preferred_element_type=_F32))
                if bi == 0:
                    a_nat = a_ref[rows, :].astype(_F32)
                else:
                    a_nat = jnp.dot(unperm_ref[bi - 1], a_ref[rows, :], preferred_element_type=_F32)
                term = w_full * a_nat
                mixed = term if mixed is None else mixed + term
            a = mixed.astype(_BF16)
        y = jnp.dot(a, wo_ref[...], preferred_element_type=_F32)
        hn = _layer_norm(ALPHA * h_ref[rows, :] + y, g_ref[...], b_ref[...])
        nsl = d // LANES
        _rows_to_tiles(h1_ref.at[pl.ds(sub * POST_SUB * nsl, POST_SUB * nsl)], hn)
        logits.append(_router_logits(hn, rw_ref, rb_ref))
    route_ref[...] = _route(jnp.concatenate(logits, axis=0), tri_ref, carry_ref)
    counts_ref[...] = carry_ref[...]


def _post_attn(attn, lses, h2, w_o, g, b, rw, rb, unperm=None):
    n, d = h2.shape
    tm = POST_TM
    nb = len(attn)
    tri = jnp.asarray(np.tril(np.ones((POST_SUB, POST_SUB), np.float32), -1), _BF16)
    row = lambda i: (i, 0)
    const = lambda i: (0, 0)
    args = list(attn)
    specs = [pl.BlockSpec((tm, d), row) for _ in attn]
    if nb > 1:
        expand = np.zeros((LANES, d), np.float32)
        for hd in range(N_HEADS):
            expand[hd, hd * HEAD_DIM:(hd + 1) * HEAD_DIM] = 1.0
        args += list(lses) + [jnp.asarray(expand, _BF16), unperm]
        specs += [pl.BlockSpec((tm, LANES), row) for _ in lses] + [
            pl.BlockSpec((LANES, d), const), pl.BlockSpec(unperm.shape, lambda i: (0, 0, 0))]
    args += [w_o, h2, g, b, rw, rb, tri]
    specs += [pl.BlockSpec((d, d), const), pl.BlockSpec((tm, d), row), pl.BlockSpec((1, d), const),
              pl.BlockSpec((1, d), const), pl.BlockSpec((d, LANES), const), pl.BlockSpec((1, LANES), const),
              pl.BlockSpec((POST_SUB, POST_SUB), const)]
    return pl.pallas_call(
        functools.partial(_post_attn_kernel, n_branch=nb),
        out_shape=[jax.ShapeDtypeStruct((n * d // LANES, LANES), _F32), jax.ShapeDtypeStruct((n, LANES), _F32),
                   jax.ShapeDtypeStruct((1, LANES), _F32)],
        grid=(n // tm,),
        in_specs=specs,
        out_specs=[pl.BlockSpec((tm * d // LANES, LANES), row), pl.BlockSpec((tm, LANES), row),
                   pl.BlockSpec((1, LANES), const)],
        scratch_shapes=[pltpu.VMEM((1, LANES), _F32)],
        compiler_params=_params(("arbitrary",)),
        name="post_attn_router",
    )(*args)


def _dest_kernel(route_ref, ps_ref, o_ref):
    r = route_ref[...]
    lane = lax.broadcasted_iota(jnp.int32, r.shape, 1)
    lanef = lane.astype(_F32)
    ps = ps_ref[...]
    d = []
    for k in range(2):
        e = r[:, k:k + 1] + float(ROUTE_LANE0)
        d.append(jnp.sum(jnp.where(lanef == e, ps, 0.0), axis=-1, keepdims=True) + r[:, 4 + k:5 + k])
    cols = jnp.where(lane == 0, d[0], jnp.where(lane == 1, d[1], 0.0))
    o_ref[...] = cols.T[:TOP_K].astype(jnp.int32)


def _dest(route, pad_start_row):
    n = route.shape[0]
    tm = min(DEST_TM, n)
    return pl.pallas_call(
        _dest_kernel,
        out_shape=jax.ShapeDtypeStruct((TOP_K, n), jnp.int32),
        grid=(n // tm,),
        in_specs=[pl.BlockSpec((tm, LANES), lambda i: (i, 0)), pl.BlockSpec((1, LANES), lambda i: (0, 0))],
        out_specs=pl.BlockSpec((TOP_K, tm), lambda i: (0, i)),
        compiler_params=_params(("arbitrary",)),
        name="moe_dest",
    )(route, pad_start_row)


def _dispatch_kernel(seg_ref, dest_ref, x_ref, xs_hbm, zero_ref, zsem, sem):
    i = pl.program_id(0)
    ntok = TOK_TM
    nsl = x_ref.shape[0] // ntok
    blk_rows = MOE_BLOCK * nsl

    def zero_fill(e):
        first = pl.multiple_of((seg_ref[e] + seg_ref[N_EXPERTS + e] - MOE_BLOCK) * nsl, blk_rows)
        return pltpu.make_async_copy(zero_ref, xs_hbm.at[pl.ds(first, blk_rows)], zsem)

    def has_pad(e):
        return seg_ref[N_EXPERTS + e] > seg_ref[2 * N_EXPERTS + e]

    @pl.when(i == 0)
    def _():
        zero_ref[...] = jnp.zeros_like(zero_ref)

        def start(e, c):
            @pl.when(has_pad(e))
            def _():
                zero_fill(e).start()
            return c

        def wait(e, c):
            @pl.when(has_pad(e))
            def _():
                zero_fill(e).wait()
            return c

        lax.fori_loop(0, N_EXPERTS, start, 0)
        lax.fori_loop(0, N_EXPERTS, wait, 0)

        nvalid = seg_ref[3 * N_EXPERTS]
        nblk = xs_hbm.shape[0] // blk_rows

        def tail(j):
            first = pl.multiple_of((nvalid + j) * blk_rows, blk_rows)
            return pltpu.make_async_copy(zero_ref, xs_hbm.at[pl.ds(first, blk_rows)], zsem)

        def tail_start(j, c):
            @pl.when(nvalid + j < nblk)
            def _():
                tail(j).start()
            return c

        def tail_wait(j, c):
            @pl.when(nvalid + j < nblk)
            def _():
                tail(j).wait()
            return c

        lax.fori_loop(0, N_EXPERTS, tail_start, 0)
        lax.fori_loop(0, N_EXPERTS, tail_wait, 0)

    def row_copy(j, slot):
        return pltpu.make_async_copy(x_ref.at[pl.ds(pl.multiple_of(j * nsl, nsl), nsl)],
                                     xs_hbm.at[pl.ds(pl.multiple_of(slot * nsl, nsl), nsl)], sem)

    def issue(j, c):
        for k in range(2):
            row_copy(j, dest_ref[0, 0, TOP_K * j + k]).start(priority=k)
        return c

    def drain(j, c):
        row_copy(0, 0).wait()
        return c

    lax.fori_loop(0, ntok, issue, 0, unroll=4)
    lax.fori_loop(0, 2 * ntok, drain, 0, unroll=8)


def _dispatch(x_tiles, dest, seg, cap, n):
    nsl = x_tiles.shape[0] // n
    tm = TOK_TM
    nt = n // tm
    return pl.pallas_call(
        _dispatch_kernel,
        out_shape=jax.ShapeDtypeStruct((cap * nsl, LANES), x_tiles.dtype),
        grid_spec=pltpu.PrefetchScalarGridSpec(
            num_scalar_prefetch=1,
            grid=(nt,),
            in_specs=[pl.BlockSpec((1, 1, TOP_K * tm), lambda i, seg: (i, 0, 0), memory_space=pltpu.SMEM),
                      pl.BlockSpec((tm * nsl, LANES), lambda i, seg: (i, 0))],
            out_specs=pl.BlockSpec(memory_space=pl.ANY),
            scratch_shapes=[pltpu.VMEM((MOE_BLOCK * nsl, LANES), x_tiles.dtype), pltpu.SemaphoreType.DMA,
                            pltpu.SemaphoreType.DMA]),
        compiler_params=_params(("arbitrary",)),
        name="moe_dispatch",
    )(seg, dest, x_tiles)


def _expert_kernel(blk_ref, nvalid_ref, xs_ref, wg_ref, wu_ref, wd_ref, ys_ref, wgb, wub, wdb):
    i = pl.program_id(0)
    prev = blk_ref[jnp.maximum(i - 1, 0)]
    valid = i < nvalid_ref[0]

    @pl.when(valid & ((i == 0) | (blk_ref[i] != prev)))
    def _():
        wgb[...] = wg_ref[0].astype(_BF16)
        wub[...] = wu_ref[0].astype(_BF16)
        wdb[...] = wd_ref[0].astype(_BF16)

    @pl.when(valid)
    def _():
        d = wgb.shape[0]
        xb = _rows_from_tiles(xs_ref, MOE_BLOCK, d).astype(_BF16)
        gate = jnp.dot(xb, wgb[...], preferred_element_type=_F32)
        up = jnp.dot(xb, wub[...], preferred_element_type=_F32)
        mid = (gate * jax.nn.sigmoid(gate)) * up
        _rows_to_tiles(ys_ref, jnp.dot(mid.astype(_BF16), wdb[...], preferred_element_type=_F32))

    @pl.when(jnp.logical_not(valid))
    def _():
        ys_ref[...] = jnp.zeros_like(ys_ref)


def _experts(xs, blk_exp, nvalid, w_gate, w_up, w_down):
    d, de = w_gate.shape[1], w_gate.shape[2]
    nsl = d // LANES
    cap = xs.shape[0] // nsl
    nblk = cap // MOE_BLOCK

    def xmap(i, blk, nv):
        return (jnp.minimum(i, nv[0] - 1), 0)

    def wmap(i, blk, nv):
        return (blk[i], 0, 0)

    return pl.pallas_call(
        _expert_kernel,
        out_shape=jax.ShapeDtypeStruct((cap * nsl, LANES), _F32),
        grid_spec=pltpu.PrefetchScalarGridSpec(
            num_scalar_prefetch=2,
            grid=(nblk,),
            in_specs=[pl.BlockSpec((MOE_BLOCK * nsl, LANES), xmap),
                      pl.BlockSpec((1, d, de), wmap), pl.BlockSpec((1, d, de), wmap),
                      pl.BlockSpec((1, de, d), wmap)],
            out_specs=pl.BlockSpec((MOE_BLOCK * nsl, LANES), lambda i, blk, nv: (i, 0)),
            scratch_shapes=[pltpu.VMEM((d, de), _BF16), pltpu.VMEM((d, de), _BF16), pltpu.VMEM((de, d), _BF16)]),
        compiler_params=_params(("arbitrary",)),
        name="moe_experts",
    )(blk_exp, nvalid, xs, w_gate, w_up, w_down)


def _combine_kernel(dcur_ref, dnxt_ref, ys_hbm, h_ref, route_ref, g_ref, b_ref, o_ref, buf, sem):
    i = pl.program_id(0)
    n = pl.num_programs(0)
    tm = TOK_TM

    d = o_ref.shape[1]
    nsl = d // LANES

    def tile_copy(src_slot, slot, k, j):
        return pltpu.make_async_copy(ys_hbm.at[pl.ds(pl.multiple_of(src_slot * nsl, nsl), nsl)],
                                     buf.at[slot, k, pl.ds(pl.multiple_of(j * nsl, nsl), nsl)], sem.at[slot])

    def gather(dref, slot):
        def body(j, c):
            for k in range(2):
                tile_copy(dref[0, 0, TOP_K * j + k], slot, k, j).start(priority=k)
            return c
        lax.fori_loop(0, tm, body, 0, unroll=4)

    @pl.when(i == 0)
    def _():
        gather(dcur_ref, 0)

    @pl.when(i + 1 < n)
    def _():
        gather(dnxt_ref, (i + 1) % 2)

    slot = i % 2

    def drain(j, c):
        tile_copy(0, slot, 0, 0).wait()
        return c

    lax.fori_loop(0, 2 * tm, drain, 0, unroll=8)
    r = route_ref[...]
    y = (r[:, 2:3] * _rows_from_tiles(buf.at[slot, 0], tm, d)
         + r[:, 3:4] * _rows_from_tiles(buf.at[slot, 1], tm, d))
    o_ref[...] = _layer_norm(ALPHA * _rows_from_tiles(h_ref, tm, d) + y, g_ref[...], b_ref[...])


def _combine(ys, dest, h1_tiles, route, g, b):
    n = route.shape[0]
    d = g.shape[1]
    nsl = d // LANES
    tm = TOK_TM
    nt = n // tm
    row = lambda i: (i, 0)
    const = lambda i: (0, 0)
    return pl.pallas_call(
        _combine_kernel,
        out_shape=jax.ShapeDtypeStruct((n, d), _F32),
        grid=(nt,),
        in_specs=[pl.BlockSpec((1, 1, TOP_K * tm), lambda i: (i, 0, 0), memory_space=pltpu.SMEM),
                  pl.BlockSpec((1, 1, TOP_K * tm), lambda i: (jnp.minimum(i + 1, nt - 1), 0, 0),
                               memory_space=pltpu.SMEM),
                  pl.BlockSpec(memory_space=pl.ANY),
                  pl.BlockSpec((tm * nsl, LANES), row), pl.BlockSpec((tm, LANES), row),
                  pl.BlockSpec((1, d), const), pl.BlockSpec((1, d), const)],
        out_specs=pl.BlockSpec((tm, d), row),
        scratch_shapes=[pltpu.VMEM((2, 2, tm * nsl, LANES), _F32), pltpu.SemaphoreType.DMA((2,))],
        compiler_params=_params(("arbitrary",)),
        name="moe_combine",
    )(dest, dest, ys, h1_tiles, route, g, b)


def _moe(h1_tiles, route, counts_row, w_gate, w_up, w_down, g, b):
    n = route.shape[0]
    cap = 2 * n + N_EXPERTS * MOE_BLOCK
    counts = counts_row[0, ROUTE_LANE0:ROUTE_LANE0 + N_EXPERTS].astype(jnp.int32)
    padded = (counts + MOE_BLOCK - 1) // MOE_BLOCK * MOE_BLOCK
    ends = jnp.cumsum(padded)
    pad_start = ends - padded
    nblk = cap // MOE_BLOCK
    blk_first = jnp.arange(nblk, dtype=jnp.int32) * MOE_BLOCK
    blk_exp = jnp.minimum(jnp.sum(blk_first[:, None] >= ends[None, :], axis=1), N_EXPERTS - 1).astype(jnp.int32)
    nvalid = (ends[-1:] // MOE_BLOCK).astype(jnp.int32)
    ps_row = jnp.zeros((1, LANES), _F32).at[0, ROUTE_LANE0:ROUTE_LANE0 + N_EXPERTS].set(pad_start.astype(_F32))
    dest = _dest(route, ps_row).T.reshape(n // TOK_TM, 1, TOP_K * TOK_TM)
    seg = jnp.concatenate([pad_start, padded, counts, nvalid]).astype(jnp.int32)
    xs = _dispatch(h1_tiles, dest, seg, cap, n)
    ys = _experts(xs, blk_exp, nvalid, w_gate, w_up, w_down)
    return _combine(ys, dest, h1_tiles, route, g, b)


def _router_params(rg_w, rg_b, re_w, re_b):
    d = rg_w.shape[0]
    pad = LANES - N_GROUPS - N_EXPERTS
    rw = jnp.concatenate([rg_w, re_w, jnp.zeros((d, pad), _F32)], axis=1)
    rb = jnp.concatenate([rg_b, re_b.reshape(-1), jnp.zeros((pad,), _F32)]).reshape(1, LANES)
    return rw, rb


def kernel(x, t5_table, l0_w_qkv, l0_sink, l0_w_o, l0_ln1_g, l0_ln1_b, l0_rg_w, l0_rg_b, l0_re_w, l0_re_b, l0_w_gate, l0_w_up, l0_w_down, l0_ln2_g, l0_ln2_b, l1_w_qkv, l1_w_o, l1_ln1_g, l1_ln1_b, l1_rg_w, l1_rg_b, l1_re_w, l1_re_b, l1_w_gate, l1_w_up, l1_w_down, l1_ln2_g, l1_ln2_b, l2_w_qkv, l2_rpb, l2_w_o, l2_ln1_g, l2_ln1_b, l2_rg_w, l2_rg_b, l2_re_w, l2_re_b, l2_w_gate, l2_w_up, l2_w_down, l2_ln2_g, l2_ln2_b, l3_w_qkv, l3_sink, l3_w_o, l3_ln1_g, l3_ln1_b, l3_rg_w, l3_rg_b, l3_re_w, l3_re_b, l3_w_gate, l3_w_up, l3_w_down, l3_ln2_g, l3_ln2_b):
    mixers = [(l0_w_qkv, l0_sink, l0_w_o), (l1_w_qkv, None, l1_w_o), (l2_w_qkv, l2_rpb, l2_w_o),
              (l3_w_qkv, l3_sink, l3_w_o)]
    norm1 = [(l0_ln1_g, l0_ln1_b), (l1_ln1_g, l1_ln1_b), (l2_ln1_g, l2_ln1_b), (l3_ln1_g, l3_ln1_b)]
    routers = [(l0_rg_w, l0_rg_b, l0_re_w, l0_re_b), (l1_rg_w, l1_rg_b, l1_re_w, l1_re_b),
               (l2_rg_w, l2_rg_b, l2_re_w, l2_re_b), (l3_rg_w, l3_rg_b, l3_re_w, l3_re_b)]
    experts = [(l0_w_gate, l0_w_up, l0_w_down), (l1_w_gate, l1_w_up, l1_w_down),
               (l2_w_gate, l2_w_up, l2_w_down), (l3_w_gate, l3_w_up, l3_w_down)]
    norm2 = [(l0_ln2_g, l0_ln2_b), (l1_ln2_g, l1_ln2_b), (l2_ln2_g, l2_ln2_b), (l3_ln2_g, l3_ln2_b)]

    bsz, seq, d = x.shape
    n = bsz * seq
    inner = N_HEADS * HEAD_DIM
    scale = HEAD_DIM ** -0.5 * LOG2E
    h = x.reshape(n, d)
    bias_a = _band_bias_t(t5_table, A_BLOCK, A_WINDOW, 1, A_BAND)
    for i in range(DEPTH):
        kind = i % N_MIXERS
        w_qkv, extra, w_o = mixers[i]
        kvw = A_KV_HEADS * HEAD_DIM if kind == 0 else inner
        lses, unperm = (), None
        if kind == 1:
            perms = np.stack([_residue_perm(dil) for _, dil in B_BRANCHES if dil > 1])
            outs = _project(h, w_qkv.astype(_BF16), (inner, kvw, kvw), (scale, 1.0, 1.0), jnp.asarray(perms, _BF16))
            unperm = jnp.asarray(perms.transpose(0, 2, 1), _BF16)
            attn, lses = [], []
            for bi, (window, dil) in enumerate(B_BRANCHES):
                q, k, v = outs[3 * bi:3 * bi + 3]
                o, lse = _mixer_b_branch(q, k, v, t5_table, bsz, seq, window, dil)
                attn.append(o)
                lses.append(jnp.pad(lse.reshape(n, N_HEADS), ((0, 0), (0, LANES - N_HEADS))))
        else:
            q, k, v = _project(h, w_qkv.astype(_BF16), (inner, kvw, kvw), (scale, 1.0, 1.0))
            q, k, v = (t.reshape(bsz, seq, t.shape[1]) for t in (q, k, v))
            if kind == 0:
                attn = [_mixer_a(q, k, v, extra.astype(_F32) * LOG2E, bias_a, bsz, seq)]
            else:
                attn = [_mixer_c(q, k, v, _na_bias_t(extra, seq // GRID_W), bsz, seq)]
        attn = [a.reshape(n, inner) for a in attn]
        g1, b1 = norm1[i]
        rw, rb = _router_params(*routers[i])
        h1, route, counts = _post_attn(attn, lses, h, w_o.astype(_BF16), g1.reshape(1, d), b1.reshape(1, d), rw, rb,
                                       unperm)
        g2, b2 = norm2[i]
        h = _moe(h1, route, counts, *experts[i], g2.reshape(1, d), b2.reshape(1, d))
    return h.reshape(bsz, seq, d)
```

```python
import functools
import math

import jax
import jax.numpy as jnp
import numpy as np
from jax import lax
from jax.experimental import pallas as pl
from jax.experimental.pallas import tpu as pltpu

HEAD_DIM = 64
N_HEADS = 16
A_KV_HEADS = 4
A_WINDOW = 128
A_BLOCK = 128
B_BRANCHES = ((128, 1), (512, 4), (2048, 16))
GRID_W = 64
NA_ROWS = 8
NA_COLS = 16
T5_BUCKETS = 32
T5_MAX_DIST = 1024
N_GROUPS = 4
EXPERTS_PER_GROUP = 8
N_EXPERTS = N_GROUPS * EXPERTS_PER_GROUP
TOP_K = 2
DEPTH = 4
N_MIXERS = 3
ALPHA = (2 * DEPTH) ** 0.25
LN_EPS = 1e-5
NEG_INF = -1e30

LANES = 128
PAIR = 2 * HEAD_DIM
BF16_SUBLANES = 16
VT_ROWS = PAIR + BF16_SUBLANES
LOG2E = math.log2(math.e)
LN2 = math.log(2.0)
VMEM_LIMIT = 56 * 1024 * 1024

PROJ_TM = 512
PROJ_CHUNK = 512
POST_TM = 1024
POST_SUB = 256
TOK_TM = 512
DEST_TM = 8192
ROUTE_ROWS = 8
MOE_BLOCK = 512
A_TQ = 2048
A_BAND = 3
A_SUB_PER_GROUP = 2
B_BLOCK = 128
RESIDUE_RUN = BF16_SUBLANES
PERM_CHUNK = 256
TRANSPOSE_UNROLL = 8
UNITS_PER_GROUP = 8
C_UNIT_ROWS = 2
C_BAND_ROWS = 10
C_UNITS_PER_GROUP = 4
C_ROWS_PER_STEP = 64
ROUTE_LANE0 = N_GROUPS

_F32 = jnp.float32
_BF16 = jnp.bfloat16


def _params(sem, vmem=VMEM_LIMIT, flags=None):
    return pltpu.CompilerParams(dimension_semantics=sem, vmem_limit_bytes=vmem, flags=flags)


def _proj_kernel(x_ref, w_ref, *refs, widths, scales, nperm):
    perm_ref = refs[0] if nperm else None
    o_refs = refs[1:] if nperm else refs
    nout = len(widths)
    xb = x_ref[...].astype(_BF16)
    col = 0
    for t, (width, scale) in enumerate(zip(widths, scales)):
        for c in range(0, width, PROJ_CHUNK):
            cw = min(PROJ_CHUNK, width - c)
            acc = jnp.dot(xb, w_ref[:, col + c:col + c + cw], preferred_element_type=_F32)
            if scale != 1.0:
                acc = acc * scale
            ob = acc.astype(_BF16)
            o_refs[t][:, c:c + cw] = ob
            for p in range(nperm):
                for r0 in range(0, ob.shape[0], PERM_CHUNK):
                    moved = jnp.dot(perm_ref[p], ob[r0:r0 + PERM_CHUNK], preferred_element_type=_F32)
                    o_refs[(p + 1) * nout + t][r0:r0 + PERM_CHUNK, c:c + cw] = moved.astype(_BF16)
        col += width


def _project(x2, w_bf16, widths, scales, perms=None):
    n, d = x2.shape
    ctot = sum(widths)
    nperm = 0 if perms is None else perms.shape[0]
    args = [x2, w_bf16] + ([perms] if nperm else [])
    specs = [pl.BlockSpec((PROJ_TM, d), lambda i: (i, 0)), pl.BlockSpec((d, ctot), lambda i: (0, 0))]
    if nperm:
        specs.append(pl.BlockSpec(perms.shape, lambda i: (0, 0, 0)))
    return pl.pallas_call(
        functools.partial(_proj_kernel, widths=tuple(widths), scales=tuple(scales), nperm=nperm),
        out_shape=[jax.ShapeDtypeStruct((n, w), _BF16) for w in widths] * (nperm + 1),
        grid=(n // PROJ_TM,),
        in_specs=specs,
        out_specs=[pl.BlockSpec((PROJ_TM, w), lambda i: (i, 0)) for w in widths] * (nperm + 1),
        compiler_params=_params(("arbitrary",)),
        name="qkv_proj",
    )(*args)


def _residue_perm(dil):
    span = dil * RESIDUE_RUN
    p = np.zeros((PERM_CHUNK, PERM_CHUNK), np.float32)
    for base in range(0, PERM_CHUNK, span):
        for r in range(dil):
            for u in range(RESIDUE_RUN):
                p[base + r * RESIDUE_RUN + u, base + u * dil + r] = 1.0
    return p


def _one():
    return jnp.ones((1,), jnp.int32)


def _lane_lo(rows=1):
    return lax.broadcasted_iota(jnp.int32, (rows, PAIR), 1) < HEAD_DIM


def _swap_halves(x):
    return pltpu.roll(x.astype(_F32), HEAD_DIM, axis=1).astype(_BF16)


class _PairUnit:
    def __init__(self, q2, k, vt, bias, emit, sink_a=None, sink_b=None):
        self.q2, self.k, self.vt, self.bias, self.emit = q2, k, vt, bias, emit
        self.sink_a, self.sink_b = sink_a, sink_b


def _attend_groups(ngroups, units_of, scratch, one):
    assert ngroups % 2 == 0
    lo = _lane_lo()
    nt = (((1,), (1,)), ((), ()))

    def sink_row(unit, m2):
        lane = lax.broadcasted_iota(jnp.int32, (1, m2), 1)
        return jnp.where(lane < m2 // 2, unit.sink_a, unit.sink_b)

    def scores(g, parity):
        s_scr, m_scr = scratch[parity]
        for u, unit in enumerate(units_of(g, parity)):
            q2 = unit.q2()
            zero = jnp.zeros_like(q2)
            qq = jnp.concatenate([jnp.where(lo, q2, zero), jnp.where(lo, zero, q2)], axis=0)
            s = lax.dot_general(unit.k(), qq, nt, preferred_element_type=_F32) + unit.bias()
            m = jnp.max(s, axis=0, keepdims=True)
            if unit.sink_a is not None:
                m = jnp.maximum(m, sink_row(unit, s.shape[1]))
            s_scr[u] = s
            m_scr[u] = m

    def values(g, parity):
        s_scr, m_scr = scratch[parity]
        for u, unit in enumerate(units_of(g, parity)):
            m = m_scr[u]
            e = jnp.exp2(s_scr[u] - m)
            ot = jnp.dot(unit.vt(), e.astype(_BF16), preferred_element_type=_F32)
            den = ot[PAIR:PAIR + 1, :]
            if unit.sink_a is not None:
                den = den + jnp.exp2(sink_row(unit, e.shape[1]) - m)
            ot = ot[:PAIR] * (1.0 / den)
            mq = ot.shape[1] // 2
            top = lax.broadcasted_iota(jnp.int32, (PAIR, 1), 0) < HEAD_DIM
            pair_t = jnp.where(top, ot[:, :mq], ot[:, mq:])
            unit.emit(pair_t.T, (m + jnp.log2(den)) * LN2)

    def region(*work):
        def body(_, carry):
            for fn, g, parity in work:
                fn(g, parity)
            return carry
        lax.fori_loop(0, one, body, 0)

    region((scores, 0, 0))

    def steady(i, carry):
        g = 2 * i
        region((values, g, 0), (scores, g + 1, 1))
        region((values, g + 1, 1), (scores, g + 2, 0))
        return carry

    lax.fori_loop(0, ngroups // 2 - 1, steady, 0)
    region((values, ngroups - 2, 0), (scores, ngroups - 1, 1))
    region((values, ngroups - 1, 1))


def _transpose_blocks(v_ref_rows, nblk, store):
    def body(j, carry):
        rows = pl.ds(pl.multiple_of(j * LANES, LANES), LANES)
        vt = v_ref_rows(rows).T
        store(j, jnp.concatenate([vt, jnp.ones((VT_ROWS - PAIR, LANES), _BF16)], axis=0))
        return carry
    lax.fori_loop(0, nblk, body, 0, unroll=TRANSPOSE_UNROLL)


def _t5_bucket(rel):
    half = T5_BUCKETS // 2
    exact = half // 2
    n = np.abs(rel)
    big = exact + (np.log(np.maximum(n, 1) / exact) / math.log(T5_MAX_DIST / exact) * (half - exact)).astype(np.int32)
    return ((rel > 0) * half + np.where(n < exact, n, np.minimum(big, half - 1))).astype(np.int32)


def _band_bias_t(t5_table, blk, window, dil, nband):
    tk = nband * blk
    off = tk - 1
    rel = np.arange(-off, tk)
    vec = jnp.take(t5_table, jnp.asarray(_t5_bucket(rel * dil)), axis=0).astype(_F32)
    vec = jnp.where(jnp.asarray(np.abs(rel) <= window)[:, None], vec * LOG2E, NEG_INF).T
    h = vec.shape[0]
    vec = jnp.pad(vec, ((0, 0), (0, 1)))
    w = 2 * tk - 1
    skew = jnp.tile(vec, (1, blk))[:, :blk * w].reshape(h, blk, w)
    t = jnp.stack([skew[:, :, off - v * blk:off - v * blk + tk] for v in range(nband)], axis=1)
    t = t.reshape(h // 2, 2, nband, blk, tk).transpose(0, 2, 4, 1, 3)
    return t.reshape(h // 2, nband, tk, 2 * blk)


def _band_window(gb, blk, length, nband):
    first = jnp.clip(gb - 1, 0, length // blk - nband)
    return first, gb - first


def _attn_a_kernel(sink_ref, one_ref, q_ref, k_ref, v_ref, bias_ref, o_ref, kd_ref, vdt_ref,
                   s0_scr, m0_scr, s1_scr, m1_scr, *, seq):
    kvb = pl.program_id(0)
    qt = pl.program_id(2)
    one = one_ref[0]
    nblk = seq // A_BLOCK

    @pl.when(qt == 0)
    def _():
        lo = _lane_lo()

        def fill(j, carry):
            rows = pl.ds(pl.multiple_of(j * A_BLOCK, A_BLOCK), A_BLOCK)
            kk = k_ref[0, rows, :]
            ks = _swap_halves(kk)
            kd_ref[0, rows, :] = jnp.where(lo, kk, ks)
            kd_ref[1, rows, :] = jnp.where(lo, ks, kk)
            vt = v_ref[0, rows, :].T
            ones = jnp.ones((VT_ROWS - PAIR, A_BLOCK), _BF16)
            vdt_ref[0, j] = jnp.concatenate([vt[:HEAD_DIM], vt[:HEAD_DIM], ones], axis=0)
            vdt_ref[1, j] = jnp.concatenate([vt[HEAD_DIM:], vt[HEAD_DIM:], ones], axis=0)
            return carry

        lax.fori_loop(0, nblk, fill, 0, unroll=TRANSPOSE_UNROLL)

    nsub = A_TQ // A_BLOCK
    npair = q_ref.shape[2] // PAIR
    hbase = kvb * (2 * npair)

    def units_of(g, parity):
        units = []
        for j in range(A_SUB_PER_GROUP):
            si = g * A_SUB_PER_GROUP + j
            first, var = _band_window(qt * nsub + si, A_BLOCK, seq, A_BAND)
            rows = pl.ds(pl.multiple_of(si * A_BLOCK, A_BLOCK), A_BLOCK)
            band = pl.ds(pl.multiple_of(first * A_BLOCK, A_BLOCK), A_BAND * A_BLOCK)
            for jj in range(npair):
                hk = jj // (npair // 2)
                cols = slice(jj * PAIR, (jj + 1) * PAIR)

                def emit(o, lse, rows=rows, cols=cols):
                    o_ref[0, rows, cols] = o.astype(o_ref.dtype)

                units.append(_PairUnit(
                    q2=lambda rows=rows, cols=cols: q_ref[0, rows, cols],
                    k=lambda hk=hk, band=band: kd_ref[hk, band, :],
                    vt=lambda hk=hk, first=first: jnp.concatenate(
                        [vdt_ref[hk, first + t] for t in range(A_BAND)], axis=1),
                    bias=lambda jj=jj, var=var: bias_ref[jj, var],
                    emit=emit, sink_a=sink_ref[hbase + 2 * jj], sink_b=sink_ref[hbase + 2 * jj + 1]))
        return units

    _attend_groups(nsub // A_SUB_PER_GROUP, units_of, ((s0_scr, m0_scr), (s1_scr, m1_scr)), one)


def _mixer_a(q, k, v, sink, bias, bsz, seq):
    nkvb = k.shape[2] // PAIR
    qw = q.shape[2] // nkvb
    npair = qw // PAIR
    tk = A_BAND * A_BLOCK
    nunits = A_SUB_PER_GROUP * npair
    group_scratch = [pltpu.VMEM((nunits, tk, 2 * A_BLOCK), _F32), pltpu.VMEM((nunits, 1, 2 * A_BLOCK), _F32)]
    return pl.pallas_call(
        functools.partial(_attn_a_kernel, seq=seq),
        out_shape=jax.ShapeDtypeStruct(q.shape, _BF16),
        grid=(nkvb, bsz, seq // A_TQ),
        in_specs=[pl.BlockSpec(memory_space=pltpu.SMEM), pl.BlockSpec(memory_space=pltpu.SMEM),
                  pl.BlockSpec((1, A_TQ, qw), lambda c, b, t: (b, t, c)),
                  pl.BlockSpec((1, seq, PAIR), lambda c, b, t: (b, 0, c)),
                  pl.BlockSpec((1, seq, PAIR), lambda c, b, t: (b, 0, c)),
                  pl.BlockSpec((npair, A_BAND, tk, 2 * A_BLOCK), lambda c, b, t: (c, 0, 0, 0))],
        out_specs=pl.BlockSpec((1, A_TQ, qw), lambda c, b, t: (b, t, c)),
        scratch_shapes=[pltpu.VMEM((2, seq, PAIR), _BF16), pltpu.VMEM((2, seq // A_BLOCK, VT_ROWS, A_BLOCK), _BF16)]
        + group_scratch + group_scratch,
        compiler_params=_params(("arbitrary", "arbitrary", "arbitrary")),
        name="mixer_a",
    )(sink, _one(), q, k, v, bias)


def _attn_band_kernel(one_ref, q_ref, k_ref, v_ref, bias_ref, o_ref, lse_ref, vt_ref,
                      s0_scr, m0_scr, s1_scr, m1_scr, *, length, npairs, nband, chunked):
    col = pl.program_id(2)
    one = one_ref[0]
    nblk = length // B_BLOCK
    runs = B_BLOCK // RESIDUE_RUN

    def load(ref, blk0, nblocks, cols):
        if chunked:
            r0 = pl.multiple_of(blk0 * runs, runs)
            return ref[pl.ds(r0, nblocks * runs), :, cols].reshape(nblocks * B_BLOCK, PAIR)
        return ref[0, pl.ds(pl.multiple_of(blk0 * B_BLOCK, B_BLOCK), nblocks * B_BLOCK), cols]

    def store_out(blk0, cols, val):
        if chunked:
            o_ref[pl.ds(pl.multiple_of(blk0 * runs, runs), runs), :, cols] = val.reshape(runs, RESIDUE_RUN, PAIR)
        else:
            o_ref[0, pl.ds(pl.multiple_of(blk0 * B_BLOCK, B_BLOCK), B_BLOCK), cols] = val

    for p in range(npairs):
        pcols = slice(p * PAIR, (p + 1) * PAIR)

        def fill(j, carry, p=p, pcols=pcols):
            vt = load(v_ref, j, 1, pcols).T
            vt_ref[p, j] = jnp.concatenate([vt, jnp.ones((VT_ROWS - PAIR, B_BLOCK), _BF16)], axis=0)
            return carry

        lax.fori_loop(0, nblk, fill, 0, unroll=min(TRANSPOSE_UNROLL, nblk))

    def unit(gb, p):
        first, var = _band_window(gb, B_BLOCK, length, nband)
        cols = slice(p * PAIR, (p + 1) * PAIR)

        def emit(o, lse):
            store_out(gb, cols, o.astype(o_ref.dtype))
            ha = col * (2 * npairs) + 2 * p
            lse_ref[0, 0, gb, pl.ds(ha, 1), :] = lse[:, :B_BLOCK]
            lse_ref[0, 0, gb, pl.ds(ha + 1, 1), :] = lse[:, B_BLOCK:]

        return _PairUnit(
            q2=lambda: load(q_ref, gb, 1, cols), k=lambda: load(k_ref, first, nband, cols),
            vt=lambda: jnp.concatenate([vt_ref[p, first + t] for t in range(nband)], axis=1),
            bias=lambda: bias_ref[p, var], emit=emit)

    if npairs == 1:
        ngroups = nblk // UNITS_PER_GROUP
        units_of = lambda g, parity: [unit(g * UNITS_PER_GROUP + j, 0) for j in range(UNITS_PER_GROUP)]
    else:
        assert npairs == UNITS_PER_GROUP
        ngroups = nblk
        units_of = lambda g, parity: [unit(g, p) for p in range(npairs)]
    _attend_groups(ngroups, units_of, ((s0_scr, m0_scr), (s1_scr, m1_scr)), one)


def _mixer_b_branch(q, k, v, t5_table, bsz, seq, window, dil):
    length = seq // dil
    width = q.shape[1]
    nblk = length // B_BLOCK
    nband = min(3, nblk)
    tk = nband * B_BLOCK
    npairs = (width // PAIR) if length * width * 2 <= 2 * 1024 * 1024 else 1
    ncol = width // (npairs * PAIR)
    bw = npairs * PAIR
    bias = _band_bias_t(t5_table, B_BLOCK, window // 2 // dil, dil, nband)
    chunked = dil > 1
    if chunked:
        nrun = length // RESIDUE_RUN
        shape = (bsz * nrun, dil, RESIDUE_RUN, width)
        spec = pl.BlockSpec((nrun, None, RESIDUE_RUN, bw), lambda b, r, c: (b, r, 0, c))
    else:
        shape = (bsz, seq, width)
        spec = pl.BlockSpec((1, seq, bw), lambda b, r, c: (b, 0, c))

    o, lse = pl.pallas_call(
        functools.partial(_attn_band_kernel, length=length, npairs=npairs, nband=nband, chunked=chunked),
        out_shape=[jax.ShapeDtypeStruct(shape, _BF16),
                   jax.ShapeDtypeStruct((bsz, dil, nblk, N_HEADS, B_BLOCK), _F32)],
        grid=(bsz, dil, ncol),
        in_specs=[pl.BlockSpec(memory_space=pltpu.SMEM), spec, spec, spec,
                  pl.BlockSpec((npairs, nband, tk, 2 * B_BLOCK), lambda b, r, c: (c, 0, 0, 0))],
        out_specs=[spec, pl.BlockSpec((1, 1, nblk, N_HEADS, B_BLOCK), lambda b, r, c: (b, r, 0, 0, 0))],
        scratch_shapes=[pltpu.VMEM((npairs, nblk, VT_ROWS, B_BLOCK), _BF16)] + 2 * [
            pltpu.VMEM((UNITS_PER_GROUP, tk, 2 * B_BLOCK), _F32), pltpu.VMEM((UNITS_PER_GROUP, 1, 2 * B_BLOCK), _F32)],
        compiler_params=_params(("arbitrary", "arbitrary", "arbitrary")),
        name=f"mixer_b_dil{dil}",
    )(_one(), q.reshape(shape), k.reshape(shape), v.reshape(shape), bias)
    lse = lse.transpose(0, 2, 4, 1, 3).reshape(bsz, seq, N_HEADS)
    return o.reshape(bsz * seq, width), lse


def _na_bias_t(rpb, rows):
    canon = (0, 2, 4, rows - 4, rows - 2)
    h = rpb.shape[0]
    kc = np.arange(GRID_W)[:, None]
    qc = np.arange(GRID_W)[None, :]
    ws = np.clip(qc - NA_COLS // 2, 0, GRID_W - NA_COLS)
    col_ok = (kc >= ws) & (kc < ws + NA_COLS)
    col_idx = np.clip(kc - qc, 1 - NA_COLS, NA_COLS - 1) + NA_COLS - 1
    colpart = jnp.where(jnp.asarray(col_ok), rpb.astype(_F32)[:, :, jnp.asarray(col_idx)] * LOG2E, NEG_INF)
    masked = jnp.full((h, GRID_W, GRID_W), NEG_INF, _F32)
    variants = []
    for i in canon:
        start = int(np.clip(i - NA_ROWS // 2, 0, rows - C_BAND_ROWS))
        per_key_row = []
        for kr in range(C_BAND_ROWS):
            per_query_row = []
            for qr in range(C_UNIT_ROWS):
                iq, kabs = i + qr, start + kr
                first_key_row = int(np.clip(iq - NA_ROWS // 2, 0, rows - NA_ROWS))
                inside = first_key_row <= kabs < first_key_row + NA_ROWS
                per_query_row.append(colpart[:, kabs - iq + NA_ROWS - 1] if inside else masked)
            per_key_row.append(jnp.stack(per_query_row, axis=2))
        variants.append(jnp.stack(per_key_row, axis=1))
    b = jnp.stack(variants, axis=1)
    nq, nk = C_UNIT_ROWS * GRID_W, C_BAND_ROWS * GRID_W
    b = b.reshape(h // 2, 2, len(canon), nk, nq)
    return b.transpose(0, 2, 3, 1, 4).reshape(h // 2, len(canon), nk, 2 * nq)


def _attn_c_kernel(one_ref, q_ref, k_ref, v_ref, bias_ref, o_ref, vt_ref, s0_scr, m0_scr, s1_scr, m1_scr, *, rows):
    rt = pl.program_id(2)
    nq = C_UNIT_ROWS * GRID_W
    band_blocks = C_BAND_ROWS * GRID_W // LANES

    @pl.when(rt == 0)
    def _():
        def store(j, blk):
            vt_ref[j] = blk
        _transpose_blocks(lambda r: v_ref[0, r, :], rows * GRID_W // LANES, store)

    def unit(u):
        i = rt * C_ROWS_PER_STEP + C_UNIT_ROWS * u
        start = jnp.clip(i - NA_ROWS // 2, 0, rows - C_BAND_ROWS)
        var = lax.shift_right_logical(i - start, 1)
        first = lax.shift_right_logical(start, 1)
        keys = pl.ds(pl.multiple_of(first * LANES, LANES), C_BAND_ROWS * GRID_W)
        qrows = pl.ds(pl.multiple_of(u * nq, nq), nq)

        def emit(o, lse):
            o_ref[0, qrows, :] = o.astype(o_ref.dtype)

        return _PairUnit(
            q2=lambda: q_ref[0, qrows, :], k=lambda: k_ref[0, keys, :],
            vt=lambda: jnp.concatenate([vt_ref[first + t] for t in range(band_blocks)], axis=1),
            bias=lambda: bias_ref[0, var], emit=emit)

    ngroups = C_ROWS_PER_STEP // C_UNIT_ROWS // C_UNITS_PER_GROUP
    units_of = lambda g, parity: [unit(g * C_UNITS_PER_GROUP + j) for j in range(C_UNITS_PER_GROUP)]
    _attend_groups(ngroups, units_of, ((s0_scr, m0_scr), (s1_scr, m1_scr)), one_ref[0])


def _mixer_c(q, k, v, bias, bsz, seq):
    rows = seq // GRID_W
    tq = C_ROWS_PER_STEP * GRID_W
    npair = q.shape[2] // PAIR
    nq, nk = C_UNIT_ROWS * GRID_W, C_BAND_ROWS * GRID_W
    return pl.pallas_call(
        functools.partial(_attn_c_kernel, rows=rows),
        out_shape=jax.ShapeDtypeStruct(q.shape, _BF16),
        grid=(npair, bsz, seq // tq),
        in_specs=[pl.BlockSpec(memory_space=pltpu.SMEM),
                  pl.BlockSpec((1, tq, PAIR), lambda c, b, t: (b, t, c)),
                  pl.BlockSpec((1, seq, PAIR), lambda c, b, t: (b, 0, c)),
                  pl.BlockSpec((1, seq, PAIR), lambda c, b, t: (b, 0, c)),
                  pl.BlockSpec((1, bias.shape[1], nk, 2 * nq), lambda c, b, t: (c, 0, 0, 0))],
        out_specs=pl.BlockSpec((1, tq, PAIR), lambda c, b, t: (b, t, c)),
        scratch_shapes=[pltpu.VMEM((seq // LANES, VT_ROWS, LANES), _BF16)] + 2 * [
            pltpu.VMEM((C_UNITS_PER_GROUP, nk, 2 * nq), _F32), pltpu.VMEM((C_UNITS_PER_GROUP, 1, 2 * nq), _F32)],
        compiler_params=_params(("arbitrary", "arbitrary", "arbitrary")),
        name="mixer_c",
    )(_one(), q, k, v, bias)


def _layer_norm(z, g, b):
    mu = jnp.mean(z, axis=-1, keepdims=True)
    zc = z - mu
    var = jnp.mean(zc * zc, axis=-1, keepdims=True)
    return zc * lax.rsqrt(var + LN_EPS) * g + b


def _rows_to_tiles(ref, x):
    nrows, d = x.shape
    nsl = d // LANES
    for s in range(nsl):
        ref[pl.ds(s, nrows, stride=nsl), :] = x[:, s * LANES:(s + 1) * LANES]


def _rows_from_tiles(ref, nrows, d):
    nsl = d // LANES
    return jnp.concatenate([ref[pl.ds(s, nrows, stride=nsl), :] for s in range(nsl)], axis=1)


def _split_bf16(x):
    hi = x.astype(_BF16)
    return hi, (x - hi.astype(_F32)).astype(_BF16)


def _router_logits(hn, rw_ref, rb_ref):
    h_hi, h_lo = _split_bf16(hn)
    w_hi, w_lo = _split_bf16(rw_ref[...])
    both = jnp.dot(h_hi, jnp.concatenate([w_hi, w_lo], axis=1), preferred_element_type=_F32)
    return (both[:, :LANES]
            + (both[:, LANES:] + jnp.dot(h_lo, w_hi, preferred_element_type=_F32))) + rb_ref[...]


def _route(logits, tri_ref, carry_ref):
    tm = logits.shape[0]
    lane = lax.broadcasted_iota(jnp.int32, (tm, LANES), 1).astype(_F32)
    ninf = -jnp.inf
    is_grp = lane < N_GROUPS
    lg = jnp.where(is_grp, logits, ninf)
    gmax = jnp.max(lg, axis=-1, keepdims=True)
    grp = jnp.min(jnp.where(lg == gmax, lane, float(LANES)), axis=-1, keepdims=True)
    p_grp = 1.0 / jnp.sum(jnp.where(is_grp, jnp.exp(lg - gmax), 0.0), axis=-1, keepdims=True)
    lo_lane = ROUTE_LANE0 + EXPERTS_PER_GROUP * grp
    in_grp = (lane >= lo_lane) & (lane < lo_lane + EXPERTS_PER_GROUP)
    le = jnp.where(in_grp, logits, ninf)
    v1 = jnp.max(le, axis=-1, keepdims=True)
    i1 = jnp.min(jnp.where(le == v1, lane, float(LANES)), axis=-1, keepdims=True)
    le2 = jnp.where(lane == i1, ninf, le)
    v2 = jnp.max(le2, axis=-1, keepdims=True)
    i2 = jnp.min(jnp.where(le2 == v2, lane, float(LANES)), axis=-1, keepdims=True)
    t = jnp.exp(v2 - v1)
    s1 = 1.0 / (1.0 + t)
    g1 = p_grp * s1
    g2 = p_grp * (t * s1)
    sel1 = lane == i1
    sel2 = lane == i2
    onehot = jnp.where(sel1 | sel2, 1.0, 0.0)
    sub = tri_ref.shape[0]
    carry = carry_ref[...]
    parts = []
    for r0 in range(0, tm, sub):
        part = onehot[r0:r0 + sub]
        parts.append(jnp.dot(tri_ref[...], part.astype(_BF16), preferred_element_type=_F32) + carry)
        carry = carry + jnp.sum(part, axis=0, keepdims=True)
    carry_ref[...] = carry
    before = jnp.concatenate(parts, axis=0) if len(parts) > 1 else parts[0]
    rank1 = jnp.sum(jnp.where(sel1, before, 0.0), axis=-1, keepdims=True)
    rank2 = jnp.sum(jnp.where(sel2, before, 0.0), axis=-1, keepdims=True)
    out = jnp.zeros((tm, LANES), _F32)
    for idx, val in enumerate((i1 - ROUTE_LANE0, i2 - ROUTE_LANE0, g1, g2, rank1, rank2)):
        out = jnp.where(lane == idx, val, out)
    return out


def _post_attn_kernel(*refs, n_branch):
    attn_refs = refs[:n_branch]
    pos = n_branch
    if n_branch > 1:
        lse_refs = refs[pos:pos + n_branch]
        expand_ref, unperm_ref = refs[pos + n_branch:pos + n_branch + 2]
        pos += n_branch + 2
    wo_ref, h_ref, g_ref, b_ref, rw_ref, rb_ref, tri_ref = refs[pos:pos + 7]
    h1_ref, route_ref, routet_ref, counts_ref, carry_ref = refs[pos + 7:pos + 12]

    @pl.when(pl.program_id(0) == 0)
    def _():
        carry_ref[...] = jnp.zeros_like(carry_ref)

    d = h_ref.shape[1]
    logits = []
    for sub in range(h_ref.shape[0] // POST_SUB):
        rows = slice(sub * POST_SUB, (sub + 1) * POST_SUB)
        if n_branch == 1:
            a = attn_refs[0][rows, :]
        else:
            lses = [r[rows, :] for r in lse_refs]
            mx = functools.reduce(jnp.maximum, lses)
            es = [jnp.exp(l - mx) for l in lses]
            inv = 1.0 / functools.reduce(jnp.add, es)
            mixed = None
            for bi, (e, a_ref) in enumerate(zip(es, attn_refs)):
                w_hi, w_lo = _split_bf16(e * inv)
                w_full = jnp.dot(jnp.concatenate([w_hi, w_lo], axis=1), expand_ref[...],
                                 preferred_element_type=_F32)
                if bi == 0:
                    a_nat = a_ref[rows, :].astype(_F32)
                else:
                    a_nat = jnp.dot(unperm_ref[bi - 1], a_ref[rows, :], preferred_element_type=_F32)
                term = w_full * a_nat
                mixed = term if mixed is None else mixed + term
            a = mixed.astype(_BF16)
        y = jnp.dot(a, wo_ref[...], preferred_element_type=_F32)
        hn = _layer_norm(ALPHA * h_ref[rows, :] + y, g_ref[...], b_ref[...])
        nsl = d // LANES
        _rows_to_tiles(h1_ref.at[pl.ds(sub * POST_SUB * nsl, POST_SUB * nsl)], hn)
        logits.append(_router_logits(hn, rw_ref, rb_ref))
    route = _route(jnp.concatenate(logits, axis=0), tri_ref, carry_ref)
    route_ref[...] = route
    routet_ref[...] = route.T[:ROUTE_ROWS]
    counts_ref[...] = carry_ref[...]


def _post_attn(attn, lses, h2, w_o, g, b, rw, rb, unperm=None):
    n, d = h2.shape
    tm = POST_TM
    nb = len(attn)
    tri = jnp.asarray(np.tril(np.ones((POST_SUB, POST_SUB), np.float32), -1), _BF16)
    row = lambda i: (i, 0)
    const = lambda i: (0, 0)
    args = list(attn)
    specs = [pl.BlockSpec((tm, d), row) for _ in attn]
    if nb > 1:
        expand = np.zeros((2, LANES, d), np.float32)
        for hd in range(N_HEADS):
            expand[:, hd, hd * HEAD_DIM:(hd + 1) * HEAD_DIM] = 1.0
        args += list(lses) + [jnp.asarray(expand.reshape(2 * LANES, d), _BF16), unperm]
        specs += [pl.BlockSpec((tm, LANES), row) for _ in lses] + [
            pl.BlockSpec((2 * LANES, d), const), pl.BlockSpec(unperm.shape, lambda i: (0, 0, 0))]
    args += [w_o, h2, g, b, rw, rb, tri]
    specs += [pl.BlockSpec((d, d), const), pl.BlockSpec((tm, d), row), pl.BlockSpec((1, d), const),
              pl.BlockSpec((1, d), const), pl.BlockSpec((d, LANES), const), pl.BlockSpec((1, LANES), const),
              pl.BlockSpec((POST_SUB, POST_SUB), const)]
    return pl.pallas_call(
        functools.partial(_post_attn_kernel, n_branch=nb),
        out_shape=[jax.ShapeDtypeStruct((n * d // LANES, LANES), _F32), jax.ShapeDtypeStruct((n, LANES), _F32),
                   jax.ShapeDtypeStruct((ROUTE_ROWS, n), _F32),
                   jax.ShapeDtypeStruct((1, LANES), _F32)],
        grid=(n // tm,),
        in_specs=specs,
        out_specs=[pl.BlockSpec((tm * d // LANES, LANES), row), pl.BlockSpec((tm, LANES), row),
                   pl.BlockSpec((ROUTE_ROWS, tm), lambda i: (0, i)), pl.BlockSpec((1, LANES), const)],
        scratch_shapes=[pltpu.VMEM((1, LANES), _F32)],
        compiler_params=_params(("arbitrary",)),
        name="post_attn_router",
    )(*args)


def _dest_kernel(ps_ref, rt_ref, o_ref):
    expert = rt_ref[0:TOP_K, :]
    slot = rt_ref[4:4 + TOP_K, :]
    for e in range(N_EXPERTS):
        slot = slot + jnp.where(expert == float(e), ps_ref[e], 0.0)
    o_ref[...] = slot.astype(jnp.int32)


def _dest(route_t, pad_start):
    n = route_t.shape[1]
    tm = min(DEST_TM, n)
    return pl.pallas_call(
        _dest_kernel,
        out_shape=jax.ShapeDtypeStruct((TOP_K, n), jnp.int32),
        grid=(n // tm,),
        in_specs=[pl.BlockSpec(memory_space=pltpu.SMEM), pl.BlockSpec((ROUTE_ROWS, tm), lambda i: (0, i))],
        out_specs=pl.BlockSpec((TOP_K, tm), lambda i: (0, i)),
        compiler_params=_params(("arbitrary",)),
        name="moe_dest",
    )(pad_start, route_t)


def _dispatch_kernel(seg_ref, dest_ref, x_ref, xs_hbm, zero_ref, zsem, sem):
    i = pl.program_id(0)
    ntok = TOK_TM
    nsl = x_ref.shape[0] // ntok
    blk_rows = MOE_BLOCK * nsl

    def zero_fill(e):
        first = pl.multiple_of((seg_ref[e] + seg_ref[N_EXPERTS + e] - MOE_BLOCK) * nsl, blk_rows)
        return pltpu.make_async_copy(zero_ref, xs_hbm.at[pl.ds(first, blk_rows)], zsem)

    def has_pad(e):
        return seg_ref[N_EXPERTS + e] > seg_ref[2 * N_EXPERTS + e]

    @pl.when(i == 0)
    def _():
        zero_ref[...] = jnp.zeros_like(zero_ref)

        def start(e, c):
            @pl.when(has_pad(e))
            def _():
                zero_fill(e).start()
            return c

        def wait(e, c):
            @pl.when(has_pad(e))
            def _():
                zero_fill(e).wait()
            return c

        lax.fori_loop(0, N_EXPERTS, start, 0)
        lax.fori_loop(0, N_EXPERTS, wait, 0)

        nvalid = seg_ref[3 * N_EXPERTS]
        nblk = xs_hbm.shape[0] // blk_rows

        def tail(j):
            first = pl.multiple_of((nvalid + j) * blk_rows, blk_rows)
            return pltpu.make_async_copy(zero_ref, xs_hbm.at[pl.ds(first, blk_rows)], zsem)

        def tail_start(j, c):
            @pl.when(nvalid + j < nblk)
            def _():
                tail(j).start()
            return c

        def tail_wait(j, c):
            @pl.when(nvalid + j < nblk)
            def _():
                tail(j).wait()
            return c

        lax.fori_loop(0, N_EXPERTS, tail_start, 0)
        lax.fori_loop(0, N_EXPERTS, tail_wait, 0)

    def row_copy(j, slot):
        return pltpu.make_async_copy(x_ref.at[pl.ds(pl.multiple_of(j * nsl, nsl), nsl)],
                                     xs_hbm.at[pl.ds(pl.multiple_of(slot * nsl, nsl), nsl)], sem)

    def issue(j, c):
        for k in range(2):
            row_copy(j, dest_ref[0, 0, TOP_K * j + k]).start(priority=k)
        return c

    def drain(j, c):
        row_copy(0, 0).wait()
        return c

    lax.fori_loop(0, ntok, issue, 0, unroll=4)
    lax.fori_loop(0, 2 * ntok, drain, 0, unroll=8)


def _dispatch(x_tiles, dest, seg, cap, n):
    nsl = x_tiles.shape[0] // n
    tm = TOK_TM
    nt = n // tm
    return pl.pallas_call(
        _dispatch_kernel,
        out_shape=jax.ShapeDtypeStruct((cap * nsl, LANES), x_tiles.dtype),
        grid_spec=pltpu.PrefetchScalarGridSpec(
            num_scalar_prefetch=1,
            grid=(nt,),
            in_specs=[pl.BlockSpec((1, 1, TOP_K * tm), lambda i, seg: (i, 0, 0), memory_space=pltpu.SMEM),
                      pl.BlockSpec((tm * nsl, LANES), lambda i, seg: (i, 0))],
            out_specs=pl.BlockSpec(memory_space=pl.ANY),
            scratch_shapes=[pltpu.VMEM((MOE_BLOCK * nsl, LANES), x_tiles.dtype), pltpu.SemaphoreType.DMA,
                            pltpu.SemaphoreType.DMA]),
        compiler_params=_params(("arbitrary",)),
        name="moe_dispatch",
    )(seg, dest, x_tiles)


def _expert_kernel(blk_ref, nvalid_ref, xs_ref, wg_ref, wu_ref, wd_ref, ys_ref, wgb, wub, wdb):
    i = pl.program_id(0)
    prev = blk_ref[jnp.maximum(i - 1, 0)]
    valid = i < nvalid_ref[0]

    @pl.when(valid & ((i == 0) | (blk_ref[i] != prev)))
    def _():
        wgb[...] = wg_ref[0].astype(_BF16)
        wub[...] = wu_ref[0].astype(_BF16)
        wdb[...] = wd_ref[0].astype(_BF16)

    @pl.when(valid)
    def _():
        d = wgb.shape[0]
        xb = _rows_from_tiles(xs_ref, MOE_BLOCK, d).astype(_BF16)
        gate = jnp.dot(xb, wgb[...], preferred_element_type=_F32)
        up = jnp.dot(xb, wub[...], preferred_element_type=_F32)
        mid = (gate * jax.nn.sigmoid(gate)) * up
        _rows_to_tiles(ys_ref, jnp.dot(mid.astype(_BF16), wdb[...], preferred_element_type=_F32))

    @pl.when(jnp.logical_not(valid))
    def _():
        ys_ref[...] = jnp.zeros_like(ys_ref)


def _experts(xs, blk_exp, nvalid, w_gate, w_up, w_down):
    d, de = w_gate.shape[1], w_gate.shape[2]
    nsl = d // LANES
    cap = xs.shape[0] // nsl
    nblk = cap // MOE_BLOCK

    def xmap(i, blk, nv):
        return (jnp.minimum(i, nv[0] - 1), 0)

    def wmap(i, blk, nv):
        return (blk[i], 0, 0)

    return pl.pallas_call(
        _expert_kernel,
        out_shape=jax.ShapeDtypeStruct((cap * nsl, LANES), _F32),
        grid_spec=pltpu.PrefetchScalarGridSpec(
            num_scalar_prefetch=2,
            grid=(nblk,),
            in_specs=[pl.BlockSpec((MOE_BLOCK * nsl, LANES), xmap),
                      pl.BlockSpec((1, d, de), wmap), pl.BlockSpec((1, d, de), wmap),
                      pl.BlockSpec((1, de, d), wmap)],
            out_specs=pl.BlockSpec((MOE_BLOCK * nsl, LANES), lambda i, blk, nv: (i, 0)),
            scratch_shapes=[pltpu.VMEM((d, de), _BF16), pltpu.VMEM((d, de), _BF16), pltpu.VMEM((de, d), _BF16)]),
        compiler_params=_params(("arbitrary",)),
        name="moe_experts",
    )(blk_exp, nvalid, xs, w_gate, w_up, w_down)


def _combine_kernel(dcur_ref, dnxt_ref, ys_hbm, h_ref, route_ref, g_ref, b_ref, o_ref, buf, sem):
    i = pl.program_id(0)
    n = pl.num_programs(0)
    tm = TOK_TM

    d = o_ref.shape[1]
    nsl = d // LANES

    def tile_copy(src_slot, slot, k, j):
        return pltpu.make_async_copy(ys_hbm.at[pl.ds(pl.multiple_of(src_slot * nsl, nsl), nsl)],
                                     buf.at[slot, k, pl.ds(pl.multiple_of(j * nsl, nsl), nsl)], sem.at[slot])

    def gather(dref, slot):
        def body(j, c):
            for k in range(2):
                tile_copy(dref[0, 0, TOP_K * j + k], slot, k, j).start(priority=k)
            return c
        lax.fori_loop(0, tm, body, 0, unroll=4)

    @pl.when(i == 0)
    def _():
        gather(dcur_ref, 0)

    @pl.when(i + 1 < n)
    def _():
        gather(dnxt_ref, (i + 1) % 2)

    slot = i % 2

    def drain(j, c):
        tile_copy(0, slot, 0, 0).wait()
        return c

    lax.fori_loop(0, 2 * tm, drain, 0, unroll=8)
    r = route_ref[...]
    y = (r[:, 2:3] * _rows_from_tiles(buf.at[slot, 0], tm, d)
         + r[:, 3:4] * _rows_from_tiles(buf.at[slot, 1], tm, d))
    o_ref[...] = _layer_norm(ALPHA * _rows_from_tiles(h_ref, tm, d) + y, g_ref[...], b_ref[...])


def _combine(ys, dest, h1_tiles, route, g, b):
    n = route.shape[0]
    d = g.shape[1]
    nsl = d // LANES
    tm = TOK_TM
    nt = n // tm
    row = lambda i: (i, 0)
    const = lambda i: (0, 0)
    return pl.pallas_call(
        _combine_kernel,
        out_shape=jax.ShapeDtypeStruct((n, d), _F32),
        grid=(nt,),
        in_specs=[pl.BlockSpec((1, 1, TOP_K * tm), lambda i: (i, 0, 0), memory_space=pltpu.SMEM),
                  pl.BlockSpec((1, 1, TOP_K * tm), lambda i: (jnp.minimum(i + 1, nt - 1), 0, 0),
                               memory_space=pltpu.SMEM),
                  pl.BlockSpec(memory_space=pl.ANY),
                  pl.BlockSpec((tm * nsl, LANES), row), pl.BlockSpec((tm, LANES), row),
                  pl.BlockSpec((1, d), const), pl.BlockSpec((1, d), const)],
        out_specs=pl.BlockSpec((tm, d), row),
        scratch_shapes=[pltpu.VMEM((2, 2, tm * nsl, LANES), _F32), pltpu.SemaphoreType.DMA((2,))],
        compiler_params=_params(("arbitrary",)),
        name="moe_combine",
    )(dest, dest, ys, h1_tiles, route, g, b)


def _moe(h1_tiles, route, route_t, counts_row, w_gate, w_up, w_down, g, b):
    n = route.shape[0]
    cap = 2 * n + N_EXPERTS * MOE_BLOCK
    counts = counts_row[0, ROUTE_LANE0:ROUTE_LANE0 + N_EXPERTS].astype(jnp.int32)
    padded = (counts + MOE_BLOCK - 1) // MOE_BLOCK * MOE_BLOCK
    ends = jnp.cumsum(padded)
    pad_start = ends - padded
    nblk = cap // MOE_BLOCK
    blk_first = jnp.arange(nblk, dtype=jnp.int32) * MOE_BLOCK
    blk_exp = jnp.minimum(jnp.sum(blk_first[:, None] >= ends[None, :], axis=1), N_EXPERTS - 1).astype(jnp.int32)
    nvalid = (ends[-1:] // MOE_BLOCK).astype(jnp.int32)
    dest = _dest(route_t, pad_start.astype(_F32)).T.reshape(n // TOK_TM, 1, TOP_K * TOK_TM)
    seg = jnp.concatenate([pad_start, padded, counts, nvalid]).astype(jnp.int32)
    xs = _dispatch(h1_tiles, dest, seg, cap, n)
    ys = _experts(xs, blk_exp, nvalid, w_gate, w_up, w_down)
    return _combine(ys, dest, h1_tiles, route, g, b)


def _router_params(rg_w, rg_b, re_w, re_b):
    d = rg_w.shape[0]
    pad = LANES - N_GROUPS - N_EXPERTS
    rw = jnp.concatenate([rg_w, re_w, jnp.zeros((d, pad), _F32)], axis=1)
    rb = jnp.concatenate([rg_b, re_b.reshape(-1), jnp.zeros((pad,), _F32)]).reshape(1, LANES)
    return rw, rb


def kernel(x, t5_table, l0_w_qkv, l0_sink, l0_w_o, l0_ln1_g, l0_ln1_b, l0_rg_w, l0_rg_b, l0_re_w, l0_re_b, l0_w_gate, l0_w_up, l0_w_down, l0_ln2_g, l0_ln2_b, l1_w_qkv, l1_w_o, l1_ln1_g, l1_ln1_b, l1_rg_w, l1_rg_b, l1_re_w, l1_re_b, l1_w_gate, l1_w_up, l1_w_down, l1_ln2_g, l1_ln2_b, l2_w_qkv, l2_rpb, l2_w_o, l2_ln1_g, l2_ln1_b, l2_rg_w, l2_rg_b, l2_re_w, l2_re_b, l2_w_gate, l2_w_up, l2_w_down, l2_ln2_g, l2_ln2_b, l3_w_qkv, l3_sink, l3_w_o, l3_ln1_g, l3_ln1_b, l3_rg_w, l3_rg_b, l3_re_w, l3_re_b, l3_w_gate, l3_w_up, l3_w_down, l3_ln2_g, l3_ln2_b):
    mixers = [(l0_w_qkv, l0_sink, l0_w_o), (l1_w_qkv, None, l1_w_o), (l2_w_qkv, l2_rpb, l2_w_o),
              (l3_w_qkv, l3_sink, l3_w_o)]
    norm1 = [(l0_ln1_g, l0_ln1_b), (l1_ln1_g, l1_ln1_b), (l2_ln1_g, l2_ln1_b), (l3_ln1_g, l3_ln1_b)]
    routers = [(l0_rg_w, l0_rg_b, l0_re_w, l0_re_b), (l1_rg_w, l1_rg_b, l1_re_w, l1_re_b),
               (l2_rg_w, l2_rg_b, l2_re_w, l2_re_b), (l3_rg_w, l3_rg_b, l3_re_w, l3_re_b)]
    experts = [(l0_w_gate, l0_w_up, l0_w_down), (l1_w_gate, l1_w_up, l1_w_down),
               (l2_w_gate, l2_w_up, l2_w_down), (l3_w_gate, l3_w_up, l3_w_down)]
    norm2 = [(l0_ln2_g, l0_ln2_b), (l1_ln2_g, l1_ln2_b), (l2_ln2_g, l2_ln2_b), (l3_ln2_g, l3_ln2_b)]

    bsz, seq, d = x.shape
    n = bsz * seq
    inner = N_HEADS * HEAD_DIM
    scale = HEAD_DIM ** -0.5 * LOG2E
    h = x.reshape(n, d)
    bias_a = _band_bias_t(t5_table, A_BLOCK, A_WINDOW, 1, A_BAND)
    for i in range(DEPTH):
        kind = i % N_MIXERS
        w_qkv, extra, w_o = mixers[i]
        kvw = A_KV_HEADS * HEAD_DIM if kind == 0 else inner
        lses, unperm = (), None
        if kind == 1:
            perms = np.stack([_residue_perm(dil) for _, dil in B_BRANCHES if dil > 1])
            outs = _project(h, w_qkv.astype(_BF16), (inner, kvw, kvw), (scale, 1.0, 1.0), jnp.asarray(perms, _BF16))
            unperm = jnp.asarray(perms.transpose(0, 2, 1), _BF16)
            attn, lses = [], []
            for bi, (window, dil) in enumerate(B_BRANCHES):
                q, k, v = outs[3 * bi:3 * bi + 3]
                o, lse = _mixer_b_branch(q, k, v, t5_table, bsz, seq, window, dil)
                attn.append(o)
                lses.append(jnp.pad(lse.reshape(n, N_HEADS), ((0, 0), (0, LANES - N_HEADS))))
        else:
            q, k, v = _project(h, w_qkv.astype(_BF16), (inner, kvw, kvw), (scale, 1.0, 1.0))
            q, k, v = (t.reshape(bsz, seq, t.shape[1]) for t in (q, k, v))
            if kind == 0:
                attn = [_mixer_a(q, k, v, extra.astype(_F32) * LOG2E, bias_a, bsz, seq)]
            else:
                attn = [_mixer_c(q, k, v, _na_bias_t(extra, seq // GRID_W), bsz, seq)]
        attn = [a.reshape(n, inner) for a in attn]
        g1, b1 = norm1[i]
        rw, rb = _router_params(*routers[i])
        h1, route, route_t, counts = _post_attn(attn, lses, h, w_o.astype(_BF16), g1.reshape(1, d), b1.reshape(1, d),
                                                rw, rb, unperm)
        g2, b2 = norm2[i]
        h = _moe(h1, route, route_t, counts, *experts[i], g2.reshape(1, d), b2.reshape(1, d))
    return h.reshape(bsz, seq, d)
```

```python
import functools
import math

import jax
import jax.numpy as jnp
import numpy as np
from jax import lax
from jax.experimental import pallas as pl
from jax.experimental.pallas import tpu as pltpu

HEAD_DIM = 64
N_HEADS = 16
A_KV_HEADS = 4
A_WINDOW = 128
A_BLOCK = 128
B_BRANCHES = ((128, 1), (512, 4), (2048, 16))
GRID_W = 64
NA_ROWS = 8
NA_COLS = 16
T5_BUCKETS = 32
T5_MAX_DIST = 1024
N_GROUPS = 4
EXPERTS_PER_GROUP = 8
N_EXPERTS = N_GROUPS * EXPERTS_PER_GROUP
TOP_K = 2
DEPTH = 4
N_MIXERS = 3
ALPHA = (2 * DEPTH) ** 0.25
LN_EPS = 1e-5
NEG_INF = -1e30

LANES = 128
PAIR = 2 * HEAD_DIM
BF16_SUBLANES = 16
VT_ROWS = PAIR + BF16_SUBLANES
LOG2E = math.log2(math.e)
LN2 = math.log(2.0)
VMEM_LIMIT = 56 * 1024 * 1024

PROJ_TM = 512
PROJ_CHUNK = 512
POST_TM = 1024
POST_SUB = 256
TOK_TM = 1024
DEST_TM = 8192
ROUTE_ROWS = 8
MOE_BLOCK = 512
A_TQ = 2048
A_BAND = 3
A_SUB_PER_GROUP = 2
B_BLOCK = 128
RESIDUE_RUN = BF16_SUBLANES
PERM_CHUNK = 256
TRANSPOSE_UNROLL = 8
UNITS_PER_GROUP = 8
C_UNIT_ROWS = 2
C_BAND_ROWS = 10
C_UNITS_PER_GROUP = 4
C_ROWS_PER_STEP = 64
ROUTE_LANE0 = N_GROUPS

_F32 = jnp.float32
_BF16 = jnp.bfloat16


def _params(sem, vmem=VMEM_LIMIT, flags=None):
    return pltpu.CompilerParams(dimension_semantics=sem, vmem_limit_bytes=vmem, flags=flags)


def _proj_kernel(x_ref, w_ref, *refs, widths, scales, nperm):
    perm_ref = refs[0] if nperm else None
    o_refs = refs[1:] if nperm else refs
    nout = len(widths)
    xb = x_ref[...].astype(_BF16)
    col = 0
    for t, (width, scale) in enumerate(zip(widths, scales)):
        for c in range(0, width, PROJ_CHUNK):
            cw = min(PROJ_CHUNK, width - c)
            acc = jnp.dot(xb, w_ref[:, col + c:col + c + cw], preferred_element_type=_F32)
            if scale != 1.0:
                acc = acc * scale
            ob = acc.astype(_BF16)
            o_refs[t][:, c:c + cw] = ob
            for p in range(nperm):
                for r0 in range(0, ob.shape[0], PERM_CHUNK):
                    moved = jnp.dot(perm_ref[p], ob[r0:r0 + PERM_CHUNK], preferred_element_type=_F32)
                    o_refs[(p + 1) * nout + t][r0:r0 + PERM_CHUNK, c:c + cw] = moved.astype(_BF16)
        col += width


def _project(x2, w_bf16, widths, scales, perms=None):
    n, d = x2.shape
    ctot = sum(widths)
    nperm = 0 if perms is None else perms.shape[0]
    args = [x2, w_bf16] + ([perms] if nperm else [])
    specs = [pl.BlockSpec((PROJ_TM, d), lambda i: (i, 0)), pl.BlockSpec((d, ctot), lambda i: (0, 0))]
    if nperm:
        specs.append(pl.BlockSpec(perms.shape, lambda i: (0, 0, 0)))
    return pl.pallas_call(
        functools.partial(_proj_kernel, widths=tuple(widths), scales=tuple(scales), nperm=nperm),
        out_shape=[jax.ShapeDtypeStruct((n, w), _BF16) for w in widths] * (nperm + 1),
        grid=(n // PROJ_TM,),
        in_specs=specs,
        out_specs=[pl.BlockSpec((PROJ_TM, w), lambda i: (i, 0)) for w in widths] * (nperm + 1),
        compiler_params=_params(("arbitrary",)),
        name="qkv_proj",
    )(*args)


def _residue_perm(dil):
    span = dil * RESIDUE_RUN
    p = np.zeros((PERM_CHUNK, PERM_CHUNK), np.float32)
    for base in range(0, PERM_CHUNK, span):
        for r in range(dil):
            for u in range(RESIDUE_RUN):
                p[base + r * RESIDUE_RUN + u, base + u * dil + r] = 1.0
    return p


def _one():
    return jnp.ones((1,), jnp.int32)


def _lane_lo(rows=1):
    return lax.broadcasted_iota(jnp.int32, (rows, PAIR), 1) < HEAD_DIM


def _swap_halves(x):
    return pltpu.roll(x.astype(_F32), HEAD_DIM, axis=1).astype(_BF16)


class _PairUnit:
    def __init__(self, q2, k, vt, bias, emit, sink_a=None, sink_b=None):
        self.q2, self.k, self.vt, self.bias, self.emit = q2, k, vt, bias, emit
        self.sink_a, self.sink_b = sink_a, sink_b


def _attend_groups(ngroups, units_of, scratch, one):
    assert ngroups % 2 == 0
    lo = _lane_lo()
    nt = (((1,), (1,)), ((), ()))

    def sink_row(unit, m2):
        lane = lax.broadcasted_iota(jnp.int32, (1, m2), 1)
        return jnp.where(lane < m2 // 2, unit.sink_a, unit.sink_b)

    def scores(g, parity):
        s_scr, m_scr = scratch[parity]
        for u, unit in enumerate(units_of(g, parity)):
            q2 = unit.q2()
            zero = jnp.zeros_like(q2)
            qq = jnp.concatenate([jnp.where(lo, q2, zero), jnp.where(lo, zero, q2)], axis=0)
            s = lax.dot_general(unit.k(), qq, nt, preferred_element_type=_F32) + unit.bias()
            m = jnp.max(s, axis=0, keepdims=True)
            if unit.sink_a is not None:
                m = jnp.maximum(m, sink_row(unit, s.shape[1]))
            s_scr[u] = s
            m_scr[u] = m

    def values(g, parity):
        s_scr, m_scr = scratch[parity]
        for u, unit in enumerate(units_of(g, parity)):
            m = m_scr[u]
            e = jnp.exp2(s_scr[u] - m)
            ot = jnp.dot(unit.vt(), e.astype(_BF16), preferred_element_type=_F32)
            den = ot[PAIR:PAIR + 1, :]
            if unit.sink_a is not None:
                den = den + jnp.exp2(sink_row(unit, e.shape[1]) - m)
            ot = ot[:PAIR] * (1.0 / den)
            mq = ot.shape[1] // 2
            top = lax.broadcasted_iota(jnp.int32, (PAIR, 1), 0) < HEAD_DIM
            pair_t = jnp.where(top, ot[:, :mq], ot[:, mq:])
            unit.emit(pair_t.T, (m + jnp.log2(den)) * LN2)

    def region(*work):
        def body(_, carry):
            for fn, g, parity in work:
                fn(g, parity)
            return carry
        lax.fori_loop(0, one, body, 0)

    region((scores, 0, 0))

    def steady(i, carry):
        g = 2 * i
        region((values, g, 0), (scores, g + 1, 1))
        region((values, g + 1, 1), (scores, g + 2, 0))
        return carry

    lax.fori_loop(0, ngroups // 2 - 1, steady, 0)
    region((values, ngroups - 2, 0), (scores, ngroups - 1, 1))
    region((values, ngroups - 1, 1))


def _transpose_blocks(v_ref_rows, nblk, store):
    def body(j, carry):
        rows = pl.ds(pl.multiple_of(j * LANES, LANES), LANES)
        vt = v_ref_rows(rows).T
        store(j, jnp.concatenate([vt, jnp.ones((VT_ROWS - PAIR, LANES), _BF16)], axis=0))
        return carry
    lax.fori_loop(0, nblk, body, 0, unroll=TRANSPOSE_UNROLL)


def _t5_bucket(rel):
    half = T5_BUCKETS // 2
    exact = half // 2
    n = np.abs(rel)
    big = exact + (np.log(np.maximum(n, 1) / exact) / math.log(T5_MAX_DIST / exact) * (half - exact)).astype(np.int32)
    return ((rel > 0) * half + np.where(n < exact, n, np.minimum(big, half - 1))).astype(np.int32)


def _band_bias_t(t5_table, blk, window, dil, nband):
    tk = nband * blk
    off = tk - 1
    rel = np.arange(-off, tk)
    vec = jnp.take(t5_table, jnp.asarray(_t5_bucket(rel * dil)), axis=0).astype(_F32)
    vec = jnp.where(jnp.asarray(np.abs(rel) <= window)[:, None], vec * LOG2E, NEG_INF).T
    h = vec.shape[0]
    vec = jnp.pad(vec, ((0, 0), (0, 1)))
    w = 2 * tk - 1
    skew = jnp.tile(vec, (1, blk))[:, :blk * w].reshape(h, blk, w)
    t = jnp.stack([skew[:, :, off - v * blk:off - v * blk + tk] for v in range(nband)], axis=1)
    t = t.reshape(h // 2, 2, nband, blk, tk).transpose(0, 2, 4, 1, 3)
    return t.reshape(h // 2, nband, tk, 2 * blk)


def _band_window(gb, blk, length, nband):
    first = jnp.clip(gb - 1, 0, length // blk - nband)
    return first, gb - first


def _attn_a_kernel(sink_ref, one_ref, q_ref, k_ref, v_ref, bias_ref, o_ref, kd_ref, vdt_ref,
                   s0_scr, m0_scr, s1_scr, m1_scr, *, seq):
    kvb = pl.program_id(0)
    qt = pl.program_id(2)
    one = one_ref[0]
    nblk = seq // A_BLOCK

    @pl.when(qt == 0)
    def _():
        lo = _lane_lo()

        def fill(j, carry):
            rows = pl.ds(pl.multiple_of(j * A_BLOCK, A_BLOCK), A_BLOCK)
            kk = k_ref[0, rows, :]
            ks = _swap_halves(kk)
            kd_ref[0, rows, :] = jnp.where(lo, kk, ks)
            kd_ref[1, rows, :] = jnp.where(lo, ks, kk)
            vt = v_ref[0, rows, :].T
            ones = jnp.ones((VT_ROWS - PAIR, A_BLOCK), _BF16)
            vdt_ref[0, j] = jnp.concatenate([vt[:HEAD_DIM], vt[:HEAD_DIM], ones], axis=0)
            vdt_ref[1, j] = jnp.concatenate([vt[HEAD_DIM:], vt[HEAD_DIM:], ones], axis=0)
            return carry

        lax.fori_loop(0, nblk, fill, 0, unroll=TRANSPOSE_UNROLL)

    nsub = A_TQ // A_BLOCK
    npair = q_ref.shape[2] // PAIR
    hbase = kvb * (2 * npair)

    def units_of(g, parity):
        units = []
        for j in range(A_SUB_PER_GROUP):
            si = g * A_SUB_PER_GROUP + j
            first, var = _band_window(qt * nsub + si, A_BLOCK, seq, A_BAND)
            rows = pl.ds(pl.multiple_of(si * A_BLOCK, A_BLOCK), A_BLOCK)
            band = pl.ds(pl.multiple_of(first * A_BLOCK, A_BLOCK), A_BAND * A_BLOCK)
            for jj in range(npair):
                hk = jj // (npair // 2)
                cols = slice(jj * PAIR, (jj + 1) * PAIR)

                def emit(o, lse, rows=rows, cols=cols):
                    o_ref[0, rows, cols] = o.astype(o_ref.dtype)

                units.append(_PairUnit(
                    q2=lambda rows=rows, cols=cols: q_ref[0, rows, cols],
                    k=lambda hk=hk, band=band: kd_ref[hk, band, :],
                    vt=lambda hk=hk, first=first: jnp.concatenate(
                        [vdt_ref[hk, first + t] for t in range(A_BAND)], axis=1),
                    bias=lambda jj=jj, var=var: bias_ref[jj, var],
                    emit=emit, sink_a=sink_ref[hbase + 2 * jj], sink_b=sink_ref[hbase + 2 * jj + 1]))
        return units

    _attend_groups(nsub // A_SUB_PER_GROUP, units_of, ((s0_scr, m0_scr), (s1_scr, m1_scr)), one)


def _mixer_a(q, k, v, sink, bias, bsz, seq):
    nkvb = k.shape[2] // PAIR
    qw = q.shape[2] // nkvb
    npair = qw // PAIR
    tk = A_BAND * A_BLOCK
    nunits = A_SUB_PER_GROUP * npair
    group_scratch = [pltpu.VMEM((nunits, tk, 2 * A_BLOCK), _F32), pltpu.VMEM((nunits, 1, 2 * A_BLOCK), _F32)]
    return pl.pallas_call(
        functools.partial(_attn_a_kernel, seq=seq),
        out_shape=jax.ShapeDtypeStruct(q.shape, _BF16),
        grid=(nkvb, bsz, seq // A_TQ),
        in_specs=[pl.BlockSpec(memory_space=pltpu.SMEM), pl.BlockSpec(memory_space=pltpu.SMEM),
                  pl.BlockSpec((1, A_TQ, qw), lambda c, b, t: (b, t, c)),
                  pl.BlockSpec((1, seq, PAIR), lambda c, b, t: (b, 0, c)),
                  pl.BlockSpec((1, seq, PAIR), lambda c, b, t: (b, 0, c)),
                  pl.BlockSpec((npair, A_BAND, tk, 2 * A_BLOCK), lambda c, b, t: (c, 0, 0, 0))],
        out_specs=pl.BlockSpec((1, A_TQ, qw), lambda c, b, t: (b, t, c)),
        scratch_shapes=[pltpu.VMEM((2, seq, PAIR), _BF16), pltpu.VMEM((2, seq // A_BLOCK, VT_ROWS, A_BLOCK), _BF16)]
        + group_scratch + group_scratch,
        compiler_params=_params(("arbitrary", "arbitrary", "arbitrary")),
        name="mixer_a",
    )(sink, _one(), q, k, v, bias)


def _attn_band_kernel(one_ref, q_ref, k_ref, v_ref, bias_ref, o_ref, lse_ref, vt_ref,
                      s0_scr, m0_scr, s1_scr, m1_scr, *, length, npairs, nband, chunked):
    col = pl.program_id(2)
    one = one_ref[0]
    nblk = length // B_BLOCK
    runs = B_BLOCK // RESIDUE_RUN

    def load(ref, blk0, nblocks, cols):
        if chunked:
            r0 = pl.multiple_of(blk0 * runs, runs)
            return ref[pl.ds(r0, nblocks * runs), :, cols].reshape(nblocks * B_BLOCK, PAIR)
        return ref[0, pl.ds(pl.multiple_of(blk0 * B_BLOCK, B_BLOCK), nblocks * B_BLOCK), cols]

    def store_out(blk0, cols, val):
        if chunked:
            o_ref[pl.ds(pl.multiple_of(blk0 * runs, runs), runs), :, cols] = val.reshape(runs, RESIDUE_RUN, PAIR)
        else:
            o_ref[0, pl.ds(pl.multiple_of(blk0 * B_BLOCK, B_BLOCK), B_BLOCK), cols] = val

    for p in range(npairs):
        pcols = slice(p * PAIR, (p + 1) * PAIR)

        def fill(j, carry, p=p, pcols=pcols):
            vt = load(v_ref, j, 1, pcols).T
            vt_ref[p, j] = jnp.concatenate([vt, jnp.ones((VT_ROWS - PAIR, B_BLOCK), _BF16)], axis=0)
            return carry

        lax.fori_loop(0, nblk, fill, 0, unroll=min(TRANSPOSE_UNROLL, nblk))

    def unit(gb, p):
        first, var = _band_window(gb, B_BLOCK, length, nband)
        cols = slice(p * PAIR, (p + 1) * PAIR)

        def emit(o, lse):
            store_out(gb, cols, o.astype(o_ref.dtype))
            ha = col * (2 * npairs) + 2 * p
            lse_ref[0, 0, gb, pl.ds(ha, 1), :] = lse[:, :B_BLOCK]
            lse_ref[0, 0, gb, pl.ds(ha + 1, 1), :] = lse[:, B_BLOCK:]

        return _PairUnit(
            q2=lambda: load(q_ref, gb, 1, cols), k=lambda: load(k_ref, first, nband, cols),
            vt=lambda: jnp.concatenate([vt_ref[p, first + t] for t in range(nband)], axis=1),
            bias=lambda: bias_ref[p, var], emit=emit)

    if npairs == 1:
        ngroups = nblk // UNITS_PER_GROUP
        units_of = lambda g, parity: [unit(g * UNITS_PER_GROUP + j, 0) for j in range(UNITS_PER_GROUP)]
    else:
        assert npairs == UNITS_PER_GROUP
        ngroups = nblk
        units_of = lambda g, parity: [unit(g, p) for p in range(npairs)]
    _attend_groups(ngroups, units_of, ((s0_scr, m0_scr), (s1_scr, m1_scr)), one)


def _mixer_b_branch(q, k, v, t5_table, bsz, seq, window, dil):
    length = seq // dil
    width = q.shape[1]
    nblk = length // B_BLOCK
    nband = min(3, nblk)
    tk = nband * B_BLOCK
    npairs = (width // PAIR) if length * width * 2 <= 2 * 1024 * 1024 else 1
    ncol = width // (npairs * PAIR)
    bw = npairs * PAIR
    bias = _band_bias_t(t5_table, B_BLOCK, window // 2 // dil, dil, nband)
    chunked = dil > 1
    if chunked:
        nrun = length // RESIDUE_RUN
        shape = (bsz * nrun, dil, RESIDUE_RUN, width)
        spec = pl.BlockSpec((nrun, None, RESIDUE_RUN, bw), lambda b, r, c: (b, r, 0, c))
    else:
        shape = (bsz, seq, width)
        spec = pl.BlockSpec((1, seq, bw), lambda b, r, c: (b, 0, c))

    o, lse = pl.pallas_call(
        functools.partial(_attn_band_kernel, length=length, npairs=npairs, nband=nband, chunked=chunked),
        out_shape=[jax.ShapeDtypeStruct(shape, _BF16),
                   jax.ShapeDtypeStruct((bsz, dil, nblk, N_HEADS, B_BLOCK), _F32)],
        grid=(bsz, dil, ncol),
        in_specs=[pl.BlockSpec(memory_space=pltpu.SMEM), spec, spec, spec,
                  pl.BlockSpec((npairs, nband, tk, 2 * B_BLOCK), lambda b, r, c: (c, 0, 0, 0))],
        out_specs=[spec, pl.BlockSpec((1, 1, nblk, N_HEADS, B_BLOCK), lambda b, r, c: (b, r, 0, 0, 0))],
        scratch_shapes=[pltpu.VMEM((npairs, nblk, VT_ROWS, B_BLOCK), _BF16)] + 2 * [
            pltpu.VMEM((UNITS_PER_GROUP, tk, 2 * B_BLOCK), _F32), pltpu.VMEM((UNITS_PER_GROUP, 1, 2 * B_BLOCK), _F32)],
        compiler_params=_params(("arbitrary", "arbitrary", "arbitrary")),
        name=f"mixer_b_dil{dil}",
    )(_one(), q.reshape(shape), k.reshape(shape), v.reshape(shape), bias)
    lse = lse.transpose(0, 2, 4, 1, 3).reshape(bsz, seq, N_HEADS)
    return o.reshape(bsz * seq, width), lse


def _na_bias_t(rpb, rows):
    canon = (0, 2, 4, rows - 4, rows - 2)
    h = rpb.shape[0]
    kc = np.arange(GRID_W)[:, None]
    qc = np.arange(GRID_W)[None, :]
    ws = np.clip(qc - NA_COLS // 2, 0, GRID_W - NA_COLS)
    col_ok = (kc >= ws) & (kc < ws + NA_COLS)
    col_idx = np.clip(kc - qc, 1 - NA_COLS, NA_COLS - 1) + NA_COLS - 1
    colpart = jnp.where(jnp.asarray(col_ok), rpb.astype(_F32)[:, :, jnp.asarray(col_idx)] * LOG2E, NEG_INF)
    masked = jnp.full((h, GRID_W, GRID_W), NEG_INF, _F32)
    variants = []
    for i in canon:
        start = int(np.clip(i - NA_ROWS // 2, 0, rows - C_BAND_ROWS))
        per_key_row = []
        for kr in range(C_BAND_ROWS):
            per_query_row = []
            for qr in range(C_UNIT_ROWS):
                iq, kabs = i + qr, start + kr
                first_key_row = int(np.clip(iq - NA_ROWS // 2, 0, rows - NA_ROWS))
                inside = first_key_row <= kabs < first_key_row + NA_ROWS
                per_query_row.append(colpart[:, kabs - iq + NA_ROWS - 1] if inside else masked)
            per_key_row.append(jnp.stack(per_query_row, axis=2))
        variants.append(jnp.stack(per_key_row, axis=1))
    b = jnp.stack(variants, axis=1)
    nq, nk = C_UNIT_ROWS * GRID_W, C_BAND_ROWS * GRID_W
    b = b.reshape(h // 2, 2, len(canon), nk, nq)
    return b.transpose(0, 2, 3, 1, 4).reshape(h // 2, len(canon), nk, 2 * nq)


def _attn_c_kernel(one_ref, q_ref, k_ref, v_ref, bias_ref, o_ref, vt_ref, s0_scr, m0_scr, s1_scr, m1_scr, *, rows):
    rt = pl.program_id(2)
    nq = C_UNIT_ROWS * GRID_W
    band_blocks = C_BAND_ROWS * GRID_W // LANES

    @pl.when(rt == 0)
    def _():
        def store(j, blk):
            vt_ref[j] = blk
        _transpose_blocks(lambda r: v_ref[0, r, :], rows * GRID_W // LANES, store)

    def unit(u):
        i = rt * C_ROWS_PER_STEP + C_UNIT_ROWS * u
        start = jnp.clip(i - NA_ROWS // 2, 0, rows - C_BAND_ROWS)
        var = lax.shift_right_logical(i - start, 1)
        first = lax.shift_right_logical(start, 1)
        keys = pl.ds(pl.multiple_of(first * LANES, LANES), C_BAND_ROWS * GRID_W)
        qrows = pl.ds(pl.multiple_of(u * nq, nq), nq)

        def emit(o, lse):
            o_ref[0, qrows, :] = o.astype(o_ref.dtype)

        return _PairUnit(
            q2=lambda: q_ref[0, qrows, :], k=lambda: k_ref[0, keys, :],
            vt=lambda: jnp.concatenate([vt_ref[first + t] for t in range(band_blocks)], axis=1),
            bias=lambda: bias_ref[0, var], emit=emit)

    ngroups = C_ROWS_PER_STEP // C_UNIT_ROWS // C_UNITS_PER_GROUP
    units_of = lambda g, parity: [unit(g * C_UNITS_PER_GROUP + j) for j in range(C_UNITS_PER_GROUP)]
    _attend_groups(ngroups, units_of, ((s0_scr, m0_scr), (s1_scr, m1_scr)), one_ref[0])


def _mixer_c(q, k, v, bias, bsz, seq):
    rows = seq // GRID_W
    tq = C_ROWS_PER_STEP * GRID_W
    npair = q.shape[2] // PAIR
    nq, nk = C_UNIT_ROWS * GRID_W, C_BAND_ROWS * GRID_W
    return pl.pallas_call(
        functools.partial(_attn_c_kernel, rows=rows),
        out_shape=jax.ShapeDtypeStruct(q.shape, _BF16),
        grid=(npair, bsz, seq // tq),
        in_specs=[pl.BlockSpec(memory_space=pltpu.SMEM),
                  pl.BlockSpec((1, tq, PAIR), lambda c, b, t: (b, t, c)),
                  pl.BlockSpec((1, seq, PAIR), lambda c, b, t: (b, 0, c)),
                  pl.BlockSpec((1, seq, PAIR), lambda c, b, t: (b, 0, c)),
                  pl.BlockSpec((1, bias.shape[1], nk, 2 * nq), lambda c, b, t: (c, 0, 0, 0))],
        out_specs=pl.BlockSpec((1, tq, PAIR), lambda c, b, t: (b, t, c)),
        scratch_shapes=[pltpu.VMEM((seq // LANES, VT_ROWS, LANES), _BF16)] + 2 * [
            pltpu.VMEM((C_UNITS_PER_GROUP, nk, 2 * nq), _F32), pltpu.VMEM((C_UNITS_PER_GROUP, 1, 2 * nq), _F32)],
        compiler_params=_params(("arbitrary", "arbitrary", "arbitrary")),
        name="mixer_c",
    )(_one(), q, k, v, bias)


def _layer_norm(z, g, b):
    mu = jnp.mean(z, axis=-1, keepdims=True)
    zc = z - mu
    var = jnp.mean(zc * zc, axis=-1, keepdims=True)
    return zc * lax.rsqrt(var + LN_EPS) * g + b


def _rows_to_tiles(ref, x):
    nrows, d = x.shape
    nsl = d // LANES
    for s in range(nsl):
        ref[pl.ds(s, nrows, stride=nsl), :] = x[:, s * LANES:(s + 1) * LANES]


def _rows_from_tiles(ref, nrows, d):
    nsl = d // LANES
    return jnp.concatenate([ref[pl.ds(s, nrows, stride=nsl), :] for s in range(nsl)], axis=1)


def _split_bf16(x):
    hi = x.astype(_BF16)
    return hi, (x - hi.astype(_F32)).astype(_BF16)


def _router_logits(hn, rw_ref, rb_ref):
    h_hi, h_lo = _split_bf16(hn)
    w_hi, w_lo = _split_bf16(rw_ref[...])
    both = jnp.dot(h_hi, jnp.concatenate([w_hi, w_lo], axis=1), preferred_element_type=_F32)
    return (both[:, :LANES]
            + (both[:, LANES:] + jnp.dot(h_lo, w_hi, preferred_element_type=_F32))) + rb_ref[...]


def _route(logits, tri_ref, carry_ref):
    tm = logits.shape[0]
    lane = lax.broadcasted_iota(jnp.int32, (tm, LANES), 1).astype(_F32)
    ninf = -jnp.inf
    is_grp = lane < N_GROUPS
    lg = jnp.where(is_grp, logits, ninf)
    gmax = jnp.max(lg, axis=-1, keepdims=True)
    grp = jnp.min(jnp.where(lg == gmax, lane, float(LANES)), axis=-1, keepdims=True)
    p_grp = 1.0 / jnp.sum(jnp.where(is_grp, jnp.exp(lg - gmax), 0.0), axis=-1, keepdims=True)
    lo_lane = ROUTE_LANE0 + EXPERTS_PER_GROUP * grp
    in_grp = (lane >= lo_lane) & (lane < lo_lane + EXPERTS_PER_GROUP)
    le = jnp.where(in_grp, logits, ninf)
    v1 = jnp.max(le, axis=-1, keepdims=True)
    i1 = jnp.min(jnp.where(le == v1, lane, float(LANES)), axis=-1, keepdims=True)
    le2 = jnp.where(lane == i1, ninf, le)
    v2 = jnp.max(le2, axis=-1, keepdims=True)
    i2 = jnp.min(jnp.where(le2 == v2, lane, float(LANES)), axis=-1, keepdims=True)
    t = jnp.exp(v2 - v1)
    s1 = 1.0 / (1.0 + t)
    g1 = p_grp * s1
    g2 = p_grp * (t * s1)
    sel1 = lane == i1
    sel2 = lane == i2
    onehot = jnp.where(sel1 | sel2, 1.0, 0.0)
    sub = tri_ref.shape[0]
    carry = carry_ref[...]
    parts = []
    for r0 in range(0, tm, sub):
        part = onehot[r0:r0 + sub]
        parts.append(jnp.dot(tri_ref[...], part.astype(_BF16), preferred_element_type=_F32) + carry)
        carry = carry + jnp.sum(part, axis=0, keepdims=True)
    carry_ref[...] = carry
    before = jnp.concatenate(parts, axis=0) if len(parts) > 1 else parts[0]
    rank1 = jnp.sum(jnp.where(sel1, before, 0.0), axis=-1, keepdims=True)
    rank2 = jnp.sum(jnp.where(sel2, before, 0.0), axis=-1, keepdims=True)
    out = jnp.zeros((tm, LANES), _F32)
    for idx, val in enumerate((i1 - ROUTE_LANE0, i2 - ROUTE_LANE0, g1, g2, rank1, rank2)):
        out = jnp.where(lane == idx, val, out)
    return out


def _post_attn_kernel(*refs, n_branch):
    attn_refs = refs[:n_branch]
    pos = n_branch
    if n_branch > 1:
        lse_refs = refs[pos:pos + n_branch]
        expand_ref, unperm_ref = refs[pos + n_branch:pos + n_branch + 2]
        pos += n_branch + 2
    wo_ref, h_ref, g_ref, b_ref, rw_ref, rb_ref, tri_ref = refs[pos:pos + 7]
    h1_ref, route_ref, routet_ref, counts_ref, carry_ref = refs[pos + 7:pos + 12]

    @pl.when(pl.program_id(0) == 0)
    def _():
        carry_ref[...] = jnp.zeros_like(carry_ref)

    d = h_ref.shape[1]
    logits = []
    for sub in range(h_ref.shape[0] // POST_SUB):
        rows = slice(sub * POST_SUB, (sub + 1) * POST_SUB)
        if n_branch == 1:
            a = attn_refs[0][rows, :]
        else:
            lses = [r[rows, :] for r in lse_refs]
            mx = functools.reduce(jnp.maximum, lses)
            es = [jnp.exp(l - mx) for l in lses]
            inv = 1.0 / functools.reduce(jnp.add, es)
            mixed = None
            for bi, (e, a_ref) in enumerate(zip(es, attn_refs)):
                w_hi, w_lo = _split_bf16(e * inv)
                w_full = jnp.dot(jnp.concatenate([w_hi, w_lo], axis=1), expand_ref[...],
                                 preferred_element_type=_F32)
                if bi == 0:
                    a_nat = a_ref[rows, :].astype(_F32)
                else:
                    a_nat = jnp.dot(unperm_ref[bi - 1], a_ref[rows, :], preferred_element_type=_F32)
                term = w_full * a_nat
                mixed = term if mixed is None else mixed + term
            a = mixed.astype(_BF16)
        y = jnp.dot(a, wo_ref[...], preferred_element_type=_F32)
        hn = _layer_norm(ALPHA * h_ref[rows, :] + y, g_ref[...], b_ref[...])
        nsl = d // LANES
        _rows_to_tiles(h1_ref.at[pl.ds(sub * POST_SUB * nsl, POST_SUB * nsl)], hn)
        logits.append(_router_logits(hn, rw_ref, rb_ref))
    route = _route(jnp.concatenate(logits, axis=0), tri_ref, carry_ref)
    route_ref[...] = route
    routet_ref[...] = route.T[:ROUTE_ROWS]
    counts_ref[...] = carry_ref[...]


def _post_attn(attn, lses, h2, w_o, g, b, rw, rb, unperm=None):
    n, d = h2.shape
    tm = POST_TM
    nb = len(attn)
    tri = jnp.asarray(np.tril(np.ones((POST_SUB, POST_SUB), np.float32), -1), _BF16)
    row = lambda i: (i, 0)
    const = lambda i: (0, 0)
    args = list(attn)
    specs = [pl.BlockSpec((tm, d), row) for _ in attn]
    if nb > 1:
        expand = np.zeros((2, LANES, d), np.float32)
        for hd in range(N_HEADS):
            expand[:, hd, hd * HEAD_DIM:(hd + 1) * HEAD_DIM] = 1.0
        args += list(lses) + [jnp.asarray(expand.reshape(2 * LANES, d), _BF16), unperm]
        specs += [pl.BlockSpec((tm, LANES), row) for _ in lses] + [
            pl.BlockSpec((2 * LANES, d), const), pl.BlockSpec(unperm.shape, lambda i: (0, 0, 0))]
    args += [w_o, h2, g, b, rw, rb, tri]
    specs += [pl.BlockSpec((d, d), const), pl.BlockSpec((tm, d), row), pl.BlockSpec((1, d), const),
              pl.BlockSpec((1, d), const), pl.BlockSpec((d, LANES), const), pl.BlockSpec((1, LANES), const),
              pl.BlockSpec((POST_SUB, POST_SUB), const)]
    return pl.pallas_call(
        functools.partial(_post_attn_kernel, n_branch=nb),
        out_shape=[jax.ShapeDtypeStruct((n * d // LANES, LANES), _F32), jax.ShapeDtypeStruct((n, LANES), _F32),
                   jax.ShapeDtypeStruct((ROUTE_ROWS, n), _F32),
                   jax.ShapeDtypeStruct((1, LANES), _F32)],
        grid=(n // tm,),
        in_specs=specs,
        out_specs=[pl.BlockSpec((tm * d // LANES, LANES), row), pl.BlockSpec((tm, LANES), row),
                   pl.BlockSpec((ROUTE_ROWS, tm), lambda i: (0, i)), pl.BlockSpec((1, LANES), const)],
        scratch_shapes=[pltpu.VMEM((1, LANES), _F32)],
        compiler_params=_params(("arbitrary",)),
        name="post_attn_router",
    )(*args)


def _dest_kernel(ps_ref, rt_ref, o_ref):
    expert = rt_ref[0:TOP_K, :]
    slot = rt_ref[4:4 + TOP_K, :]
    for e in range(N_EXPERTS):
        slot = slot + jnp.where(expert == float(e), ps_ref[e], 0.0)
    o_ref[...] = slot.astype(jnp.int32)


def _dest(route_t, pad_start):
    n = route_t.shape[1]
    tm = min(DEST_TM, n)
    return pl.pallas_call(
        _dest_kernel,
        out_shape=jax.ShapeDtypeStruct((TOP_K, n), jnp.int32),
        grid=(n // tm,),
        in_specs=[pl.BlockSpec(memory_space=pltpu.SMEM), pl.BlockSpec((ROUTE_ROWS, tm), lambda i: (0, i))],
        out_specs=pl.BlockSpec((TOP_K, tm), lambda i: (0, i)),
        compiler_params=_params(("arbitrary",)),
        name="moe_dest",
    )(pad_start, route_t)


def _dispatch_kernel(seg_ref, dest_ref, x_ref, xs_hbm, zero_ref, zsem, sem):
    i = pl.program_id(0)
    ntok = TOK_TM
    nsl = x_ref.shape[0] // ntok
    blk_rows = MOE_BLOCK * nsl

    def zero_fill(e):
        first = pl.multiple_of((seg_ref[e] + seg_ref[N_EXPERTS + e] - MOE_BLOCK) * nsl, blk_rows)
        return pltpu.make_async_copy(zero_ref, xs_hbm.at[pl.ds(first, blk_rows)], zsem)

    def has_pad(e):
        return seg_ref[N_EXPERTS + e] > seg_ref[2 * N_EXPERTS + e]

    @pl.when(i == 0)
    def _():
        zero_ref[...] = jnp.zeros_like(zero_ref)

        def start(e, c):
            @pl.when(has_pad(e))
            def _():
                zero_fill(e).start()
            return c

        def wait(e, c):
            @pl.when(has_pad(e))
            def _():
                zero_fill(e).wait()
            return c

        lax.fori_loop(0, N_EXPERTS, start, 0)
        lax.fori_loop(0, N_EXPERTS, wait, 0)

        nvalid = seg_ref[3 * N_EXPERTS]
        nblk = xs_hbm.shape[0] // blk_rows

        def tail(j):
            first = pl.multiple_of((nvalid + j) * blk_rows, blk_rows)
            return pltpu.make_async_copy(zero_ref, xs_hbm.at[pl.ds(first, blk_rows)], zsem)

        def tail_start(j, c):
            @pl.when(nvalid + j < nblk)
            def _():
                tail(j).start()
            return c

        def tail_wait(j, c):
            @pl.when(nvalid + j < nblk)
            def _():
                tail(j).wait()
            return c

        lax.fori_loop(0, N_EXPERTS, tail_start, 0)
        lax.fori_loop(0, N_EXPERTS, tail_wait, 0)

    def row_copy(j, slot):
        return pltpu.make_async_copy(x_ref.at[pl.ds(pl.multiple_of(j * nsl, nsl), nsl)],
                                     xs_hbm.at[pl.ds(pl.multiple_of(slot * nsl, nsl), nsl)], sem)

    def issue(j, c):
        for k in range(2):
            row_copy(j, dest_ref[0, 0, TOP_K * j + k]).start(priority=k)
        return c

    lax.fori_loop(0, ntok, issue, 0, unroll=4)
    for _ in range(TOP_K):
        pltpu.make_async_copy(x_ref, xs_hbm.at[pl.ds(0, ntok * nsl)], sem).wait()


def _dispatch(x_tiles, dest, seg, cap, n):
    nsl = x_tiles.shape[0] // n
    tm = TOK_TM
    nt = n // tm
    return pl.pallas_call(
        _dispatch_kernel,
        out_shape=jax.ShapeDtypeStruct((cap * nsl, LANES), x_tiles.dtype),
        grid_spec=pltpu.PrefetchScalarGridSpec(
            num_scalar_prefetch=1,
            grid=(nt,),
            in_specs=[pl.BlockSpec((1, 1, TOP_K * tm), lambda i, seg: (i, 0, 0), memory_space=pltpu.SMEM),
                      pl.BlockSpec((tm * nsl, LANES), lambda i, seg: (i, 0))],
            out_specs=pl.BlockSpec(memory_space=pl.ANY),
            scratch_shapes=[pltpu.VMEM((MOE_BLOCK * nsl, LANES), x_tiles.dtype), pltpu.SemaphoreType.DMA,
                            pltpu.SemaphoreType.DMA]),
        compiler_params=_params(("arbitrary",)),
        name="moe_dispatch",
    )(seg, dest, x_tiles)


def _expert_kernel(blk_ref, nvalid_ref, xs_ref, wg_ref, wu_ref, wd_ref, ys_ref, wgb, wub, wdb):
    i = pl.program_id(0)
    prev = blk_ref[jnp.maximum(i - 1, 0)]
    valid = i < nvalid_ref[0]

    @pl.when(valid & ((i == 0) | (blk_ref[i] != prev)))
    def _():
        wgb[...] = wg_ref[0].astype(_BF16)
        wub[...] = wu_ref[0].astype(_BF16)
        wdb[...] = wd_ref[0].astype(_BF16)

    @pl.when(valid)
    def _():
        d = wgb.shape[0]
        xb = _rows_from_tiles(xs_ref, MOE_BLOCK, d).astype(_BF16)
        gate = jnp.dot(xb, wgb[...], preferred_element_type=_F32)
        up = jnp.dot(xb, wub[...], preferred_element_type=_F32)
        mid = (gate * jax.nn.sigmoid(gate)) * up
        _rows_to_tiles(ys_ref, jnp.dot(mid.astype(_BF16), wdb[...], preferred_element_type=_F32))

    @pl.when(jnp.logical_not(valid))
    def _():
        ys_ref[...] = jnp.zeros_like(ys_ref)


def _experts(xs, blk_exp, nvalid, w_gate, w_up, w_down):
    d, de = w_gate.shape[1], w_gate.shape[2]
    nsl = d // LANES
    cap = xs.shape[0] // nsl
    nblk = cap // MOE_BLOCK

    def xmap(i, blk, nv):
        return (jnp.minimum(i, nv[0] - 1), 0)

    def wmap(i, blk, nv):
        return (blk[i], 0, 0)

    return pl.pallas_call(
        _expert_kernel,
        out_shape=jax.ShapeDtypeStruct((cap * nsl, LANES), _F32),
        grid_spec=pltpu.PrefetchScalarGridSpec(
            num_scalar_prefetch=2,
            grid=(nblk,),
            in_specs=[pl.BlockSpec((MOE_BLOCK * nsl, LANES), xmap),
                      pl.BlockSpec((1, d, de), wmap), pl.BlockSpec((1, d, de), wmap),
                      pl.BlockSpec((1, de, d), wmap)],
            out_specs=pl.BlockSpec((MOE_BLOCK * nsl, LANES), lambda i, blk, nv: (i, 0)),
            scratch_shapes=[pltpu.VMEM((d, de), _BF16), pltpu.VMEM((d, de), _BF16), pltpu.VMEM((de, d), _BF16)]),
        compiler_params=_params(("arbitrary",)),
        name="moe_experts",
    )(blk_exp, nvalid, xs, w_gate, w_up, w_down)


def _combine_kernel(dcur_ref, dnxt_ref, ys_hbm, h_ref, route_ref, g_ref, b_ref, o_ref, buf, sem):
    i = pl.program_id(0)
    n = pl.num_programs(0)
    tm = TOK_TM

    d = o_ref.shape[1]
    nsl = d // LANES

    def tile_copy(src_slot, slot, k, j):
        return pltpu.make_async_copy(ys_hbm.at[pl.ds(pl.multiple_of(src_slot * nsl, nsl), nsl)],
                                     buf.at[slot, k, pl.ds(pl.multiple_of(j * nsl, nsl), nsl)], sem.at[slot])

    def gather(dref, slot):
        def body(j, c):
            for k in range(2):
                tile_copy(dref[0, 0, TOP_K * j + k], slot, k, j).start(priority=k)
            return c
        lax.fori_loop(0, tm, body, 0, unroll=4)

    @pl.when(i == 0)
    def _():
        gather(dcur_ref, 0)

    @pl.when(i + 1 < n)
    def _():
        gather(dnxt_ref, (i + 1) % 2)

    slot = i % 2

    for k in range(TOP_K):
        pltpu.make_async_copy(ys_hbm.at[pl.ds(0, tm * nsl)], buf.at[slot, k], sem.at[slot]).wait()
    r = route_ref[...]
    y = (r[:, 2:3] * _rows_from_tiles(buf.at[slot, 0], tm, d)
         + r[:, 3:4] * _rows_from_tiles(buf.at[slot, 1], tm, d))
    o_ref[...] = _layer_norm(ALPHA * _rows_from_tiles(h_ref, tm, d) + y, g_ref[...], b_ref[...])


def _combine(ys, dest, h1_tiles, route, g, b):
    n = route.shape[0]
    d = g.shape[1]
    nsl = d // LANES
    tm = TOK_TM
    nt = n // tm
    row = lambda i: (i, 0)
    const = lambda i: (0, 0)
    return pl.pallas_call(
        _combine_kernel,
        out_shape=jax.ShapeDtypeStruct((n, d), _F32),
        grid=(nt,),
        in_specs=[pl.BlockSpec((1, 1, TOP_K * tm), lambda i: (i, 0, 0), memory_space=pltpu.SMEM),
                  pl.BlockSpec((1, 1, TOP_K * tm), lambda i: (jnp.minimum(i + 1, nt - 1), 0, 0),
                               memory_space=pltpu.SMEM),
                  pl.BlockSpec(memory_space=pl.ANY),
                  pl.BlockSpec((tm * nsl, LANES), row), pl.BlockSpec((tm, LANES), row),
                  pl.BlockSpec((1, d), const), pl.BlockSpec((1, d), const)],
        out_specs=pl.BlockSpec((tm, d), row),
        scratch_shapes=[pltpu.VMEM((2, 2, tm * nsl, LANES), _F32), pltpu.SemaphoreType.DMA((2,))],
        compiler_params=_params(("arbitrary",)),
        name="moe_combine",
    )(dest, dest, ys, h1_tiles, route, g, b)


def _moe(h1_tiles, route, route_t, counts_row, w_gate, w_up, w_down, g, b):
    n = route.shape[0]
    cap = 2 * n + N_EXPERTS * MOE_BLOCK
    counts = counts_row[0, ROUTE_LANE0:ROUTE_LANE0 + N_EXPERTS].astype(jnp.int32)
    padded = (counts + MOE_BLOCK - 1) // MOE_BLOCK * MOE_BLOCK
    ends = jnp.cumsum(padded)
    pad_start = ends - padded
    nblk = cap // MOE_BLOCK
    blk_first = jnp.arange(nblk, dtype=jnp.int32) * MOE_BLOCK
    blk_exp = jnp.minimum(jnp.sum(blk_first[:, None] >= ends[None, :], axis=1), N_EXPERTS - 1).astype(jnp.int32)
    nvalid = (ends[-1:] // MOE_BLOCK).astype(jnp.int32)
    dest = _dest(route_t, pad_start.astype(_F32)).T.reshape(n // TOK_TM, 1, TOP_K * TOK_TM)
    seg = jnp.concatenate([pad_start, padded, counts, nvalid]).astype(jnp.int32)
    xs = _dispatch(h1_tiles, dest, seg, cap, n)
    ys = _experts(xs, blk_exp, nvalid, w_gate, w_up, w_down)
    return _combine(ys, dest, h1_tiles, route, g, b)


def _router_params(rg_w, rg_b, re_w, re_b):
    d = rg_w.shape[0]
    pad = LANES - N_GROUPS - N_EXPERTS
    rw = jnp.concatenate([rg_w, re_w, jnp.zeros((d, pad), _F32)], axis=1)
    rb = jnp.concatenate([rg_b, re_b.reshape(-1), jnp.zeros((pad,), _F32)]).reshape(1, LANES)
    return rw, rb


def kernel(x, t5_table, l0_w_qkv, l0_sink, l0_w_o, l0_ln1_g, l0_ln1_b, l0_rg_w, l0_rg_b, l0_re_w, l0_re_b, l0_w_gate, l0_w_up, l0_w_down, l0_ln2_g, l0_ln2_b, l1_w_qkv, l1_w_o, l1_ln1_g, l1_ln1_b, l1_rg_w, l1_rg_b, l1_re_w, l1_re_b, l1_w_gate, l1_w_up, l1_w_down, l1_ln2_g, l1_ln2_b, l2_w_qkv, l2_rpb, l2_w_o, l2_ln1_g, l2_ln1_b, l2_rg_w, l2_rg_b, l2_re_w, l2_re_b, l2_w_gate, l2_w_up, l2_w_down, l2_ln2_g, l2_ln2_b, l3_w_qkv, l3_sink, l3_w_o, l3_ln1_g, l3_ln1_b, l3_rg_w, l3_rg_b, l3_re_w, l3_re_b, l3_w_gate, l3_w_up, l3_w_down, l3_ln2_g, l3_ln2_b):
    mixers = [(l0_w_qkv, l0_sink, l0_w_o), (l1_w_qkv, None, l1_w_o), (l2_w_qkv, l2_rpb, l2_w_o),
              (l3_w_qkv, l3_sink, l3_w_o)]
    norm1 = [(l0_ln1_g, l0_ln1_b), (l1_ln1_g, l1_ln1_b), (l2_ln1_g, l2_ln1_b), (l3_ln1_g, l3_ln1_b)]
    routers = [(l0_rg_w, l0_rg_b, l0_re_w, l0_re_b), (l1_rg_w, l1_rg_b, l1_re_w, l1_re_b),
               (l2_rg_w, l2_rg_b, l2_re_w, l2_re_b), (l3_rg_w, l3_rg_b, l3_re_w, l3_re_b)]
    experts = [(l0_w_gate, l0_w_up, l0_w_down), (l1_w_gate, l1_w_up, l1_w_down),
               (l2_w_gate, l2_w_up, l2_w_down), (l3_w_gate, l3_w_up, l3_w_down)]
    norm2 = [(l0_ln2_g, l0_ln2_b), (l1_ln2_g, l1_ln2_b), (l2_ln2_g, l2_ln2_b), (l3_ln2_g, l3_ln2_b)]

    bsz, seq, d = x.shape
    n = bsz * seq
    inner = N_HEADS * HEAD_DIM
    scale = HEAD_DIM ** -0.5 * LOG2E
    h = x.reshape(n, d)
    bias_a = _band_bias_t(t5_table, A_BLOCK, A_WINDOW, 1, A_BAND)
    for i in range(DEPTH):
        kind = i % N_MIXERS
        w_qkv, extra, w_o = mixers[i]
        kvw = A_KV_HEADS * HEAD_DIM if kind == 0 else inner
        lses, unperm = (), None
        if kind == 1:
            perms = np.stack([_residue_perm(dil) for _, dil in B_BRANCHES if dil > 1])
            outs = _project(h, w_qkv.astype(_BF16), (inner, kvw, kvw), (scale, 1.0, 1.0), jnp.asarray(perms, _BF16))
            unperm = jnp.asarray(perms.transpose(0, 2, 1), _BF16)
            attn, lses = [], []
            for bi, (window, dil) in enumerate(B_BRANCHES):
                q, k, v = outs[3 * bi:3 * bi + 3]
                o, lse = _mixer_b_branch(q, k, v, t5_table, bsz, seq, window, dil)
                attn.append(o)
                lses.append(jnp.pad(lse.reshape(n, N_HEADS), ((0, 0), (0, LANES - N_HEADS))))
        else:
            q, k, v = _project(h, w_qkv.astype(_BF16), (inner, kvw, kvw), (scale, 1.0, 1.0))
            q, k, v = (t.reshape(bsz, seq, t.shape[1]) for t in (q, k, v))
            if kind == 0:
                attn = [_mixer_a(q, k, v, extra.astype(_F32) * LOG2E, bias_a, bsz, seq)]
            else:
                attn = [_mixer_c(q, k, v, _na_bias_t(extra, seq // GRID_W), bsz, seq)]
        attn = [a.reshape(n, inner) for a in attn]
        g1, b1 = norm1[i]
        rw, rb = _router_params(*routers[i])
        h1, route, route_t, counts = _post_attn(attn, lses, h, w_o.astype(_BF16), g1.reshape(1, d), b1.reshape(1, d),
                                                rw, rb, unperm)
        g2, b2 = norm2[i]
        h = _moe(h1, route, route_t, counts, *experts[i], g2.reshape(1, d), b2.reshape(1, d))
    return h.reshape(bsz, seq, d)
```

```python
import functools
import math

import jax
import jax.numpy as jnp
import numpy as np
from jax import lax
from jax.experimental import pallas as pl
from jax.experimental.pallas import tpu as pltpu

HEAD_DIM = 64
N_HEADS = 16
A_KV_HEADS = 4
A_WINDOW = 128
A_BLOCK = 128
B_BRANCHES = ((128, 1), (512, 4), (2048, 16))
GRID_W = 64
NA_ROWS = 8
NA_COLS = 16
T5_BUCKETS = 32
T5_MAX_DIST = 1024
N_GROUPS = 4
EXPERTS_PER_GROUP = 8
N_EXPERTS = N_GROUPS * EXPERTS_PER_GROUP
TOP_K = 2
DEPTH = 4
N_MIXERS = 3
ALPHA = (2 * DEPTH) ** 0.25
LN_EPS = 1e-5
NEG_INF = -1e30

LANES = 128
PAIR = 2 * HEAD_DIM
BF16_SUBLANES = 16
VT_ROWS = PAIR + BF16_SUBLANES
LOG2E = math.log2(math.e)
LN2 = math.log(2.0)
VMEM_LIMIT = 56 * 1024 * 1024

PROJ_TM = 512
PROJ_CHUNK = 512
POST_TM = 1024
POST_SUB = 256
TOK_TM = 1024
DEST_TM = 8192
ROUTE_ROWS = 8
MOE_BLOCK = 512
A_TQ = 2048
A_BAND = 3
A_SUB_PER_GROUP = 2
B_BLOCK = 128
RESIDUE_RUN = BF16_SUBLANES
PERM_CHUNK = 256
TRANSPOSE_UNROLL = 8
UNITS_PER_GROUP = 8
C_UNIT_ROWS = 2
C_BAND_ROWS = 10
C_UNITS_PER_GROUP = 4
C_ROWS_PER_STEP = 64
ROUTE_LANE0 = N_GROUPS

_F32 = jnp.float32
_BF16 = jnp.bfloat16


def _params(sem, vmem=VMEM_LIMIT, flags=None):
    return pltpu.CompilerParams(dimension_semantics=sem, vmem_limit_bytes=vmem, flags=flags)


def _proj_kernel(x_ref, w_ref, *refs, widths, scales, nperm):
    perm_ref = refs[0] if nperm else None
    o_refs = refs[1:] if nperm else refs
    nout = len(widths)
    xb = x_ref[...].astype(_BF16)
    col = 0
    for t, (width, scale) in enumerate(zip(widths, scales)):
        for c in range(0, width, PROJ_CHUNK):
            cw = min(PROJ_CHUNK, width - c)
            acc = jnp.dot(xb, w_ref[:, col + c:col + c + cw], preferred_element_type=_F32)
            if scale != 1.0:
                acc = acc * scale
            ob = acc.astype(_BF16)
            o_refs[t][:, c:c + cw] = ob
            for p in range(nperm):
                for r0 in range(0, ob.shape[0], PERM_CHUNK):
                    moved = jnp.dot(perm_ref[p], ob[r0:r0 + PERM_CHUNK], preferred_element_type=_F32)
                    o_refs[(p + 1) * nout + t][r0:r0 + PERM_CHUNK, c:c + cw] = moved.astype(_BF16)
        col += width


def _project(x2, w_bf16, widths, scales, perms=None):
    n, d = x2.shape
    ctot = sum(widths)
    nperm = 0 if perms is None else perms.shape[0]
    args = [x2, w_bf16] + ([perms] if nperm else [])
    specs = [pl.BlockSpec((PROJ_TM, d), lambda i: (i, 0)), pl.BlockSpec((d, ctot), lambda i: (0, 0))]
    if nperm:
        specs.append(pl.BlockSpec(perms.shape, lambda i: (0, 0, 0)))
    return pl.pallas_call(
        functools.partial(_proj_kernel, widths=tuple(widths), scales=tuple(scales), nperm=nperm),
        out_shape=[jax.ShapeDtypeStruct((n, w), _BF16) for w in widths] * (nperm + 1),
        grid=(n // PROJ_TM,),
        in_specs=specs,
        out_specs=[pl.BlockSpec((PROJ_TM, w), lambda i: (i, 0)) for w in widths] * (nperm + 1),
        compiler_params=_params(("arbitrary",)),
        name="qkv_proj",
    )(*args)


def _residue_perm(dil):
    span = dil * RESIDUE_RUN
    p = np.zeros((PERM_CHUNK, PERM_CHUNK), np.float32)
    for base in range(0, PERM_CHUNK, span):
        for r in range(dil):
            for u in range(RESIDUE_RUN):
                p[base + r * RESIDUE_RUN + u, base + u * dil + r] = 1.0
    return p


def _one():
    return jnp.ones((1,), jnp.int32)


def _lane_lo(rows=1):
    return lax.broadcasted_iota(jnp.int32, (rows, PAIR), 1) < HEAD_DIM


def _swap_halves(x):
    return pltpu.roll(x.astype(_F32), HEAD_DIM, axis=1).astype(_BF16)


class _PairUnit:
    def __init__(self, q2, k, vt, bias, emit, sink_a=None, sink_b=None):
        self.q2, self.k, self.vt, self.bias, self.emit = q2, k, vt, bias, emit
        self.sink_a, self.sink_b = sink_a, sink_b


def _attend_groups(ngroups, units_of, scratch, one):
    assert ngroups % 2 == 0
    lo = _lane_lo()
    nt = (((1,), (1,)), ((), ()))

    def sink_row(unit, m2):
        lane = lax.broadcasted_iota(jnp.int32, (1, m2), 1)
        return jnp.where(lane < m2 // 2, unit.sink_a, unit.sink_b)

    def scores(g, parity):
        s_scr, m_scr = scratch[parity]
        for u, unit in enumerate(units_of(g, parity)):
            q2 = unit.q2()
            zero = jnp.zeros_like(q2)
            qq = jnp.concatenate([jnp.where(lo, q2, zero), jnp.where(lo, zero, q2)], axis=0)
            s = lax.dot_general(unit.k(), qq, nt, preferred_element_type=_F32) + unit.bias()
            m = jnp.max(s, axis=0, keepdims=True)
            if unit.sink_a is not None:
                m = jnp.maximum(m, sink_row(unit, s.shape[1]))
            s_scr[u] = s
            m_scr[u] = m

    def values(g, parity):
        s_scr, m_scr = scratch[parity]
        for u, unit in enumerate(units_of(g, parity)):
            m = m_scr[u]
            e = jnp.exp2(s_scr[u] - m)
            ot = jnp.dot(unit.vt(), e.astype(_BF16), preferred_element_type=_F32)
            den = ot[PAIR:PAIR + 1, :]
            if unit.sink_a is not None:
                den = den + jnp.exp2(sink_row(unit, e.shape[1]) - m)
            ot = ot[:PAIR] * (1.0 / den)
            mq = ot.shape[1] // 2
            top = lax.broadcasted_iota(jnp.int32, (PAIR, 1), 0) < HEAD_DIM
            pair_t = jnp.where(top, ot[:, :mq], ot[:, mq:])
            unit.emit(pair_t.T, (m + jnp.log2(den)) * LN2)

    def region(*work):
        def body(_, carry):
            for fn, g, parity in work:
                fn(g, parity)
            return carry
        lax.fori_loop(0, one, body, 0)

    region((scores, 0, 0))

    def steady(i, carry):
        g = 2 * i
        region((values, g, 0), (scores, g + 1, 1))
        region((values, g + 1, 1), (scores, g + 2, 0))
        return carry

    lax.fori_loop(0, ngroups // 2 - 1, steady, 0)
    region((values, ngroups - 2, 0), (scores, ngroups - 1, 1))
    region((values, ngroups - 1, 1))


def _transpose_blocks(v_ref_rows, nblk, store):
    def body(j, carry):
        rows = pl.ds(pl.multiple_of(j * LANES, LANES), LANES)
        vt = v_ref_rows(rows).T
        store(j, jnp.concatenate([vt, jnp.ones((VT_ROWS - PAIR, LANES), _BF16)], axis=0))
        return carry
    lax.fori_loop(0, nblk, body, 0, unroll=TRANSPOSE_UNROLL)


def _t5_bucket(rel):
    half = T5_BUCKETS // 2
    exact = half // 2
    n = np.abs(rel)
    big = exact + (np.log(np.maximum(n, 1) / exact) / math.log(T5_MAX_DIST / exact) * (half - exact)).astype(np.int32)
    return ((rel > 0) * half + np.where(n < exact, n, np.minimum(big, half - 1))).astype(np.int32)


def _band_bias_t(t5_table, blk, window, dil, nband):
    tk = nband * blk
    off = tk - 1
    rel = np.arange(-off, tk)
    vec = jnp.take(t5_table, jnp.asarray(_t5_bucket(rel * dil)), axis=0).astype(_F32)
    vec = jnp.where(jnp.asarray(np.abs(rel) <= window)[:, None], vec * LOG2E, NEG_INF).T
    h = vec.shape[0]
    vec = jnp.pad(vec, ((0, 0), (0, 1)))
    w = 2 * tk - 1
    skew = jnp.tile(vec, (1, blk))[:, :blk * w].reshape(h, blk, w)
    t = jnp.stack([skew[:, :, off - v * blk:off - v * blk + tk] for v in range(nband)], axis=1)
    t = t.reshape(h // 2, 2, nband, blk, tk).transpose(0, 2, 4, 1, 3)
    return t.reshape(h // 2, nband, tk, 2 * blk)


def _band_window(gb, blk, length, nband):
    first = jnp.clip(gb - 1, 0, length // blk - nband)
    return first, gb - first


def _attn_a_kernel(sink_ref, one_ref, q_ref, k_ref, v_ref, bias_ref, o_ref, kd_ref, vdt_ref,
                   s0_scr, m0_scr, s1_scr, m1_scr, *, seq):
    kvb = pl.program_id(0)
    qt = pl.program_id(2)
    one = one_ref[0]
    nblk = seq // A_BLOCK

    @pl.when(qt == 0)
    def _():
        lo = _lane_lo()

        def fill(j, carry):
            rows = pl.ds(pl.multiple_of(j * A_BLOCK, A_BLOCK), A_BLOCK)
            kk = k_ref[0, rows, :]
            ks = _swap_halves(kk)
            kd_ref[0, rows, :] = jnp.where(lo, kk, ks)
            kd_ref[1, rows, :] = jnp.where(lo, ks, kk)
            vt = v_ref[0, rows, :].T
            ones = jnp.ones((VT_ROWS - PAIR, A_BLOCK), _BF16)
            vdt_ref[0, j] = jnp.concatenate([vt[:HEAD_DIM], vt[:HEAD_DIM], ones], axis=0)
            vdt_ref[1, j] = jnp.concatenate([vt[HEAD_DIM:], vt[HEAD_DIM:], ones], axis=0)
            return carry

        lax.fori_loop(0, nblk, fill, 0, unroll=TRANSPOSE_UNROLL)

    nsub = A_TQ // A_BLOCK
    npair = q_ref.shape[2] // PAIR
    hbase = kvb * (2 * npair)

    def units_of(g, parity):
        units = []
        for j in range(A_SUB_PER_GROUP):
            si = g * A_SUB_PER_GROUP + j
            first, var = _band_window(qt * nsub + si, A_BLOCK, seq, A_BAND)
            rows = pl.ds(pl.multiple_of(si * A_BLOCK, A_BLOCK), A_BLOCK)
            band = pl.ds(pl.multiple_of(first * A_BLOCK, A_BLOCK), A_BAND * A_BLOCK)
            for jj in range(npair):
                hk = jj // (npair // 2)
                cols = slice(jj * PAIR, (jj + 1) * PAIR)

                def emit(o, lse, rows=rows, cols=cols):
                    o_ref[0, rows, cols] = o.astype(o_ref.dtype)

                units.append(_PairUnit(
                    q2=lambda rows=rows, cols=cols: q_ref[0, rows, cols],
                    k=lambda hk=hk, band=band: kd_ref[hk, band, :],
                    vt=lambda hk=hk, first=first: jnp.concatenate(
                        [vdt_ref[hk, first + t] for t in range(A_BAND)], axis=1),
                    bias=lambda jj=jj, var=var: bias_ref[jj, var],
                    emit=emit, sink_a=sink_ref[hbase + 2 * jj], sink_b=sink_ref[hbase + 2 * jj + 1]))
        return units

    _attend_groups(nsub // A_SUB_PER_GROUP, units_of, ((s0_scr, m0_scr), (s1_scr, m1_scr)), one)


def _mixer_a(q, k, v, sink, bias, bsz, seq):
    nkvb = k.shape[2] // PAIR
    qw = q.shape[2] // nkvb
    npair = qw // PAIR
    tk = A_BAND * A_BLOCK
    nunits = A_SUB_PER_GROUP * npair
    group_scratch = [pltpu.VMEM((nunits, tk, 2 * A_BLOCK), _F32), pltpu.VMEM((nunits, 1, 2 * A_BLOCK), _F32)]
    return pl.pallas_call(
        functools.partial(_attn_a_kernel, seq=seq),
        out_shape=jax.ShapeDtypeStruct(q.shape, _BF16),
        grid=(nkvb, bsz, seq // A_TQ),
        in_specs=[pl.BlockSpec(memory_space=pltpu.SMEM), pl.BlockSpec(memory_space=pltpu.SMEM),
                  pl.BlockSpec((1, A_TQ, qw), lambda c, b, t: (b, t, c)),
                  pl.BlockSpec((1, seq, PAIR), lambda c, b, t: (b, 0, c)),
                  pl.BlockSpec((1, seq, PAIR), lambda c, b, t: (b, 0, c)),
                  pl.BlockSpec((npair, A_BAND, tk, 2 * A_BLOCK), lambda c, b, t: (c, 0, 0, 0))],
        out_specs=pl.BlockSpec((1, A_TQ, qw), lambda c, b, t: (b, t, c)),
        scratch_shapes=[pltpu.VMEM((2, seq, PAIR), _BF16), pltpu.VMEM((2, seq // A_BLOCK, VT_ROWS, A_BLOCK), _BF16)]
        + group_scratch + group_scratch,
        compiler_params=_params(("arbitrary", "arbitrary", "arbitrary")),
        name="mixer_a",
    )(sink, _one(), q, k, v, bias)


def _attn_band_kernel(one_ref, q_ref, k_ref, v_ref, bias_ref, o_ref, lse_ref, vt_ref,
                      s0_scr, m0_scr, s1_scr, m1_scr, *, length, npairs, nband, chunked):
    col = pl.program_id(2)
    one = one_ref[0]
    nblk = length // B_BLOCK
    runs = B_BLOCK // RESIDUE_RUN

    def load(ref, blk0, nblocks, cols):
        if chunked:
            r0 = pl.multiple_of(blk0 * runs, runs)
            return ref[pl.ds(r0, nblocks * runs), :, cols].reshape(nblocks * B_BLOCK, PAIR)
        return ref[0, pl.ds(pl.multiple_of(blk0 * B_BLOCK, B_BLOCK), nblocks * B_BLOCK), cols]

    def store_out(blk0, cols, val):
        if chunked:
            o_ref[pl.ds(pl.multiple_of(blk0 * runs, runs), runs), :, cols] = val.reshape(runs, RESIDUE_RUN, PAIR)
        else:
            o_ref[0, pl.ds(pl.multiple_of(blk0 * B_BLOCK, B_BLOCK), B_BLOCK), cols] = val

    pairs_per_loop = max(1, min(npairs, TRANSPOSE_UNROLL // min(TRANSPOSE_UNROLL, nblk) * 2))
    for p0 in range(0, npairs, pairs_per_loop):
        def fill(j, carry, p0=p0):
            for p in range(p0, p0 + pairs_per_loop):
                vt = load(v_ref, j, 1, slice(p * PAIR, (p + 1) * PAIR)).T
                vt_ref[p, j] = jnp.concatenate([vt, jnp.ones((VT_ROWS - PAIR, B_BLOCK), _BF16)], axis=0)
            return carry

        lax.fori_loop(0, nblk, fill, 0, unroll=min(TRANSPOSE_UNROLL, nblk))

    def unit(gb, p):
        first, var = _band_window(gb, B_BLOCK, length, nband)
        cols = slice(p * PAIR, (p + 1) * PAIR)

        def emit(o, lse):
            store_out(gb, cols, o.astype(o_ref.dtype))
            ha = col * (2 * npairs) + 2 * p
            lse_ref[0, 0, gb, pl.ds(ha, 1), :] = lse[:, :B_BLOCK]
            lse_ref[0, 0, gb, pl.ds(ha + 1, 1), :] = lse[:, B_BLOCK:]

        return _PairUnit(
            q2=lambda: load(q_ref, gb, 1, cols), k=lambda: load(k_ref, first, nband, cols),
            vt=lambda: jnp.concatenate([vt_ref[p, first + t] for t in range(nband)], axis=1),
            bias=lambda: bias_ref[p, var], emit=emit)

    if npairs == 1:
        ngroups = nblk // UNITS_PER_GROUP
        units_of = lambda g, parity: [unit(g * UNITS_PER_GROUP + j, 0) for j in range(UNITS_PER_GROUP)]
    else:
        assert npairs == UNITS_PER_GROUP
        ngroups = nblk
        units_of = lambda g, parity: [unit(g, p) for p in range(npairs)]
    _attend_groups(ngroups, units_of, ((s0_scr, m0_scr), (s1_scr, m1_scr)), one)


def _mixer_b_branch(q, k, v, t5_table, bsz, seq, window, dil):
    length = seq // dil
    width = q.shape[1]
    nblk = length // B_BLOCK
    nband = min(3, nblk)
    tk = nband * B_BLOCK
    npairs = (width // PAIR) if length * width * 2 <= 2 * 1024 * 1024 else 1
    ncol = width // (npairs * PAIR)
    bw = npairs * PAIR
    bias = _band_bias_t(t5_table, B_BLOCK, window // 2 // dil, dil, nband)
    chunked = dil > 1
    if chunked:
        nrun = length // RESIDUE_RUN
        shape = (bsz * nrun, dil, RESIDUE_RUN, width)
        spec = pl.BlockSpec((nrun, None, RESIDUE_RUN, bw), lambda b, r, c: (b, r, 0, c))
    else:
        shape = (bsz, seq, width)
        spec = pl.BlockSpec((1, seq, bw), lambda b, r, c: (b, 0, c))

    o, lse = pl.pallas_call(
        functools.partial(_attn_band_kernel, length=length, npairs=npairs, nband=nband, chunked=chunked),
        out_shape=[jax.ShapeDtypeStruct(shape, _BF16),
                   jax.ShapeDtypeStruct((bsz, dil, nblk, N_HEADS, B_BLOCK), _F32)],
        grid=(bsz, dil, ncol),
        in_specs=[pl.BlockSpec(memory_space=pltpu.SMEM), spec, spec, spec,
                  pl.BlockSpec((npairs, nband, tk, 2 * B_BLOCK), lambda b, r, c: (c, 0, 0, 0))],
        out_specs=[spec, pl.BlockSpec((1, 1, nblk, N_HEADS, B_BLOCK), lambda b, r, c: (b, r, 0, 0, 0))],
        scratch_shapes=[pltpu.VMEM((npairs, nblk, VT_ROWS, B_BLOCK), _BF16)] + 2 * [
            pltpu.VMEM((UNITS_PER_GROUP, tk, 2 * B_BLOCK), _F32), pltpu.VMEM((UNITS_PER_GROUP, 1, 2 * B_BLOCK), _F32)],
        compiler_params=_params(("arbitrary", "arbitrary", "arbitrary")),
        name=f"mixer_b_dil{dil}",
    )(_one(), q.reshape(shape), k.reshape(shape), v.reshape(shape), bias)
    lse = lse.transpose(0, 2, 4, 1, 3).reshape(bsz, seq, N_HEADS)
    return o.reshape(bsz * seq, width), lse


def _na_bias_t(rpb, rows):
    canon = (0, 2, 4, rows - 4, rows - 2)
    h = rpb.shape[0]
    kc = np.arange(GRID_W)[:, None]
    qc = np.arange(GRID_W)[None, :]
    ws = np.clip(qc - NA_COLS // 2, 0, GRID_W - NA_COLS)
    col_ok = (kc >= ws) & (kc < ws + NA_COLS)
    col_idx = np.clip(kc - qc, 1 - NA_COLS, NA_COLS - 1) + NA_COLS - 1
    colpart = jnp.where(jnp.asarray(col_ok), rpb.astype(_F32)[:, :, jnp.asarray(col_idx)] * LOG2E, NEG_INF)
    masked = jnp.full((h, GRID_W, GRID_W), NEG_INF, _F32)
    variants = []
    for i in canon:
        start = int(np.clip(i - NA_ROWS // 2, 0, rows - C_BAND_ROWS))
        per_key_row = []
        for kr in range(C_BAND_ROWS):
            per_query_row = []
            for qr in range(C_UNIT_ROWS):
                iq, kabs = i + qr, start + kr
                first_key_row = int(np.clip(iq - NA_ROWS // 2, 0, rows - NA_ROWS))
                inside = first_key_row <= kabs < first_key_row + NA_ROWS
                per_query_row.append(colpart[:, kabs - iq + NA_ROWS - 1] if inside else masked)
            per_key_row.append(jnp.stack(per_query_row, axis=2))
        variants.append(jnp.stack(per_key_row, axis=1))
    b = jnp.stack(variants, axis=1)
    nq, nk = C_UNIT_ROWS * GRID_W, C_BAND_ROWS * GRID_W
    b = b.reshape(h // 2, 2, len(canon), nk, nq)
    return b.transpose(0, 2, 3, 1, 4).reshape(h // 2, len(canon), nk, 2 * nq)


def _attn_c_kernel(one_ref, q_ref, k_ref, v_ref, bias_ref, o_ref, vt_ref, s0_scr, m0_scr, s1_scr, m1_scr, *, rows):
    rt = pl.program_id(2)
    nq = C_UNIT_ROWS * GRID_W
    band_blocks = C_BAND_ROWS * GRID_W // LANES

    @pl.when(rt == 0)
    def _():
        def store(j, blk):
            vt_ref[j] = blk
        _transpose_blocks(lambda r: v_ref[0, r, :], rows * GRID_W // LANES, store)

    def unit(u):
        i = rt * C_ROWS_PER_STEP + C_UNIT_ROWS * u
        start = jnp.clip(i - NA_ROWS // 2, 0, rows - C_BAND_ROWS)
        var = lax.shift_right_logical(i - start, 1)
        first = lax.shift_right_logical(start, 1)
        keys = pl.ds(pl.multiple_of(first * LANES, LANES), C_BAND_ROWS * GRID_W)
        qrows = pl.ds(pl.multiple_of(u * nq, nq), nq)

        def emit(o, lse):
            o_ref[0, qrows, :] = o.astype(o_ref.dtype)

        return _PairUnit(
            q2=lambda: q_ref[0, qrows, :], k=lambda: k_ref[0, keys, :],
            vt=lambda: jnp.concatenate([vt_ref[first + t] for t in range(band_blocks)], axis=1),
            bias=lambda: bias_ref[0, var], emit=emit)

    ngroups = C_ROWS_PER_STEP // C_UNIT_ROWS // C_UNITS_PER_GROUP
    units_of = lambda g, parity: [unit(g * C_UNITS_PER_GROUP + j) for j in range(C_UNITS_PER_GROUP)]
    _attend_groups(ngroups, units_of, ((s0_scr, m0_scr), (s1_scr, m1_scr)), one_ref[0])


def _mixer_c(q, k, v, bias, bsz, seq):
    rows = seq // GRID_W
    tq = C_ROWS_PER_STEP * GRID_W
    npair = q.shape[2] // PAIR
    nq, nk = C_UNIT_ROWS * GRID_W, C_BAND_ROWS * GRID_W
    return pl.pallas_call(
        functools.partial(_attn_c_kernel, rows=rows),
        out_shape=jax.ShapeDtypeStruct(q.shape, _BF16),
        grid=(npair, bsz, seq // tq),
        in_specs=[pl.BlockSpec(memory_space=pltpu.SMEM),
                  pl.BlockSpec((1, tq, PAIR), lambda c, b, t: (b, t, c)),
                  pl.BlockSpec((1, seq, PAIR), lambda c, b, t: (b, 0, c)),
                  pl.BlockSpec((1, seq, PAIR), lambda c, b, t: (b, 0, c)),
                  pl.BlockSpec((1, bias.shape[1], nk, 2 * nq), lambda c, b, t: (c, 0, 0, 0))],
        out_specs=pl.BlockSpec((1, tq, PAIR), lambda c, b, t: (b, t, c)),
        scratch_shapes=[pltpu.VMEM((seq // LANES, VT_ROWS, LANES), _BF16)] + 2 * [
            pltpu.VMEM((C_UNITS_PER_GROUP, nk, 2 * nq), _F32), pltpu.VMEM((C_UNITS_PER_GROUP, 1, 2 * nq), _F32)],
        compiler_params=_params(("arbitrary", "arbitrary", "arbitrary")),
        name="mixer_c",
    )(_one(), q, k, v, bias)


def _layer_norm(z, g, b):
    mu = jnp.mean(z, axis=-1, keepdims=True)
    zc = z - mu
    var = jnp.mean(zc * zc, axis=-1, keepdims=True)
    return zc * lax.rsqrt(var + LN_EPS) * g + b


def _rows_to_tiles(ref, x):
    nrows, d = x.shape
    nsl = d // LANES
    for s in range(nsl):
        ref[pl.ds(s, nrows, stride=nsl), :] = x[:, s * LANES:(s + 1) * LANES]


def _rows_from_tiles(ref, nrows, d):
    nsl = d // LANES
    return jnp.concatenate([ref[pl.ds(s, nrows, stride=nsl), :] for s in range(nsl)], axis=1)


def _split_bf16(x):
    hi = x.astype(_BF16)
    return hi, (x - hi.astype(_F32)).astype(_BF16)


def _router_logits(hn, rw_ref, rb_ref):
    h_hi, h_lo = _split_bf16(hn)
    w_hi, w_lo = _split_bf16(rw_ref[...])
    both = jnp.dot(h_hi, jnp.concatenate([w_hi, w_lo], axis=1), preferred_element_type=_F32)
    return (both[:, :LANES]
            + (both[:, LANES:] + jnp.dot(h_lo, w_hi, preferred_element_type=_F32))) + rb_ref[...]


def _route(logits, tri_ref, carry_ref):
    tm = logits.shape[0]
    lane = lax.broadcasted_iota(jnp.int32, (tm, LANES), 1).astype(_F32)
    ninf = -jnp.inf
    is_grp = lane < N_GROUPS
    lg = jnp.where(is_grp, logits, ninf)
    gmax = jnp.max(lg, axis=-1, keepdims=True)
    grp = jnp.min(jnp.where(lg == gmax, lane, float(LANES)), axis=-1, keepdims=True)
    p_grp = 1.0 / jnp.sum(jnp.where(is_grp, jnp.exp(lg - gmax), 0.0), axis=-1, keepdims=True)
    lo_lane = ROUTE_LANE0 + EXPERTS_PER_GROUP * grp
    in_grp = (lane >= lo_lane) & (lane < lo_lane + EXPERTS_PER_GROUP)
    le = jnp.where(in_grp, logits, ninf)
    v1 = jnp.max(le, axis=-1, keepdims=True)
    i1 = jnp.min(jnp.where(le == v1, lane, float(LANES)), axis=-1, keepdims=True)
    le2 = jnp.where(lane == i1, ninf, le)
    v2 = jnp.max(le2, axis=-1, keepdims=True)
    i2 = jnp.min(jnp.where(le2 == v2, lane, float(LANES)), axis=-1, keepdims=True)
    t = jnp.exp(v2 - v1)
    s1 = 1.0 / (1.0 + t)
    g1 = p_grp * s1
    g2 = p_grp * (t * s1)
    sel1 = lane == i1
    sel2 = lane == i2
    onehot = jnp.where(sel1 | sel2, 1.0, 0.0)
    sub = tri_ref.shape[0]
    carry = carry_ref[...]
    parts = []
    for r0 in range(0, tm, sub):
        part = onehot[r0:r0 + sub]
        parts.append(jnp.dot(tri_ref[...], part.astype(_BF16), preferred_element_type=_F32) + carry)
        carry = carry + jnp.sum(part, axis=0, keepdims=True)
    carry_ref[...] = carry
    before = jnp.concatenate(parts, axis=0) if len(parts) > 1 else parts[0]
    rank1 = jnp.sum(jnp.where(sel1, before, 0.0), axis=-1, keepdims=True)
    rank2 = jnp.sum(jnp.where(sel2, before, 0.0), axis=-1, keepdims=True)
    out = jnp.zeros((tm, LANES), _F32)
    for idx, val in enumerate((i1 - ROUTE_LANE0, i2 - ROUTE_LANE0, g1, g2, rank1, rank2)):
        out = jnp.where(lane == idx, val, out)
    return out


def _post_attn_kernel(*refs, n_branch):
    attn_refs = refs[:n_branch]
    pos = n_branch
    if n_branch > 1:
        lse_refs = refs[pos:pos + n_branch]
        expand_ref, unperm_ref = refs[pos + n_branch:pos + n_branch + 2]
        pos += n_branch + 2
    wo_ref, h_ref, g_ref, b_ref, rw_ref, rb_ref, tri_ref = refs[pos:pos + 7]
    h1_ref, route_ref, routet_ref, counts_ref, carry_ref = refs[pos + 7:pos + 12]

    @pl.when(pl.program_id(0) == 0)
    def _():
        carry_ref[...] = jnp.zeros_like(carry_ref)

    d = h_ref.shape[1]
    logits = []
    for sub in range(h_ref.shape[0] // POST_SUB):
        rows = slice(sub * POST_SUB, (sub + 1) * POST_SUB)
        if n_branch == 1:
            a = attn_refs[0][rows, :]
        else:
            lses = [r[rows, :] for r in lse_refs]
            mx = functools.reduce(jnp.maximum, lses)
            es = [jnp.exp(l - mx) for l in lses]
            inv = 1.0 / functools.reduce(jnp.add, es)
            mixed = None
            for bi, (e, a_ref) in enumerate(zip(es, attn_refs)):
                w_hi, w_lo = _split_bf16(e * inv)
                w_full = jnp.dot(jnp.concatenate([w_hi, w_lo], axis=1), expand_ref[...],
                                 preferred_element_type=_F32)
                if bi == 0:
                    a_nat = a_ref[rows, :].astype(_F32)
                else:
                    a_nat = jnp.dot(unperm_ref[bi - 1], a_ref[rows, :], preferred_element_type=_F32)
                term = w_full * a_nat
                mixed = term if mixed is None else mixed + term
            a = mixed.astype(_BF16)
        y = jnp.dot(a, wo_ref[...], preferred_element_type=_F32)
        hn = _layer_norm(ALPHA * h_ref[rows, :] + y, g_ref[...], b_ref[...])
        nsl = d // LANES
        _rows_to_tiles(h1_ref.at[pl.ds(sub * POST_SUB * nsl, POST_SUB * nsl)], hn)
        logits.append(_router_logits(hn, rw_ref, rb_ref))
    route = _route(jnp.concatenate(logits, axis=0), tri_ref, carry_ref)
    route_ref[...] = route
    routet_ref[...] = route.T[:ROUTE_ROWS]
    counts_ref[...] = carry_ref[...]


def _post_attn(attn, lses, h2, w_o, g, b, rw, rb, unperm=None):
    n, d = h2.shape
    tm = POST_TM
    nb = len(attn)
    tri = jnp.asarray(np.tril(np.ones((POST_SUB, POST_SUB), np.float32), -1), _BF16)
    row = lambda i: (i, 0)
    const = lambda i: (0, 0)
    args = list(attn)
    specs = [pl.BlockSpec((tm, d), row) for _ in attn]
    if nb > 1:
        expand = np.zeros((2, LANES, d), np.float32)
        for hd in range(N_HEADS):
            expand[:, hd, hd * HEAD_DIM:(hd + 1) * HEAD_DIM] = 1.0
        args += list(lses) + [jnp.asarray(expand.reshape(2 * LANES, d), _BF16), unperm]
        specs += [pl.BlockSpec((tm, LANES), row) for _ in lses] + [
            pl.BlockSpec((2 * LANES, d), const), pl.BlockSpec(unperm.shape, lambda i: (0, 0, 0))]
    args += [w_o, h2, g, b, rw, rb, tri]
    specs += [pl.BlockSpec((d, d), const), pl.BlockSpec((tm, d), row), pl.BlockSpec((1, d), const),
              pl.BlockSpec((1, d), const), pl.BlockSpec((d, LANES), const), pl.BlockSpec((1, LANES), const),
              pl.BlockSpec((POST_SUB, POST_SUB), const)]
    return pl.pallas_call(
        functools.partial(_post_attn_kernel, n_branch=nb),
        out_shape=[jax.ShapeDtypeStruct((n * d // LANES, LANES), _F32), jax.ShapeDtypeStruct((n, LANES), _F32),
                   jax.ShapeDtypeStruct((ROUTE_ROWS, n), _F32),
                   jax.ShapeDtypeStruct((1, LANES), _F32)],
        grid=(n // tm,),
        in_specs=specs,
        out_specs=[pl.BlockSpec((tm * d // LANES, LANES), row), pl.BlockSpec((tm, LANES), row),
                   pl.BlockSpec((ROUTE_ROWS, tm), lambda i: (0, i)), pl.BlockSpec((1, LANES), const)],
        scratch_shapes=[pltpu.VMEM((1, LANES), _F32)],
        compiler_params=_params(("arbitrary",)),
        name="post_attn_router",
    )(*args)


def _dest_kernel(ps_ref, rt_ref, o_ref):
    expert = rt_ref[0:TOP_K, :]
    slot = rt_ref[4:4 + TOP_K, :]
    for e in range(N_EXPERTS):
        slot = slot + jnp.where(expert == float(e), ps_ref[e], 0.0)
    o_ref[...] = slot.astype(jnp.int32)


def _dest(route_t, pad_start):
    n = route_t.shape[1]
    tm = min(DEST_TM, n)
    return pl.pallas_call(
        _dest_kernel,
        out_shape=jax.ShapeDtypeStruct((TOP_K, n), jnp.int32),
        grid=(n // tm,),
        in_specs=[pl.BlockSpec(memory_space=pltpu.SMEM), pl.BlockSpec((ROUTE_ROWS, tm), lambda i: (0, i))],
        out_specs=pl.BlockSpec((TOP_K, tm), lambda i: (0, i)),
        compiler_params=_params(("arbitrary",)),
        name="moe_dest",
    )(pad_start, route_t)


def _dispatch_kernel(seg_ref, d0_ref, d1_ref, x_ref, xs_hbm, zero_ref, zsem, sem):
    i = pl.program_id(0)
    ntok = TOK_TM
    nsl = x_ref.shape[0] // ntok
    blk_rows = MOE_BLOCK * nsl

    def zero_fill(e):
        first = pl.multiple_of((seg_ref[e] + seg_ref[N_EXPERTS + e] - MOE_BLOCK) * nsl, blk_rows)
        return pltpu.make_async_copy(zero_ref, xs_hbm.at[pl.ds(first, blk_rows)], zsem)

    def has_pad(e):
        return seg_ref[N_EXPERTS + e] > seg_ref[2 * N_EXPERTS + e]

    @pl.when(i == 0)
    def _():
        zero_ref[...] = jnp.zeros_like(zero_ref)

        def start(e, c):
            @pl.when(has_pad(e))
            def _():
                zero_fill(e).start()
            return c

        def wait(e, c):
            @pl.when(has_pad(e))
            def _():
                zero_fill(e).wait()
            return c

        lax.fori_loop(0, N_EXPERTS, start, 0)
        lax.fori_loop(0, N_EXPERTS, wait, 0)

        nvalid = seg_ref[3 * N_EXPERTS]
        nblk = xs_hbm.shape[0] // blk_rows

        def tail(j):
            first = pl.multiple_of((nvalid + j) * blk_rows, blk_rows)
            return pltpu.make_async_copy(zero_ref, xs_hbm.at[pl.ds(first, blk_rows)], zsem)

        def tail_start(j, c):
            @pl.when(nvalid + j < nblk)
            def _():
                tail(j).start()
            return c

        def tail_wait(j, c):
            @pl.when(nvalid + j < nblk)
            def _():
                tail(j).wait()
            return c

        lax.fori_loop(0, N_EXPERTS, tail_start, 0)
        lax.fori_loop(0, N_EXPERTS, tail_wait, 0)

    def row_copy(j, slot):
        return pltpu.make_async_copy(x_ref.at[pl.ds(pl.multiple_of(j * nsl, nsl), nsl)],
                                     xs_hbm.at[pl.ds(pl.multiple_of(slot * nsl, nsl), nsl)], sem)

    def issue(j, c):
        for k in range(2):
            row_copy(j, (d0_ref, d1_ref)[k][0, 0, j]).start(priority=k)
        return c

    lax.fori_loop(0, ntok, issue, 0, unroll=4)
    for _ in range(TOP_K):
        pltpu.make_async_copy(x_ref, xs_hbm.at[pl.ds(0, ntok * nsl)], sem).wait()


def _dispatch(x_tiles, dest, seg, cap, n):
    nsl = x_tiles.shape[0] // n
    tm = TOK_TM
    nt = n // tm
    return pl.pallas_call(
        _dispatch_kernel,
        out_shape=jax.ShapeDtypeStruct((cap * nsl, LANES), x_tiles.dtype),
        grid_spec=pltpu.PrefetchScalarGridSpec(
            num_scalar_prefetch=1,
            grid=(nt,),
            in_specs=[pl.BlockSpec((1, 1, tm), lambda i, seg: (i, 0, 0), memory_space=pltpu.SMEM),
                      pl.BlockSpec((1, 1, tm), lambda i, seg: (i, 0, 0), memory_space=pltpu.SMEM),
                      pl.BlockSpec((tm * nsl, LANES), lambda i, seg: (i, 0))],
            out_specs=pl.BlockSpec(memory_space=pl.ANY),
            scratch_shapes=[pltpu.VMEM((MOE_BLOCK * nsl, LANES), x_tiles.dtype), pltpu.SemaphoreType.DMA,
                            pltpu.SemaphoreType.DMA]),
        compiler_params=_params(("arbitrary",)),
        name="moe_dispatch",
    )(seg, dest[0], dest[1], x_tiles)


def _expert_kernel(blk_ref, nvalid_ref, xs_ref, wg_ref, wu_ref, wd_ref, ys_ref, wgb, wub, wdb):
    i = pl.program_id(0)
    prev = blk_ref[jnp.maximum(i - 1, 0)]
    valid = i < nvalid_ref[0]

    @pl.when(valid & ((i == 0) | (blk_ref[i] != prev)))
    def _():
        wgb[...] = wg_ref[0].astype(_BF16)
        wub[...] = wu_ref[0].astype(_BF16)
        wdb[...] = wd_ref[0].astype(_BF16)

    @pl.when(valid)
    def _():
        d = wgb.shape[0]
        xb = _rows_from_tiles(xs_ref, MOE_BLOCK, d).astype(_BF16)
        gate = jnp.dot(xb, wgb[...], preferred_element_type=_F32)
        up = jnp.dot(xb, wub[...], preferred_element_type=_F32)
        mid = (gate * jax.nn.sigmoid(gate)) * up
        _rows_to_tiles(ys_ref, jnp.dot(mid.astype(_BF16), wdb[...], preferred_element_type=_F32))

    @pl.when(jnp.logical_not(valid))
    def _():
        ys_ref[...] = jnp.zeros_like(ys_ref)


def _experts(xs, blk_exp, nvalid, w_gate, w_up, w_down):
    d, de = w_gate.shape[1], w_gate.shape[2]
    nsl = d // LANES
    cap = xs.shape[0] // nsl
    nblk = cap // MOE_BLOCK

    def xmap(i, blk, nv):
        return (jnp.minimum(i, nv[0] - 1), 0)

    def wmap(i, blk, nv):
        return (blk[i], 0, 0)

    return pl.pallas_call(
        _expert_kernel,
        out_shape=jax.ShapeDtypeStruct((cap * nsl, LANES), _F32),
        grid_spec=pltpu.PrefetchScalarGridSpec(
            num_scalar_prefetch=2,
            grid=(nblk,),
            in_specs=[pl.BlockSpec((MOE_BLOCK * nsl, LANES), xmap),
                      pl.BlockSpec((1, d, de), wmap), pl.BlockSpec((1, d, de), wmap),
                      pl.BlockSpec((1, de, d), wmap)],
            out_specs=pl.BlockSpec((MOE_BLOCK * nsl, LANES), lambda i, blk, nv: (i, 0)),
            scratch_shapes=[pltpu.VMEM((d, de), _BF16), pltpu.VMEM((d, de), _BF16), pltpu.VMEM((de, d), _BF16)]),
        compiler_params=_params(("arbitrary",)),
        name="moe_experts",
    )(blk_exp, nvalid, xs, w_gate, w_up, w_down)


def _combine_kernel(c0_ref, c1_ref, n0_ref, n1_ref, ys_hbm, h_ref, route_ref, g_ref, b_ref, o_ref, buf, sem):
    i = pl.program_id(0)
    n = pl.num_programs(0)
    tm = TOK_TM

    d = o_ref.shape[1]
    nsl = d // LANES

    def tile_copy(src_slot, slot, k, j):
        return pltpu.make_async_copy(ys_hbm.at[pl.ds(pl.multiple_of(src_slot * nsl, nsl), nsl)],
                                     buf.at[slot, k, pl.ds(pl.multiple_of(j * nsl, nsl), nsl)], sem.at[slot])

    def gather(drefs, slot):
        def body(j, c):
            for k in range(2):
                tile_copy(drefs[k][0, 0, j], slot, k, j).start(priority=k)
            return c
        lax.fori_loop(0, tm, body, 0, unroll=4)

    @pl.when(i == 0)
    def _():
        gather((c0_ref, c1_ref), 0)

    @pl.when(i + 1 < n)
    def _():
        gather((n0_ref, n1_ref), (i + 1) % 2)

    slot = i % 2

    for k in range(TOP_K):
        pltpu.make_async_copy(ys_hbm.at[pl.ds(0, tm * nsl)], buf.at[slot, k], sem.at[slot]).wait()
    r = route_ref[...]
    y = (r[:, 2:3] * _rows_from_tiles(buf.at[slot, 0], tm, d)
         + r[:, 3:4] * _rows_from_tiles(buf.at[slot, 1], tm, d))
    o_ref[...] = _layer_norm(ALPHA * _rows_from_tiles(h_ref, tm, d) + y, g_ref[...], b_ref[...])


def _combine(ys, dest, h1_tiles, route, g, b):
    n = route.shape[0]
    d = g.shape[1]
    nsl = d // LANES
    tm = TOK_TM
    nt = n // tm
    row = lambda i: (i, 0)
    const = lambda i: (0, 0)
    return pl.pallas_call(
        _combine_kernel,
        out_shape=jax.ShapeDtypeStruct((n, d), _F32),
        grid=(nt,),
        in_specs=[pl.BlockSpec((1, 1, tm), lambda i: (i, 0, 0), memory_space=pltpu.SMEM)] * TOP_K
        + [pl.BlockSpec((1, 1, tm), lambda i: (jnp.minimum(i + 1, nt - 1), 0, 0), memory_space=pltpu.SMEM)] * TOP_K
        + [pl.BlockSpec(memory_space=pl.ANY),
                  pl.BlockSpec((tm * nsl, LANES), row), pl.BlockSpec((tm, LANES), row),
                  pl.BlockSpec((1, d), const), pl.BlockSpec((1, d), const)],
        out_specs=pl.BlockSpec((tm, d), row),
        scratch_shapes=[pltpu.VMEM((2, 2, tm * nsl, LANES), _F32), pltpu.SemaphoreType.DMA((2,))],
        compiler_params=_params(("arbitrary",)),
        name="moe_combine",
    )(dest[0], dest[1], dest[0], dest[1], ys, h1_tiles, route, g, b)


def _moe(h1_tiles, route, route_t, counts_row, w_gate, w_up, w_down, g, b):
    n = route.shape[0]
    cap = 2 * n + N_EXPERTS * MOE_BLOCK
    counts = counts_row[0, ROUTE_LANE0:ROUTE_LANE0 + N_EXPERTS].astype(jnp.int32)
    padded = (counts + MOE_BLOCK - 1) // MOE_BLOCK * MOE_BLOCK
    ends = jnp.cumsum(padded)
    pad_start = ends - padded
    nblk = cap // MOE_BLOCK
    blk_first = jnp.arange(nblk, dtype=jnp.int32) * MOE_BLOCK
    blk_exp = jnp.minimum(jnp.sum(blk_first[:, None] >= ends[None, :], axis=1), N_EXPERTS - 1).astype(jnp.int32)
    nvalid = (ends[-1:] // MOE_BLOCK).astype(jnp.int32)
    dest = _dest(route_t, pad_start.astype(_F32)).reshape(TOP_K, n // TOK_TM, 1, TOK_TM)
    seg = jnp.concatenate([pad_start, padded, counts, nvalid]).astype(jnp.int32)
    xs = _dispatch(h1_tiles, dest, seg, cap, n)
    ys = _experts(xs, blk_exp, nvalid, w_gate, w_up, w_down)
    return _combine(ys, dest, h1_tiles, route, g, b)


def _router_params(rg_w, rg_b, re_w, re_b):
    d = rg_w.shape[0]
    pad = LANES - N_GROUPS - N_EXPERTS
    rw = jnp.concatenate([rg_w, re_w, jnp.zeros((d, pad), _F32)], axis=1)
    rb = jnp.concatenate([rg_b, re_b.reshape(-1), jnp.zeros((pad,), _F32)]).reshape(1, LANES)
    return rw, rb


def kernel(x, t5_table, l0_w_qkv, l0_sink, l0_w_o, l0_ln1_g, l0_ln1_b, l0_rg_w, l0_rg_b, l0_re_w, l0_re_b, l0_w_gate, l0_w_up, l0_w_down, l0_ln2_g, l0_ln2_b, l1_w_qkv, l1_w_o, l1_ln1_g, l1_ln1_b, l1_rg_w, l1_rg_b, l1_re_w, l1_re_b, l1_w_gate, l1_w_up, l1_w_down, l1_ln2_g, l1_ln2_b, l2_w_qkv, l2_rpb, l2_w_o, l2_ln1_g, l2_ln1_b, l2_rg_w, l2_rg_b, l2_re_w, l2_re_b, l2_w_gate, l2_w_up, l2_w_down, l2_ln2_g, l2_ln2_b, l3_w_qkv, l3_sink, l3_w_o, l3_ln1_g, l3_ln1_b, l3_rg_w, l3_rg_b, l3_re_w, l3_re_b, l3_w_gate, l3_w_up, l3_w_down, l3_ln2_g, l3_ln2_b):
    mixers = [(l0_w_qkv, l0_sink, l0_w_o), (l1_w_qkv, None, l1_w_o), (l2_w_qkv, l2_rpb, l2_w_o),
              (l3_w_qkv, l3_sink, l3_w_o)]
    norm1 = [(l0_ln1_g, l0_ln1_b), (l1_ln1_g, l1_ln1_b), (l2_ln1_g, l2_ln1_b), (l3_ln1_g, l3_ln1_b)]
    routers = [(l0_rg_w, l0_rg_b, l0_re_w, l0_re_b), (l1_rg_w, l1_rg_b, l1_re_w, l1_re_b),
               (l2_rg_w, l2_rg_b, l2_re_w, l2_re_b), (l3_rg_w, l3_rg_b, l3_re_w, l3_re_b)]
    experts = [(l0_w_gate, l0_w_up, l0_w_down), (l1_w_gate, l1_w_up, l1_w_down),
               (l2_w_gate, l2_w_up, l2_w_down), (l3_w_gate, l3_w_up, l3_w_down)]
    norm2 = [(l0_ln2_g, l0_ln2_b), (l1_ln2_g, l1_ln2_b), (l2_ln2_g, l2_ln2_b), (l3_ln2_g, l3_ln2_b)]

    bsz, seq, d = x.shape
    n = bsz * seq
    inner = N_HEADS * HEAD_DIM
    scale = HEAD_DIM ** -0.5 * LOG2E
    h = x.reshape(n, d)
    bias_a = _band_bias_t(t5_table, A_BLOCK, A_WINDOW, 1, A_BAND)
    for i in range(DEPTH):
        kind = i % N_MIXERS
        w_qkv, extra, w_o = mixers[i]
        kvw = A_KV_HEADS * HEAD_DIM if kind == 0 else inner
        lses, unperm = (), None
        if kind == 1:
            perms = np.stack([_residue_perm(dil) for _, dil in B_BRANCHES if dil > 1])
            outs = _project(h, w_qkv.astype(_BF16), (inner, kvw, kvw), (scale, 1.0, 1.0), jnp.asarray(perms, _BF16))
            unperm = jnp.asarray(perms.transpose(0, 2, 1), _BF16)
            attn, lses = [], []
            for bi, (window, dil) in enumerate(B_BRANCHES):
                q, k, v = outs[3 * bi:3 * bi + 3]
                o, lse = _mixer_b_branch(q, k, v, t5_table, bsz, seq, window, dil)
                attn.append(o)
                lses.append(jnp.pad(lse.reshape(n, N_HEADS), ((0, 0), (0, LANES - N_HEADS))))
        else:
            q, k, v = _project(h, w_qkv.astype(_BF16), (inner, kvw, kvw), (scale, 1.0, 1.0))
            q, k, v = (t.reshape(bsz, seq, t.shape[1]) for t in (q, k, v))
            if kind == 0:
                attn = [_mixer_a(q, k, v, extra.astype(_F32) * LOG2E, bias_a, bsz, seq)]
            else:
                attn = [_mixer_c(q, k, v, _na_bias_t(extra, seq // GRID_W), bsz, seq)]
        attn = [a.reshape(n, inner) for a in attn]
        g1, b1 = norm1[i]
        rw, rb = _router_params(*routers[i])
        h1, route, route_t, counts = _post_attn(attn, lses, h, w_o.astype(_BF16), g1.reshape(1, d), b1.reshape(1, d),
                                                rw, rb, unperm)
        g2, b2 = norm2[i]
        h = _moe(h1, route, route_t, counts, *experts[i], g2.reshape(1, d), b2.reshape(1, d))
    return h.reshape(bsz, seq, d)
```

```python
import functools
import math

import jax
import jax.numpy as jnp
import numpy as np
from jax import lax
from jax.experimental import pallas as pl
from jax.experimental.pallas import tpu as pltpu

HEAD_DIM = 64
N_HEADS = 16
A_KV_HEADS = 4
A_WINDOW = 128
A_BLOCK = 128
B_BRANCHES = ((128, 1), (512, 4), (2048, 16))
GRID_W = 64
NA_ROWS = 8
NA_COLS = 16
T5_BUCKETS = 32
T5_MAX_DIST = 1024
N_GROUPS = 4
EXPERTS_PER_GROUP = 8
N_EXPERTS = N_GROUPS * EXPERTS_PER_GROUP
TOP_K = 2
DEPTH = 4
N_MIXERS = 3
ALPHA = (2 * DEPTH) ** 0.25
LN_EPS = 1e-5
NEG_INF = -1e30

LANES = 128
PAIR = 2 * HEAD_DIM
BF16_SUBLANES = 16
VT_ROWS = PAIR + BF16_SUBLANES
LOG2E = math.log2(math.e)
LN2 = math.log(2.0)
VMEM_LIMIT = 56 * 1024 * 1024

PROJ_TM = 512
PROJ_CHUNK = 512
POST_TM = 1024
POST_SUB = 256
TOK_TM = 1024
DEST_TM = 8192
ROUTE_ROWS = 8
MOE_BLOCK = 512
A_TQ = 2048
A_BAND = 3
A_SUB_PER_GROUP = 4
B_BLOCK = 128
RESIDUE_RUN = BF16_SUBLANES
PERM_CHUNK = 256
TRANSPOSE_UNROLL = 8
UNITS_PER_GROUP = 16
C_UNIT_ROWS = 2
C_BAND_ROWS = 10
C_UNITS_PER_GROUP = 8
C_ROWS_PER_STEP = 64
ROUTE_LANE0 = N_GROUPS

_F32 = jnp.float32
_BF16 = jnp.bfloat16


def _params(sem, vmem=VMEM_LIMIT, flags=None):
    return pltpu.CompilerParams(dimension_semantics=sem, vmem_limit_bytes=vmem, flags=flags)


def _proj_kernel(x_ref, w_ref, *refs, widths, scales, nperm):
    perm_ref = refs[0] if nperm else None
    o_refs = refs[1:] if nperm else refs
    nout = len(widths)
    xb = x_ref[...].astype(_BF16)
    col = 0
    for t, (width, scale) in enumerate(zip(widths, scales)):
        for c in range(0, width, PROJ_CHUNK):
            cw = min(PROJ_CHUNK, width - c)
            acc = jnp.dot(xb, w_ref[:, col + c:col + c + cw], preferred_element_type=_F32)
            if scale != 1.0:
                acc = acc * scale
            ob = acc.astype(_BF16)
            o_refs[t][:, c:c + cw] = ob
            for p in range(nperm):
                for r0 in range(0, ob.shape[0], PERM_CHUNK):
                    moved = jnp.dot(perm_ref[p], ob[r0:r0 + PERM_CHUNK], preferred_element_type=_F32)
                    o_refs[(p + 1) * nout + t][r0:r0 + PERM_CHUNK, c:c + cw] = moved.astype(_BF16)
        col += width


def _project(x2, w_bf16, widths, scales, perms=None):
    n, d = x2.shape
    ctot = sum(widths)
    nperm = 0 if perms is None else perms.shape[0]
    args = [x2, w_bf16] + ([perms] if nperm else [])
    specs = [pl.BlockSpec((PROJ_TM, d), lambda i: (i, 0)), pl.BlockSpec((d, ctot), lambda i: (0, 0))]
    if nperm:
        specs.append(pl.BlockSpec(perms.shape, lambda i: (0, 0, 0)))
    return pl.pallas_call(
        functools.partial(_proj_kernel, widths=tuple(widths), scales=tuple(scales), nperm=nperm),
        out_shape=[jax.ShapeDtypeStruct((n, w), _BF16) for w in widths] * (nperm + 1),
        grid=(n // PROJ_TM,),
        in_specs=specs,
        out_specs=[pl.BlockSpec((PROJ_TM, w), lambda i: (i, 0)) for w in widths] * (nperm + 1),
        compiler_params=_params(("arbitrary",)),
        name="qkv_proj",
    )(*args)


def _residue_perm(dil):
    span = dil * RESIDUE_RUN
    p = np.zeros((PERM_CHUNK, PERM_CHUNK), np.float32)
    for base in range(0, PERM_CHUNK, span):
        for r in range(dil):
            for u in range(RESIDUE_RUN):
                p[base + r * RESIDUE_RUN + u, base + u * dil + r] = 1.0
    return p


def _one():
    return jnp.ones((1,), jnp.int32)


def _lane_lo(rows=1):
    return lax.broadcasted_iota(jnp.int32, (rows, PAIR), 1) < HEAD_DIM


def _swap_halves(x):
    return pltpu.roll(x.astype(_F32), HEAD_DIM, axis=1).astype(_BF16)


class _PairUnit:
    def __init__(self, q2, k, vt, bias, emit, sink_a=None, sink_b=None):
        self.q2, self.k, self.vt, self.bias, self.emit = q2, k, vt, bias, emit
        self.sink_a, self.sink_b = sink_a, sink_b


def _attend_groups(ngroups, units_of, scratch, one):
    assert ngroups % 2 == 0
    lo = _lane_lo()
    nt = (((1,), (1,)), ((), ()))

    def sink_row(unit, m2):
        lane = lax.broadcasted_iota(jnp.int32, (1, m2), 1)
        return jnp.where(lane < m2 // 2, unit.sink_a, unit.sink_b)

    def scores(g, parity):
        s_scr, m_scr = scratch[parity]
        for u, unit in enumerate(units_of(g, parity)):
            q2 = unit.q2()
            zero = jnp.zeros_like(q2)
            qq = jnp.concatenate([jnp.where(lo, q2, zero), jnp.where(lo, zero, q2)], axis=0)
            s = lax.dot_general(unit.k(), qq, nt, preferred_element_type=_F32) + unit.bias()
            m = jnp.max(s, axis=0, keepdims=True)
            if unit.sink_a is not None:
                m = jnp.maximum(m, sink_row(unit, s.shape[1]))
            s_scr[u] = s
            m_scr[u] = m

    def values(g, parity):
        s_scr, m_scr = scratch[parity]
        for u, unit in enumerate(units_of(g, parity)):
            m = m_scr[u]
            e = jnp.exp2(s_scr[u] - m)
            ot = jnp.dot(unit.vt(), e.astype(_BF16), preferred_element_type=_F32)
            den = ot[PAIR:PAIR + 1, :]
            if unit.sink_a is not None:
                den = den + jnp.exp2(sink_row(unit, e.shape[1]) - m)
            ot = ot[:PAIR] * (1.0 / den)
            mq = ot.shape[1] // 2
            top = lax.broadcasted_iota(jnp.int32, (PAIR, 1), 0) < HEAD_DIM
            pair_t = jnp.where(top, ot[:, :mq], ot[:, mq:])
            unit.emit(pair_t.T, (m + jnp.log2(den)) * LN2)

    def region(*work):
        def body(_, carry):
            for fn, g, parity in work:
                fn(g, parity)
            return carry
        lax.fori_loop(0, one, body, 0)

    region((scores, 0, 0))

    def steady(i, carry):
        g = 2 * i
        region((values, g, 0), (scores, g + 1, 1))
        region((values, g + 1, 1), (scores, g + 2, 0))
        return carry

    lax.fori_loop(0, ngroups // 2 - 1, steady, 0)
    region((values, ngroups - 2, 0), (scores, ngroups - 1, 1))
    region((values, ngroups - 1, 1))


def _transpose_blocks(v_ref_rows, nblk, store):
    def body(j, carry):
        rows = pl.ds(pl.multiple_of(j * LANES, LANES), LANES)
        vt = v_ref_rows(rows).T
        store(j, jnp.concatenate([vt, jnp.ones((VT_ROWS - PAIR, LANES), _BF16)], axis=0))
        return carry
    lax.fori_loop(0, nblk, body, 0, unroll=TRANSPOSE_UNROLL)


def _t5_bucket(rel):
    half = T5_BUCKETS // 2
    exact = half // 2
    n = np.abs(rel)
    big = exact + (np.log(np.maximum(n, 1) / exact) / math.log(T5_MAX_DIST / exact) * (half - exact)).astype(np.int32)
    return ((rel > 0) * half + np.where(n < exact, n, np.minimum(big, half - 1))).astype(np.int32)


def _band_bias_t(t5_table, blk, window, dil, nband):
    tk = nband * blk
    off = tk - 1
    rel = np.arange(-off, tk)
    vec = jnp.take(t5_table, jnp.asarray(_t5_bucket(rel * dil)), axis=0).astype(_F32)
    vec = jnp.where(jnp.asarray(np.abs(rel) <= window)[:, None], vec * LOG2E, NEG_INF).T
    h = vec.shape[0]
    vec = jnp.pad(vec, ((0, 0), (0, 1)))
    w = 2 * tk - 1
    skew = jnp.tile(vec, (1, blk))[:, :blk * w].reshape(h, blk, w)
    t = jnp.stack([skew[:, :, off - v * blk:off - v * blk + tk] for v in range(nband)], axis=1)
    t = t.reshape(h // 2, 2, nband, blk, tk).transpose(0, 2, 4, 1, 3)
    return t.reshape(h // 2, nband, tk, 2 * blk)


def _band_window(gb, blk, length, nband):
    first = jnp.clip(gb - 1, 0, length // blk - nband)
    return first, gb - first


def _attn_a_kernel(sink_ref, one_ref, q_ref, k_ref, v_ref, bias_ref, o_ref, kd_ref, vdt_ref,
                   s0_scr, m0_scr, s1_scr, m1_scr, *, seq):
    kvb = pl.program_id(0)
    qt = pl.program_id(2)
    one = one_ref[0]
    nblk = seq // A_BLOCK

    @pl.when(qt == 0)
    def _():
        lo = _lane_lo()

        def fill(j, carry):
            rows = pl.ds(pl.multiple_of(j * A_BLOCK, A_BLOCK), A_BLOCK)
            kk = k_ref[0, rows, :]
            ks = _swap_halves(kk)
            kd_ref[0, rows, :] = jnp.where(lo, kk, ks)
            kd_ref[1, rows, :] = jnp.where(lo, ks, kk)
            vt = v_ref[0, rows, :].T
            ones = jnp.ones((VT_ROWS - PAIR, A_BLOCK), _BF16)
            vdt_ref[0, j] = jnp.concatenate([vt[:HEAD_DIM], vt[:HEAD_DIM], ones], axis=0)
            vdt_ref[1, j] = jnp.concatenate([vt[HEAD_DIM:], vt[HEAD_DIM:], ones], axis=0)
            return carry

        lax.fori_loop(0, nblk, fill, 0, unroll=TRANSPOSE_UNROLL)

    nsub = A_TQ // A_BLOCK
    npair = q_ref.shape[2] // PAIR
    hbase = kvb * (2 * npair)

    def units_of(g, parity):
        units = []
        for j in range(A_SUB_PER_GROUP):
            si = g * A_SUB_PER_GROUP + j
            first, var = _band_window(qt * nsub + si, A_BLOCK, seq, A_BAND)
            rows = pl.ds(pl.multiple_of(si * A_BLOCK, A_BLOCK), A_BLOCK)
            band = pl.ds(pl.multiple_of(first * A_BLOCK, A_BLOCK), A_BAND * A_BLOCK)
            for jj in range(npair):
                hk = jj // (npair // 2)
                cols = slice(jj * PAIR, (jj + 1) * PAIR)

                def emit(o, lse, rows=rows, cols=cols):
                    o_ref[0, rows, cols] = o.astype(o_ref.dtype)

                units.append(_PairUnit(
                    q2=lambda rows=rows, cols=cols: q_ref[0, rows, cols],
                    k=lambda hk=hk, band=band: kd_ref[hk, band, :],
                    vt=lambda hk=hk, first=first: jnp.concatenate(
                        [vdt_ref[hk, first + t] for t in range(A_BAND)], axis=1),
                    bias=lambda jj=jj, var=var: bias_ref[jj, var],
                    emit=emit, sink_a=sink_ref[hbase + 2 * jj], sink_b=sink_ref[hbase + 2 * jj + 1]))
        return units

    _attend_groups(nsub // A_SUB_PER_GROUP, units_of, ((s0_scr, m0_scr), (s1_scr, m1_scr)), one)


def _mixer_a(q, k, v, sink, bias, bsz, seq):
    nkvb = k.shape[2] // PAIR
    qw = q.shape[2] // nkvb
    npair = qw // PAIR
    tk = A_BAND * A_BLOCK
    nunits = A_SUB_PER_GROUP * npair
    group_scratch = [pltpu.VMEM((nunits, tk, 2 * A_BLOCK), _F32), pltpu.VMEM((nunits, 1, 2 * A_BLOCK), _F32)]
    return pl.pallas_call(
        functools.partial(_attn_a_kernel, seq=seq),
        out_shape=jax.ShapeDtypeStruct(q.shape, _BF16),
        grid=(nkvb, bsz, seq // A_TQ),
        in_specs=[pl.BlockSpec(memory_space=pltpu.SMEM), pl.BlockSpec(memory_space=pltpu.SMEM),
                  pl.BlockSpec((1, A_TQ, qw), lambda c, b, t: (b, t, c)),
                  pl.BlockSpec((1, seq, PAIR), lambda c, b, t: (b, 0, c)),
                  pl.BlockSpec((1, seq, PAIR), lambda c, b, t: (b, 0, c)),
                  pl.BlockSpec((npair, A_BAND, tk, 2 * A_BLOCK), lambda c, b, t: (c, 0, 0, 0))],
        out_specs=pl.BlockSpec((1, A_TQ, qw), lambda c, b, t: (b, t, c)),
        scratch_shapes=[pltpu.VMEM((2, seq, PAIR), _BF16), pltpu.VMEM((2, seq // A_BLOCK, VT_ROWS, A_BLOCK), _BF16)]
        + group_scratch + group_scratch,
        compiler_params=_params(("arbitrary", "arbitrary", "arbitrary")),
        name="mixer_a",
    )(sink, _one(), q, k, v, bias)


def _attn_band_kernel(one_ref, q_ref, k_ref, v_ref, bias_ref, o_ref, lse_ref, vt_ref,
                      s0_scr, m0_scr, s1_scr, m1_scr, *, length, npairs, nband, chunked):
    col = pl.program_id(2)
    one = one_ref[0]
    nblk = length // B_BLOCK
    runs = B_BLOCK // RESIDUE_RUN

    def load(ref, blk0, nblocks, cols):
        if chunked:
            r0 = pl.multiple_of(blk0 * runs, runs)
            return ref[pl.ds(r0, nblocks * runs), :, cols].reshape(nblocks * B_BLOCK, PAIR)
        return ref[0, pl.ds(pl.multiple_of(blk0 * B_BLOCK, B_BLOCK), nblocks * B_BLOCK), cols]

    def store_out(blk0, cols, val):
        if chunked:
            o_ref[pl.ds(pl.multiple_of(blk0 * runs, runs), runs), :, cols] = val.reshape(runs, RESIDUE_RUN, PAIR)
        else:
            o_ref[0, pl.ds(pl.multiple_of(blk0 * B_BLOCK, B_BLOCK), B_BLOCK), cols] = val

    pairs_per_loop = max(1, min(npairs, TRANSPOSE_UNROLL // min(TRANSPOSE_UNROLL, nblk) * 2))
    for p0 in range(0, npairs, pairs_per_loop):
        def fill(j, carry, p0=p0):
            for p in range(p0, p0 + pairs_per_loop):
                vt = load(v_ref, j, 1, slice(p * PAIR, (p + 1) * PAIR)).T
                vt_ref[p, j] = jnp.concatenate([vt, jnp.ones((VT_ROWS - PAIR, B_BLOCK), _BF16)], axis=0)
            return carry

        lax.fori_loop(0, nblk, fill, 0, unroll=min(TRANSPOSE_UNROLL, nblk))

    def unit(gb, p):
        first, var = _band_window(gb, B_BLOCK, length, nband)
        cols = slice(p * PAIR, (p + 1) * PAIR)

        def emit(o, lse):
            store_out(gb, cols, o.astype(o_ref.dtype))
            ha = col * (2 * npairs) + 2 * p
            lse_ref[0, 0, gb, pl.ds(ha, 1), :] = lse[:, :B_BLOCK]
            lse_ref[0, 0, gb, pl.ds(ha + 1, 1), :] = lse[:, B_BLOCK:]

        return _PairUnit(
            q2=lambda: load(q_ref, gb, 1, cols), k=lambda: load(k_ref, first, nband, cols),
            vt=lambda: jnp.concatenate([vt_ref[p, first + t] for t in range(nband)], axis=1),
            bias=lambda: bias_ref[p, var], emit=emit)

    per_group = _band_blocks_per_group(nblk, npairs)
    units_of = lambda g, parity: [unit(g * per_group + j, p) for j in range(per_group) for p in range(npairs)]
    _attend_groups(nblk // per_group, units_of, ((s0_scr, m0_scr), (s1_scr, m1_scr)), one)


def _band_blocks_per_group(nblk, npairs):
    return max(1, min(UNITS_PER_GROUP // npairs, nblk // 2))


def _mixer_b_branch(q, k, v, t5_table, bsz, seq, window, dil):
    length = seq // dil
    width = q.shape[1]
    nblk = length // B_BLOCK
    nband = min(3, nblk)
    tk = nband * B_BLOCK
    npairs = (width // PAIR) if length * width * 2 <= 2 * 1024 * 1024 else 1
    ncol = width // (npairs * PAIR)
    bw = npairs * PAIR
    nunits = _band_blocks_per_group(nblk, npairs) * npairs
    bias = _band_bias_t(t5_table, B_BLOCK, window // 2 // dil, dil, nband)
    chunked = dil > 1
    if chunked:
        nrun = length // RESIDUE_RUN
        shape = (bsz * nrun, dil, RESIDUE_RUN, width)
        spec = pl.BlockSpec((nrun, None, RESIDUE_RUN, bw), lambda b, r, c: (b, r, 0, c))
    else:
        shape = (bsz, seq, width)
        spec = pl.BlockSpec((1, seq, bw), lambda b, r, c: (b, 0, c))

    o, lse = pl.pallas_call(
        functools.partial(_attn_band_kernel, length=length, npairs=npairs, nband=nband, chunked=chunked),
        out_shape=[jax.ShapeDtypeStruct(shape, _BF16),
                   jax.ShapeDtypeStruct((bsz, dil, nblk, N_HEADS, B_BLOCK), _F32)],
        grid=(bsz, dil, ncol),
        in_specs=[pl.BlockSpec(memory_space=pltpu.SMEM), spec, spec, spec,
                  pl.BlockSpec((npairs, nband, tk, 2 * B_BLOCK), lambda b, r, c: (c, 0, 0, 0))],
        out_specs=[spec, pl.BlockSpec((1, 1, nblk, N_HEADS, B_BLOCK), lambda b, r, c: (b, r, 0, 0, 0))],
        scratch_shapes=[pltpu.VMEM((npairs, nblk, VT_ROWS, B_BLOCK), _BF16)] + 2 * [
            pltpu.VMEM((nunits, tk, 2 * B_BLOCK), _F32), pltpu.VMEM((nunits, 1, 2 * B_BLOCK), _F32)],
        compiler_params=_params(("arbitrary", "arbitrary", "arbitrary")),
        name=f"mixer_b_dil{dil}",
    )(_one(), q.reshape(shape), k.reshape(shape), v.reshape(shape), bias)
    lse = lse.transpose(0, 2, 4, 1, 3).reshape(bsz, seq, N_HEADS)
    return o.reshape(bsz * seq, width), lse


def _na_bias_t(rpb, rows):
    canon = (0, 2, 4, rows - 4, rows - 2)
    h = rpb.shape[0]
    kc = np.arange(GRID_W)[:, None]
    qc = np.arange(GRID_W)[None, :]
    ws = np.clip(qc - NA_COLS // 2, 0, GRID_W - NA_COLS)
    col_ok = (kc >= ws) & (kc < ws + NA_COLS)
    col_idx = np.clip(kc - qc, 1 - NA_COLS, NA_COLS - 1) + NA_COLS - 1
    colpart = jnp.where(jnp.asarray(col_ok), rpb.astype(_F32)[:, :, jnp.asarray(col_idx)] * LOG2E, NEG_INF)
    masked = jnp.full((h, GRID_W, GRID_W), NEG_INF, _F32)
    variants = []
    for i in canon:
        start = int(np.clip(i - NA_ROWS // 2, 0, rows - C_BAND_ROWS))
        per_key_row = []
        for kr in range(C_BAND_ROWS):
            per_query_row = []
            for qr in range(C_UNIT_ROWS):
                iq, kabs = i + qr, start + kr
                first_key_row = int(np.clip(iq - NA_ROWS // 2, 0, rows - NA_ROWS))
                inside = first_key_row <= kabs < first_key_row + NA_ROWS
                per_query_row.append(colpart[:, kabs - iq + NA_ROWS - 1] if inside else masked)
            per_key_row.append(jnp.stack(per_query_row, axis=2))
        variants.append(jnp.stack(per_key_row, axis=1))
    b = jnp.stack(variants, axis=1)
    nq, nk = C_UNIT_ROWS * GRID_W, C_BAND_ROWS * GRID_W
    b = b.reshape(h // 2, 2, len(canon), nk, nq)
    return b.transpose(0, 2, 3, 1, 4).reshape(h // 2, len(canon), nk, 2 * nq)


def _attn_c_kernel(one_ref, q_ref, k_ref, v_ref, bias_ref, o_ref, vt_ref, s0_scr, m0_scr, s1_scr, m1_scr, *, rows):
    rt = pl.program_id(2)
    nq = C_UNIT_ROWS * GRID_W
    band_blocks = C_BAND_ROWS * GRID_W // LANES

    @pl.when(rt == 0)
    def _():
        def store(j, blk):
            vt_ref[j] = blk
        _transpose_blocks(lambda r: v_ref[0, r, :], rows * GRID_W // LANES, store)

    def unit(u):
        i = rt * C_ROWS_PER_STEP + C_UNIT_ROWS * u
        start = jnp.clip(i - NA_ROWS // 2, 0, rows - C_BAND_ROWS)
        var = lax.shift_right_logical(i - start, 1)
        first = lax.shift_right_logical(start, 1)
        keys = pl.ds(pl.multiple_of(first * LANES, LANES), C_BAND_ROWS * GRID_W)
        qrows = pl.ds(pl.multiple_of(u * nq, nq), nq)

        def emit(o, lse):
            o_ref[0, qrows, :] = o.astype(o_ref.dtype)

        return _PairUnit(
            q2=lambda: q_ref[0, qrows, :], k=lambda: k_ref[0, keys, :],
            vt=lambda: jnp.concatenate([vt_ref[first + t] for t in range(band_blocks)], axis=1),
            bias=lambda: bias_ref[0, var], emit=emit)

    ngroups = C_ROWS_PER_STEP // C_UNIT_ROWS // C_UNITS_PER_GROUP
    units_of = lambda g, parity: [unit(g * C_UNITS_PER_GROUP + j) for j in range(C_UNITS_PER_GROUP)]
    _attend_groups(ngroups, units_of, ((s0_scr, m0_scr), (s1_scr, m1_scr)), one_ref[0])


def _mixer_c(q, k, v, bias, bsz, seq):
    rows = seq // GRID_W
    tq = C_ROWS_PER_STEP * GRID_W
    npair = q.shape[2] // PAIR
    nq, nk = C_UNIT_ROWS * GRID_W, C_BAND_ROWS * GRID_W
    return pl.pallas_call(
        functools.partial(_attn_c_kernel, rows=rows),
        out_shape=jax.ShapeDtypeStruct(q.shape, _BF16),
        grid=(npair, bsz, seq // tq),
        in_specs=[pl.BlockSpec(memory_space=pltpu.SMEM),
                  pl.BlockSpec((1, tq, PAIR), lambda c, b, t: (b, t, c)),
                  pl.BlockSpec((1, seq, PAIR), lambda c, b, t: (b, 0, c)),
                  pl.BlockSpec((1, seq, PAIR), lambda c, b, t: (b, 0, c)),
                  pl.BlockSpec((1, bias.shape[1], nk, 2 * nq), lambda c, b, t: (c, 0, 0, 0))],
        out_specs=pl.BlockSpec((1, tq, PAIR), lambda c, b, t: (b, t, c)),
        scratch_shapes=[pltpu.VMEM((seq // LANES, VT_ROWS, LANES), _BF16)] + 2 * [
            pltpu.VMEM((C_UNITS_PER_GROUP, nk, 2 * nq), _F32), pltpu.VMEM((C_UNITS_PER_GROUP, 1, 2 * nq), _F32)],
        compiler_params=_params(("arbitrary", "arbitrary", "arbitrary")),
        name="mixer_c",
    )(_one(), q, k, v, bias)


def _layer_norm(z, g, b):
    mu = jnp.mean(z, axis=-1, keepdims=True)
    zc = z - mu
    var = jnp.mean(zc * zc, axis=-1, keepdims=True)
    return zc * lax.rsqrt(var + LN_EPS) * g + b


def _rows_to_tiles(ref, x):
    nrows, d = x.shape
    nsl = d // LANES
    for s in range(nsl):
        ref[pl.ds(s, nrows, stride=nsl), :] = x[:, s * LANES:(s + 1) * LANES]


def _rows_from_tiles(ref, nrows, d):
    nsl = d // LANES
    return jnp.concatenate([ref[pl.ds(s, nrows, stride=nsl), :] for s in range(nsl)], axis=1)


def _split_bf16(x):
    hi = x.astype(_BF16)
    return hi, (x - hi.astype(_F32)).astype(_BF16)


def _router_logits(hn, rw_ref, rb_ref):
    h_hi, h_lo = _split_bf16(hn)
    w_hi, w_lo = _split_bf16(rw_ref[...])
    both = jnp.dot(h_hi, jnp.concatenate([w_hi, w_lo], axis=1), preferred_element_type=_F32)
    return (both[:, :LANES]
            + (both[:, LANES:] + jnp.dot(h_lo, w_hi, preferred_element_type=_F32))) + rb_ref[...]


def _route(logits, tri_ref, carry_ref):
    tm = logits.shape[0]
    lane = lax.broadcasted_iota(jnp.int32, (tm, LANES), 1).astype(_F32)
    ninf = -jnp.inf
    is_grp = lane < N_GROUPS
    lg = jnp.where(is_grp, logits, ninf)
    gmax = jnp.max(lg, axis=-1, keepdims=True)
    grp = jnp.min(jnp.where(lg == gmax, lane, float(LANES)), axis=-1, keepdims=True)
    p_grp = 1.0 / jnp.sum(jnp.where(is_grp, jnp.exp(lg - gmax), 0.0), axis=-1, keepdims=True)
    lo_lane = ROUTE_LANE0 + EXPERTS_PER_GROUP * grp
    in_grp = (lane >= lo_lane) & (lane < lo_lane + EXPERTS_PER_GROUP)
    le = jnp.where(in_grp, logits, ninf)
    v1 = jnp.max(le, axis=-1, keepdims=True)
    i1 = jnp.min(jnp.where(le == v1, lane, float(LANES)), axis=-1, keepdims=True)
    le2 = jnp.where(lane == i1, ninf, le)
    v2 = jnp.max(le2, axis=-1, keepdims=True)
    i2 = jnp.min(jnp.where(le2 == v2, lane, float(LANES)), axis=-1, keepdims=True)
    t = jnp.exp(v2 - v1)
    s1 = 1.0 / (1.0 + t)
    g1 = p_grp * s1
    g2 = p_grp * (t * s1)
    sel1 = lane == i1
    sel2 = lane == i2
    onehot = jnp.where(sel1 | sel2, 1.0, 0.0)
    sub = tri_ref.shape[0]
    carry = carry_ref[...]
    parts = []
    for r0 in range(0, tm, sub):
        part = onehot[r0:r0 + sub]
        parts.append(jnp.dot(tri_ref[...], part.astype(_BF16), preferred_element_type=_F32) + carry)
        carry = carry + jnp.sum(part, axis=0, keepdims=True)
    carry_ref[...] = carry
    before = jnp.concatenate(parts, axis=0) if len(parts) > 1 else parts[0]
    rank1 = jnp.sum(jnp.where(sel1, before, 0.0), axis=-1, keepdims=True)
    rank2 = jnp.sum(jnp.where(sel2, before, 0.0), axis=-1, keepdims=True)
    out = jnp.zeros((tm, LANES), _F32)
    for idx, val in enumerate((i1 - ROUTE_LANE0, i2 - ROUTE_LANE0, g1, g2, rank1, rank2)):
        out = jnp.where(lane == idx, val, out)
    return out


def _post_attn_kernel(*refs, n_branch):
    attn_refs = refs[:n_branch]
    pos = n_branch
    if n_branch > 1:
        lse_refs = refs[pos:pos + n_branch]
        expand_ref, unperm_ref = refs[pos + n_branch:pos + n_branch + 2]
        pos += n_branch + 2
    wo_ref, h_ref, g_ref, b_ref, rw_ref, rb_ref, tri_ref = refs[pos:pos + 7]
    h1_ref, route_ref, routet_ref, counts_ref, carry_ref = refs[pos + 7:pos + 12]

    @pl.when(pl.program_id(0) == 0)
    def _():
        carry_ref[...] = jnp.zeros_like(carry_ref)

    d = h_ref.shape[1]
    logits = []
    for sub in range(h_ref.shape[0] // POST_SUB):
        rows = slice(sub * POST_SUB, (sub + 1) * POST_SUB)
        if n_branch == 1:
            a = attn_refs[0][rows, :]
        else:
            lses = [r[rows, :] for r in lse_refs]
            mx = functools.reduce(jnp.maximum, lses)
            es = [jnp.exp(l - mx) for l in lses]
            inv = 1.0 / functools.reduce(jnp.add, es)
            mixed = None
            for bi, (e, a_ref) in enumerate(zip(es, attn_refs)):
                w_hi, w_lo = _split_bf16(e * inv)
                w_full = jnp.dot(jnp.concatenate([w_hi, w_lo], axis=1), expand_ref[...],
                                 preferred_element_type=_F32)
                if bi == 0:
                    a_nat = a_ref[rows, :].astype(_F32)
                else:
                    a_nat = jnp.dot(unperm_ref[bi - 1], a_ref[rows, :], preferred_element_type=_F32)
                term = w_full * a_nat
                mixed = term if mixed is None else mixed + term
            a = mixed.astype(_BF16)
        y = jnp.dot(a, wo_ref[...], preferred_element_type=_F32)
        hn = _layer_norm(ALPHA * h_ref[rows, :] + y, g_ref[...], b_ref[...])
        nsl = d // LANES
        _rows_to_tiles(h1_ref.at[pl.ds(sub * POST_SUB * nsl, POST_SUB * nsl)], hn)
        logits.append(_router_logits(hn, rw_ref, rb_ref))
    route = _route(jnp.concatenate(logits, axis=0), tri_ref, carry_ref)
    route_ref[...] = route
    routet_ref[...] = route.T[:ROUTE_ROWS]
    counts_ref[...] = carry_ref[...]


def _post_attn(attn, lses, h2, w_o, g, b, rw, rb, unperm=None):
    n, d = h2.shape
    tm = POST_TM
    nb = len(attn)
    tri = jnp.asarray(np.tril(np.ones((POST_SUB, POST_SUB), np.float32), -1), _BF16)
    row = lambda i: (i, 0)
    const = lambda i: (0, 0)
    args = list(attn)
    specs = [pl.BlockSpec((tm, d), row) for _ in attn]
    if nb > 1:
        expand = np.zeros((2, LANES, d), np.float32)
        for hd in range(N_HEADS):
            expand[:, hd, hd * HEAD_DIM:(hd + 1) * HEAD_DIM] = 1.0
        args += list(lses) + [jnp.asarray(expand.reshape(2 * LANES, d), _BF16), unperm]
        specs += [pl.BlockSpec((tm, LANES), row) for _ in lses] + [
            pl.BlockSpec((2 * LANES, d), const), pl.BlockSpec(unperm.shape, lambda i: (0, 0, 0))]
    args += [w_o, h2, g, b, rw, rb, tri]
    specs += [pl.BlockSpec((d, d), const), pl.BlockSpec((tm, d), row), pl.BlockSpec((1, d), const),
              pl.BlockSpec((1, d), const), pl.BlockSpec((d, LANES), const), pl.BlockSpec((1, LANES), const),
              pl.BlockSpec((POST_SUB, POST_SUB), const)]
    return pl.pallas_call(
        functools.partial(_post_attn_kernel, n_branch=nb),
        out_shape=[jax.ShapeDtypeStruct((n * d // LANES, LANES), _F32), jax.ShapeDtypeStruct((n, LANES), _F32),
                   jax.ShapeDtypeStruct((ROUTE_ROWS, n), _F32),
                   jax.ShapeDtypeStruct((1, LANES), _F32)],
        grid=(n // tm,),
        in_specs=specs,
        out_specs=[pl.BlockSpec((tm * d // LANES, LANES), row), pl.BlockSpec((tm, LANES), row),
                   pl.BlockSpec((ROUTE_ROWS, tm), lambda i: (0, i)), pl.BlockSpec((1, LANES), const)],
        scratch_shapes=[pltpu.VMEM((1, LANES), _F32)],
        compiler_params=_params(("arbitrary",)),
        name="post_attn_router",
    )(*args)


def _dest_kernel(ps_ref, rt_ref, o_ref):
    expert = rt_ref[0:TOP_K, :]
    slot = rt_ref[4:4 + TOP_K, :]
    for e in range(N_EXPERTS):
        slot = slot + jnp.where(expert == float(e), ps_ref[e], 0.0)
    o_ref[...] = slot.astype(jnp.int32)


def _dest(route_t, pad_start):
    n = route_t.shape[1]
    tm = min(DEST_TM, n)
    return pl.pallas_call(
        _dest_kernel,
        out_shape=jax.ShapeDtypeStruct((TOP_K, n), jnp.int32),
        grid=(n // tm,),
        in_specs=[pl.BlockSpec(memory_space=pltpu.SMEM), pl.BlockSpec((ROUTE_ROWS, tm), lambda i: (0, i))],
        out_specs=pl.BlockSpec((TOP_K, tm), lambda i: (0, i)),
        compiler_params=_params(("arbitrary",)),
        name="moe_dest",
    )(pad_start, route_t)


def _dispatch_kernel(seg_ref, d0_ref, d1_ref, x_ref, xs_hbm, zero_ref, zsem, sem):
    i = pl.program_id(0)
    ntok = TOK_TM
    nsl = x_ref.shape[0] // ntok
    blk_rows = MOE_BLOCK * nsl

    def zero_fill(e):
        first = pl.multiple_of((seg_ref[e] + seg_ref[N_EXPERTS + e] - MOE_BLOCK) * nsl, blk_rows)
        return pltpu.make_async_copy(zero_ref, xs_hbm.at[pl.ds(first, blk_rows)], zsem)

    def has_pad(e):
        return seg_ref[N_EXPERTS + e] > seg_ref[2 * N_EXPERTS + e]

    @pl.when(i == 0)
    def _():
        zero_ref[...] = jnp.zeros_like(zero_ref)

        def start(e, c):
            @pl.when(has_pad(e))
            def _():
                zero_fill(e).start()
            return c

        def wait(e, c):
            @pl.when(has_pad(e))
            def _():
                zero_fill(e).wait()
            return c

        lax.fori_loop(0, N_EXPERTS, start, 0)
        lax.fori_loop(0, N_EXPERTS, wait, 0)

        nvalid = seg_ref[3 * N_EXPERTS]
        nblk = xs_hbm.shape[0] // blk_rows

        def tail(j):
            first = pl.multiple_of((nvalid + j) * blk_rows, blk_rows)
            return pltpu.make_async_copy(zero_ref, xs_hbm.at[pl.ds(first, blk_rows)], zsem)

        def tail_start(j, c):
            @pl.when(nvalid + j < nblk)
            def _():
                tail(j).start()
            return c

        def tail_wait(j, c):
            @pl.when(nvalid + j < nblk)
            def _():
                tail(j).wait()
            return c

        lax.fori_loop(0, N_EXPERTS, tail_start, 0)
        lax.fori_loop(0, N_EXPERTS, tail_wait, 0)

    def row_copy(j, slot):
        return pltpu.make_async_copy(x_ref.at[pl.ds(pl.multiple_of(j * nsl, nsl), nsl)],
                                     xs_hbm.at[pl.ds(pl.multiple_of(slot * nsl, nsl), nsl)], sem)

    def issue(j, c):
        for k in range(2):
            row_copy(j, (d0_ref, d1_ref)[k][0, 0, j]).start(priority=k)
        return c

    lax.fori_loop(0, ntok, issue, 0, unroll=4)
    for _ in range(TOP_K):
        pltpu.make_async_copy(x_ref, xs_hbm.at[pl.ds(0, ntok * nsl)], sem).wait()


def _dispatch(x_tiles, dest, seg, cap, n):
    nsl = x_tiles.shape[0] // n
    tm = TOK_TM
    nt = n // tm
    return pl.pallas_call(
        _dispatch_kernel,
        out_shape=jax.ShapeDtypeStruct((cap * nsl, LANES), x_tiles.dtype),
        grid_spec=pltpu.PrefetchScalarGridSpec(
            num_scalar_prefetch=1,
            grid=(nt,),
            in_specs=[pl.BlockSpec((1, 1, tm), lambda i, seg: (i, 0, 0), memory_space=pltpu.SMEM),
                      pl.BlockSpec((1, 1, tm), lambda i, seg: (i, 0, 0), memory_space=pltpu.SMEM),
                      pl.BlockSpec((tm * nsl, LANES), lambda i, seg: (i, 0))],
            out_specs=pl.BlockSpec(memory_space=pl.ANY),
            scratch_shapes=[pltpu.VMEM((MOE_BLOCK * nsl, LANES), x_tiles.dtype), pltpu.SemaphoreType.DMA,
                            pltpu.SemaphoreType.DMA]),
        compiler_params=_params(("arbitrary",)),
        name="moe_dispatch",
    )(seg, dest[0], dest[1], x_tiles)


def _expert_kernel(blk_ref, nvalid_ref, xs_ref, wg_ref, wu_ref, wd_ref, ys_ref, wgb, wub, wdb):
    i = pl.program_id(0)
    prev = blk_ref[jnp.maximum(i - 1, 0)]
    valid = i < nvalid_ref[0]

    @pl.when(valid & ((i == 0) | (blk_ref[i] != prev)))
    def _():
        wgb[...] = wg_ref[0].astype(_BF16)
        wub[...] = wu_ref[0].astype(_BF16)
        wdb[...] = wd_ref[0].astype(_BF16)

    @pl.when(valid)
    def _():
        d = wgb.shape[0]
        xb = _rows_from_tiles(xs_ref, MOE_BLOCK, d).astype(_BF16)
        gate = jnp.dot(xb, wgb[...], preferred_element_type=_F32)
        up = jnp.dot(xb, wub[...], preferred_element_type=_F32)
        mid = (gate * jax.nn.sigmoid(gate)) * up
        _rows_to_tiles(ys_ref, jnp.dot(mid.astype(_BF16), wdb[...], preferred_element_type=_F32))

    @pl.when(jnp.logical_not(valid))
    def _():
        ys_ref[...] = jnp.zeros_like(ys_ref)


def _experts(xs, blk_exp, nvalid, w_gate, w_up, w_down):
    d, de = w_gate.shape[1], w_gate.shape[2]
    nsl = d // LANES
    cap = xs.shape[0] // nsl
    nblk = cap // MOE_BLOCK

    def xmap(i, blk, nv):
        return (jnp.minimum(i, nv[0] - 1), 0)

    def wmap(i, blk, nv):
        return (blk[i], 0, 0)

    return pl.pallas_call(
        _expert_kernel,
        out_shape=jax.ShapeDtypeStruct((cap * nsl, LANES), _F32),
        grid_spec=pltpu.PrefetchScalarGridSpec(
            num_scalar_prefetch=2,
            grid=(nblk,),
            in_specs=[pl.BlockSpec((MOE_BLOCK * nsl, LANES), xmap),
                      pl.BlockSpec((1, d, de), wmap), pl.BlockSpec((1, d, de), wmap),
                      pl.BlockSpec((1, de, d), wmap)],
            out_specs=pl.BlockSpec((MOE_BLOCK * nsl, LANES), lambda i, blk, nv: (i, 0)),
            scratch_shapes=[pltpu.VMEM((d, de), _BF16), pltpu.VMEM((d, de), _BF16), pltpu.VMEM((de, d), _BF16)]),
        compiler_params=_params(("arbitrary",)),
        name="moe_experts",
    )(blk_exp, nvalid, xs, w_gate, w_up, w_down)


def _combine_kernel(c0_ref, c1_ref, n0_ref, n1_ref, ys_hbm, h_ref, route_ref, g_ref, b_ref, o_ref, buf, sem):
    i = pl.program_id(0)
    n = pl.num_programs(0)
    tm = TOK_TM

    d = o_ref.shape[1]
    nsl = d // LANES

    def tile_copy(src_slot, slot, k, j):
        return pltpu.make_async_copy(ys_hbm.at[pl.ds(pl.multiple_of(src_slot * nsl, nsl), nsl)],
                                     buf.at[slot, k, pl.ds(pl.multiple_of(j * nsl, nsl), nsl)], sem.at[slot])

    def gather(drefs, slot):
        def body(j, c):
            for k in range(2):
                tile_copy(drefs[k][0, 0, j], slot, k, j).start(priority=k)
            return c
        lax.fori_loop(0, tm, body, 0, unroll=4)

    @pl.when(i == 0)
    def _():
        gather((c0_ref, c1_ref), 0)

    @pl.when(i + 1 < n)
    def _():
        gather((n0_ref, n1_ref), (i + 1) % 2)

    slot = i % 2

    for k in range(TOP_K):
        pltpu.make_async_copy(ys_hbm.at[pl.ds(0, tm * nsl)], buf.at[slot, k], sem.at[slot]).wait()
    r = route_ref[...]
    y = (r[:, 2:3] * _rows_from_tiles(buf.at[slot, 0], tm, d)
         + r[:, 3:4] * _rows_from_tiles(buf.at[slot, 1], tm, d))
    o_ref[...] = _layer_norm(ALPHA * _rows_from_tiles(h_ref, tm, d) + y, g_ref[...], b_ref[...])


def _combine(ys, dest, h1_tiles, route, g, b):
    n = route.shape[0]
    d = g.shape[1]
    nsl = d // LANES
    tm = TOK_TM
    nt = n // tm
    row = lambda i: (i, 0)
    const = lambda i: (0, 0)
    return pl.pallas_call(
        _combine_kernel,
        out_shape=jax.ShapeDtypeStruct((n, d), _F32),
        grid=(nt,),
        in_specs=[pl.BlockSpec((1, 1, tm), lambda i: (i, 0, 0), memory_space=pltpu.SMEM)] * TOP_K
        + [pl.BlockSpec((1, 1, tm), lambda i: (jnp.minimum(i + 1, nt - 1), 0, 0), memory_space=pltpu.SMEM)] * TOP_K
        + [pl.BlockSpec(memory_space=pl.ANY),
                  pl.BlockSpec((tm * nsl, LANES), row), pl.BlockSpec((tm, LANES), row),
                  pl.BlockSpec((1, d), const), pl.BlockSpec((1, d), const)],
        out_specs=pl.BlockSpec((tm, d), row),
        scratch_shapes=[pltpu.VMEM((2, 2, tm * nsl, LANES), _F32), pltpu.SemaphoreType.DMA((2,))],
        compiler_params=_params(("arbitrary",)),
        name="moe_combine",
    )(dest[0], dest[1], dest[0], dest[1], ys, h1_tiles, route, g, b)


def _moe(h1_tiles, route, route_t, counts_row, w_gate, w_up, w_down, g, b):
    n = route.shape[0]
    cap = 2 * n + N_EXPERTS * MOE_BLOCK
    counts = counts_row[0, ROUTE_LANE0:ROUTE_LANE0 + N_EXPERTS].astype(jnp.int32)
    padded = (counts + MOE_BLOCK - 1) // MOE_BLOCK * MOE_BLOCK
    ends = jnp.cumsum(padded)
    pad_start = ends - padded
    nblk = cap // MOE_BLOCK
    blk_first = jnp.arange(nblk, dtype=jnp.int32) * MOE_BLOCK
    blk_exp = jnp.minimum(jnp.sum(blk_first[:, None] >= ends[None, :], axis=1), N_EXPERTS - 1).astype(jnp.int32)
    nvalid = (ends[-1:] // MOE_BLOCK).astype(jnp.int32)
    dest = _dest(route_t, pad_start.astype(_F32)).reshape(TOP_K, n // TOK_TM, 1, TOK_TM)
    seg = jnp.concatenate([pad_start, padded, counts, nvalid]).astype(jnp.int32)
    xs = _dispatch(h1_tiles, dest, seg, cap, n)
    ys = _experts(xs, blk_exp, nvalid, w_gate, w_up, w_down)
    return _combine(ys, dest, h1_tiles, route, g, b)


def _router_params(rg_w, rg_b, re_w, re_b):
    d = rg_w.shape[0]
    pad = LANES - N_GROUPS - N_EXPERTS
    rw = jnp.concatenate([rg_w, re_w, jnp.zeros((d, pad), _F32)], axis=1)
    rb = jnp.concatenate([rg_b, re_b.reshape(-1), jnp.zeros((pad,), _F32)]).reshape(1, LANES)
    return rw, rb


def kernel(x, t5_table, l0_w_qkv, l0_sink, l0_w_o, l0_ln1_g, l0_ln1_b, l0_rg_w, l0_rg_b, l0_re_w, l0_re_b, l0_w_gate, l0_w_up, l0_w_down, l0_ln2_g, l0_ln2_b, l1_w_qkv, l1_w_o, l1_ln1_g, l1_ln1_b, l1_rg_w, l1_rg_b, l1_re_w, l1_re_b, l1_w_gate, l1_w_up, l1_w_down, l1_ln2_g, l1_ln2_b, l2_w_qkv, l2_rpb, l2_w_o, l2_ln1_g, l2_ln1_b, l2_rg_w, l2_rg_b, l2_re_w, l2_re_b, l2_w_gate, l2_w_up, l2_w_down, l2_ln2_g, l2_ln2_b, l3_w_qkv, l3_sink, l3_w_o, l3_ln1_g, l3_ln1_b, l3_rg_w, l3_rg_b, l3_re_w, l3_re_b, l3_w_gate, l3_w_up, l3_w_down, l3_ln2_g, l3_ln2_b):
    mixers = [(l0_w_qkv, l0_sink, l0_w_o), (l1_w_qkv, None, l1_w_o), (l2_w_qkv, l2_rpb, l2_w_o),
              (l3_w_qkv, l3_sink, l3_w_o)]
    norm1 = [(l0_ln1_g, l0_ln1_b), (l1_ln1_g, l1_ln1_b), (l2_ln1_g, l2_ln1_b), (l3_ln1_g, l3_ln1_b)]
    routers = [(l0_rg_w, l0_rg_b, l0_re_w, l0_re_b), (l1_rg_w, l1_rg_b, l1_re_w, l1_re_b),
               (l2_rg_w, l2_rg_b, l2_re_w, l2_re_b), (l3_rg_w, l3_rg_b, l3_re_w, l3_re_b)]
    experts = [(l0_w_gate, l0_w_up, l0_w_down), (l1_w_gate, l1_w_up, l1_w_down),
               (l2_w_gate, l2_w_up, l2_w_down), (l3_w_gate, l3_w_up, l3_w_down)]
    norm2 = [(l0_ln2_g, l0_ln2_b), (l1_ln2_g, l1_ln2_b), (l2_ln2_g, l2_ln2_b), (l3_ln2_g, l3_ln2_b)]

    bsz, seq, d = x.shape
    n = bsz * seq
    inner = N_HEADS * HEAD_DIM
    scale = HEAD_DIM ** -0.5 * LOG2E
    h = x.reshape(n, d)
    bias_a = _band_bias_t(t5_table, A_BLOCK, A_WINDOW, 1, A_BAND)
    for i in range(DEPTH):
        kind = i % N_MIXERS
        w_qkv, extra, w_o = mixers[i]
        kvw = A_KV_HEADS * HEAD_DIM if kind == 0 else inner
        lses, unperm = (), None
        if kind == 1:
            perms = np.stack([_residue_perm(dil) for _, dil in B_BRANCHES if dil > 1])
            outs = _project(h, w_qkv.astype(_BF16), (inner, kvw, kvw), (scale, 1.0, 1.0), jnp.asarray(perms, _BF16))
            unperm = jnp.asarray(perms.transpose(0, 2, 1), _BF16)
            attn, lses = [], []
            for bi, (window, dil) in enumerate(B_BRANCHES):
                q, k, v = outs[3 * bi:3 * bi + 3]
                o, lse = _mixer_b_branch(q, k, v, t5_table, bsz, seq, window, dil)
                attn.append(o)
                lses.append(jnp.pad(lse.reshape(n, N_HEADS), ((0, 0), (0, LANES - N_HEADS))))
        else:
            q, k, v = _project(h, w_qkv.astype(_BF16), (inner, kvw, kvw), (scale, 1.0, 1.0))
            q, k, v = (t.reshape(bsz, seq, t.shape[1]) for t in (q, k, v))
            if kind == 0:
                attn = [_mixer_a(q, k, v, extra.astype(_F32) * LOG2E, bias_a, bsz, seq)]
            else:
                attn = [_mixer_c(q, k, v, _na_bias_t(extra, seq // GRID_W), bsz, seq)]
        attn = [a.reshape(n, inner) for a in attn]
        g1, b1 = norm1[i]
        rw, rb = _router_params(*routers[i])
        h1, route, route_t, counts = _post_attn(attn, lses, h, w_o.astype(_BF16), g1.reshape(1, d), b1.reshape(1, d),
                                                rw, rb, unperm)
        g2, b2 = norm2[i]
        h = _moe(h1, route, route_t, counts, *experts[i], g2.reshape(1, d), b2.reshape(1, d))
    return h.reshape(bsz, seq, d)
```

```python
import functools
import math

import jax
import jax.numpy as jnp
import numpy as np
from jax import lax
from jax.experimental import pallas as pl
from jax.experimental.pallas import tpu as pltpu

HEAD_DIM = 64
N_HEADS = 16
A_KV_HEADS = 4
A_WINDOW = 128
A_BLOCK = 128
B_BRANCHES = ((128, 1), (512, 4), (2048, 16))
GRID_W = 64
NA_ROWS = 8
NA_COLS = 16
T5_BUCKETS = 32
T5_MAX_DIST = 1024
N_GROUPS = 4
EXPERTS_PER_GROUP = 8
N_EXPERTS = N_GROUPS * EXPERTS_PER_GROUP
TOP_K = 2
DEPTH = 4
N_MIXERS = 3
ALPHA = (2 * DEPTH) ** 0.25
LN_EPS = 1e-5
NEG_INF = -1e30

LANES = 128
PAIR = 2 * HEAD_DIM
BF16_SUBLANES = 16
VT_ROWS = PAIR + BF16_SUBLANES
LOG2E = math.log2(math.e)
LN2 = math.log(2.0)
VMEM_LIMIT = 56 * 1024 * 1024

PROJ_TM = 512
PROJ_CHUNK = 512
POST_TM = 1024
POST_SUB = 256
TOK_TM = 1024
DEST_TM = 8192
ROUTE_ROWS = 8
MOE_BLOCK = 512
A_TQ = 2048
A_BAND = 3
A_SUB_PER_GROUP = 4
B_BLOCK = 128
RESIDUE_RUN = BF16_SUBLANES
PERM_CHUNK = 256
TRANSPOSE_UNROLL = 8
UNITS_PER_GROUP = 16
C_UNIT_ROWS = 2
C_BAND_ROWS = 10
C_UNITS_PER_GROUP = 8
C_ROWS_PER_STEP = 64
ROUTE_LANE0 = N_GROUPS

_F32 = jnp.float32
_BF16 = jnp.bfloat16


def _params(sem, vmem=VMEM_LIMIT, flags=None):
    return pltpu.CompilerParams(dimension_semantics=sem, vmem_limit_bytes=vmem, flags=flags)


def _proj_kernel(x_ref, w_ref, *refs, widths, scales, nperm):
    perm_ref = refs[0] if nperm else None
    o_refs = refs[1:] if nperm else refs
    nout = len(widths)
    xb = x_ref[...].astype(_BF16)
    col = 0
    for t, (width, scale) in enumerate(zip(widths, scales)):
        for c in range(0, width, PROJ_CHUNK):
            cw = min(PROJ_CHUNK, width - c)
            acc = jnp.dot(xb, w_ref[:, col + c:col + c + cw], preferred_element_type=_F32)
            if scale != 1.0:
                acc = acc * scale
            ob = acc.astype(_BF16)
            o_refs[t][:, c:c + cw] = ob
            for p in range(nperm):
                for r0 in range(0, ob.shape[0], PERM_CHUNK):
                    moved = jnp.dot(perm_ref[p], ob[r0:r0 + PERM_CHUNK], preferred_element_type=_F32)
                    o_refs[(p + 1) * nout + t][r0:r0 + PERM_CHUNK, c:c + cw] = moved.astype(_BF16)
        col += width


def _project(x2, w_bf16, widths, scales, perms=None):
    n, d = x2.shape
    ctot = sum(widths)
    nperm = 0 if perms is None else perms.shape[0]
    args = [x2, w_bf16] + ([perms] if nperm else [])
    specs = [pl.BlockSpec((PROJ_TM, d), lambda i: (i, 0)), pl.BlockSpec((d, ctot), lambda i: (0, 0))]
    if nperm:
        specs.append(pl.BlockSpec(perms.shape, lambda i: (0, 0, 0)))
    return pl.pallas_call(
        functools.partial(_proj_kernel, widths=tuple(widths), scales=tuple(scales), nperm=nperm),
        out_shape=[jax.ShapeDtypeStruct((n, w), _BF16) for w in widths] * (nperm + 1),
        grid=(n // PROJ_TM,),
        in_specs=specs,
        out_specs=[pl.BlockSpec((PROJ_TM, w), lambda i: (i, 0)) for w in widths] * (nperm + 1),
        compiler_params=_params(("arbitrary",)),
        name="qkv_proj",
    )(*args)


def _residue_perm(dil):
    span = dil * RESIDUE_RUN
    p = np.zeros((PERM_CHUNK, PERM_CHUNK), np.float32)
    for base in range(0, PERM_CHUNK, span):
        for r in range(dil):
            for u in range(RESIDUE_RUN):
                p[base + r * RESIDUE_RUN + u, base + u * dil + r] = 1.0
    return p


def _one():
    return jnp.ones((1,), jnp.int32)


def _lane_lo(rows=1):
    return lax.broadcasted_iota(jnp.int32, (rows, PAIR), 1) < HEAD_DIM


def _swap_halves(x):
    return pltpu.roll(x.astype(_F32), HEAD_DIM, axis=1).astype(_BF16)


class _PairUnit:
    def __init__(self, q2, k, vt, bias, emit, sink_a=None, sink_b=None):
        self.q2, self.k, self.vt, self.bias, self.emit = q2, k, vt, bias, emit
        self.sink_a, self.sink_b = sink_a, sink_b


def _attend_groups(ngroups, units_of, scratch, one):
    assert ngroups % 2 == 0
    lo = _lane_lo()
    nt = (((1,), (1,)), ((), ()))

    def sink_row(unit, m2):
        lane = lax.broadcasted_iota(jnp.int32, (1, m2), 1)
        return jnp.where(lane < m2 // 2, unit.sink_a, unit.sink_b)

    def scores(g, parity):
        s_scr, m_scr = scratch[parity]
        for u, unit in enumerate(units_of(g, parity)):
            q2 = unit.q2()
            zero = jnp.zeros_like(q2)
            qq = jnp.concatenate([jnp.where(lo, q2, zero), jnp.where(lo, zero, q2)], axis=0)
            s = lax.dot_general(unit.k(), qq, nt, preferred_element_type=_F32) + unit.bias()
            m = jnp.max(s, axis=0, keepdims=True)
            if unit.sink_a is not None:
                m = jnp.maximum(m, sink_row(unit, s.shape[1]))
            s_scr[u] = s
            m_scr[u] = m

    def values(g, parity):
        s_scr, m_scr = scratch[parity]
        for u, unit in enumerate(units_of(g, parity)):
            m = m_scr[u]
            e = jnp.exp2(s_scr[u] - m)
            ot = jnp.dot(unit.vt(), e.astype(_BF16), preferred_element_type=_F32)
            den = ot[PAIR:PAIR + 1, :]
            if unit.sink_a is not None:
                den = den + jnp.exp2(sink_row(unit, e.shape[1]) - m)
            ot = ot[:PAIR] * (1.0 / den)
            mq = ot.shape[1] // 2
            top = lax.broadcasted_iota(jnp.int32, (PAIR, 1), 0) < HEAD_DIM
            pair_t = jnp.where(top, ot[:, :mq], ot[:, mq:])
            unit.emit(pair_t.T, (m + jnp.log2(den)) * LN2)

    def region(*work):
        def body(_, carry):
            for fn, g, parity in work:
                fn(g, parity)
            return carry
        lax.fori_loop(0, one, body, 0)

    region((scores, 0, 0))

    def steady(i, carry):
        g = 2 * i
        region((values, g, 0), (scores, g + 1, 1))
        region((values, g + 1, 1), (scores, g + 2, 0))
        return carry

    lax.fori_loop(0, ngroups // 2 - 1, steady, 0)
    region((values, ngroups - 2, 0), (scores, ngroups - 1, 1))
    region((values, ngroups - 1, 1))


def _transpose_blocks(v_ref_rows, nblk, store):
    def body(j, carry):
        rows = pl.ds(pl.multiple_of(j * LANES, LANES), LANES)
        vt = v_ref_rows(rows).T
        store(j, jnp.concatenate([vt, jnp.ones((VT_ROWS - PAIR, LANES), _BF16)], axis=0))
        return carry
    lax.fori_loop(0, nblk, body, 0, unroll=TRANSPOSE_UNROLL)


def _t5_bucket(rel):
    half = T5_BUCKETS // 2
    exact = half // 2
    n = np.abs(rel)
    big = exact + (np.log(np.maximum(n, 1) / exact) / math.log(T5_MAX_DIST / exact) * (half - exact)).astype(np.int32)
    return ((rel > 0) * half + np.where(n < exact, n, np.minimum(big, half - 1))).astype(np.int32)


def _band_bias_t(t5_table, blk, window, dil, nband):
    tk = nband * blk
    off = tk - 1
    rel = np.arange(-off, tk)
    vec = jnp.take(t5_table, jnp.asarray(_t5_bucket(rel * dil)), axis=0).astype(_F32)
    vec = jnp.where(jnp.asarray(np.abs(rel) <= window)[:, None], vec * LOG2E, NEG_INF).T
    h = vec.shape[0]
    vec = jnp.pad(vec, ((0, 0), (0, 1)))
    w = 2 * tk - 1
    skew = jnp.tile(vec, (1, blk))[:, :blk * w].reshape(h, blk, w)
    t = jnp.stack([skew[:, :, off - v * blk:off - v * blk + tk] for v in range(nband)], axis=1)
    t = t.reshape(h // 2, 2, nband, blk, tk).transpose(0, 2, 4, 1, 3)
    return t.reshape(h // 2, nband, tk, 2 * blk)


def _band_window(gb, blk, length, nband):
    first = jnp.clip(gb - 1, 0, length // blk - nband)
    return first, gb - first


def _attn_a_kernel(sink_ref, one_ref, q_ref, k_ref, v_ref, bias_ref, o_ref, kd_ref, vdt_ref,
                   s0_scr, m0_scr, s1_scr, m1_scr, *, seq):
    kvb = pl.program_id(0)
    qt = pl.program_id(2)
    one = one_ref[0]
    nblk = seq // A_BLOCK

    @pl.when(qt == 0)
    def _():
        lo = _lane_lo()

        def fill(j, carry):
            rows = pl.ds(pl.multiple_of(j * A_BLOCK, A_BLOCK), A_BLOCK)
            kk = k_ref[0, rows, :]
            ks = _swap_halves(kk)
            kd_ref[0, rows, :] = jnp.where(lo, kk, ks)
            kd_ref[1, rows, :] = jnp.where(lo, ks, kk)
            vt = v_ref[0, rows, :].T
            ones = jnp.ones((VT_ROWS - PAIR, A_BLOCK), _BF16)
            vdt_ref[0, j] = jnp.concatenate([vt[:HEAD_DIM], vt[:HEAD_DIM], ones], axis=0)
            vdt_ref[1, j] = jnp.concatenate([vt[HEAD_DIM:], vt[HEAD_DIM:], ones], axis=0)
            return carry

        lax.fori_loop(0, nblk, fill, 0, unroll=TRANSPOSE_UNROLL)

    nsub = A_TQ // A_BLOCK
    npair = q_ref.shape[2] // PAIR
    hbase = kvb * (2 * npair)

    def units_of(g, parity):
        units = []
        for j in range(A_SUB_PER_GROUP):
            si = g * A_SUB_PER_GROUP + j
            first, var = _band_window(qt * nsub + si, A_BLOCK, seq, A_BAND)
            rows = pl.ds(pl.multiple_of(si * A_BLOCK, A_BLOCK), A_BLOCK)
            band = pl.ds(pl.multiple_of(first * A_BLOCK, A_BLOCK), A_BAND * A_BLOCK)
            for jj in range(npair):
                hk = jj // (npair // 2)
                cols = slice(jj * PAIR, (jj + 1) * PAIR)

                def emit(o, lse, rows=rows, cols=cols):
                    o_ref[0, rows, cols] = o.astype(o_ref.dtype)

                units.append(_PairUnit(
                    q2=lambda rows=rows, cols=cols: q_ref[0, rows, cols],
                    k=lambda hk=hk, band=band: kd_ref[hk, band, :],
                    vt=lambda hk=hk, first=first: jnp.concatenate(
                        [vdt_ref[hk, first + t] for t in range(A_BAND)], axis=1),
                    bias=lambda jj=jj, var=var: bias_ref[jj, var],
                    emit=emit, sink_a=sink_ref[hbase + 2 * jj], sink_b=sink_ref[hbase + 2 * jj + 1]))
        return units

    _attend_groups(nsub // A_SUB_PER_GROUP, units_of, ((s0_scr, m0_scr), (s1_scr, m1_scr)), one)


def _mixer_a(q, k, v, sink, bias, bsz, seq):
    nkvb = k.shape[2] // PAIR
    qw = q.shape[2] // nkvb
    npair = qw // PAIR
    tk = A_BAND * A_BLOCK
    nunits = A_SUB_PER_GROUP * npair
    group_scratch = [pltpu.VMEM((nunits, tk, 2 * A_BLOCK), _F32), pltpu.VMEM((nunits, 1, 2 * A_BLOCK), _F32)]
    return pl.pallas_call(
        functools.partial(_attn_a_kernel, seq=seq),
        out_shape=jax.ShapeDtypeStruct(q.shape, _BF16),
        grid=(nkvb, bsz, seq // A_TQ),
        in_specs=[pl.BlockSpec(memory_space=pltpu.SMEM), pl.BlockSpec(memory_space=pltpu.SMEM),
                  pl.BlockSpec((1, A_TQ, qw), lambda c, b, t: (b, t, c)),
                  pl.BlockSpec((1, seq, PAIR), lambda c, b, t: (b, 0, c)),
                  pl.BlockSpec((1, seq, PAIR), lambda c, b, t: (b, 0, c)),
                  pl.BlockSpec((npair, A_BAND, tk, 2 * A_BLOCK), lambda c, b, t: (c, 0, 0, 0))],
        out_specs=pl.BlockSpec((1, A_TQ, qw), lambda c, b, t: (b, t, c)),
        scratch_shapes=[pltpu.VMEM((2, seq, PAIR), _BF16), pltpu.VMEM((2, seq // A_BLOCK, VT_ROWS, A_BLOCK), _BF16)]
        + group_scratch + group_scratch,
        compiler_params=_params(("arbitrary", "arbitrary", "arbitrary")),
        name="mixer_a",
    )(sink, _one(), q, k, v, bias)


def _attn_band_kernel(one_ref, q_ref, k_ref, v_ref, bias_ref, o_ref, lse_ref, vt_ref,
                      s0_scr, m0_scr, s1_scr, m1_scr, *, length, npairs, nband, chunked):
    col = pl.program_id(2)
    one = one_ref[0]
    nblk = length // B_BLOCK
    runs = B_BLOCK // RESIDUE_RUN

    def load(ref, blk0, nblocks, cols):
        if chunked:
            r0 = pl.multiple_of(blk0 * runs, runs)
            return ref[pl.ds(r0, nblocks * runs), :, cols].reshape(nblocks * B_BLOCK, PAIR)
        return ref[0, pl.ds(pl.multiple_of(blk0 * B_BLOCK, B_BLOCK), nblocks * B_BLOCK), cols]

    def store_out(blk0, cols, val):
        if chunked:
            o_ref[pl.ds(pl.multiple_of(blk0 * runs, runs), runs), :, cols] = val.reshape(runs, RESIDUE_RUN, PAIR)
        else:
            o_ref[0, pl.ds(pl.multiple_of(blk0 * B_BLOCK, B_BLOCK), B_BLOCK), cols] = val

    pairs_per_loop = max(1, min(npairs, TRANSPOSE_UNROLL // min(TRANSPOSE_UNROLL, nblk) * 2))
    for p0 in range(0, npairs, pairs_per_loop):
        def fill(j, carry, p0=p0):
            for p in range(p0, p0 + pairs_per_loop):
                vt = load(v_ref, j, 1, slice(p * PAIR, (p + 1) * PAIR)).T
                vt_ref[p, j] = jnp.concatenate([vt, jnp.ones((VT_ROWS - PAIR, B_BLOCK), _BF16)], axis=0)
            return carry

        lax.fori_loop(0, nblk, fill, 0, unroll=min(TRANSPOSE_UNROLL, nblk))

    def unit(gb, p):
        first, var = _band_window(gb, B_BLOCK, length, nband)
        cols = slice(p * PAIR, (p + 1) * PAIR)

        def emit(o, lse):
            store_out(gb, cols, o.astype(o_ref.dtype))
            ha = col * (2 * npairs) + 2 * p
            lse_ref[0, 0, gb, pl.ds(ha, 1), :] = lse[:, :B_BLOCK]
            lse_ref[0, 0, gb, pl.ds(ha + 1, 1), :] = lse[:, B_BLOCK:]

        return _PairUnit(
            q2=lambda: load(q_ref, gb, 1, cols), k=lambda: load(k_ref, first, nband, cols),
            vt=lambda: jnp.concatenate([vt_ref[p, first + t] for t in range(nband)], axis=1),
            bias=lambda: bias_ref[p, var], emit=emit)

    per_group = _band_blocks_per_group(nblk, npairs)
    units_of = lambda g, parity: [unit(g * per_group + j, p) for j in range(per_group) for p in range(npairs)]
    _attend_groups(nblk // per_group, units_of, ((s0_scr, m0_scr), (s1_scr, m1_scr)), one)


def _band_blocks_per_group(nblk, npairs):
    return max(1, min(UNITS_PER_GROUP // npairs, nblk // 2))


def _mixer_b_branch(q, k, v, t5_table, bsz, seq, window, dil):
    length = seq // dil
    width = q.shape[1]
    nblk = length // B_BLOCK
    nband = min(3, nblk)
    tk = nband * B_BLOCK
    npairs = (width // PAIR) if length * width * 2 <= 2 * 1024 * 1024 else 1
    ncol = width // (npairs * PAIR)
    bw = npairs * PAIR
    nunits = _band_blocks_per_group(nblk, npairs) * npairs
    bias = _band_bias_t(t5_table, B_BLOCK, window // 2 // dil, dil, nband)
    chunked = dil > 1
    if chunked:
        nrun = length // RESIDUE_RUN
        shape = (bsz * nrun, dil, RESIDUE_RUN, width)
        spec = pl.BlockSpec((nrun, None, RESIDUE_RUN, bw), lambda b, r, c: (b, r, 0, c))
    else:
        shape = (bsz, seq, width)
        spec = pl.BlockSpec((1, seq, bw), lambda b, r, c: (b, 0, c))

    o, lse = pl.pallas_call(
        functools.partial(_attn_band_kernel, length=length, npairs=npairs, nband=nband, chunked=chunked),
        out_shape=[jax.ShapeDtypeStruct(shape, _BF16),
                   jax.ShapeDtypeStruct((bsz, dil, nblk, N_HEADS, B_BLOCK), _F32)],
        grid=(bsz, dil, ncol),
        in_specs=[pl.BlockSpec(memory_space=pltpu.SMEM), spec, spec, spec,
                  pl.BlockSpec((npairs, nband, tk, 2 * B_BLOCK), lambda b, r, c: (c, 0, 0, 0))],
        out_specs=[spec, pl.BlockSpec((1, 1, nblk, N_HEADS, B_BLOCK), lambda b, r, c: (b, r, 0, 0, 0))],
        scratch_shapes=[pltpu.VMEM((npairs, nblk, VT_ROWS, B_BLOCK), _BF16)] + 2 * [
            pltpu.VMEM((nunits, tk, 2 * B_BLOCK), _F32), pltpu.VMEM((nunits, 1, 2 * B_BLOCK), _F32)],
        compiler_params=_params(("arbitrary", "arbitrary", "arbitrary")),
        name=f"mixer_b_dil{dil}",
    )(_one(), q.reshape(shape), k.reshape(shape), v.reshape(shape), bias)
    lse = lse.transpose(0, 2, 4, 1, 3).reshape(bsz, seq, N_HEADS)
    return o.reshape(bsz * seq, width), lse


def _na_bias_t(rpb, rows):
    canon = (0, 2, 4, rows - 4, rows - 2)
    h = rpb.shape[0]
    kc = np.arange(GRID_W)[:, None]
    qc = np.arange(GRID_W)[None, :]
    ws = np.clip(qc - NA_COLS // 2, 0, GRID_W - NA_COLS)
    col_ok = (kc >= ws) & (kc < ws + NA_COLS)
    col_idx = np.clip(kc - qc, 1 - NA_COLS, NA_COLS - 1) + NA_COLS - 1
    scaled = rpb.astype(_F32) * LOG2E
    idx = jnp.asarray(np.where(col_ok, col_idx, -1))
    colpart = jnp.full((h, rpb.shape[1], GRID_W, GRID_W), NEG_INF, _F32)
    for d in range(rpb.shape[2]):
        colpart = jnp.where(idx == d, scaled[:, :, d, None, None], colpart)
    masked = jnp.full((h, GRID_W, GRID_W), NEG_INF, _F32)
    variants = []
    for i in canon:
        start = int(np.clip(i - NA_ROWS // 2, 0, rows - C_BAND_ROWS))
        per_key_row = []
        for kr in range(C_BAND_ROWS):
            per_query_row = []
            for qr in range(C_UNIT_ROWS):
                iq, kabs = i + qr, start + kr
                first_key_row = int(np.clip(iq - NA_ROWS // 2, 0, rows - NA_ROWS))
                inside = first_key_row <= kabs < first_key_row + NA_ROWS
                per_query_row.append(colpart[:, kabs - iq + NA_ROWS - 1] if inside else masked)
            per_key_row.append(jnp.concatenate(per_query_row, axis=-1))
        variants.append(jnp.concatenate(per_key_row, axis=1))
    b = jnp.stack(variants, axis=1)
    b = b.reshape(h // 2, 2, *b.shape[1:])
    return jnp.concatenate([b[:, 0], b[:, 1]], axis=-1)


def _attn_c_kernel(one_ref, q_ref, k_ref, v_ref, bias_ref, o_ref, vt_ref, s0_scr, m0_scr, s1_scr, m1_scr, *, rows):
    rt = pl.program_id(2)
    nq = C_UNIT_ROWS * GRID_W
    band_blocks = C_BAND_ROWS * GRID_W // LANES

    @pl.when(rt == 0)
    def _():
        def store(j, blk):
            vt_ref[j] = blk
        _transpose_blocks(lambda r: v_ref[0, r, :], rows * GRID_W // LANES, store)

    def unit(u):
        i = rt * C_ROWS_PER_STEP + C_UNIT_ROWS * u
        start = jnp.clip(i - NA_ROWS // 2, 0, rows - C_BAND_ROWS)
        var = lax.shift_right_logical(i - start, 1)
        first = lax.shift_right_logical(start, 1)
        keys = pl.ds(pl.multiple_of(first * LANES, LANES), C_BAND_ROWS * GRID_W)
        qrows = pl.ds(pl.multiple_of(u * nq, nq), nq)

        def emit(o, lse):
            o_ref[0, qrows, :] = o.astype(o_ref.dtype)

        return _PairUnit(
            q2=lambda: q_ref[0, qrows, :], k=lambda: k_ref[0, keys, :],
            vt=lambda: jnp.concatenate([vt_ref[first + t] for t in range(band_blocks)], axis=1),
            bias=lambda: bias_ref[0, var], emit=emit)

    ngroups = C_ROWS_PER_STEP // C_UNIT_ROWS // C_UNITS_PER_GROUP
    units_of = lambda g, parity: [unit(g * C_UNITS_PER_GROUP + j) for j in range(C_UNITS_PER_GROUP)]
    _attend_groups(ngroups, units_of, ((s0_scr, m0_scr), (s1_scr, m1_scr)), one_ref[0])


def _mixer_c(q, k, v, bias, bsz, seq):
    rows = seq // GRID_W
    tq = C_ROWS_PER_STEP * GRID_W
    npair = q.shape[2] // PAIR
    nq, nk = C_UNIT_ROWS * GRID_W, C_BAND_ROWS * GRID_W
    return pl.pallas_call(
        functools.partial(_attn_c_kernel, rows=rows),
        out_shape=jax.ShapeDtypeStruct(q.shape, _BF16),
        grid=(npair, bsz, seq // tq),
        in_specs=[pl.BlockSpec(memory_space=pltpu.SMEM),
                  pl.BlockSpec((1, tq, PAIR), lambda c, b, t: (b, t, c)),
                  pl.BlockSpec((1, seq, PAIR), lambda c, b, t: (b, 0, c)),
                  pl.BlockSpec((1, seq, PAIR), lambda c, b, t: (b, 0, c)),
                  pl.BlockSpec((1, bias.shape[1], nk, 2 * nq), lambda c, b, t: (c, 0, 0, 0))],
        out_specs=pl.BlockSpec((1, tq, PAIR), lambda c, b, t: (b, t, c)),
        scratch_shapes=[pltpu.VMEM((seq // LANES, VT_ROWS, LANES), _BF16)] + 2 * [
            pltpu.VMEM((C_UNITS_PER_GROUP, nk, 2 * nq), _F32), pltpu.VMEM((C_UNITS_PER_GROUP, 1, 2 * nq), _F32)],
        compiler_params=_params(("arbitrary", "arbitrary", "arbitrary")),
        name="mixer_c",
    )(_one(), q, k, v, bias)


def _layer_norm(z, g, b):
    mu = jnp.mean(z, axis=-1, keepdims=True)
    zc = z - mu
    var = jnp.mean(zc * zc, axis=-1, keepdims=True)
    return zc * lax.rsqrt(var + LN_EPS) * g + b


def _rows_to_tiles(ref, x):
    nrows, d = x.shape
    nsl = d // LANES
    for s in range(nsl):
        ref[pl.ds(s, nrows, stride=nsl), :] = x[:, s * LANES:(s + 1) * LANES]


def _rows_from_tiles(ref, nrows, d):
    nsl = d // LANES
    return jnp.concatenate([ref[pl.ds(s, nrows, stride=nsl), :] for s in range(nsl)], axis=1)


def _split_bf16(x):
    hi = x.astype(_BF16)
    return hi, (x - hi.astype(_F32)).astype(_BF16)


def _router_logits(hn, rw_ref, rb_ref):
    h_hi, h_lo = _split_bf16(hn)
    w_hi, w_lo = _split_bf16(rw_ref[...])
    both = jnp.dot(h_hi, jnp.concatenate([w_hi, w_lo], axis=1), preferred_element_type=_F32)
    return (both[:, :LANES]
            + (both[:, LANES:] + jnp.dot(h_lo, w_hi, preferred_element_type=_F32))) + rb_ref[...]


def _route(logits, tri_ref, carry_ref):
    tm = logits.shape[0]
    lane = lax.broadcasted_iota(jnp.int32, (tm, LANES), 1).astype(_F32)
    ninf = -jnp.inf
    is_grp = lane < N_GROUPS
    lg = jnp.where(is_grp, logits, ninf)
    gmax = jnp.max(lg, axis=-1, keepdims=True)
    grp = jnp.min(jnp.where(lg == gmax, lane, float(LANES)), axis=-1, keepdims=True)
    p_grp = 1.0 / jnp.sum(jnp.where(is_grp, jnp.exp(lg - gmax), 0.0), axis=-1, keepdims=True)
    lo_lane = ROUTE_LANE0 + EXPERTS_PER_GROUP * grp
    in_grp = (lane >= lo_lane) & (lane < lo_lane + EXPERTS_PER_GROUP)
    le = jnp.where(in_grp, logits, ninf)
    v1 = jnp.max(le, axis=-1, keepdims=True)
    i1 = jnp.min(jnp.where(le == v1, lane, float(LANES)), axis=-1, keepdims=True)
    le2 = jnp.where(lane == i1, ninf, le)
    v2 = jnp.max(le2, axis=-1, keepdims=True)
    i2 = jnp.min(jnp.where(le2 == v2, lane, float(LANES)), axis=-1, keepdims=True)
    t = jnp.exp(v2 - v1)
    s1 = 1.0 / (1.0 + t)
    g1 = p_grp * s1
    g2 = p_grp * (t * s1)
    sel1 = lane == i1
    sel2 = lane == i2
    onehot = jnp.where(sel1 | sel2, 1.0, 0.0)
    sub = tri_ref.shape[0]
    carry = carry_ref[...]
    parts = []
    for r0 in range(0, tm, sub):
        part = onehot[r0:r0 + sub]
        parts.append(jnp.dot(tri_ref[...], part.astype(_BF16), preferred_element_type=_F32) + carry)
        carry = carry + jnp.sum(part, axis=0, keepdims=True)
    carry_ref[...] = carry
    before = jnp.concatenate(parts, axis=0) if len(parts) > 1 else parts[0]
    rank1 = jnp.sum(jnp.where(sel1, before, 0.0), axis=-1, keepdims=True)
    rank2 = jnp.sum(jnp.where(sel2, before, 0.0), axis=-1, keepdims=True)
    out = jnp.zeros((tm, LANES), _F32)
    for idx, val in enumerate((i1 - ROUTE_LANE0, i2 - ROUTE_LANE0, g1, g2, rank1, rank2)):
        out = jnp.where(lane == idx, val, out)
    return out


def _post_attn_kernel(*refs, n_branch):
    attn_refs = refs[:n_branch]
    pos = n_branch
    if n_branch > 1:
        lse_refs = refs[pos:pos + n_branch]
        expand_ref, unperm_ref = refs[pos + n_branch:pos + n_branch + 2]
        pos += n_branch + 2
    wo_ref, h_ref, g_ref, b_ref, rw_ref, rb_ref, tri_ref = refs[pos:pos + 7]
    h1_ref, route_ref, routet_ref, counts_ref, carry_ref = refs[pos + 7:pos + 12]

    @pl.when(pl.program_id(0) == 0)
    def _():
        carry_ref[...] = jnp.zeros_like(carry_ref)

    d = h_ref.shape[1]
    logits = []
    for sub in range(h_ref.shape[0] // POST_SUB):
        rows = slice(sub * POST_SUB, (sub + 1) * POST_SUB)
        if n_branch == 1:
            a = attn_refs[0][rows, :]
        else:
            lses = [r[rows, :] for r in lse_refs]
            mx = functools.reduce(jnp.maximum, lses)
            es = [jnp.exp(l - mx) for l in lses]
            inv = 1.0 / functools.reduce(jnp.add, es)
            mixed = None
            for bi, (e, a_ref) in enumerate(zip(es, attn_refs)):
                w_hi, w_lo = _split_bf16(e * inv)
                w_full = jnp.dot(jnp.concatenate([w_hi, w_lo], axis=1), expand_ref[...],
                                 preferred_element_type=_F32)
                if bi == 0:
                    a_nat = a_ref[rows, :].astype(_F32)
                else:
                    a_nat = jnp.dot(unperm_ref[bi - 1], a_ref[rows, :], preferred_element_type=_F32)
                term = w_full * a_nat
                mixed = term if mixed is None else mixed + term
            a = mixed.astype(_BF16)
        y = jnp.dot(a, wo_ref[...], preferred_element_type=_F32)
        hn = _layer_norm(ALPHA * h_ref[rows, :] + y, g_ref[...], b_ref[...])
        nsl = d // LANES
        _rows_to_tiles(h1_ref.at[pl.ds(sub * POST_SUB * nsl, POST_SUB * nsl)], hn)
        logits.append(_router_logits(hn, rw_ref, rb_ref))
    route = _route(jnp.concatenate(logits, axis=0), tri_ref, carry_ref)
    route_ref[...] = route
    routet_ref[...] = route.T[:ROUTE_ROWS]
    counts_ref[...] = carry_ref[...]


def _post_attn(attn, lses, h2, w_o, g, b, rw, rb, unperm=None):
    n, d = h2.shape
    tm = POST_TM
    nb = len(attn)
    tri = jnp.asarray(np.tril(np.ones((POST_SUB, POST_SUB), np.float32), -1), _BF16)
    row = lambda i: (i, 0)
    const = lambda i: (0, 0)
    args = list(attn)
    specs = [pl.BlockSpec((tm, d), row) for _ in attn]
    if nb > 1:
        expand = np.zeros((2, LANES, d), np.float32)
        for hd in range(N_HEADS):
            expand[:, hd, hd * HEAD_DIM:(hd + 1) * HEAD_DIM] = 1.0
        args += list(lses) + [jnp.asarray(expand.reshape(2 * LANES, d), _BF16), unperm]
        specs += [pl.BlockSpec((tm, LANES), row) for _ in lses] + [
            pl.BlockSpec((2 * LANES, d), const), pl.BlockSpec(unperm.shape, lambda i: (0, 0, 0))]
    args += [w_o, h2, g, b, rw, rb, tri]
    specs += [pl.BlockSpec((d, d), const), pl.BlockSpec((tm, d), row), pl.BlockSpec((1, d), const),
              pl.BlockSpec((1, d), const), pl.BlockSpec((d, LANES), const), pl.BlockSpec((1, LANES), const),
              pl.BlockSpec((POST_SUB, POST_SUB), const)]
    return pl.pallas_call(
        functools.partial(_post_attn_kernel, n_branch=nb),
        out_shape=[jax.ShapeDtypeStruct((n * d // LANES, LANES), _F32), jax.ShapeDtypeStruct((n, LANES), _F32),
                   jax.ShapeDtypeStruct((ROUTE_ROWS, n), _F32),
                   jax.ShapeDtypeStruct((1, LANES), _F32)],
        grid=(n // tm,),
        in_specs=specs,
        out_specs=[pl.BlockSpec((tm * d // LANES, LANES), row), pl.BlockSpec((tm, LANES), row),
                   pl.BlockSpec((ROUTE_ROWS, tm), lambda i: (0, i)), pl.BlockSpec((1, LANES), const)],
        scratch_shapes=[pltpu.VMEM((1, LANES), _F32)],
        compiler_params=_params(("arbitrary",)),
        name="post_attn_router",
    )(*args)


def _dest_kernel(ps_ref, rt_ref, o_ref):
    expert = rt_ref[0:TOP_K, :]
    slot = rt_ref[4:4 + TOP_K, :]
    for e in range(N_EXPERTS):
        slot = slot + jnp.where(expert == float(e), ps_ref[e], 0.0)
    o_ref[...] = slot.astype(jnp.int32)


def _dest(route_t, pad_start):
    n = route_t.shape[1]
    tm = min(DEST_TM, n)
    return pl.pallas_call(
        _dest_kernel,
        out_shape=jax.ShapeDtypeStruct((TOP_K, n), jnp.int32),
        grid=(n // tm,),
        in_specs=[pl.BlockSpec(memory_space=pltpu.SMEM), pl.BlockSpec((ROUTE_ROWS, tm), lambda i: (0, i))],
        out_specs=pl.BlockSpec((TOP_K, tm), lambda i: (0, i)),
        compiler_params=_params(("arbitrary",)),
        name="moe_dest",
    )(pad_start, route_t)


def _dispatch_kernel(seg_ref, d0_ref, d1_ref, x_ref, xs_hbm, zero_ref, zsem, sem):
    i = pl.program_id(0)
    ntok = TOK_TM
    nsl = x_ref.shape[0] // ntok
    blk_rows = MOE_BLOCK * nsl

    def zero_fill(e):
        first = pl.multiple_of((seg_ref[e] + seg_ref[N_EXPERTS + e] - MOE_BLOCK) * nsl, blk_rows)
        return pltpu.make_async_copy(zero_ref, xs_hbm.at[pl.ds(first, blk_rows)], zsem)

    def has_pad(e):
        return seg_ref[N_EXPERTS + e] > seg_ref[2 * N_EXPERTS + e]

    @pl.when(i == 0)
    def _():
        zero_ref[...] = jnp.zeros_like(zero_ref)

        def start(e, c):
            @pl.when(has_pad(e))
            def _():
                zero_fill(e).start()
            return c

        def wait(e, c):
            @pl.when(has_pad(e))
            def _():
                zero_fill(e).wait()
            return c

        lax.fori_loop(0, N_EXPERTS, start, 0)
        lax.fori_loop(0, N_EXPERTS, wait, 0)

        nvalid = seg_ref[3 * N_EXPERTS]
        nblk = xs_hbm.shape[0] // blk_rows

        def tail(j):
            first = pl.multiple_of((nvalid + j) * blk_rows, blk_rows)
            return pltpu.make_async_copy(zero_ref, xs_hbm.at[pl.ds(first, blk_rows)], zsem)

        def tail_start(j, c):
            @pl.when(nvalid + j < nblk)
            def _():
                tail(j).start()
            return c

        def tail_wait(j, c):
            @pl.when(nvalid + j < nblk)
            def _():
                tail(j).wait()
            return c

        lax.fori_loop(0, N_EXPERTS, tail_start, 0)
        lax.fori_loop(0, N_EXPERTS, tail_wait, 0)

    def row_copy(j, slot):
        return pltpu.make_async_copy(x_ref.at[pl.ds(pl.multiple_of(j * nsl, nsl), nsl)],
                                     xs_hbm.at[pl.ds(pl.multiple_of(slot * nsl, nsl), nsl)], sem)

    def issue(j, c):
        for k in range(2):
            row_copy(j, (d0_ref, d1_ref)[k][0, 0, j]).start(priority=k)
        return c

    lax.fori_loop(0, ntok, issue, 0, unroll=4)
    for _ in range(TOP_K):
        pltpu.make_async_copy(x_ref, xs_hbm.at[pl.ds(0, ntok * nsl)], sem).wait()


def _dispatch(x_tiles, dest, seg, cap, n):
    nsl = x_tiles.shape[0] // n
    tm = TOK_TM
    nt = n // tm
    return pl.pallas_call(
        _dispatch_kernel,
        out_shape=jax.ShapeDtypeStruct((cap * nsl, LANES), x_tiles.dtype),
        grid_spec=pltpu.PrefetchScalarGridSpec(
            num_scalar_prefetch=1,
            grid=(nt,),
            in_specs=[pl.BlockSpec((1, 1, tm), lambda i, seg: (i, 0, 0), memory_space=pltpu.SMEM),
                      pl.BlockSpec((1, 1, tm), lambda i, seg: (i, 0, 0), memory_space=pltpu.SMEM),
                      pl.BlockSpec((tm * nsl, LANES), lambda i, seg: (i, 0))],
            out_specs=pl.BlockSpec(memory_space=pl.ANY),
            scratch_shapes=[pltpu.VMEM((MOE_BLOCK * nsl, LANES), x_tiles.dtype), pltpu.SemaphoreType.DMA,
                            pltpu.SemaphoreType.DMA]),
        compiler_params=_params(("arbitrary",)),
        name="moe_dispatch",
    )(seg, dest[0], dest[1], x_tiles)


def _expert_kernel(blk_ref, nvalid_ref, xs_ref, wg_ref, wu_ref, wd_ref, ys_ref, wgb, wub, wdb):
    i = pl.program_id(0)
    prev = blk_ref[jnp.maximum(i - 1, 0)]
    valid = i < nvalid_ref[0]

    @pl.when(valid & ((i == 0) | (blk_ref[i] != prev)))
    def _():
        wgb[...] = wg_ref[0].astype(_BF16)
        wub[...] = wu_ref[0].astype(_BF16)
        wdb[...] = wd_ref[0].astype(_BF16)

    @pl.when(valid)
    def _():
        d = wgb.shape[0]
        xb = _rows_from_tiles(xs_ref, MOE_BLOCK, d).astype(_BF16)
        gate = jnp.dot(xb, wgb[...], preferred_element_type=_F32)
        up = jnp.dot(xb, wub[...], preferred_element_type=_F32)
        mid = (gate * jax.nn.sigmoid(gate)) * up
        _rows_to_tiles(ys_ref, jnp.dot(mid.astype(_BF16), wdb[...], preferred_element_type=_F32))

    @pl.when(jnp.logical_not(valid))
    def _():
        ys_ref[...] = jnp.zeros_like(ys_ref)


def _experts(xs, blk_exp, nvalid, w_gate, w_up, w_down):
    d, de = w_gate.shape[1], w_gate.shape[2]
    nsl = d // LANES
    cap = xs.shape[0] // nsl
    nblk = cap // MOE_BLOCK

    def xmap(i, blk, nv):
        return (jnp.minimum(i, nv[0] - 1), 0)

    def wmap(i, blk, nv):
        return (blk[i], 0, 0)

    return pl.pallas_call(
        _expert_kernel,
        out_shape=jax.ShapeDtypeStruct((cap * nsl, LANES), _F32),
        grid_spec=pltpu.PrefetchScalarGridSpec(
            num_scalar_prefetch=2,
            grid=(nblk,),
            in_specs=[pl.BlockSpec((MOE_BLOCK * nsl, LANES), xmap),
                      pl.BlockSpec((1, d, de), wmap), pl.BlockSpec((1, d, de), wmap),
                      pl.BlockSpec((1, de, d), wmap)],
            out_specs=pl.BlockSpec((MOE_BLOCK * nsl, LANES), lambda i, blk, nv: (i, 0)),
            scratch_shapes=[pltpu.VMEM((d, de), _BF16), pltpu.VMEM((d, de), _BF16), pltpu.VMEM((de, d), _BF16)]),
        compiler_params=_params(("arbitrary",)),
        name="moe_experts",
    )(blk_exp, nvalid, xs, w_gate, w_up, w_down)


def _combine_kernel(c0_ref, c1_ref, n0_ref, n1_ref, ys_hbm, h_ref, route_ref, g_ref, b_ref, o_ref, buf, sem):
    i = pl.program_id(0)
    n = pl.num_programs(0)
    tm = TOK_TM

    d = o_ref.shape[1]
    nsl = d // LANES

    def tile_copy(src_slot, slot, k, j):
        return pltpu.make_async_copy(ys_hbm.at[pl.ds(pl.multiple_of(src_slot * nsl, nsl), nsl)],
                                     buf.at[slot, k, pl.ds(pl.multiple_of(j * nsl, nsl), nsl)], sem.at[slot])

    def gather(drefs, slot):
        def body(j, c):
            for k in range(2):
                tile_copy(drefs[k][0, 0, j], slot, k, j).start(priority=k)
            return c
        lax.fori_loop(0, tm, body, 0, unroll=4)

    @pl.when(i == 0)
    def _():
        gather((c0_ref, c1_ref), 0)

    @pl.when(i + 1 < n)
    def _():
        gather((n0_ref, n1_ref), (i + 1) % 2)

    slot = i % 2

    for k in range(TOP_K):
        pltpu.make_async_copy(ys_hbm.at[pl.ds(0, tm * nsl)], buf.at[slot, k], sem.at[slot]).wait()
    r = route_ref[...]
    y = (r[:, 2:3] * _rows_from_tiles(buf.at[slot, 0], tm, d)
         + r[:, 3:4] * _rows_from_tiles(buf.at[slot, 1], tm, d))
    o_ref[...] = _layer_norm(ALPHA * _rows_from_tiles(h_ref, tm, d) + y, g_ref[...], b_ref[...])


def _combine(ys, dest, h1_tiles, route, g, b):
    n = route.shape[0]
    d = g.shape[1]
    nsl = d // LANES
    tm = TOK_TM
    nt = n // tm
    row = lambda i: (i, 0)
    const = lambda i: (0, 0)
    return pl.pallas_call(
        _combine_kernel,
        out_shape=jax.ShapeDtypeStruct((n, d), _F32),
        grid=(nt,),
        in_specs=[pl.BlockSpec((1, 1, tm), lambda i: (i, 0, 0), memory_space=pltpu.SMEM)] * TOP_K
        + [pl.BlockSpec((1, 1, tm), lambda i: (jnp.minimum(i + 1, nt - 1), 0, 0), memory_space=pltpu.SMEM)] * TOP_K
        + [pl.BlockSpec(memory_space=pl.ANY),
                  pl.BlockSpec((tm * nsl, LANES), row), pl.BlockSpec((tm, LANES), row),
                  pl.BlockSpec((1, d), const), pl.BlockSpec((1, d), const)],
        out_specs=pl.BlockSpec((tm, d), row),
        scratch_shapes=[pltpu.VMEM((2, 2, tm * nsl, LANES), _F32), pltpu.SemaphoreType.DMA((2,))],
        compiler_params=_params(("arbitrary",)),
        name="moe_combine",
    )(dest[0], dest[1], dest[0], dest[1], ys, h1_tiles, route, g, b)


def _moe(h1_tiles, route, route_t, counts_row, w_gate, w_up, w_down, g, b):
    n = route.shape[0]
    cap = 2 * n + N_EXPERTS * MOE_BLOCK
    counts = counts_row[0, ROUTE_LANE0:ROUTE_LANE0 + N_EXPERTS].astype(jnp.int32)
    padded = (counts + MOE_BLOCK - 1) // MOE_BLOCK * MOE_BLOCK
    ends = jnp.cumsum(padded)
    pad_start = ends - padded
    nblk = cap // MOE_BLOCK
    blk_first = jnp.arange(nblk, dtype=jnp.int32) * MOE_BLOCK
    blk_exp = jnp.minimum(jnp.sum(blk_first[:, None] >= ends[None, :], axis=1), N_EXPERTS - 1).astype(jnp.int32)
    nvalid = (ends[-1:] // MOE_BLOCK).astype(jnp.int32)
    dest = _dest(route_t, pad_start.astype(_F32)).reshape(TOP_K, n // TOK_TM, 1, TOK_TM)
    seg = jnp.concatenate([pad_start, padded, counts, nvalid]).astype(jnp.int32)
    xs = _dispatch(h1_tiles, dest, seg, cap, n)
    ys = _experts(xs, blk_exp, nvalid, w_gate, w_up, w_down)
    return _combine(ys, dest, h1_tiles, route, g, b)


def _router_params(rg_w, rg_b, re_w, re_b):
    d = rg_w.shape[0]
    pad = LANES - N_GROUPS - N_EXPERTS
    rw = jnp.concatenate([rg_w, re_w, jnp.zeros((d, pad), _F32)], axis=1)
    rb = jnp.concatenate([rg_b, re_b.reshape(-1), jnp.zeros((pad,), _F32)]).reshape(1, LANES)
    return rw, rb


def kernel(x, t5_table, l0_w_qkv, l0_sink, l0_w_o, l0_ln1_g, l0_ln1_b, l0_rg_w, l0_rg_b, l0_re_w, l0_re_b, l0_w_gate, l0_w_up, l0_w_down, l0_ln2_g, l0_ln2_b, l1_w_qkv, l1_w_o, l1_ln1_g, l1_ln1_b, l1_rg_w, l1_rg_b, l1_re_w, l1_re_b, l1_w_gate, l1_w_up, l1_w_down, l1_ln2_g, l1_ln2_b, l2_w_qkv, l2_rpb, l2_w_o, l2_ln1_g, l2_ln1_b, l2_rg_w, l2_rg_b, l2_re_w, l2_re_b, l2_w_gate, l2_w_up, l2_w_down, l2_ln2_g, l2_ln2_b, l3_w_qkv, l3_sink, l3_w_o, l3_ln1_g, l3_ln1_b, l3_rg_w, l3_rg_b, l3_re_w, l3_re_b, l3_w_gate, l3_w_up, l3_w_down, l3_ln2_g, l3_ln2_b):
    mixers = [(l0_w_qkv, l0_sink, l0_w_o), (l1_w_qkv, None, l1_w_o), (l2_w_qkv, l2_rpb, l2_w_o),
              (l3_w_qkv, l3_sink, l3_w_o)]
    norm1 = [(l0_ln1_g, l0_ln1_b), (l1_ln1_g, l1_ln1_b), (l2_ln1_g, l2_ln1_b), (l3_ln1_g, l3_ln1_b)]
    routers = [(l0_rg_w, l0_rg_b, l0_re_w, l0_re_b), (l1_rg_w, l1_rg_b, l1_re_w, l1_re_b),
               (l2_rg_w, l2_rg_b, l2_re_w, l2_re_b), (l3_rg_w, l3_rg_b, l3_re_w, l3_re_b)]
    experts = [(l0_w_gate, l0_w_up, l0_w_down), (l1_w_gate, l1_w_up, l1_w_down),
               (l2_w_gate, l2_w_up, l2_w_down), (l3_w_gate, l3_w_up, l3_w_down)]
    norm2 = [(l0_ln2_g, l0_ln2_b), (l1_ln2_g, l1_ln2_b), (l2_ln2_g, l2_ln2_b), (l3_ln2_g, l3_ln2_b)]

    bsz, seq, d = x.shape
    n = bsz * seq
    inner = N_HEADS * HEAD_DIM
    scale = HEAD_DIM ** -0.5 * LOG2E
    h = x.reshape(n, d)
    bias_a = _band_bias_t(t5_table, A_BLOCK, A_WINDOW, 1, A_BAND)
    for i in range(DEPTH):
        kind = i % N_MIXERS
        w_qkv, extra, w_o = mixers[i]
        kvw = A_KV_HEADS * HEAD_DIM if kind == 0 else inner
        lses, unperm = (), None
        if kind == 1:
            perms = np.stack([_residue_perm(dil) for _, dil in B_BRANCHES if dil > 1])
            outs = _project(h, w_qkv.astype(_BF16), (inner, kvw, kvw), (scale, 1.0, 1.0), jnp.asarray(perms, _BF16))
            unperm = jnp.asarray(perms.transpose(0, 2, 1), _BF16)
            attn, lses = [], []
            for bi, (window, dil) in enumerate(B_BRANCHES):
                q, k, v = outs[3 * bi:3 * bi + 3]
                o, lse = _mixer_b_branch(q, k, v, t5_table, bsz, seq, window, dil)
                attn.append(o)
                lses.append(jnp.pad(lse.reshape(n, N_HEADS), ((0, 0), (0, LANES - N_HEADS))))
        else:
            q, k, v = _project(h, w_qkv.astype(_BF16), (inner, kvw, kvw), (scale, 1.0, 1.0))
            q, k, v = (t.reshape(bsz, seq, t.shape[1]) for t in (q, k, v))
            if kind == 0:
                attn = [_mixer_a(q, k, v, extra.astype(_F32) * LOG2E, bias_a, bsz, seq)]
            else:
                attn = [_mixer_c(q, k, v, _na_bias_t(extra, seq // GRID_W), bsz, seq)]
        attn = [a.reshape(n, inner) for a in attn]
        g1, b1 = norm1[i]
        rw, rb = _router_params(*routers[i])
        h1, route, route_t, counts = _post_attn(attn, lses, h, w_o.astype(_BF16), g1.reshape(1, d), b1.reshape(1, d),
                                                rw, rb, unperm)
        g2, b2 = norm2[i]
        h = _moe(h1, route, route_t, counts, *experts[i], g2.reshape(1, d), b2.reshape(1, d))
    return h.reshape(bsz, seq, d)
```

```python
import functools
import math

import jax
import jax.numpy as jnp
import numpy as np
from jax import lax
from jax.experimental import pallas as pl
from jax.experimental.pallas import tpu as pltpu

HEAD_DIM = 64
N_HEADS = 16
A_KV_HEADS = 4
A_WINDOW = 128
A_BLOCK = 128
B_BRANCHES = ((128, 1), (512, 4), (2048, 16))
GRID_W = 64
NA_ROWS = 8
NA_COLS = 16
T5_BUCKETS = 32
T5_MAX_DIST = 1024
N_GROUPS = 4
EXPERTS_PER_GROUP = 8
N_EXPERTS = N_GROUPS * EXPERTS_PER_GROUP
TOP_K = 2
DEPTH = 4
N_MIXERS = 3
ALPHA = (2 * DEPTH) ** 0.25
LN_EPS = 1e-5
NEG_INF = -1e30

LANES = 128
PAIR = 2 * HEAD_DIM
BF16_SUBLANES = 16
VT_ROWS = PAIR + BF16_SUBLANES
LOG2E = math.log2(math.e)
LN2 = math.log(2.0)
VMEM_LIMIT = 56 * 1024 * 1024

PROJ_TM = 512
PROJ_CHUNK = 512
POST_TM = 1024
POST_SUB = 256
TOK_TM = 1024
DEST_TM = 8192
ROUTE_ROWS = 8
MOE_BLOCK = 512
A_TQ = 2048
A_BAND = 3
A_SUB_PER_GROUP = 4
B_BLOCK = 128
RESIDUE_RUN = BF16_SUBLANES
PERM_CHUNK = 256
TRANSPOSE_UNROLL = 8
UNITS_PER_GROUP = 16
C_UNIT_ROWS = 2
C_BAND_ROWS = 10
C_UNITS_PER_GROUP = 8
C_ROWS_PER_STEP = 64
ROUTE_LANE0 = N_GROUPS

_F32 = jnp.float32
_BF16 = jnp.bfloat16


def _params(sem, vmem=VMEM_LIMIT, flags=None):
    return pltpu.CompilerParams(dimension_semantics=sem, vmem_limit_bytes=vmem, flags=flags)


def _proj_kernel(x_ref, w_ref, *refs, widths, scales, nperm):
    perm_ref = refs[0] if nperm else None
    o_refs = refs[1:] if nperm else refs
    nout = len(widths)
    xb = x_ref[...].astype(_BF16)
    col = 0
    for t, (width, scale) in enumerate(zip(widths, scales)):
        for c in range(0, width, PROJ_CHUNK):
            cw = min(PROJ_CHUNK, width - c)
            acc = jnp.dot(xb, w_ref[:, col + c:col + c + cw], preferred_element_type=_F32)
            if scale != 1.0:
                acc = acc * scale
            ob = acc.astype(_BF16)
            o_refs[t][:, c:c + cw] = ob
            for p in range(nperm):
                for r0 in range(0, ob.shape[0], PERM_CHUNK):
                    moved = jnp.dot(perm_ref[p], ob[r0:r0 + PERM_CHUNK], preferred_element_type=_F32)
                    o_refs[(p + 1) * nout + t][r0:r0 + PERM_CHUNK, c:c + cw] = moved.astype(_BF16)
        col += width


def _project(x2, w_bf16, widths, scales, perms=None):
    n, d = x2.shape
    ctot = sum(widths)
    nperm = 0 if perms is None else perms.shape[0]
    args = [x2, w_bf16] + ([perms] if nperm else [])
    specs = [pl.BlockSpec((PROJ_TM, d), lambda i: (i, 0)), pl.BlockSpec((d, ctot), lambda i: (0, 0))]
    if nperm:
        specs.append(pl.BlockSpec(perms.shape, lambda i: (0, 0, 0)))
    return pl.pallas_call(
        functools.partial(_proj_kernel, widths=tuple(widths), scales=tuple(scales), nperm=nperm),
        out_shape=[jax.ShapeDtypeStruct((n, w), _BF16) for w in widths] * (nperm + 1),
        grid=(n // PROJ_TM,),
        in_specs=specs,
        out_specs=[pl.BlockSpec((PROJ_TM, w), lambda i: (i, 0)) for w in widths] * (nperm + 1),
        compiler_params=_params(("arbitrary",)),
        name="qkv_proj",
    )(*args)


def _residue_perm(dil):
    span = dil * RESIDUE_RUN
    p = np.zeros((PERM_CHUNK, PERM_CHUNK), np.float32)
    for base in range(0, PERM_CHUNK, span):
        for r in range(dil):
            for u in range(RESIDUE_RUN):
                p[base + r * RESIDUE_RUN + u, base + u * dil + r] = 1.0
    return p


def _one():
    return jnp.ones((1,), jnp.int32)


def _lane_lo(rows=1):
    return lax.broadcasted_iota(jnp.int32, (rows, PAIR), 1) < HEAD_DIM


def _swap_halves(x):
    return pltpu.roll(x.astype(_F32), HEAD_DIM, axis=1).astype(_BF16)


class _PairUnit:
    def __init__(self, q2, k, vt, bias, emit, sink_a=None, sink_b=None):
        self.q2, self.k, self.vt, self.bias, self.emit = q2, k, vt, bias, emit
        self.sink_a, self.sink_b = sink_a, sink_b


def _attend_groups(ngroups, units_of, scratch, one):
    assert ngroups % 2 == 0
    lo = _lane_lo()
    nt = (((1,), (1,)), ((), ()))

    def sink_row(unit, m2):
        lane = lax.broadcasted_iota(jnp.int32, (1, m2), 1)
        return jnp.where(lane < m2 // 2, unit.sink_a, unit.sink_b)

    def scores(g, parity):
        s_scr, m_scr = scratch[parity]
        for u, unit in enumerate(units_of(g, parity)):
            q2 = unit.q2()
            zero = jnp.zeros_like(q2)
            qq = jnp.concatenate([jnp.where(lo, q2, zero), jnp.where(lo, zero, q2)], axis=0)
            s = lax.dot_general(unit.k(), qq, nt, preferred_element_type=_F32) + unit.bias()
            m = jnp.max(s, axis=0, keepdims=True)
            if unit.sink_a is not None:
                m = jnp.maximum(m, sink_row(unit, s.shape[1]))
            s_scr[u] = s
            m_scr[u] = m

    def values(g, parity):
        s_scr, m_scr = scratch[parity]
        for u, unit in enumerate(units_of(g, parity)):
            m = m_scr[u]
            e = jnp.exp2(s_scr[u] - m)
            ot = jnp.dot(unit.vt(), e.astype(_BF16), preferred_element_type=_F32)
            den = ot[PAIR:PAIR + 1, :]
            if unit.sink_a is not None:
                den = den + jnp.exp2(sink_row(unit, e.shape[1]) - m)
            ot = ot[:PAIR] * (1.0 / den)
            mq = ot.shape[1] // 2
            top = lax.broadcasted_iota(jnp.int32, (PAIR, 1), 0) < HEAD_DIM
            pair_t = jnp.where(top, ot[:, :mq], ot[:, mq:])
            unit.emit(pair_t.T, (m + jnp.log2(den)) * LN2)

    def region(*work):
        def body(_, carry):
            for fn, g, parity in work:
                fn(g, parity)
            return carry
        lax.fori_loop(0, one, body, 0)

    region((scores, 0, 0))

    def steady(i, carry):
        g = 2 * i
        region((values, g, 0), (scores, g + 1, 1))
        region((values, g + 1, 1), (scores, g + 2, 0))
        return carry

    lax.fori_loop(0, ngroups // 2 - 1, steady, 0)
    region((values, ngroups - 2, 0), (scores, ngroups - 1, 1))
    region((values, ngroups - 1, 1))


def _transpose_blocks(v_ref_rows, nblk, store):
    def body(j, carry):
        rows = pl.ds(pl.multiple_of(j * LANES, LANES), LANES)
        vt = v_ref_rows(rows).T
        store(j, jnp.concatenate([vt, jnp.ones((VT_ROWS - PAIR, LANES), _BF16)], axis=0))
        return carry
    lax.fori_loop(0, nblk, body, 0, unroll=TRANSPOSE_UNROLL)


def _t5_bucket(rel):
    half = T5_BUCKETS // 2
    exact = half // 2
    n = np.abs(rel)
    big = exact + (np.log(np.maximum(n, 1) / exact) / math.log(T5_MAX_DIST / exact) * (half - exact)).astype(np.int32)
    return ((rel > 0) * half + np.where(n < exact, n, np.minimum(big, half - 1))).astype(np.int32)


def _band_bias_t(t5_table, blk, window, dil, nband):
    tk = nband * blk
    off = tk - 1
    rel = np.arange(-off, tk)
    vec = jnp.take(t5_table, jnp.asarray(_t5_bucket(rel * dil)), axis=0).astype(_F32)
    vec = jnp.where(jnp.asarray(np.abs(rel) <= window)[:, None], vec * LOG2E, NEG_INF).T
    h = vec.shape[0]
    vec = jnp.pad(vec, ((0, 0), (0, 1)))
    w = 2 * tk - 1
    skew = jnp.tile(vec, (1, blk))[:, :blk * w].reshape(h, blk, w)
    t = jnp.stack([skew[:, :, off - v * blk:off - v * blk + tk] for v in range(nband)], axis=1)
    t = t.reshape(h // 2, 2, nband, blk, tk).transpose(0, 2, 4, 1, 3)
    return t.reshape(h // 2, nband, tk, 2 * blk)


def _band_window(gb, blk, length, nband):
    first = jnp.clip(gb - 1, 0, length // blk - nband)
    return first, gb - first


def _attn_a_kernel(sink_ref, one_ref, q_ref, k_ref, v_ref, bias_ref, o_ref, kd_ref, vdt_ref,
                   s0_scr, m0_scr, s1_scr, m1_scr, *, seq):
    kvb = pl.program_id(0)
    qt = pl.program_id(2)
    one = one_ref[0]
    nblk = seq // A_BLOCK

    @pl.when(qt == 0)
    def _():
        lo = _lane_lo()

        def fill(j, carry):
            rows = pl.ds(pl.multiple_of(j * A_BLOCK, A_BLOCK), A_BLOCK)
            kk = k_ref[0, rows, :]
            ks = _swap_halves(kk)
            kd_ref[0, rows, :] = jnp.where(lo, kk, ks)
            kd_ref[1, rows, :] = jnp.where(lo, ks, kk)
            vt = v_ref[0, rows, :].T
            ones = jnp.ones((VT_ROWS - PAIR, A_BLOCK), _BF16)
            vdt_ref[0, j] = jnp.concatenate([vt[:HEAD_DIM], vt[:HEAD_DIM], ones], axis=0)
            vdt_ref[1, j] = jnp.concatenate([vt[HEAD_DIM:], vt[HEAD_DIM:], ones], axis=0)
            return carry

        lax.fori_loop(0, nblk, fill, 0, unroll=TRANSPOSE_UNROLL)

    nsub = A_TQ // A_BLOCK
    npair = q_ref.shape[2] // PAIR
    hbase = kvb * (2 * npair)

    def units_of(g, parity):
        units = []
        for j in range(A_SUB_PER_GROUP):
            si = g * A_SUB_PER_GROUP + j
            first, var = _band_window(qt * nsub + si, A_BLOCK, seq, A_BAND)
            rows = pl.ds(pl.multiple_of(si * A_BLOCK, A_BLOCK), A_BLOCK)
            band = pl.ds(pl.multiple_of(first * A_BLOCK, A_BLOCK), A_BAND * A_BLOCK)
            for jj in range(npair):
                hk = jj // (npair // 2)
                cols = slice(jj * PAIR, (jj + 1) * PAIR)

                def emit(o, lse, rows=rows, cols=cols):
                    o_ref[0, rows, cols] = o.astype(o_ref.dtype)

                units.append(_PairUnit(
                    q2=lambda rows=rows, cols=cols: q_ref[0, rows, cols],
                    k=lambda hk=hk, band=band: kd_ref[hk, band, :],
                    vt=lambda hk=hk, first=first: jnp.concatenate(
                        [vdt_ref[hk, first + t] for t in range(A_BAND)], axis=1),
                    bias=lambda jj=jj, var=var: bias_ref[jj, var],
                    emit=emit, sink_a=sink_ref[hbase + 2 * jj], sink_b=sink_ref[hbase + 2 * jj + 1]))
        return units

    _attend_groups(nsub // A_SUB_PER_GROUP, units_of, ((s0_scr, m0_scr), (s1_scr, m1_scr)), one)


def _mixer_a(q, k, v, sink, bias, bsz, seq):
    nkvb = k.shape[2] // PAIR
    qw = q.shape[2] // nkvb
    npair = qw // PAIR
    tk = A_BAND * A_BLOCK
    nunits = A_SUB_PER_GROUP * npair
    group_scratch = [pltpu.VMEM((nunits, tk, 2 * A_BLOCK), _F32), pltpu.VMEM((nunits, 1, 2 * A_BLOCK), _F32)]
    return pl.pallas_call(
        functools.partial(_attn_a_kernel, seq=seq),
        out_shape=jax.ShapeDtypeStruct(q.shape, _BF16),
        grid=(nkvb, bsz, seq // A_TQ),
        in_specs=[pl.BlockSpec(memory_space=pltpu.SMEM), pl.BlockSpec(memory_space=pltpu.SMEM),
                  pl.BlockSpec((1, A_TQ, qw), lambda c, b, t: (b, t, c)),
                  pl.BlockSpec((1, seq, PAIR), lambda c, b, t: (b, 0, c)),
                  pl.BlockSpec((1, seq, PAIR), lambda c, b, t: (b, 0, c)),
                  pl.BlockSpec((npair, A_BAND, tk, 2 * A_BLOCK), lambda c, b, t: (c, 0, 0, 0))],
        out_specs=pl.BlockSpec((1, A_TQ, qw), lambda c, b, t: (b, t, c)),
        scratch_shapes=[pltpu.VMEM((2, seq, PAIR), _BF16), pltpu.VMEM((2, seq // A_BLOCK, VT_ROWS, A_BLOCK), _BF16)]
        + group_scratch + group_scratch,
        compiler_params=_params(("arbitrary", "arbitrary", "arbitrary")),
        name="mixer_a",
    )(sink, _one(), q, k, v, bias)


def _attn_band_kernel(one_ref, q_ref, k_ref, v_ref, bias_ref, o_ref, lse_ref, vt_ref,
                      s0_scr, m0_scr, s1_scr, m1_scr, *, length, npairs, nband, chunked):
    col = pl.program_id(2)
    one = one_ref[0]
    nblk = length // B_BLOCK
    runs = B_BLOCK // RESIDUE_RUN

    def load(ref, blk0, nblocks, cols):
        if chunked:
            r0 = pl.multiple_of(blk0 * runs, runs)
            return ref[pl.ds(r0, nblocks * runs), :, cols].reshape(nblocks * B_BLOCK, PAIR)
        return ref[0, pl.ds(pl.multiple_of(blk0 * B_BLOCK, B_BLOCK), nblocks * B_BLOCK), cols]

    def store_out(blk0, cols, val):
        if chunked:
            o_ref[pl.ds(pl.multiple_of(blk0 * runs, runs), runs), :, cols] = val.reshape(runs, RESIDUE_RUN, PAIR)
        else:
            o_ref[0, pl.ds(pl.multiple_of(blk0 * B_BLOCK, B_BLOCK), B_BLOCK), cols] = val

    pairs_per_loop = max(1, min(npairs, TRANSPOSE_UNROLL // min(TRANSPOSE_UNROLL, nblk) * 2))
    for p0 in range(0, npairs, pairs_per_loop):
        def fill(j, carry, p0=p0):
            for p in range(p0, p0 + pairs_per_loop):
                vt = load(v_ref, j, 1, slice(p * PAIR, (p + 1) * PAIR)).T
                vt_ref[p, j] = jnp.concatenate([vt, jnp.ones((VT_ROWS - PAIR, B_BLOCK), _BF16)], axis=0)
            return carry

        lax.fori_loop(0, nblk, fill, 0, unroll=min(TRANSPOSE_UNROLL, nblk))

    def unit(gb, p):
        first, var = _band_window(gb, B_BLOCK, length, nband)
        cols = slice(p * PAIR, (p + 1) * PAIR)

        def emit(o, lse):
            store_out(gb, cols, o.astype(o_ref.dtype))
            ha = col * (2 * npairs) + 2 * p
            lse_ref[0, 0, gb, pl.ds(ha, 1), :] = lse[:, :B_BLOCK]
            lse_ref[0, 0, gb, pl.ds(ha + 1, 1), :] = lse[:, B_BLOCK:]

        return _PairUnit(
            q2=lambda: load(q_ref, gb, 1, cols), k=lambda: load(k_ref, first, nband, cols),
            vt=lambda: jnp.concatenate([vt_ref[p, first + t] for t in range(nband)], axis=1),
            bias=lambda: bias_ref[p, var], emit=emit)

    per_group = _band_blocks_per_group(nblk, npairs)
    units_of = lambda g, parity: [unit(g * per_group + j, p) for j in range(per_group) for p in range(npairs)]
    _attend_groups(nblk // per_group, units_of, ((s0_scr, m0_scr), (s1_scr, m1_scr)), one)


def _band_blocks_per_group(nblk, npairs):
    return max(1, min(UNITS_PER_GROUP // npairs, nblk // 2))


def _mixer_b_branch(q, k, v, t5_table, bsz, seq, window, dil):
    length = seq // dil
    width = q.shape[1]
    nblk = length // B_BLOCK
    nband = min(3, nblk)
    tk = nband * B_BLOCK
    npairs = (width // PAIR) if length * width * 2 <= 2 * 1024 * 1024 else 1
    ncol = width // (npairs * PAIR)
    bw = npairs * PAIR
    nunits = _band_blocks_per_group(nblk, npairs) * npairs
    bias = _band_bias_t(t5_table, B_BLOCK, window // 2 // dil, dil, nband)
    chunked = dil > 1
    if chunked:
        nrun = length // RESIDUE_RUN
        shape = (bsz * nrun, dil, RESIDUE_RUN, width)
        spec = pl.BlockSpec((nrun, None, RESIDUE_RUN, bw), lambda b, r, c: (b, r, 0, c))
    else:
        shape = (bsz, seq, width)
        spec = pl.BlockSpec((1, seq, bw), lambda b, r, c: (b, 0, c))

    o, lse = pl.pallas_call(
        functools.partial(_attn_band_kernel, length=length, npairs=npairs, nband=nband, chunked=chunked),
        out_shape=[jax.ShapeDtypeStruct(shape, _BF16),
                   jax.ShapeDtypeStruct((bsz, dil, nblk, N_HEADS, B_BLOCK), _F32)],
        grid=(bsz, dil, ncol),
        in_specs=[pl.BlockSpec(memory_space=pltpu.SMEM), spec, spec, spec,
                  pl.BlockSpec((npairs, nband, tk, 2 * B_BLOCK), lambda b, r, c: (c, 0, 0, 0))],
        out_specs=[spec, pl.BlockSpec((1, 1, nblk, N_HEADS, B_BLOCK), lambda b, r, c: (b, r, 0, 0, 0))],
        scratch_shapes=[pltpu.VMEM((npairs, nblk, VT_ROWS, B_BLOCK), _BF16)] + 2 * [
            pltpu.VMEM((nunits, tk, 2 * B_BLOCK), _F32), pltpu.VMEM((nunits, 1, 2 * B_BLOCK), _F32)],
        compiler_params=_params(("arbitrary", "arbitrary", "arbitrary")),
        name=f"mixer_b_dil{dil}",
    )(_one(), q.reshape(shape), k.reshape(shape), v.reshape(shape), bias)
    lse = lse.transpose(0, 2, 4, 1, 3).reshape(bsz, seq, N_HEADS)
    return o.reshape(bsz * seq, width), lse


def _na_bias_t(rpb, rows):
    canon = (0, 2, 4, rows - 4, rows - 2)
    h = rpb.shape[0]
    kc = np.arange(GRID_W)[:, None]
    qc = np.arange(GRID_W)[None, :]
    ws = np.clip(qc - NA_COLS // 2, 0, GRID_W - NA_COLS)
    col_ok = (kc >= ws) & (kc < ws + NA_COLS)
    col_idx = np.clip(kc - qc, 1 - NA_COLS, NA_COLS - 1) + NA_COLS - 1
    scaled = rpb.astype(_F32) * LOG2E
    idx = jnp.asarray(np.where(col_ok, col_idx, -1))
    colpart = jnp.full((h, rpb.shape[1], GRID_W, GRID_W), NEG_INF, _F32)
    for d in range(rpb.shape[2]):
        colpart = jnp.where(idx == d, scaled[:, :, d, None, None], colpart)
    masked = jnp.full((h, GRID_W, GRID_W), NEG_INF, _F32)
    variants = []
    for i in canon:
        start = int(np.clip(i - NA_ROWS // 2, 0, rows - C_BAND_ROWS))
        per_key_row = []
        for kr in range(C_BAND_ROWS):
            per_query_row = []
            for qr in range(C_UNIT_ROWS):
                iq, kabs = i + qr, start + kr
                first_key_row = int(np.clip(iq - NA_ROWS // 2, 0, rows - NA_ROWS))
                inside = first_key_row <= kabs < first_key_row + NA_ROWS
                per_query_row.append(colpart[:, kabs - iq + NA_ROWS - 1] if inside else masked)
            per_key_row.append(jnp.concatenate(per_query_row, axis=-1))
        variants.append(jnp.concatenate(per_key_row, axis=1))
    b = jnp.stack(variants, axis=1)
    b = b.reshape(h // 2, 2, *b.shape[1:])
    return jnp.concatenate([b[:, 0], b[:, 1]], axis=-1)


def _attn_c_kernel(one_ref, q_ref, k_ref, v_ref, bias_ref, o_ref, vt_ref, s0_scr, m0_scr, s1_scr, m1_scr, *, rows):
    rt = pl.program_id(2)
    nq = C_UNIT_ROWS * GRID_W
    band_blocks = C_BAND_ROWS * GRID_W // LANES

    @pl.when(rt == 0)
    def _():
        def store(j, blk):
            vt_ref[j] = blk
        _transpose_blocks(lambda r: v_ref[0, r, :], rows * GRID_W // LANES, store)

    def unit(u):
        i = rt * C_ROWS_PER_STEP + C_UNIT_ROWS * u
        start = jnp.clip(i - NA_ROWS // 2, 0, rows - C_BAND_ROWS)
        var = lax.shift_right_logical(i - start, 1)
        first = lax.shift_right_logical(start, 1)
        keys = pl.ds(pl.multiple_of(first * LANES, LANES), C_BAND_ROWS * GRID_W)
        qrows = pl.ds(pl.multiple_of(u * nq, nq), nq)

        def emit(o, lse):
            o_ref[0, qrows, :] = o.astype(o_ref.dtype)

        return _PairUnit(
            q2=lambda: q_ref[0, qrows, :], k=lambda: k_ref[0, keys, :],
            vt=lambda: jnp.concatenate([vt_ref[first + t] for t in range(band_blocks)], axis=1),
            bias=lambda: bias_ref[0, var], emit=emit)

    ngroups = C_ROWS_PER_STEP // C_UNIT_ROWS // C_UNITS_PER_GROUP
    units_of = lambda g, parity: [unit(g * C_UNITS_PER_GROUP + j) for j in range(C_UNITS_PER_GROUP)]
    _attend_groups(ngroups, units_of, ((s0_scr, m0_scr), (s1_scr, m1_scr)), one_ref[0])


def _mixer_c(q, k, v, bias, bsz, seq):
    rows = seq // GRID_W
    tq = C_ROWS_PER_STEP * GRID_W
    npair = q.shape[2] // PAIR
    nq, nk = C_UNIT_ROWS * GRID_W, C_BAND_ROWS * GRID_W
    return pl.pallas_call(
        functools.partial(_attn_c_kernel, rows=rows),
        out_shape=jax.ShapeDtypeStruct(q.shape, _BF16),
        grid=(npair, bsz, seq // tq),
        in_specs=[pl.BlockSpec(memory_space=pltpu.SMEM),
                  pl.BlockSpec((1, tq, PAIR), lambda c, b, t: (b, t, c)),
                  pl.BlockSpec((1, seq, PAIR), lambda c, b, t: (b, 0, c)),
                  pl.BlockSpec((1, seq, PAIR), lambda c, b, t: (b, 0, c)),
                  pl.BlockSpec((1, bias.shape[1], nk, 2 * nq), lambda c, b, t: (c, 0, 0, 0))],
        out_specs=pl.BlockSpec((1, tq, PAIR), lambda c, b, t: (b, t, c)),
        scratch_shapes=[pltpu.VMEM((seq // LANES, VT_ROWS, LANES), _BF16)] + 2 * [
            pltpu.VMEM((C_UNITS_PER_GROUP, nk, 2 * nq), _F32), pltpu.VMEM((C_UNITS_PER_GROUP, 1, 2 * nq), _F32)],
        compiler_params=_params(("arbitrary", "arbitrary", "arbitrary")),
        name="mixer_c",
    )(_one(), q, k, v, bias)


def _layer_norm(z, g, b):
    mu = jnp.mean(z, axis=-1, keepdims=True)
    zc = z - mu
    var = jnp.mean(zc * zc, axis=-1, keepdims=True)
    return zc * lax.rsqrt(var + LN_EPS) * g + b


def _rows_to_tiles(ref, x):
    nrows, d = x.shape
    nsl = d // LANES
    for s in range(nsl):
        ref[pl.ds(s, nrows, stride=nsl), :] = x[:, s * LANES:(s + 1) * LANES]


def _rows_from_tiles(ref, nrows, d):
    nsl = d // LANES
    return jnp.concatenate([ref[pl.ds(s, nrows, stride=nsl), :] for s in range(nsl)], axis=1)


def _split_bf16(x):
    hi = x.astype(_BF16)
    return hi, (x - hi.astype(_F32)).astype(_BF16)


def _router_logits(hn, rw_ref, rb_ref):
    h_hi, h_lo = _split_bf16(hn)
    w_hi, w_lo = _split_bf16(rw_ref[...])
    both = jnp.dot(h_hi, jnp.concatenate([w_hi, w_lo], axis=1), preferred_element_type=_F32)
    return (both[:, :LANES]
            + (both[:, LANES:] + jnp.dot(h_lo, w_hi, preferred_element_type=_F32))) + rb_ref[...]


def _route(logits, tri_ref, carry_ref):
    tm = logits.shape[0]
    lane = lax.broadcasted_iota(jnp.int32, (tm, LANES), 1).astype(_F32)
    ninf = -jnp.inf
    is_grp = lane < N_GROUPS
    lg = jnp.where(is_grp, logits, ninf)
    gmax = jnp.max(lg, axis=-1, keepdims=True)
    grp = jnp.min(jnp.where(lg == gmax, lane, float(LANES)), axis=-1, keepdims=True)
    p_grp = 1.0 / jnp.sum(jnp.where(is_grp, jnp.exp(lg - gmax), 0.0), axis=-1, keepdims=True)
    lo_lane = ROUTE_LANE0 + EXPERTS_PER_GROUP * grp
    in_grp = (lane >= lo_lane) & (lane < lo_lane + EXPERTS_PER_GROUP)
    le = jnp.where(in_grp, logits, ninf)
    v1 = jnp.max(le, axis=-1, keepdims=True)
    i1 = jnp.min(jnp.where(le == v1, lane, float(LANES)), axis=-1, keepdims=True)
    le2 = jnp.where(lane == i1, ninf, le)
    v2 = jnp.max(le2, axis=-1, keepdims=True)
    i2 = jnp.min(jnp.where(le2 == v2, lane, float(LANES)), axis=-1, keepdims=True)
    t = jnp.exp(v2 - v1)
    s1 = 1.0 / (1.0 + t)
    g1 = p_grp * s1
    g2 = p_grp * (t * s1)
    sel1 = lane == i1
    sel2 = lane == i2
    onehot = jnp.where(sel1 | sel2, 1.0, 0.0)
    sub = tri_ref.shape[0]
    carry = carry_ref[...]
    parts = []
    for r0 in range(0, tm, sub):
        part = onehot[r0:r0 + sub]
        parts.append(jnp.dot(tri_ref[...], part.astype(_BF16), preferred_element_type=_F32) + carry)
        carry = carry + jnp.sum(part, axis=0, keepdims=True)
    carry_ref[...] = carry
    before = jnp.concatenate(parts, axis=0) if len(parts) > 1 else parts[0]
    rank1 = jnp.sum(jnp.where(sel1, before, 0.0), axis=-1, keepdims=True)
    rank2 = jnp.sum(jnp.where(sel2, before, 0.0), axis=-1, keepdims=True)
    out = jnp.zeros((tm, LANES), _F32)
    for idx, val in enumerate((i1 - ROUTE_LANE0, i2 - ROUTE_LANE0, g1, g2, rank1, rank2)):
        out = jnp.where(lane == idx, val, out)
    return out


def _post_attn_kernel(*refs, n_branch):
    attn_refs = refs[:n_branch]
    pos = n_branch
    if n_branch > 1:
        lse_refs = refs[pos:pos + n_branch]
        expand_ref, unperm_ref = refs[pos + n_branch:pos + n_branch + 2]
        pos += n_branch + 2
    wo_ref, h_ref, g_ref, b_ref, rw_ref, rb_ref, tri_ref = refs[pos:pos + 7]
    h1_ref, route_ref, routet_ref, counts_ref, carry_ref = refs[pos + 7:pos + 12]

    @pl.when(pl.program_id(0) == 0)
    def _():
        carry_ref[...] = jnp.zeros_like(carry_ref)

    d = h_ref.shape[1]
    logits = []
    for sub in range(h_ref.shape[0] // POST_SUB):
        rows = slice(sub * POST_SUB, (sub + 1) * POST_SUB)
        if n_branch == 1:
            a = attn_refs[0][rows, :]
        else:
            lses = [r[rows, :] for r in lse_refs]
            mx = functools.reduce(jnp.maximum, lses)
            es = [jnp.exp(l - mx) for l in lses]
            inv = 1.0 / functools.reduce(jnp.add, es)
            mixed = None
            for bi, (e, a_ref) in enumerate(zip(es, attn_refs)):
                w_hi, w_lo = _split_bf16(e * inv)
                w_full = (jnp.dot(w_hi, expand_ref[...], preferred_element_type=_F32)
                          + jnp.dot(w_lo, expand_ref[...], preferred_element_type=_F32))
                if bi == 0:
                    a_nat = a_ref[rows, :].astype(_F32)
                else:
                    a_nat = jnp.dot(unperm_ref[bi - 1], a_ref[rows, :], preferred_element_type=_F32)
                term = w_full * a_nat
                mixed = term if mixed is None else mixed + term
            a = mixed.astype(_BF16)
        y = jnp.dot(a, wo_ref[...], preferred_element_type=_F32)
        hn = _layer_norm(ALPHA * h_ref[rows, :] + y, g_ref[...], b_ref[...])
        nsl = d // LANES
        _rows_to_tiles(h1_ref.at[pl.ds(sub * POST_SUB * nsl, POST_SUB * nsl)], hn)
        logits.append(_router_logits(hn, rw_ref, rb_ref))
    route = _route(jnp.concatenate(logits, axis=0), tri_ref, carry_ref)
    route_ref[...] = route
    routet_ref[...] = route.T[:ROUTE_ROWS]
    counts_ref[...] = carry_ref[...]


def _post_attn(attn, lses, h2, w_o, g, b, rw, rb, unperm=None):
    n, d = h2.shape
    tm = POST_TM
    nb = len(attn)
    tri = jnp.asarray(np.tril(np.ones((POST_SUB, POST_SUB), np.float32), -1), _BF16)
    row = lambda i: (i, 0)
    const = lambda i: (0, 0)
    args = list(attn)
    specs = [pl.BlockSpec((tm, d), row) for _ in attn]
    if nb > 1:
        expand = np.zeros((N_HEADS, d), np.float32)
        for hd in range(N_HEADS):
            expand[hd, hd * HEAD_DIM:(hd + 1) * HEAD_DIM] = 1.0
        args += list(lses) + [jnp.asarray(expand, _BF16), unperm]
        specs += [pl.BlockSpec((tm, N_HEADS), row) for _ in lses] + [
            pl.BlockSpec((N_HEADS, d), const), pl.BlockSpec(unperm.shape, lambda i: (0, 0, 0))]
    args += [w_o, h2, g, b, rw, rb, tri]
    specs += [pl.BlockSpec((d, d), const), pl.BlockSpec((tm, d), row), pl.BlockSpec((1, d), const),
              pl.BlockSpec((1, d), const), pl.BlockSpec((d, LANES), const), pl.BlockSpec((1, LANES), const),
              pl.BlockSpec((POST_SUB, POST_SUB), const)]
    return pl.pallas_call(
        functools.partial(_post_attn_kernel, n_branch=nb),
        out_shape=[jax.ShapeDtypeStruct((n * d // LANES, LANES), _F32), jax.ShapeDtypeStruct((n, LANES), _F32),
                   jax.ShapeDtypeStruct((ROUTE_ROWS, n), _F32),
                   jax.ShapeDtypeStruct((1, LANES), _F32)],
        grid=(n // tm,),
        in_specs=specs,
        out_specs=[pl.BlockSpec((tm * d // LANES, LANES), row), pl.BlockSpec((tm, LANES), row),
                   pl.BlockSpec((ROUTE_ROWS, tm), lambda i: (0, i)), pl.BlockSpec((1, LANES), const)],
        scratch_shapes=[pltpu.VMEM((1, LANES), _F32)],
        compiler_params=_params(("arbitrary",)),
        name="post_attn_router",
    )(*args)


def _dest_kernel(ps_ref, rt_ref, o_ref):
    expert = rt_ref[0:TOP_K, :]
    slot = rt_ref[4:4 + TOP_K, :]
    for e in range(N_EXPERTS):
        slot = slot + jnp.where(expert == float(e), ps_ref[e], 0.0)
    o_ref[...] = slot.astype(jnp.int32)


def _dest(route_t, pad_start):
    n = route_t.shape[1]
    tm = min(DEST_TM, n)
    return pl.pallas_call(
        _dest_kernel,
        out_shape=jax.ShapeDtypeStruct((TOP_K, n), jnp.int32),
        grid=(n // tm,),
        in_specs=[pl.BlockSpec(memory_space=pltpu.SMEM), pl.BlockSpec((ROUTE_ROWS, tm), lambda i: (0, i))],
        out_specs=pl.BlockSpec((TOP_K, tm), lambda i: (0, i)),
        compiler_params=_params(("arbitrary",)),
        name="moe_dest",
    )(pad_start, route_t)


def _dispatch_kernel(seg_ref, d0_ref, d1_ref, x_ref, xs_hbm, zero_ref, zsem, sem):
    i = pl.program_id(0)
    ntok = TOK_TM
    nsl = x_ref.shape[0] // ntok
    blk_rows = MOE_BLOCK * nsl

    def zero_fill(e):
        first = pl.multiple_of((seg_ref[e] + seg_ref[N_EXPERTS + e] - MOE_BLOCK) * nsl, blk_rows)
        return pltpu.make_async_copy(zero_ref, xs_hbm.at[pl.ds(first, blk_rows)], zsem)

    def has_pad(e):
        return seg_ref[N_EXPERTS + e] > seg_ref[2 * N_EXPERTS + e]

    @pl.when(i == 0)
    def _():
        zero_ref[...] = jnp.zeros_like(zero_ref)

        def start(e, c):
            @pl.when(has_pad(e))
            def _():
                zero_fill(e).start()
            return c

        def wait(e, c):
            @pl.when(has_pad(e))
            def _():
                zero_fill(e).wait()
            return c

        lax.fori_loop(0, N_EXPERTS, start, 0)
        lax.fori_loop(0, N_EXPERTS, wait, 0)

        nvalid = seg_ref[3 * N_EXPERTS]
        nblk = xs_hbm.shape[0] // blk_rows

        def tail(j):
            first = pl.multiple_of((nvalid + j) * blk_rows, blk_rows)
            return pltpu.make_async_copy(zero_ref, xs_hbm.at[pl.ds(first, blk_rows)], zsem)

        def tail_start(j, c):
            @pl.when(nvalid + j < nblk)
            def _():
                tail(j).start()
            return c

        def tail_wait(j, c):
            @pl.when(nvalid + j < nblk)
            def _():
                tail(j).wait()
            return c

        lax.fori_loop(0, N_EXPERTS, tail_start, 0)
        lax.fori_loop(0, N_EXPERTS, tail_wait, 0)

    def row_copy(j, slot):
        return pltpu.make_async_copy(x_ref.at[pl.ds(pl.multiple_of(j * nsl, nsl), nsl)],
                                     xs_hbm.at[pl.ds(pl.multiple_of(slot * nsl, nsl), nsl)], sem)

    def issue(j, c):
        for k in range(2):
            row_copy(j, (d0_ref, d1_ref)[k][0, 0, j]).start(priority=k)
        return c

    lax.fori_loop(0, ntok, issue, 0, unroll=4)
    for _ in range(TOP_K):
        pltpu.make_async_copy(x_ref, xs_hbm.at[pl.ds(0, ntok * nsl)], sem).wait()


def _dispatch(x_tiles, dest, seg, cap, n):
    nsl = x_tiles.shape[0] // n
    tm = TOK_TM
    nt = n // tm
    return pl.pallas_call(
        _dispatch_kernel,
        out_shape=jax.ShapeDtypeStruct((cap * nsl, LANES), x_tiles.dtype),
        grid_spec=pltpu.PrefetchScalarGridSpec(
            num_scalar_prefetch=1,
            grid=(nt,),
            in_specs=[pl.BlockSpec((1, 1, tm), lambda i, seg: (i, 0, 0), memory_space=pltpu.SMEM),
                      pl.BlockSpec((1, 1, tm), lambda i, seg: (i, 0, 0), memory_space=pltpu.SMEM),
                      pl.BlockSpec((tm * nsl, LANES), lambda i, seg: (i, 0))],
            out_specs=pl.BlockSpec(memory_space=pl.ANY),
            scratch_shapes=[pltpu.VMEM((MOE_BLOCK * nsl, LANES), x_tiles.dtype), pltpu.SemaphoreType.DMA,
                            pltpu.SemaphoreType.DMA]),
        compiler_params=_params(("arbitrary",)),
        name="moe_dispatch",
    )(seg, dest[0], dest[1], x_tiles)


def _expert_kernel(blk_ref, nvalid_ref, xs_ref, wg_ref, wu_ref, wd_ref, ys_ref, wgb, wub, wdb):
    i = pl.program_id(0)
    prev = blk_ref[jnp.maximum(i - 1, 0)]
    valid = i < nvalid_ref[0]

    @pl.when(valid & ((i == 0) | (blk_ref[i] != prev)))
    def _():
        wgb[...] = wg_ref[0].astype(_BF16)
        wub[...] = wu_ref[0].astype(_BF16)
        wdb[...] = wd_ref[0].astype(_BF16)

    @pl.when(valid)
    def _():
        d = wgb.shape[0]
        xb = _rows_from_tiles(xs_ref, MOE_BLOCK, d).astype(_BF16)
        gate = jnp.dot(xb, wgb[...], preferred_element_type=_F32)
        up = jnp.dot(xb, wub[...], preferred_element_type=_F32)
        mid = (gate * jax.nn.sigmoid(gate)) * up
        _rows_to_tiles(ys_ref, jnp.dot(mid.astype(_BF16), wdb[...], preferred_element_type=_F32))

    @pl.when(jnp.logical_not(valid))
    def _():
        ys_ref[...] = jnp.zeros_like(ys_ref)


def _experts(xs, blk_exp, nvalid, w_gate, w_up, w_down):
    d, de = w_gate.shape[1], w_gate.shape[2]
    nsl = d // LANES
    cap = xs.shape[0] // nsl
    nblk = cap // MOE_BLOCK

    def xmap(i, blk, nv):
        return (jnp.minimum(i, nv[0] - 1), 0)

    def wmap(i, blk, nv):
        return (blk[i], 0, 0)

    return pl.pallas_call(
        _expert_kernel,
        out_shape=jax.ShapeDtypeStruct((cap * nsl, LANES), _F32),
        grid_spec=pltpu.PrefetchScalarGridSpec(
            num_scalar_prefetch=2,
            grid=(nblk,),
            in_specs=[pl.BlockSpec((MOE_BLOCK * nsl, LANES), xmap),
                      pl.BlockSpec((1, d, de), wmap), pl.BlockSpec((1, d, de), wmap),
                      pl.BlockSpec((1, de, d), wmap)],
            out_specs=pl.BlockSpec((MOE_BLOCK * nsl, LANES), lambda i, blk, nv: (i, 0)),
            scratch_shapes=[pltpu.VMEM((d, de), _BF16), pltpu.VMEM((d, de), _BF16), pltpu.VMEM((de, d), _BF16)]),
        compiler_params=_params(("arbitrary",)),
        name="moe_experts",
    )(blk_exp, nvalid, xs, w_gate, w_up, w_down)


def _combine_kernel(c0_ref, c1_ref, n0_ref, n1_ref, ys_hbm, h_ref, route_ref, g_ref, b_ref, o_ref, buf, sem):
    i = pl.program_id(0)
    n = pl.num_programs(0)
    tm = TOK_TM

    d = o_ref.shape[1]
    nsl = d // LANES

    def tile_copy(src_slot, slot, k, j):
        return pltpu.make_async_copy(ys_hbm.at[pl.ds(pl.multiple_of(src_slot * nsl, nsl), nsl)],
                                     buf.at[slot, k, pl.ds(pl.multiple_of(j * nsl, nsl), nsl)], sem.at[slot])

    def gather(drefs, slot):
        def body(j, c):
            for k in range(2):
                tile_copy(drefs[k][0, 0, j], slot, k, j).start(priority=k)
            return c
        lax.fori_loop(0, tm, body, 0, unroll=4)

    @pl.when(i == 0)
    def _():
        gather((c0_ref, c1_ref), 0)

    @pl.when(i + 1 < n)
    def _():
        gather((n0_ref, n1_ref), (i + 1) % 2)

    slot = i % 2

    for k in range(TOP_K):
        pltpu.make_async_copy(ys_hbm.at[pl.ds(0, tm * nsl)], buf.at[slot, k], sem.at[slot]).wait()
    r = route_ref[...]
    y = (r[:, 2:3] * _rows_from_tiles(buf.at[slot, 0], tm, d)
         + r[:, 3:4] * _rows_from_tiles(buf.at[slot, 1], tm, d))
    o_ref[...] = _layer_norm(ALPHA * _rows_from_tiles(h_ref, tm, d) + y, g_ref[...], b_ref[...])


def _combine(ys, dest, h1_tiles, route, g, b):
    n = route.shape[0]
    d = g.shape[1]
    nsl = d // LANES
    tm = TOK_TM
    nt = n // tm
    row = lambda i: (i, 0)
    const = lambda i: (0, 0)
    return pl.pallas_call(
        _combine_kernel,
        out_shape=jax.ShapeDtypeStruct((n, d), _F32),
        grid=(nt,),
        in_specs=[pl.BlockSpec((1, 1, tm), lambda i: (i, 0, 0), memory_space=pltpu.SMEM)] * TOP_K
        + [pl.BlockSpec((1, 1, tm), lambda i: (jnp.minimum(i + 1, nt - 1), 0, 0), memory_space=pltpu.SMEM)] * TOP_K
        + [pl.BlockSpec(memory_space=pl.ANY),
                  pl.BlockSpec((tm * nsl, LANES), row), pl.BlockSpec((tm, LANES), row),
                  pl.BlockSpec((1, d), const), pl.BlockSpec((1, d), const)],
        out_specs=pl.BlockSpec((tm, d), row),
        scratch_shapes=[pltpu.VMEM((2, 2, tm * nsl, LANES), _F32), pltpu.SemaphoreType.DMA((2,))],
        compiler_params=_params(("arbitrary",)),
        name="moe_combine",
    )(dest[0], dest[1], dest[0], dest[1], ys, h1_tiles, route, g, b)


def _moe(h1_tiles, route, route_t, counts_row, w_gate, w_up, w_down, g, b):
    n = route.shape[0]
    cap = 2 * n + N_EXPERTS * MOE_BLOCK
    counts = counts_row[0, ROUTE_LANE0:ROUTE_LANE0 + N_EXPERTS].astype(jnp.int32)
    padded = (counts + MOE_BLOCK - 1) // MOE_BLOCK * MOE_BLOCK
    ends = jnp.cumsum(padded)
    pad_start = ends - padded
    nblk = cap // MOE_BLOCK
    blk_first = jnp.arange(nblk, dtype=jnp.int32) * MOE_BLOCK
    blk_exp = jnp.minimum(jnp.sum(blk_first[:, None] >= ends[None, :], axis=1), N_EXPERTS - 1).astype(jnp.int32)
    nvalid = (ends[-1:] // MOE_BLOCK).astype(jnp.int32)
    dest = _dest(route_t, pad_start.astype(_F32)).reshape(TOP_K, n // TOK_TM, 1, TOK_TM)
    seg = jnp.concatenate([pad_start, padded, counts, nvalid]).astype(jnp.int32)
    xs = _dispatch(h1_tiles, dest, seg, cap, n)
    ys = _experts(xs, blk_exp, nvalid, w_gate, w_up, w_down)
    return _combine(ys, dest, h1_tiles, route, g, b)


def _router_params(rg_w, rg_b, re_w, re_b):
    d = rg_w.shape[0]
    pad = LANES - N_GROUPS - N_EXPERTS
    rw = jnp.concatenate([rg_w, re_w, jnp.zeros((d, pad), _F32)], axis=1)
    rb = jnp.concatenate([rg_b, re_b.reshape(-1), jnp.zeros((pad,), _F32)]).reshape(1, LANES)
    return rw, rb


def kernel(x, t5_table, l0_w_qkv, l0_sink, l0_w_o, l0_ln1_g, l0_ln1_b, l0_rg_w, l0_rg_b, l0_re_w, l0_re_b, l0_w_gate, l0_w_up, l0_w_down, l0_ln2_g, l0_ln2_b, l1_w_qkv, l1_w_o, l1_ln1_g, l1_ln1_b, l1_rg_w, l1_rg_b, l1_re_w, l1_re_b, l1_w_gate, l1_w_up, l1_w_down, l1_ln2_g, l1_ln2_b, l2_w_qkv, l2_rpb, l2_w_o, l2_ln1_g, l2_ln1_b, l2_rg_w, l2_rg_b, l2_re_w, l2_re_b, l2_w_gate, l2_w_up, l2_w_down, l2_ln2_g, l2_ln2_b, l3_w_qkv, l3_sink, l3_w_o, l3_ln1_g, l3_ln1_b, l3_rg_w, l3_rg_b, l3_re_w, l3_re_b, l3_w_gate, l3_w_up, l3_w_down, l3_ln2_g, l3_ln2_b):
    mixers = [(l0_w_qkv, l0_sink, l0_w_o), (l1_w_qkv, None, l1_w_o), (l2_w_qkv, l2_rpb, l2_w_o),
              (l3_w_qkv, l3_sink, l3_w_o)]
    norm1 = [(l0_ln1_g, l0_ln1_b), (l1_ln1_g, l1_ln1_b), (l2_ln1_g, l2_ln1_b), (l3_ln1_g, l3_ln1_b)]
    routers = [(l0_rg_w, l0_rg_b, l0_re_w, l0_re_b), (l1_rg_w, l1_rg_b, l1_re_w, l1_re_b),
               (l2_rg_w, l2_rg_b, l2_re_w, l2_re_b), (l3_rg_w, l3_rg_b, l3_re_w, l3_re_b)]
    experts = [(l0_w_gate, l0_w_up, l0_w_down), (l1_w_gate, l1_w_up, l1_w_down),
               (l2_w_gate, l2_w_up, l2_w_down), (l3_w_gate, l3_w_up, l3_w_down)]
    norm2 = [(l0_ln2_g, l0_ln2_b), (l1_ln2_g, l1_ln2_b), (l2_ln2_g, l2_ln2_b), (l3_ln2_g, l3_ln2_b)]

    bsz, seq, d = x.shape
    n = bsz * seq
    inner = N_HEADS * HEAD_DIM
    scale = HEAD_DIM ** -0.5 * LOG2E
    h = x.reshape(n, d)
    bias_a = _band_bias_t(t5_table, A_BLOCK, A_WINDOW, 1, A_BAND)
    for i in range(DEPTH):
        kind = i % N_MIXERS
        w_qkv, extra, w_o = mixers[i]
        kvw = A_KV_HEADS * HEAD_DIM if kind == 0 else inner
        lses, unperm = (), None
        if kind == 1:
            perms = np.stack([_residue_perm(dil) for _, dil in B_BRANCHES if dil > 1])
            outs = _project(h, w_qkv.astype(_BF16), (inner, kvw, kvw), (scale, 1.0, 1.0), jnp.asarray(perms, _BF16))
            unperm = jnp.asarray(perms.transpose(0, 2, 1), _BF16)
            attn, lses = [], []
            for bi, (window, dil) in enumerate(B_BRANCHES):
                q, k, v = outs[3 * bi:3 * bi + 3]
                o, lse = _mixer_b_branch(q, k, v, t5_table, bsz, seq, window, dil)
                attn.append(o)
                lses.append(lse.reshape(n, N_HEADS))
        else:
            q, k, v = _project(h, w_qkv.astype(_BF16), (inner, kvw, kvw), (scale, 1.0, 1.0))
            q, k, v = (t.reshape(bsz, seq, t.shape[1]) for t in (q, k, v))
            if kind == 0:
                attn = [_mixer_a(q, k, v, extra.astype(_F32) * LOG2E, bias_a, bsz, seq)]
            else:
                attn = [_mixer_c(q, k, v, _na_bias_t(extra, seq // GRID_W), bsz, seq)]
        attn = [a.reshape(n, inner) for a in attn]
        g1, b1 = norm1[i]
        rw, rb = _router_params(*routers[i])
        h1, route, route_t, counts = _post_attn(attn, lses, h, w_o.astype(_BF16), g1.reshape(1, d), b1.reshape(1, d),
                                                rw, rb, unperm)
        g2, b2 = norm2[i]
        h = _moe(h1, route, route_t, counts, *experts[i], g2.reshape(1, d), b2.reshape(1, d))
    return h.reshape(bsz, seq, d)
```
